```python
import math, functools
import jax, jax.numpy as jnp
from jax import lax
import numpy as np

D_MODEL = 2048
BATCH = 4
SEQ = 2048
DEPTH = 2
DEC_BATCH = 128
DEC_SEQ = 1
PAST_LEN = 16384
PAGE_SIZE = 128

N_EVEN = (DEPTH + 1) // 2
N_ODD = DEPTH // 2
EPS = 1e-6
SGU_CHUNK = 128
SGU_GROUPS = 4
SGU_WIDTH = D_MODEL // 2
SGU_GROUP_DIM = SGU_WIDTH // SGU_GROUPS
GLA_HEADS = 4
GLA_DV = (D_MODEL // 2) // GLA_HEADS
GLA_DK = GLA_DV // 2
GLA_RANK = 16
GLA_TAU = 16.0
GLA_CHUNK = 16
GLA_KEY_W = GLA_HEADS * GLA_DK
GLA_VAL_W = GLA_HEADS * GLA_DV
IN_SIZES = (SGU_WIDTH, SGU_WIDTH, GLA_KEY_W, GLA_KEY_W, GLA_VAL_W, GLA_VAL_W, GLA_RANK)
IN_COLS = sum(IN_SIZES)
MIX_OUT = SGU_WIDTH + GLA_VAL_W
CONV_DIM = D_MODEL
CONV_WIDTH = 3
FF_DENSE = 5632
N_EXPERTS = 8
TOP_K = 2
FF_EXPERT = 7168
MOE_BLOCK = 128

kernel_name = 'hybrid_sgu_gla_shortconv_moe_step'


def rms_norm(x, g):
    xf = x.astype(jnp.float32)
    xf = xf * lax.rsqrt(jnp.mean(xf * xf, axis=-1, keepdims=True) + EPS)
    return xf.astype(x.dtype) * g


def sgu_chunk_mix(v, w_s, b_s):
    n, L, _ = v.shape
    c = min(SGU_CHUNK, L)
    nc = -(-L // c)
    vp = jnp.pad(v, ((0, 0), (0, nc * c - L), (0, 0))).reshape(n, nc, c, SGU_GROUPS, SGU_GROUP_DIM)
    w_c = jnp.where(jnp.tril(jnp.ones((c, c), dtype=bool)), w_s[:, :c, :c], 0)
    mixed = jnp.einsum('gij,ncjgd->ncigd', w_c, vp) + b_s[:, :c].T[None, None, :, :, None]
    return mixed.reshape(n, nc * c, SGU_WIDTH)[:, :L]


def gla_scan(q, k, v, log_a, s0):
    n, L = q.shape[0], q.shape[1]
    c = min(GLA_CHUNK, L)
    nc = -(-L // c)
    pad = nc * c - L

    def blocks(t):
        t = jnp.pad(t, ((0, 0), (0, pad), (0, 0), (0, 0)))
        return t.reshape(n, nc, c, t.shape[2], t.shape[3]).transpose(1, 0, 3, 2, 4)

    causal = jnp.tril(jnp.ones((c, c), dtype=bool))[:, :, None]

    def step(s, blk):
        qc, kc, vc, ac = blk
        cum = jnp.cumsum(ac, axis=2)
        rel = jnp.where(causal, cum[:, :, :, None, :] - cum[:, :, None, :, :], -jnp.inf)
        att = jnp.einsum('nhid,nhjd,nhijd->nhij', qc, kc, jnp.exp(rel))
        o = jnp.einsum('nhij,nhjv->nhiv', att, vc) + jnp.einsum('nhid,nhdv->nhiv', qc * jnp.exp(cum), s)
        last = cum[:, :, -1, :]
        s = jnp.exp(last)[..., None] * s + jnp.einsum(
            'nhjd,nhjv->nhdv', kc * jnp.exp(last[:, :, None, :] - cum), vc)
        return s, o

    s, o = lax.scan(step, s0, (blocks(q), blocks(k), blocks(v), blocks(log_a)))
    o = o.transpose(1, 0, 3, 2, 4).reshape(n, nc * c, q.shape[2], v.shape[3])[:, :L]
    return o, s


def even_mixer(x, s0, norm_g, w_in, sgu_norm, sgu_w, sgu_b, w_alpha, b_alpha, gla_norm, w_out):
    n, L, _ = x.shape
    h = rms_norm(x, norm_g)
    a_u, a_v, q, k, v, r, g_lr = jnp.split(h @ w_in, np.cumsum(IN_SIZES)[:-1].tolist(), axis=-1)
    a_u = jax.nn.gelu(a_u)
    a_v = rms_norm(jax.nn.gelu(a_v), sgu_norm)
    a_out = a_u * sgu_chunk_mix(a_v, sgu_w, sgu_b)
    f32 = jnp.float32
    qh = (q.reshape(n, L, GLA_HEADS, GLA_DK) * (GLA_DK ** -0.5)).astype(f32)
    kh = k.reshape(n, L, GLA_HEADS, GLA_DK).astype(f32)
    vh = v.reshape(n, L, GLA_HEADS, GLA_DV).astype(f32)
    log_a = (jax.nn.log_sigmoid((g_lr @ w_alpha + b_alpha).astype(f32)) / GLA_TAU).reshape(n, L, GLA_HEADS, GLA_DK)
    o, s_new = gla_scan(qh, kh, vh, log_a, s0.astype(f32))
    o = rms_norm(o.astype(x.dtype), gla_norm).reshape(n, L, GLA_VAL_W) * jax.nn.silu(r)
    y = jnp.concatenate([a_out, o], axis=-1) @ w_out
    return y, s_new.astype(s0.dtype), a_v


def odd_mixer(x, buf, norm_g, w_in, conv_w, w_out):
    L = x.shape[1]
    h = rms_norm(x, norm_g)
    b_g, c_g, u = jnp.split(h @ w_in, 3, axis=-1)
    z = c_g * u
    zc = jnp.concatenate([buf.astype(z.dtype), z], axis=1)
    conv = conv_w[0] * zc[:, 0:L]
    for t in range(1, CONV_WIDTH):
        conv = conv + conv_w[t] * zc[:, t:t + L]
    y = (b_g * conv) @ w_out
    return y, zc[:, -(CONV_WIDTH - 1):].astype(buf.dtype)


def dense_swiglu(x, norm_g, w_gu, w_down):
    g, u = jnp.split(rms_norm(x, norm_g) @ w_gu, 2, axis=-1)
    return (jax.nn.silu(g) * u) @ w_down


def moe_swiglu(x, norm_g, router_w, w_gu, w_down):
    n, L, D = x.shape
    h = rms_norm(x, norm_g).reshape(n * L, D)
    T = n * L
    logits = (h @ router_w).astype(jnp.float32)
    top_val, top_idx = lax.top_k(logits, TOP_K)
    gates = jax.nn.softmax(top_val, axis=-1).astype(h.dtype)
    M = T * TOP_K
    flat_e = top_idx.reshape(M).astype(jnp.int32)
    flat_tok = jnp.repeat(jnp.arange(T, dtype=jnp.int32), TOP_K)
    flat_g = gates.reshape(M)
    order = jnp.argsort(flat_e)
    e_sorted = flat_e[order]
    counts = jnp.bincount(flat_e, length=N_EXPERTS)
    padded = (counts + MOE_BLOCK - 1) // MOE_BLOCK * MOE_BLOCK
    pad_end = jnp.cumsum(padded)
    pad_start = pad_end - padded
    start = jnp.cumsum(counts) - counts
    slot = pad_start[e_sorted] + jnp.arange(M, dtype=jnp.int32) - start[e_sorted]
    n_blocks = -(-M // MOE_BLOCK) + N_EXPERTS
    P = n_blocks * MOE_BLOCK
    slot_tok = jnp.full((P,), T, dtype=jnp.int32).at[slot].set(flat_tok[order])
    slot_gate = jnp.zeros((P,), h.dtype).at[slot].set(flat_g[order])
    block_e = jnp.minimum(
        jnp.searchsorted(pad_end, jnp.arange(n_blocks, dtype=jnp.int32) * MOE_BLOCK, side='right'),
        N_EXPERTS - 1)
    h_pad = jnp.concatenate([h, jnp.zeros((1, D), h.dtype)], axis=0)
    xb = h_pad[slot_tok].reshape(n_blocks, MOE_BLOCK, D)

    def expert_block(args):
        xi, e = args
        g, u = jnp.split(xi @ w_gu[e], 2, axis=-1)
        return (jax.nn.silu(g) * u) @ w_down[e]

    yb = lax.map(expert_block, (xb, block_e)).reshape(P, D)
    y = jnp.zeros((T + 1, D), h.dtype).at[slot_tok].add(yb * slot_gate[:, None])
    return y[:T].reshape(n, L, D)


def setup_inputs(seed: int = 0) -> dict:
    key = jax.random.key(seed)
    ks = jax.random.split(key, 32)

    def nrm(k, shape, scale):
        return jax.random.normal(k, shape, jnp.float32) * scale

    def gain(k, shape):
        return 1.0 + 0.02 * jax.random.normal(k, shape, jnp.float32)

    return {
        'x_prompt': nrm(ks[0], (BATCH, SEQ, D_MODEL), 1.0),
        'x_sample': nrm(ks[1], (DEC_BATCH, DEC_SEQ, D_MODEL), 1.0),
        'state_gla': nrm(ks[2], (N_EVEN, DEC_BATCH, GLA_HEADS, GLA_DK, GLA_DV), 1.0),
        'state_conv': nrm(ks[3], (N_ODD, DEC_BATCH, CONV_WIDTH - 1, CONV_DIM), 1.0),
        'ev_norm_mix': gain(ks[4], (N_EVEN, D_MODEL)),
        'ev_w_in': nrm(ks[5], (N_EVEN, D_MODEL, IN_COLS), D_MODEL ** -0.5),
        'ev_sgu_norm': gain(ks[6], (N_EVEN, SGU_WIDTH)),
        'ev_sgu_w': nrm(ks[7], (N_EVEN, SGU_GROUPS, SGU_CHUNK, SGU_CHUNK), SGU_CHUNK ** -0.5),
        'ev_sgu_b': 1.0 + 0.1 * jax.random.normal(ks[8], (N_EVEN, SGU_GROUPS, SGU_CHUNK), jnp.float32),
        'ev_gla_w_alpha': nrm(ks[9], (N_EVEN, GLA_RANK, GLA_KEY_W), GLA_RANK ** -0.5),
        'ev_gla_b_alpha': nrm(ks[10], (N_EVEN, GLA_KEY_W), 0.1),
        'ev_gla_norm': gain(ks[11], (N_EVEN, GLA_DV)),
        'ev_w_out': nrm(ks[12], (N_EVEN, MIX_OUT, D_MODEL), MIX_OUT ** -0.5),
        'ev_norm_ffn': gain(ks[13], (N_EVEN, D_MODEL)),
        'ev_ffn_w_gu': nrm(ks[14], (N_EVEN, D_MODEL, 2 * FF_DENSE), D_MODEL ** -0.5),
        'ev_ffn_w_down': nrm(ks[15], (N_EVEN, FF_DENSE, D_MODEL), FF_DENSE ** -0.5),
        'od_norm_mix': gain(ks[16], (N_ODD, D_MODEL)),
        'od_conv_w_in': nrm(ks[17], (N_ODD, D_MODEL, 3 * CONV_DIM), D_MODEL ** -0.5),
        'od_conv_w': nrm(ks[18], (N_ODD, CONV_WIDTH, CONV_DIM), CONV_WIDTH ** -0.5),
        'od_conv_w_out': nrm(ks[19], (N_ODD, CONV_DIM, D_MODEL), CONV_DIM ** -0.5),
        'od_norm_ffn': gain(ks[20], (N_ODD, D_MODEL)),
        'od_router_w': nrm(ks[21], (N_ODD, D_MODEL, N_EXPERTS), D_MODEL ** -0.5),
        'od_moe_w_gu': nrm(ks[22], (N_ODD, N_EXPERTS, D_MODEL, 2 * FF_EXPERT), D_MODEL ** -0.5),
        'od_moe_w_down': nrm(ks[23], (N_ODD, N_EXPERTS, FF_EXPERT, D_MODEL), FF_EXPERT ** -0.5),
        'final_norm': gain(ks[24], (D_MODEL,)),
    }


def reference(x_prompt, x_sample, state_gla, state_conv, ev_norm_mix, ev_w_in, ev_sgu_norm, ev_sgu_w,
              ev_sgu_b, ev_gla_w_alpha, ev_gla_b_alpha, ev_gla_norm, ev_w_out, ev_norm_ffn, ev_ffn_w_gu,
              ev_ffn_w_down, od_norm_mix, od_conv_w_in, od_conv_w, od_conv_w_out, od_norm_ffn, od_router_w,
              od_moe_w_gu, od_moe_w_down, final_norm):
    xp, xs = x_prompt, x_sample
    n_p = xp.shape[0]
    gla_p, gla_s, sgu_v_s, conv_p, conv_s = [], [], [], [], []
    for layer in range(DEPTH):
        i = layer // 2
        if layer % 2 == 0:
            mw = (ev_norm_mix[i], ev_w_in[i], ev_sgu_norm[i], ev_sgu_w[i], ev_sgu_b[i],
                  ev_gla_w_alpha[i], ev_gla_b_alpha[i], ev_gla_norm[i], ev_w_out[i])
            s0_p = jnp.zeros((n_p,) + state_gla.shape[2:], state_gla.dtype)
            dp, sp, _ = even_mixer(xp, s0_p, *mw)
            ds, ss, vs = even_mixer(xs, state_gla[i], *mw)
            xp = xp + dp
            xs = xs + ds
            gla_p.append(sp)
            gla_s.append(ss)
            sgu_v_s.append(vs)
            xp = xp + dense_swiglu(xp, ev_norm_ffn[i], ev_ffn_w_gu[i], ev_ffn_w_down[i])
            xs = xs + dense_swiglu(xs, ev_norm_ffn[i], ev_ffn_w_gu[i], ev_ffn_w_down[i])
        else:
            mw = (od_norm_mix[i], od_conv_w_in[i], od_conv_w[i], od_conv_w_out[i])
            buf_p = jnp.zeros((n_p,) + state_conv.shape[2:], state_conv.dtype)
            dp, bp = odd_mixer(xp, buf_p, *mw)
            ds, bs = odd_mixer(xs, state_conv[i], *mw)
            xp = xp + dp
            xs = xs + ds
            conv_p.append(bp)
            conv_s.append(bs)
            xp = xp + moe_swiglu(xp, od_norm_ffn[i], od_router_w[i], od_moe_w_gu[i], od_moe_w_down[i])
            xs = xs + moe_swiglu(xs, od_norm_ffn[i], od_router_w[i], od_moe_w_gu[i], od_moe_w_down[i])
    y_prompt = rms_norm(xp, final_norm)
    y_sample = rms_norm(xs, final_norm)
    return (y_prompt, y_sample, jnp.stack(gla_p), jnp.stack(gla_s), jnp.stack(sgu_v_s),
            jnp.stack(conv_p), jnp.stack(conv_s))
```

```python
import functools
import math

import numpy as np
import jax
import jax.numpy as jnp
from jax import lax
from jax.experimental import pallas as pl
from jax.experimental.pallas import tpu as pltpu

F32 = jnp.float32
BF16 = jnp.bfloat16
EPS = 1e-6

SGU_CHUNK = 128
SGU_GROUPS = 4
GLA_HEADS = 4
GLA_TAU = 16.0
GLA_CHUNK = 128
N_EXPERTS = 8
TOP_K = 2
MOE_SUB = 256
LANES = 128
VMEM_LIMIT = 52 * 1024 * 1024


def _cparams(n_axes, vmem=VMEM_LIMIT):
    return pltpu.CompilerParams(dimension_semantics=("arbitrary",) * n_axes,
                                vmem_limit_bytes=vmem)


def _pick_tile(n, cap, mult=16):
    best = None
    for t in range(mult, min(n, cap) + 1, mult):
        if n % t == 0:
            best = t
    assert best is not None, (n, cap, mult)
    return best


def _rms(x, g):
    ms = jnp.mean(x * x, axis=-1, keepdims=True)
    return (x * lax.rsqrt(ms + EPS)) * g


def _gelu(x):
    c = math.sqrt(2.0 / math.pi)
    return x * (0.5 * (1.0 + jnp.tanh(c * (x + 0.044715 * (x * x * x)))))


def _silu(x):
    return x * (1.0 / (1.0 + jnp.exp(-x)))


def _dot(a, b):
    return jnp.dot(a, b, preferred_element_type=F32)


def _dot_nt(a, b):
    return lax.dot_general(a, b, (((1,), (1,)), ((), ())), preferred_element_type=F32)


def _norm_kernel(x_ref, g_ref, o_ref):
    o_ref[...] = _rms(x_ref[...], g_ref[...]).astype(BF16)


def _norm_bf16(x, g):
    T, D = x.shape
    tr = _pick_tile(T, 512)
    return pl.pallas_call(
        _norm_kernel,
        grid=(T // tr,),
        in_specs=[pl.BlockSpec((tr, D), lambda i: (i, 0)),
                  pl.BlockSpec((1, D), lambda i: (0, 0))],
        out_specs=pl.BlockSpec((tr, D), lambda i: (i, 0)),
        out_shape=jax.ShapeDtypeStruct((T, D), BF16),
        compiler_params=_cparams(1),
        name="rmsnorm_bf16",
    )(x, g.reshape(1, D))


def _mm_kernel(x_ref, w_ref, o_ref):
    o_ref[...] = _dot(x_ref[...], w_ref[...].astype(BF16)).astype(o_ref.dtype)


def _mm_res_kernel(x_ref, w_ref, r_ref, o_ref):
    o_ref[...] = r_ref[...] + _dot(x_ref[...], w_ref[...].astype(BF16))


def _matmul(x, w, n_cols, tm, tn, res=None, name="matmul"):
    T, K = x.shape
    grid = (T // tm, n_cols // tn)
    in_specs = [pl.BlockSpec((tm, K), lambda i, j: (i, 0)),
                pl.BlockSpec((K, tn), lambda i, j: (0, j))]
    args = [x, w]
    kern = _mm_kernel
    if res is not None:
        in_specs.append(pl.BlockSpec((tm, tn), lambda i, j: (i, j)))
        args.append(res)
        kern = _mm_res_kernel
    return pl.pallas_call(
        kern, grid=grid, in_specs=in_specs,
        out_specs=pl.BlockSpec((tm, tn), lambda i, j: (i, j)),
        out_shape=jax.ShapeDtypeStruct((T, n_cols), F32),
        compiler_params=_cparams(2),
        name=name,
    )(*args)


def _swiglu_kernel(x_ref, wg_ref, wu_ref, o_ref):
    x = x_ref[...]
    g = _dot(x, wg_ref[...].astype(BF16))
    u = _dot(x, wu_ref[...].astype(BF16))
    o_ref[...] = (_silu(g) * u).astype(o_ref.dtype)


def _swiglu_up(x, w_gu, tm, tn):
    T, K = x.shape
    F = w_gu.shape[1] // 2
    nj = F // tn
    return pl.pallas_call(
        _swiglu_kernel,
        grid=(T // tm, nj),
        in_specs=[pl.BlockSpec((tm, K), lambda i, j: (i, 0)),
                  pl.BlockSpec((K, tn), lambda i, j: (0, j)),
                  pl.BlockSpec((K, tn), lambda i, j: (0, j + nj))],
        out_specs=pl.BlockSpec((tm, tn), lambda i, j: (i, j)),
        out_shape=jax.ShapeDtypeStruct((T, F), BF16),
        compiler_params=_cparams(2),
        name="ffn_gate_up",
    )(x, w_gu, w_gu)


def _conv_in_kernel(x_ref, wb_ref, wc_ref, wu_ref, b_ref, z_ref):
    x = x_ref[...]
    b_ref[...] = _dot(x, wb_ref[...].astype(BF16))
    c = _dot(x, wc_ref[...].astype(BF16))
    u = _dot(x, wu_ref[...].astype(BF16))
    z_ref[...] = c * u


def _conv_in(x, w, tm, tn):
    T, K = x.shape
    C = w.shape[1] // 3
    nj = C // tn
    return pl.pallas_call(
        _conv_in_kernel,
        grid=(T // tm, nj),
        in_specs=[pl.BlockSpec((tm, K), lambda i, j: (i, 0)),
                  pl.BlockSpec((K, tn), lambda i, j: (0, j)),
                  pl.BlockSpec((K, tn), lambda i, j: (0, j + nj)),
                  pl.BlockSpec((K, tn), lambda i, j: (0, j + 2 * nj))],
        out_specs=[pl.BlockSpec((tm, tn), lambda i, j: (i, j)),
                   pl.BlockSpec((tm, tn), lambda i, j: (i, j))],
        out_shape=[jax.ShapeDtypeStruct((T, C), F32),
                   jax.ShapeDtypeStruct((T, C), F32)],
        compiler_params=_cparams(2),
        name="conv_in_proj",
    )(x, w, w, w)


def _mm_kt_kernel(x_ref, w_ref, r_ref, o_ref, acc_ref):
    k = pl.program_id(2)

    @pl.when(k == 0)
    def _():
        acc_ref[...] = r_ref[...]

    acc_ref[...] += _dot(x_ref[...], w_ref[...].astype(BF16))

    @pl.when(k == pl.num_programs(2) - 1)
    def _():
        o_ref[...] = acc_ref[...]


def _matmul_kt(x, w, res, tm, tn, tk):
    T, K = x.shape
    N = w.shape[1]
    return pl.pallas_call(
        _mm_kt_kernel,
        grid=(T // tm, N // tn, K // tk),
        in_specs=[pl.BlockSpec((tm, tk), lambda i, j, k: (i, k)),
                  pl.BlockSpec((tk, tn), lambda i, j, k: (k, j)),
                  pl.BlockSpec((tm, tn), lambda i, j, k: (i, j))],
        out_specs=pl.BlockSpec((tm, tn), lambda i, j, k: (i, j)),
        out_shape=jax.ShapeDtypeStruct((T, N), F32),
        scratch_shapes=[pltpu.VMEM((tm, tn), F32)],
        compiler_params=_cparams(3),
        name="ffn_down",
    )(x, w, res)


def _loga_kernel(x_ref, wl_ref, wa_ref, b_ref, o_ref):
    g = _dot(x_ref[...], wl_ref[...].astype(BF16))
    z = _dot(g.astype(BF16), wa_ref[...].astype(BF16)) + b_ref[...]
    ls = jnp.minimum(z, 0.0) - jnp.log1p(jnp.exp(-jnp.abs(z)))
    o_ref[...] = ls * (1.0 / GLA_TAU)


def _log_decay(x, w_lr, w_alpha, b_alpha, tm):
    T, K = x.shape
    R, KW = w_alpha.shape
    return pl.pallas_call(
        _loga_kernel,
        grid=(T // tm,),
        in_specs=[pl.BlockSpec((tm, K), lambda i: (i, 0)),
                  pl.BlockSpec((K, R), lambda i: (0, 0)),
                  pl.BlockSpec((R, KW), lambda i: (0, 0)),
                  pl.BlockSpec((1, KW), lambda i: (0, 0))],
        out_specs=pl.BlockSpec((tm, KW), lambda i: (i, 0)),
        out_shape=jax.ShapeDtypeStruct((T, KW), F32),
        compiler_params=_cparams(1),
        name="gla_log_decay",
    )(x, w_lr, w_alpha, b_alpha.reshape(1, KW))


def _sgu_kernel(n_prompt_blocks, u_ref, v_ref, ng_ref, w_ref, bt_ref, o_ref, vs_ref):
    i = pl.program_id(0)
    is_sample = i >= n_prompt_blocks
    C = SGU_CHUNK
    gu = _gelu(u_ref[...])
    vn = _rms(_gelu(v_ref[...]), ng_ref[...])

    @pl.when(is_sample)
    def _():
        vs_ref[...] = vn

    gd = vn.shape[1] // SGU_GROUPS
    row = lax.broadcasted_iota(jnp.int32, (C, C), 0)
    col = lax.broadcasted_iota(jnp.int32, (C, C), 1)
    vb = vn.astype(BF16)
    for g in range(SGU_GROUPS):
        w = w_ref[g]
        w_prompt = jnp.where(col <= row, w, 0.0)
        w_sample = jnp.where(col == row, w_ref[g, 0:1, 0:1], 0.0)
        w_eff = jnp.where(is_sample, w_sample, w_prompt).astype(BF16)
        bias = jnp.where(is_sample, bt_ref[0:1, g:g + 1], bt_ref[:, g:g + 1])
        mixed = _dot(w_eff, vb[:, g * gd:(g + 1) * gd]) + bias
        o_ref[:, g * gd:(g + 1) * gd] = (gu[:, g * gd:(g + 1) * gd] * mixed).astype(o_ref.dtype)


def _sgu(P, n_prompt, n_sample, width, sgu_norm, sgu_w, sgu_b):
    C = SGU_CHUNK
    assert n_prompt % C == 0 and n_sample == C
    T = n_prompt + n_sample
    npb = n_prompt // C
    return pl.pallas_call(
        functools.partial(_sgu_kernel, npb),
        grid=(T // C,),
        in_specs=[pl.BlockSpec((C, width), lambda i: (i, 0)),
                  pl.BlockSpec((C, width), lambda i: (i, 1)),
                  pl.BlockSpec((1, width), lambda i: (0, 0)),
                  pl.BlockSpec((SGU_GROUPS, C, C), lambda i: (0, 0, 0)),
                  pl.BlockSpec((C, SGU_GROUPS), lambda i: (0, 0))],
        out_specs=[pl.BlockSpec((C, width), lambda i: (i, 0)),
                   pl.BlockSpec((C, width), lambda i: (0, 0))],
        out_shape=[jax.ShapeDtypeStruct((T, width), BF16),
                   jax.ShapeDtypeStruct((n_sample, width), F32)],
        compiler_params=_cparams(1),
        name="sgu_mix",
    )(P, P, sgu_norm.reshape(1, width), sgu_w, sgu_b.T)


def _gla_constants():
    C = GLA_CHUNK
    s = np.arange(C)[None, :]
    i = np.arange(C)[:, None]
    gs = [s <= i, s > i]
    masks = []
    b = C // 2
    while b >= 1:
        upper = (i % (2 * b)) >= b
        m = (i // (2 * b)) * (2 * b) + b - 1
        gs.append(np.where(upper, (s > m) & (s <= i), (s > i) & (s <= m)))
        same = (i // (2 * b)) == (s // (2 * b))
        masks.append(upper & same & ~upper.T)
        b //= 2
    masks.append(np.eye(C, dtype=bool))
    G = np.concatenate(gs, axis=0).astype(np.float32)
    M = np.stack(masks).astype(np.float32)
    return jnp.asarray(G, BF16), jnp.asarray(M, F32)


def _gla_kernel(q_ref, k_ref, v_ref, r_ref, la_ref, G_ref, M_ref, gn_ref,
                o_ref, s_out_ref, st_ref):
    c = pl.program_id(1)
    C = GLA_CHUNK
    dk = q_ref.shape[1] // GLA_HEADS
    dv = v_ref.shape[1] // GLA_HEADS
    nlev = M_ref.shape[0] - 1

    @pl.when(c == 0)
    def _():
        st_ref[...] = jnp.zeros_like(st_ref)

    la = la_ref[...]
    hi = la.astype(BF16)
    r1 = la - hi.astype(F32)
    mid = r1.astype(BF16)
    lo = (r1 - mid.astype(F32)).astype(BF16)
    G = G_ref[...]
    X = jnp.exp(_dot(G, hi) + _dot(G, mid) + _dot(G, lo))

    scale = dk ** -0.5
    for h in range(GLA_HEADS):
        Xh = X[:, h * dk:(h + 1) * dk]
        qh = q_ref[:, h * dk:(h + 1) * dk] * scale
        kh = k_ref[:, h * dk:(h + 1) * dk]
        vh = v_ref[:, h * dv:(h + 1) * dv]
        x_cum = Xh[0:C]
        x_rest = Xh[C:2 * C]
        st_old = st_ref[h]
        o = _dot_nt((qh * x_cum).astype(BF16), st_old.astype(BF16))
        qb = qh.astype(BF16)
        kb = kh.astype(BF16)
        att = _dot_nt(qb, kb) * M_ref[nlev]
        for l in range(nlev):
            xl = Xh[(2 + l) * C:(3 + l) * C]
            att = att + _dot_nt((qh * xl).astype(BF16), (kh * xl).astype(BF16)) * M_ref[l]
        vb = vh.astype(BF16)
        o = o + _dot(att.astype(BF16), vb)
        vt = vh.T.astype(BF16)
        st_new = st_old * x_cum[C - 1:C, :] + _dot(vt, (kh * x_rest).astype(BF16))
        st_ref[h] = st_new
        on = _rms(o, gn_ref[...]) * _silu(r_ref[:, h * dv:(h + 1) * dv])
        o_ref[:, h * dv:(h + 1) * dv] = on.astype(o_ref.dtype)

    @pl.when(c == pl.num_programs(1) - 1)
    def _():
        for h in range(GLA_HEADS):
            s_out_ref[0, h] = st_ref[h].T


def _gla_prompt(P, log_a, n_seq, seq, col_q, kw, vw, gla_norm):
    C = GLA_CHUNK
    assert seq % C == 0
    nc = seq // C
    dk = kw // GLA_HEADS
    dv = vw // GLA_HEADS
    qb, kb = col_q // kw, col_q // kw + 1
    vb = (col_q + 2 * kw) // vw
    rb = vb + 1
    assert col_q % kw == 0 and (col_q + 2 * kw) % vw == 0
    G, M = _gla_constants()
    rows = lambda n, c: n * nc + c
    return pl.pallas_call(
        _gla_kernel,
        grid=(n_seq, nc),
        in_specs=[pl.BlockSpec((C, kw), lambda n, c: (rows(n, c), qb)),
                  pl.BlockSpec((C, kw), lambda n, c: (rows(n, c), kb)),
                  pl.BlockSpec((C, vw), lambda n, c: (rows(n, c), vb)),
                  pl.BlockSpec((C, vw), lambda n, c: (rows(n, c), rb)),
                  pl.BlockSpec((C, kw), lambda n, c: (rows(n, c), 0)),
                  pl.BlockSpec(G.shape, lambda n, c: (0, 0)),
                  pl.BlockSpec(M.shape, lambda n, c: (0, 0, 0)),
                  pl.BlockSpec((1, dv), lambda n, c: (0, 0))],
        out_specs=[pl.BlockSpec((C, vw), lambda n, c: (rows(n, c), 0)),
                   pl.BlockSpec((1, GLA_HEADS, dk, dv), lambda n, c: (n, 0, 0, 0))],
        out_shape=[jax.ShapeDtypeStruct((n_seq * seq, vw), BF16),
                   jax.ShapeDtypeStruct((n_seq, GLA_HEADS, dk, dv), F32)],
        scratch_shapes=[pltpu.VMEM((GLA_HEADS, dv, dk), F32)],
        compiler_params=_cparams(2),
        name="gla_prompt",
    )(P, P, P, P, log_a, G, M, gla_norm.reshape(1, dv))


def _gla_step_kernel(s_ref, a_ref, k_ref, q_ref, v_ref, r_ref, gn_ref, sn_ref, o_ref):
    dk = s_ref.shape[2]
    s_new = jnp.exp(a_ref[...]) * s_ref[...] + k_ref[...] * v_ref[...]
    sn_ref[...] = s_new
    o = jnp.sum((q_ref[...] * (dk ** -0.5)) * s_new, axis=2, keepdims=True)
    o_ref[...] = _rms(o, gn_ref[...]) * _silu(r_ref[...])


def _gla_sample(state, la, k, q, v, r, gla_norm):
    B, H, dk, dv = state.shape
    bb = 4
    col = lambda t: t.reshape(B, H, dk, 1)
    rowv = lambda t: t.reshape(B, H, 1, dv)
    cspec = pl.BlockSpec((bb, H, dk, 1), lambda i: (i, 0, 0, 0))
    rspec = pl.BlockSpec((bb, H, 1, dv), lambda i: (i, 0, 0, 0))
    sspec = pl.BlockSpec((bb, H, dk, dv), lambda i: (i, 0, 0, 0))
    return pl.pallas_call(
        _gla_step_kernel,
        grid=(B // bb,),
        in_specs=[sspec, cspec, cspec, cspec, rspec, rspec,
                  pl.BlockSpec((1, 1, 1, dv), lambda i: (0, 0, 0, 0))],
        out_specs=[sspec, rspec],
        out_shape=[jax.ShapeDtypeStruct((B, H, dk, dv), F32),
                   jax.ShapeDtypeStruct((B, H, 1, dv), F32)],
        compiler_params=_cparams(1),
        name="gla_sample_step",
    )(state, col(la), col(k), col(q), rowv(v), rowv(r), gla_norm.reshape(1, 1, 1, dv))


def _conv_kernel(n_prompt_blocks, blocks_per_seq, z_ref, halo_ref, b_ref, w_ref,
                 h0_ref, h1_ref, o_ref):
    i = pl.program_id(0)
    z = z_ref[...]
    w0, w1, w2 = w_ref[0:1, :], w_ref[1:2, :], w_ref[2:3, :]

    @pl.when(i < n_prompt_blocks)
    def _():
        row = lax.broadcasted_iota(jnp.int32, z.shape, 0)
        halo = jnp.where(i % blocks_per_seq == 0, 0.0, halo_ref[...])
        z1 = jnp.where(row == 0, halo[7:8, :], pltpu.roll(z, 1, 0))
        z2 = pltpu.roll(z, 2, 0)
        z2 = jnp.where(row == 0, halo[6:7, :], jnp.where(row == 1, halo[7:8, :], z2))
        o_ref[...] = (b_ref[...] * (w0 * z2 + w1 * z1 + w2 * z)).astype(o_ref.dtype)

    @pl.when(i >= n_prompt_blocks)
    def _():
        conv = w0 * h0_ref[...] + w1 * h1_ref[...] + w2 * z
        o_ref[...] = (b_ref[...] * conv).astype(o_ref.dtype)


def _short_conv(z, bgate, conv_w, hist0, hist1, n_prompt, seq):
    T, Cd = z.shape
    tr = 128
    n_sample = T - n_prompt
    assert n_sample == tr and seq % tr == 0
    npb = n_prompt // tr
    return pl.pallas_call(
        functools.partial(_conv_kernel, npb, seq // tr),
        grid=(T // tr,),
        in_specs=[pl.BlockSpec((tr, Cd), lambda i: (i, 0)),
                  pl.BlockSpec((8, Cd), lambda i: (jnp.maximum(i * (tr // 8) - 1, 0), 0)),
                  pl.BlockSpec((tr, Cd), lambda i: (i, 0)),
                  pl.BlockSpec(conv_w.shape, lambda i: (0, 0)),
                  pl.BlockSpec((tr, Cd), lambda i: (0, 0)),
                  pl.BlockSpec((tr, Cd), lambda i: (0, 0))],
        out_specs=pl.BlockSpec((tr, Cd), lambda i: (i, 0)),
        out_shape=jax.ShapeDtypeStruct((T, Cd), BF16),
        compiler_params=_cparams(1),
        name="short_conv",
    )(z, z, bgate, conv_w, hist0, hist1)


def _router_kernel(x_ref, g_ref, w_ref, idx_ref, gate_ref):
    h = _rms(x_ref[...], g_ref[...])
    logits = jnp.dot(h, w_ref[...], preferred_element_type=F32,
                     precision=lax.Precision.HIGHEST)
    lane = lax.broadcasted_iota(jnp.int32, logits.shape, 1)
    neg = jnp.float32(-jnp.inf)
    logits = jnp.where(lane < N_EXPERTS, logits, neg)
    v1 = jnp.max(logits, axis=-1, keepdims=True)
    i1 = jnp.min(jnp.where(logits == v1, lane, LANES), axis=-1, keepdims=True)
    rest = jnp.where(lane == i1, neg, logits)
    v2 = jnp.max(rest, axis=-1, keepdims=True)
    i2 = jnp.min(jnp.where(rest == v2, lane, LANES), axis=-1, keepdims=True)
    e2 = jnp.exp(v2 - v1)
    g1 = 1.0 / (1.0 + e2)
    g2 = e2 / (1.0 + e2)
    idx_ref[...] = jnp.where(lane == 0, i1, jnp.where(lane == 1, i2, 0))
    gate_ref[...] = jnp.where(lane == 0, g1, jnp.where(lane == 1, g2, 0.0))


def _router(x, g, router_w):
    T, D = x.shape
    tr = _pick_tile(T, 512)
    wpad = jnp.zeros((D, LANES), F32).at[:, :N_EXPERTS].set(router_w)
    return pl.pallas_call(
        _router_kernel,
        grid=(T // tr,),
        in_specs=[pl.BlockSpec((tr, D), lambda i: (i, 0)),
                  pl.BlockSpec((1, D), lambda i: (0, 0)),
                  pl.BlockSpec((D, LANES), lambda i: (0, 0))],
        out_specs=[pl.BlockSpec((tr, LANES), lambda i: (i, 0)),
                   pl.BlockSpec((tr, LANES), lambda i: (i, 0))],
        out_shape=[jax.ShapeDtypeStruct((T, LANES), jnp.int32),
                   jax.ShapeDtypeStruct((T, LANES), F32)],
        compiler_params=_cparams(1),
        name="moe_router",
    )(x, g.reshape(1, D), wpad)


def _dispatch_kernel(tok_ref, used_ref, x_hbm, g_ref, o_ref, buf_ref, sem):
    i = pl.program_id(0)
    n = buf_ref.shape[0]

    def row_copy(r):
        return pltpu.make_async_copy(x_hbm.at[pl.ds(tok_ref[0, 0, r], 1)],
                                     buf_ref.at[pl.ds(r, 1)], sem)

    @pl.when(used_ref[i] > 0)
    def _():
        def start(r, carry):
            row_copy(r).start()
            return carry

        lax.fori_loop(0, n, start, 0)

        def wait(r, carry):
            row_copy(r).wait()
            return carry

        lax.fori_loop(0, n, wait, 0)
        o_ref[...] = _rms(buf_ref[...], g_ref[...]).astype(o_ref.dtype)

    @pl.when(used_ref[i] == 0)
    def _():
        o_ref[...] = jnp.zeros_like(o_ref)


def _dispatch(x, g, slot_tok, sub_used):
    T, D = x.shape
    nsub = sub_used.shape[0]
    SB = MOE_SUB
    return pl.pallas_call(
        _dispatch_kernel,
        grid_spec=pltpu.PrefetchScalarGridSpec(
            num_scalar_prefetch=0,
            grid=(nsub,),
            in_specs=[pl.BlockSpec((1, 1, SB), lambda i: (i, 0, 0), memory_space=pltpu.SMEM),
                      pl.BlockSpec(memory_space=pltpu.SMEM),
                      pl.BlockSpec(memory_space=pl.ANY),
                      pl.BlockSpec((1, D), lambda i: (0, 0))],
            out_specs=pl.BlockSpec((SB, D), lambda i: (i, 0)),
            scratch_shapes=[pltpu.VMEM((SB, D), F32), pltpu.SemaphoreType.DMA(())],
        ),
        out_shape=jax.ShapeDtypeStruct((nsub * SB, D), BF16),
        compiler_params=_cparams(1),
        name="moe_dispatch",
    )(slot_tok.reshape(nsub, 1, SB), sub_used, x, g.reshape(1, D))


def _moe_up_kernel(ce_ref, cs_ref, cr_ref, nu_ref, x_ref, wg_ref, wu_ref, o_ref):
    c = pl.program_id(0)
    SB = MOE_SUB
    nsb = x_ref.shape[0] // SB

    @pl.when(c < nu_ref[0])
    def _():
        wg = wg_ref[0].astype(BF16)
        wu = wu_ref[0].astype(BF16)
        rows = cr_ref[c]
        for sb in range(nsb):
            sl = pl.ds(sb * SB, SB)

            @pl.when(sb * SB < rows)
            def _():
                x = x_ref[sl, :]
                o_ref[sl, :] = (_silu(_dot(x, wg)) * _dot(x, wu)).astype(o_ref.dtype)

            @pl.when(sb * SB >= rows)
            def _():
                o_ref[sl, :] = jnp.zeros((SB, o_ref.shape[1]), o_ref.dtype)

    @pl.when(c >= nu_ref[0])
    def _():
        o_ref[...] = jnp.zeros_like(o_ref)


def _moe_up(xs, w_gu, meta, R, tf):
    ce, cs, cr, nu = meta
    nch = ce.shape[0]
    D = xs.shape[1]
    F = w_gu.shape[2] // 2
    nf = F // tf

    def fidx(c, f, nu):
        return jnp.where(c < nu[0], f, nf - 1)

    return pl.pallas_call(
        _moe_up_kernel,
        grid_spec=pltpu.PrefetchScalarGridSpec(
            num_scalar_prefetch=4,
            grid=(nch, nf),
            in_specs=[pl.BlockSpec((R, D), lambda c, f, ce, cs, cr, nu: (cs[c], 0)),
                      pl.BlockSpec((1, D, tf), lambda c, f, ce, cs, cr, nu: (ce[c], 0, fidx(c, f, nu))),
                      pl.BlockSpec((1, D, tf), lambda c, f, ce, cs, cr, nu: (ce[c], 0, nf + fidx(c, f, nu)))],
            out_specs=pl.BlockSpec((R, tf), lambda c, f, ce, cs, cr, nu: (c, f)),
        ),
        out_shape=jax.ShapeDtypeStruct((nch * R, F), BF16),
        compiler_params=_cparams(2),
        name="moe_gate_up",
    )(ce, cs, cr, nu, xs, w_gu, w_gu)


def _moe_down_kernel(ce_ref, cs_ref, cr_ref, nu_ref, h_ref, w_ref, o_ref, acc_ref):
    c = pl.program_id(0)
    k = pl.program_id(2)
    nk = pl.num_programs(2)
    SB = MOE_SUB
    nsb = h_ref.shape[0] // SB

    @pl.when(c < nu_ref[0])
    def _():
        w = w_ref[0].astype(BF16)
        rows = cr_ref[c]
        for sb in range(nsb):
            sl = pl.ds(sb * SB, SB)

            @pl.when(sb * SB < rows)
            def _():
                part = _dot(h_ref[sl, :], w)

                @pl.when(k == 0)
                def _():
                    acc_ref[sl, :] = part

                @pl.when(k > 0)
                def _():
                    acc_ref[sl, :] += part

            @pl.when(jnp.logical_and(sb * SB >= rows, k == 0))
            def _():
                acc_ref[sl, :] = jnp.zeros((SB, acc_ref.shape[1]), F32)

        @pl.when(k == nk - 1)
        def _():
            o_ref[...] = acc_ref[...]

    @pl.when(jnp.logical_and(c >= nu_ref[0], k == nk - 1))
    def _():
        o_ref[...] = jnp.zeros_like(o_ref)


def _moe_down(hid, w_down, meta, R, tn, tk):
    ce, cs, cr, nu = meta
    nch = ce.shape[0]
    F = hid.shape[1]
    D = w_down.shape[2]
    nn, nk = D // tn, F // tk

    def last_if_unused(c, v, nu, last):
        return jnp.where(c < nu[0], v, last)

    return pl.pallas_call(
        _moe_down_kernel,
        grid_spec=pltpu.PrefetchScalarGridSpec(
            num_scalar_prefetch=4,
            grid=(nch, nn, nk),
            in_specs=[pl.BlockSpec((R, tk), lambda c, n, k, ce, cs, cr, nu:
                                   (cs[c], last_if_unused(c, k, nu, nk - 1))),
                      pl.BlockSpec((1, tk, tn), lambda c, n, k, ce, cs, cr, nu:
                                   (ce[c], last_if_unused(c, k, nu, nk - 1),
                                    last_if_unused(c, n, nu, nn - 1)))],
            out_specs=pl.BlockSpec((R, tn), lambda c, n, k, ce, cs, cr, nu: (c, n)),
            scratch_shapes=[pltpu.VMEM((R, tn), F32)],
        ),
        out_shape=jax.ShapeDtypeStruct((nch * R, D), F32),
        compiler_params=_cparams(3),
        name="moe_down",
    )(ce, cs, cr, nu, hid, w_down)


def _combine_kernel(p1_ref, p2_ref, x_ref, g1_ref, g2_ref, fg_ref, ys_hbm, o_ref,
                    buf_ref, sem):
    n = x_ref.shape[0]

    def copies(r):
        return (pltpu.make_async_copy(ys_hbm.at[pl.ds(p1_ref[0, 0, r], 1)],
                                      buf_ref.at[0, pl.ds(r, 1)], sem),
                pltpu.make_async_copy(ys_hbm.at[pl.ds(p2_ref[0, 0, r], 1)],
                                      buf_ref.at[1, pl.ds(r, 1)], sem))

    def start(r, carry):
        a, b = copies(r)
        a.start()
        b.start()
        return carry

    lax.fori_loop(0, n, start, 0)

    def wait(r, carry):
        a, b = copies(r)
        a.wait()
        b.wait()
        return carry

    lax.fori_loop(0, n, wait, 0)
    y = x_ref[...] + (g1_ref[...] * buf_ref[0] + g2_ref[...] * buf_ref[1])
    o_ref[...] = _rms(y, fg_ref[...])


def _combine(x, ys, pos1, pos2, g1, g2, final_g):
    T, D = x.shape
    tr = 128
    assert T % tr == 0
    nb = T // tr
    smem_spec = pl.BlockSpec((1, 1, tr), lambda i: (i, 0, 0), memory_space=pltpu.SMEM)
    return pl.pallas_call(
        _combine_kernel,
        grid_spec=pltpu.PrefetchScalarGridSpec(
            num_scalar_prefetch=0,
            grid=(nb,),
            in_specs=[smem_spec, smem_spec,
                      pl.BlockSpec((tr, D), lambda i: (i, 0)),
                      pl.BlockSpec((tr, 1), lambda i: (i, 0)),
                      pl.BlockSpec((tr, 1), lambda i: (i, 0)),
                      pl.BlockSpec((1, D), lambda i: (0, 0)),
                      pl.BlockSpec(memory_space=pl.ANY)],
            out_specs=pl.BlockSpec((tr, D), lambda i: (i, 0)),
            scratch_shapes=[pltpu.VMEM((2, tr, D), F32), pltpu.SemaphoreType.DMA(())],
        ),
        out_shape=jax.ShapeDtypeStruct((T, D), F32),
        compiler_params=_cparams(1),
        name="moe_combine_final_norm",
    )(pos1.reshape(nb, 1, tr), pos2.reshape(nb, 1, tr), x, g1, g2,
      final_g.reshape(1, D), ys)


def _moe_plan(idx, T):
    M = T * TOP_K
    R = -(-int(M / N_EXPERTS * 1.1) // MOE_SUB) * MOE_SUB
    nch = -(-M // R) + N_EXPERTS
    flat_e = idx.reshape(M)
    onehot = (flat_e[:, None] == jnp.arange(N_EXPERTS, dtype=jnp.int32)[None, :]).astype(jnp.int32)
    csum = jnp.cumsum(onehot, axis=0)
    rank = jnp.sum((csum - onehot) * onehot, axis=1)
    counts = csum[-1]
    e_chunks = (counts + R - 1) // R
    e_first = jnp.cumsum(e_chunks) - e_chunks
    n_used = jnp.sum(e_chunks).astype(jnp.int32)
    slot = (e_first[flat_e] + rank // R) * R + rank % R
    flat_tok = jnp.arange(M, dtype=jnp.int32) // TOP_K
    slot_tok = jnp.zeros((nch * R,), jnp.int32).at[slot].set(flat_tok)
    cidx = jnp.arange(nch, dtype=jnp.int32)
    c_e = jnp.clip(jnp.searchsorted(jnp.cumsum(e_chunks), cidx, side='right'), 0, N_EXPERTS - 1).astype(jnp.int32)
    c_rows = jnp.clip(counts[c_e] - (cidx - e_first[c_e]) * R, 0, R).astype(jnp.int32)
    used = cidx < n_used
    last = jnp.maximum(n_used - 1, 0)
    c_src = jnp.where(used, cidx, last).astype(jnp.int32)
    c_e = jnp.where(used, c_e, c_e[last]).astype(jnp.int32)
    c_rows = jnp.where(used, c_rows, 0).astype(jnp.int32)
    spc = R // MOE_SUB
    sub = jnp.arange(nch * spc, dtype=jnp.int32)
    sub_used = ((sub % spc) * MOE_SUB < c_rows[sub // spc]).astype(jnp.int32)
    pos = slot.reshape(T, TOP_K).astype(jnp.int32)
    meta = (c_e, c_src, c_rows, n_used.reshape(1))
    return R, slot_tok, sub_used, meta, pos[:, 0], pos[:, 1]


def kernel(x_prompt, x_sample, state_gla, state_conv, ev_norm_mix, ev_w_in, ev_sgu_norm, ev_sgu_w, ev_sgu_b, ev_gla_w_alpha, ev_gla_b_alpha, ev_gla_norm, ev_w_out, ev_norm_ffn, ev_ffn_w_gu, ev_ffn_w_down, od_norm_mix, od_conv_w_in, od_conv_w, od_conv_w_out, od_norm_ffn, od_router_w, od_moe_w_gu, od_moe_w_down, final_norm):
    nb, seq, D = x_prompt.shape
    B = x_sample.shape[0]
    assert x_sample.shape[1] == 1
    Tp = nb * seq
    T = Tp + B
    x0 = jnp.concatenate([x_prompt.reshape(Tp, D), x_sample.reshape(B, D)], axis=0)

    sw = ev_sgu_norm.shape[1]
    kw = ev_gla_w_alpha.shape[2]
    rank = ev_gla_w_alpha.shape[1]
    vw = GLA_HEADS * ev_gla_norm.shape[1]
    n_main = 2 * sw + 2 * kw + 2 * vw
    assert ev_w_in.shape[2] == n_main + rank
    tm = _pick_tile(T, 1664)

    h = _norm_bf16(x0, ev_norm_mix[0])
    P = _matmul(h, ev_w_in[0], n_main, tm, 512, name="mix_in_proj")
    log_a = _log_decay(h, ev_w_in[0][:, n_main:], ev_gla_w_alpha[0], ev_gla_b_alpha[0], tm)
    a_out, v_sample = _sgu(P, Tp, B, sw, ev_sgu_norm[0], ev_sgu_w[0], ev_sgu_b[0])
    col_q = 2 * sw
    o_p, gla_p = _gla_prompt(P, log_a, nb, seq, col_q, kw, vw, ev_gla_norm[0])
    Ps = P[Tp:]
    q_s = Ps[:, col_q:col_q + kw]
    k_s = Ps[:, col_q + kw:col_q + 2 * kw]
    v_s = Ps[:, col_q + 2 * kw:col_q + 2 * kw + vw]
    r_s = Ps[:, col_q + 2 * kw + vw:]
    gla_s, o_s = _gla_sample(state_gla[0], log_a[Tp:], k_s, q_s, v_s, r_s, ev_gla_norm[0])
    o_all = jnp.concatenate([o_p, o_s.reshape(B, vw).astype(BF16)], axis=0)
    mix = jnp.concatenate([a_out, o_all], axis=1)
    x1 = _matmul(mix, ev_w_out[0], D, tm, 512, res=x0, name="mix_out_proj")

    h = _norm_bf16(x1, ev_norm_ffn[0])
    hid = _swiglu_up(h, ev_ffn_w_gu[0], tm, 512)
    x2 = _matmul_kt(hid, ev_ffn_w_down[0], x1, tm, 512, 512)

    h = _norm_bf16(x2, od_norm_mix[0])
    bgate, z = _conv_in(h, od_conv_w_in[0], tm, 256)
    yc = _short_conv(z, bgate, od_conv_w[0], state_conv[0][:, 0], state_conv[0][:, 1], Tp, seq)
    x3 = _matmul(yc, od_conv_w_out[0], D, tm, 512, res=x2, name="conv_out_proj")
    zp = z[:Tp].reshape(nb, seq, D)
    conv_p = zp[:, seq - 2:]
    conv_s = jnp.concatenate([state_conv[0][:, 1:], z[Tp:].reshape(B, 1, D)], axis=1)

    idx, gates = _router(x3, od_norm_ffn[0], od_router_w[0])
    R, slot_tok, sub_used, meta, pos1, pos2 = _moe_plan(idx[:, :TOP_K], T)
    xs = _dispatch(x3, od_norm_ffn[0], slot_tok, sub_used)
    hid = _moe_up(xs, od_moe_w_gu[0], meta, R, 256)
    ys = _moe_down(hid, od_moe_w_down[0], meta, R, 1024, 512)
    y = _combine(x3, ys, pos1, pos2, gates[:, 0:1], gates[:, 1:2], final_norm)

    y_prompt = y[:Tp].reshape(nb, seq, D)
    y_sample = y[Tp:].reshape(B, 1, D)
    return (y_prompt, y_sample, gla_p[None], gla_s[None],
            v_sample.reshape(1, B, 1, sw), conv_p[None], conv_s[None])
```

```python
import functools
import math

import numpy as np
import jax
import jax.numpy as jnp
from jax import lax
from jax.experimental import pallas as pl
from jax.experimental.pallas import tpu as pltpu

F32 = jnp.float32
BF16 = jnp.bfloat16
EPS = 1e-6

SGU_CHUNK = 128
SGU_GROUPS = 4
GLA_HEADS = 4
GLA_TAU = 16.0
GLA_CHUNK = 128
N_EXPERTS = 8
TOP_K = 2
MOE_SUB = 256
MOE_BIG = 1024
MOE_SMALL = 128
LANES = 128
VMEM_LIMIT = 52 * 1024 * 1024


def _cparams(n_axes, vmem=VMEM_LIMIT):
    return pltpu.CompilerParams(dimension_semantics=("arbitrary",) * n_axes,
                                vmem_limit_bytes=vmem)


def _pick_tile(n, cap, mult=16):
    best = None
    for t in range(mult, min(n, cap) + 1, mult):
        if n % t == 0:
            best = t
    assert best is not None, (n, cap, mult)
    return best


def _rms(x, g):
    ms = jnp.mean(x * x, axis=-1, keepdims=True)
    return (x * lax.rsqrt(ms + EPS)) * g


def _gelu(x):
    c = math.sqrt(2.0 / math.pi)
    return x * (0.5 * (1.0 + jnp.tanh(c * (x + 0.044715 * (x * x * x)))))


def _silu(x):
    return x * (1.0 / (1.0 + jnp.exp(-x)))


def _dot(a, b):
    return jnp.dot(a, b, preferred_element_type=F32)


def _dot_nt(a, b):
    return lax.dot_general(a, b, (((1,), (1,)), ((), ())), preferred_element_type=F32)


def _norm_kernel(x_ref, g_ref, o_ref):
    o_ref[...] = _rms(x_ref[...], g_ref[...]).astype(BF16)


def _norm_bf16(x, g):
    T, D = x.shape
    tr = _pick_tile(T, 512)
    return pl.pallas_call(
        _norm_kernel,
        grid=(T // tr,),
        in_specs=[pl.BlockSpec((tr, D), lambda i: (i, 0)),
                  pl.BlockSpec((1, D), lambda i: (0, 0))],
        out_specs=pl.BlockSpec((tr, D), lambda i: (i, 0)),
        out_shape=jax.ShapeDtypeStruct((T, D), BF16),
        compiler_params=_cparams(1),
        name="rmsnorm_bf16",
    )(x, g.reshape(1, D))


def _mm_kernel(x_ref, w_ref, o_ref):
    o_ref[...] = _dot(x_ref[...], w_ref[...].astype(BF16)).astype(o_ref.dtype)


def _mm_res_kernel(x_ref, w_ref, r_ref, o_ref):
    o_ref[...] = r_ref[...] + _dot(x_ref[...], w_ref[...].astype(BF16))


def _matmul(x, w, n_cols, tm, tn, res=None, name="matmul"):
    T, K = x.shape
    grid = (T // tm, n_cols // tn)
    in_specs = [pl.BlockSpec((tm, K), lambda i, j: (i, 0)),
                pl.BlockSpec((K, tn), lambda i, j: (0, j))]
    args = [x, w]
    kern = _mm_kernel
    if res is not None:
        in_specs.append(pl.BlockSpec((tm, tn), lambda i, j: (i, j)))
        args.append(res)
        kern = _mm_res_kernel
    return pl.pallas_call(
        kern, grid=grid, in_specs=in_specs,
        out_specs=pl.BlockSpec((tm, tn), lambda i, j: (i, j)),
        out_shape=jax.ShapeDtypeStruct((T, n_cols), F32),
        compiler_params=_cparams(2),
        name=name,
    )(*args)


def _swiglu_kernel(x_ref, wg_ref, wu_ref, o_ref):
    x = x_ref[...]
    g = _dot(x, wg_ref[...].astype(BF16))
    u = _dot(x, wu_ref[...].astype(BF16))
    o_ref[...] = (_silu(g) * u).astype(o_ref.dtype)


def _swiglu_up(x, w_gu, tm, tn):
    T, K = x.shape
    F = w_gu.shape[1] // 2
    nj = F // tn
    return pl.pallas_call(
        _swiglu_kernel,
        grid=(T // tm, nj),
        in_specs=[pl.BlockSpec((tm, K), lambda i, j: (i, 0)),
                  pl.BlockSpec((K, tn), lambda i, j: (0, j)),
                  pl.BlockSpec((K, tn), lambda i, j: (0, j + nj))],
        out_specs=pl.BlockSpec((tm, tn), lambda i, j: (i, j)),
        out_shape=jax.ShapeDtypeStruct((T, F), BF16),
        compiler_params=_cparams(2),
        name="ffn_gate_up",
    )(x, w_gu, w_gu)


def _conv_in_kernel(x_ref, wb_ref, wc_ref, wu_ref, b_ref, z_ref):
    x = x_ref[...]
    b_ref[...] = _dot(x, wb_ref[...].astype(BF16))
    c = _dot(x, wc_ref[...].astype(BF16))
    u = _dot(x, wu_ref[...].astype(BF16))
    z_ref[...] = c * u


def _conv_in(x, w, tm, tn):
    T, K = x.shape
    C = w.shape[1] // 3
    nj = C // tn
    return pl.pallas_call(
        _conv_in_kernel,
        grid=(T // tm, nj),
        in_specs=[pl.BlockSpec((tm, K), lambda i, j: (i, 0)),
                  pl.BlockSpec((K, tn), lambda i, j: (0, j)),
                  pl.BlockSpec((K, tn), lambda i, j: (0, j + nj)),
                  pl.BlockSpec((K, tn), lambda i, j: (0, j + 2 * nj))],
        out_specs=[pl.BlockSpec((tm, tn), lambda i, j: (i, j)),
                   pl.BlockSpec((tm, tn), lambda i, j: (i, j))],
        out_shape=[jax.ShapeDtypeStruct((T, C), F32),
                   jax.ShapeDtypeStruct((T, C), F32)],
        compiler_params=_cparams(2),
        name="conv_in_proj",
    )(x, w, w, w)


def _loga_kernel(x_ref, wl_ref, wa_ref, b_ref, o_ref):
    g = _dot(x_ref[...], wl_ref[...].astype(BF16))
    z = _dot(g.astype(BF16), wa_ref[...].astype(BF16)) + b_ref[...]
    ls = jnp.minimum(z, 0.0) - jnp.log1p(jnp.exp(-jnp.abs(z)))
    o_ref[...] = ls * (1.0 / GLA_TAU)


def _log_decay(x, w_lr, w_alpha, b_alpha, tm):
    T, K = x.shape
    R, KW = w_alpha.shape
    return pl.pallas_call(
        _loga_kernel,
        grid=(T // tm,),
        in_specs=[pl.BlockSpec((tm, K), lambda i: (i, 0)),
                  pl.BlockSpec((K, R), lambda i: (0, 0)),
                  pl.BlockSpec((R, KW), lambda i: (0, 0)),
                  pl.BlockSpec((1, KW), lambda i: (0, 0))],
        out_specs=pl.BlockSpec((tm, KW), lambda i: (i, 0)),
        out_shape=jax.ShapeDtypeStruct((T, KW), F32),
        compiler_params=_cparams(1),
        name="gla_log_decay",
    )(x, w_lr, w_alpha, b_alpha.reshape(1, KW))


def _sgu_kernel(n_prompt_blocks, u_ref, v_ref, ng_ref, w_ref, bt_ref, o_ref, vs_ref):
    i = pl.program_id(0)
    is_sample = i >= n_prompt_blocks
    C = SGU_CHUNK
    gu = _gelu(u_ref[...])
    vn = _rms(_gelu(v_ref[...]), ng_ref[...])

    @pl.when(is_sample)
    def _():
        vs_ref[...] = vn

    gd = vn.shape[1] // SGU_GROUPS
    row = lax.broadcasted_iota(jnp.int32, (C, C), 0)
    col = lax.broadcasted_iota(jnp.int32, (C, C), 1)
    vb = vn.astype(BF16)
    for g in range(SGU_GROUPS):
        w = w_ref[g]
        w_prompt = jnp.where(col <= row, w, 0.0)
        w_sample = jnp.where(col == row, w_ref[g, 0:1, 0:1], 0.0)
        w_eff = jnp.where(is_sample, w_sample, w_prompt).astype(BF16)
        bias = jnp.where(is_sample, bt_ref[0:1, g:g + 1], bt_ref[:, g:g + 1])
        mixed = _dot(w_eff, vb[:, g * gd:(g + 1) * gd]) + bias
        o_ref[:, g * gd:(g + 1) * gd] = (gu[:, g * gd:(g + 1) * gd] * mixed).astype(o_ref.dtype)


def _sgu(P, n_prompt, n_sample, width, sgu_norm, sgu_w, sgu_b):
    C = SGU_CHUNK
    assert n_prompt % C == 0 and n_sample == C
    T = n_prompt + n_sample
    npb = n_prompt // C
    return pl.pallas_call(
        functools.partial(_sgu_kernel, npb),
        grid=(T // C,),
        in_specs=[pl.BlockSpec((C, width), lambda i: (i, 0)),
                  pl.BlockSpec((C, width), lambda i: (i, 1)),
                  pl.BlockSpec((1, width), lambda i: (0, 0)),
                  pl.BlockSpec((SGU_GROUPS, C, C), lambda i: (0, 0, 0)),
                  pl.BlockSpec((C, SGU_GROUPS), lambda i: (0, 0))],
        out_specs=[pl.BlockSpec((C, width), lambda i: (i, 0)),
                   pl.BlockSpec((C, width), lambda i: (0, 0))],
        out_shape=[jax.ShapeDtypeStruct((T, width), BF16),
                   jax.ShapeDtypeStruct((n_sample, width), F32)],
        compiler_params=_cparams(1),
        name="sgu_mix",
    )(P, P, sgu_norm.reshape(1, width), sgu_w, sgu_b.T)


def _gla_constants():
    C = GLA_CHUNK
    s = np.arange(C)[None, :]
    i = np.arange(C)[:, None]
    gs = [s <= i, s > i]
    masks = []
    b = C // 2
    while b >= 1:
        upper = (i % (2 * b)) >= b
        m = (i // (2 * b)) * (2 * b) + b - 1
        gs.append(np.where(upper, (s > m) & (s <= i), (s > i) & (s <= m)))
        same = (i // (2 * b)) == (s // (2 * b))
        masks.append(upper & same & ~upper.T)
        b //= 2
    masks.append(np.eye(C, dtype=bool))
    G = np.concatenate(gs, axis=0).astype(np.float32)
    M = np.stack(masks).astype(np.float32)
    return jnp.asarray(G, BF16), jnp.asarray(M, F32)


def _gla_kernel(q_ref, k_ref, v_ref, r_ref, la_ref, G_ref, M_ref, gn_ref,
                o_ref, s_out_ref, st_ref):
    c = pl.program_id(1)
    C = GLA_CHUNK
    dk = q_ref.shape[1] // GLA_HEADS
    dv = v_ref.shape[1] // GLA_HEADS
    nlev = M_ref.shape[0] - 1

    @pl.when(c == 0)
    def _():
        st_ref[...] = jnp.zeros_like(st_ref)

    la = la_ref[...]
    hi = la.astype(BF16)
    r1 = la - hi.astype(F32)
    mid = r1.astype(BF16)
    lo = (r1 - mid.astype(F32)).astype(BF16)
    G = G_ref[...]
    X = jnp.exp(_dot(G, hi) + _dot(G, mid) + _dot(G, lo))

    scale = dk ** -0.5
    for h in range(GLA_HEADS):
        Xh = X[:, h * dk:(h + 1) * dk]
        qh = q_ref[:, h * dk:(h + 1) * dk] * scale
        kh = k_ref[:, h * dk:(h + 1) * dk]
        vh = v_ref[:, h * dv:(h + 1) * dv]
        x_cum = Xh[0:C]
        x_rest = Xh[C:2 * C]
        st_old = st_ref[h]
        o = _dot_nt((qh * x_cum).astype(BF16), st_old.astype(BF16))
        qb = qh.astype(BF16)
        kb = kh.astype(BF16)
        att = _dot_nt(qb, kb) * M_ref[nlev]
        for l in range(nlev):
            xl = Xh[(2 + l) * C:(3 + l) * C]
            att = att + _dot_nt((qh * xl).astype(BF16), (kh * xl).astype(BF16)) * M_ref[l]
        vb = vh.astype(BF16)
        o = o + _dot(att.astype(BF16), vb)
        vt = vh.T.astype(BF16)
        st_new = st_old * x_cum[C - 1:C, :] + _dot(vt, (kh * x_rest).astype(BF16))
        st_ref[h] = st_new
        on = _rms(o, gn_ref[...]) * _silu(r_ref[:, h * dv:(h + 1) * dv])
        o_ref[:, h * dv:(h + 1) * dv] = on.astype(o_ref.dtype)

    @pl.when(c == pl.num_programs(1) - 1)
    def _():
        for h in range(GLA_HEADS):
            s_out_ref[0, h] = st_ref[h].T


def _gla_prompt(P, log_a, n_seq, seq, col_q, kw, vw, gla_norm):
    C = GLA_CHUNK
    assert seq % C == 0
    nc = seq // C
    dk = kw // GLA_HEADS
    dv = vw // GLA_HEADS
    qb, kb = col_q // kw, col_q // kw + 1
    vb = (col_q + 2 * kw) // vw
    rb = vb + 1
    assert col_q % kw == 0 and (col_q + 2 * kw) % vw == 0
    G, M = _gla_constants()
    rows = lambda n, c: n * nc + c
    return pl.pallas_call(
        _gla_kernel,
        grid=(n_seq, nc),
        in_specs=[pl.BlockSpec((C, kw), lambda n, c: (rows(n, c), qb)),
                  pl.BlockSpec((C, kw), lambda n, c: (rows(n, c), kb)),
                  pl.BlockSpec((C, vw), lambda n, c: (rows(n, c), vb)),
                  pl.BlockSpec((C, vw), lambda n, c: (rows(n, c), rb)),
                  pl.BlockSpec((C, kw), lambda n, c: (rows(n, c), 0)),
                  pl.BlockSpec(G.shape, lambda n, c: (0, 0)),
                  pl.BlockSpec(M.shape, lambda n, c: (0, 0, 0)),
                  pl.BlockSpec((1, dv), lambda n, c: (0, 0))],
        out_specs=[pl.BlockSpec((C, vw), lambda n, c: (rows(n, c), 0)),
                   pl.BlockSpec((1, GLA_HEADS, dk, dv), lambda n, c: (n, 0, 0, 0))],
        out_shape=[jax.ShapeDtypeStruct((n_seq * seq, vw), BF16),
                   jax.ShapeDtypeStruct((n_seq, GLA_HEADS, dk, dv), F32)],
        scratch_shapes=[pltpu.VMEM((GLA_HEADS, dv, dk), F32)],
        compiler_params=_cparams(2),
        name="gla_prompt",
    )(P, P, P, P, log_a, G, M, gla_norm.reshape(1, dv))


def _gla_step_kernel(s_ref, a_ref, k_ref, q_ref, v_ref, r_ref, gn_ref, sn_ref, o_ref):
    dk = s_ref.shape[2]
    s_new = jnp.exp(a_ref[...]) * s_ref[...] + k_ref[...] * v_ref[...]
    sn_ref[...] = s_new
    o = jnp.sum((q_ref[...] * (dk ** -0.5)) * s_new, axis=2, keepdims=True)
    o_ref[...] = _rms(o, gn_ref[...]) * _silu(r_ref[...])


def _gla_sample(state, la, k, q, v, r, gla_norm):
    B, H, dk, dv = state.shape
    bb = 4
    col = lambda t: t.reshape(B, H, dk, 1)
    rowv = lambda t: t.reshape(B, H, 1, dv)
    cspec = pl.BlockSpec((bb, H, dk, 1), lambda i: (i, 0, 0, 0))
    rspec = pl.BlockSpec((bb, H, 1, dv), lambda i: (i, 0, 0, 0))
    sspec = pl.BlockSpec((bb, H, dk, dv), lambda i: (i, 0, 0, 0))
    return pl.pallas_call(
        _gla_step_kernel,
        grid=(B // bb,),
        in_specs=[sspec, cspec, cspec, cspec, rspec, rspec,
                  pl.BlockSpec((1, 1, 1, dv), lambda i: (0, 0, 0, 0))],
        out_specs=[sspec, rspec],
        out_shape=[jax.ShapeDtypeStruct((B, H, dk, dv), F32),
                   jax.ShapeDtypeStruct((B, H, 1, dv), F32)],
        compiler_params=_cparams(1),
        name="gla_sample_step",
    )(state, col(la), col(k), col(q), rowv(v), rowv(r), gla_norm.reshape(1, 1, 1, dv))


def _conv_kernel(n_prompt_blocks, blocks_per_seq, z_ref, halo_ref, b_ref, w_ref,
                 h0_ref, h1_ref, o_ref):
    i = pl.program_id(0)
    z = z_ref[...]
    w0, w1, w2 = w_ref[0:1, :], w_ref[1:2, :], w_ref[2:3, :]

    @pl.when(i < n_prompt_blocks)
    def _():
        row = lax.broadcasted_iota(jnp.int32, z.shape, 0)
        halo = jnp.where(i % blocks_per_seq == 0, 0.0, halo_ref[...])
        z1 = jnp.where(row == 0, halo[7:8, :], pltpu.roll(z, 1, 0))
        z2 = pltpu.roll(z, 2, 0)
        z2 = jnp.where(row == 0, halo[6:7, :], jnp.where(row == 1, halo[7:8, :], z2))
        o_ref[...] = (b_ref[...] * (w0 * z2 + w1 * z1 + w2 * z)).astype(o_ref.dtype)

    @pl.when(i >= n_prompt_blocks)
    def _():
        conv = w0 * h0_ref[...] + w1 * h1_ref[...] + w2 * z
        o_ref[...] = (b_ref[...] * conv).astype(o_ref.dtype)


def _short_conv(z, bgate, conv_w, hist0, hist1, n_prompt, seq):
    T, Cd = z.shape
    tr = 128
    n_sample = T - n_prompt
    assert n_sample == tr and seq % tr == 0
    npb = n_prompt // tr
    return pl.pallas_call(
        functools.partial(_conv_kernel, npb, seq // tr),
        grid=(T // tr,),
        in_specs=[pl.BlockSpec((tr, Cd), lambda i: (i, 0)),
                  pl.BlockSpec((8, Cd), lambda i: (jnp.maximum(i * (tr // 8) - 1, 0), 0)),
                  pl.BlockSpec((tr, Cd), lambda i: (i, 0)),
                  pl.BlockSpec(conv_w.shape, lambda i: (0, 0)),
                  pl.BlockSpec((tr, Cd), lambda i: (0, 0)),
                  pl.BlockSpec((tr, Cd), lambda i: (0, 0))],
        out_specs=pl.BlockSpec((tr, Cd), lambda i: (i, 0)),
        out_shape=jax.ShapeDtypeStruct((T, Cd), BF16),
        compiler_params=_cparams(1),
        name="short_conv",
    )(z, z, bgate, conv_w, hist0, hist1)


def _router_kernel(x_ref, g_ref, w_ref, idx_ref, gate_ref):
    h = _rms(x_ref[...], g_ref[...])
    logits = jnp.dot(h, w_ref[...], preferred_element_type=F32,
                     precision=lax.Precision.HIGHEST)
    lane = lax.broadcasted_iota(jnp.int32, logits.shape, 1)
    neg = jnp.float32(-jnp.inf)
    logits = jnp.where(lane < N_EXPERTS, logits, neg)
    v1 = jnp.max(logits, axis=-1, keepdims=True)
    i1 = jnp.min(jnp.where(logits == v1, lane, LANES), axis=-1, keepdims=True)
    rest = jnp.where(lane == i1, neg, logits)
    v2 = jnp.max(rest, axis=-1, keepdims=True)
    i2 = jnp.min(jnp.where(rest == v2, lane, LANES), axis=-1, keepdims=True)
    e2 = jnp.exp(v2 - v1)
    g1 = 1.0 / (1.0 + e2)
    g2 = e2 / (1.0 + e2)
    idx_ref[...] = jnp.where(lane == 0, i1, jnp.where(lane == 1, i2, 0))
    gate_ref[...] = jnp.where(lane == 0, g1, jnp.where(lane == 1, g2, 0.0))


def _router(x, g, router_w):
    T, D = x.shape
    tr = _pick_tile(T, 512)
    wpad = jnp.zeros((D, LANES), F32).at[:, :N_EXPERTS].set(router_w)
    return pl.pallas_call(
        _router_kernel,
        grid=(T // tr,),
        in_specs=[pl.BlockSpec((tr, D), lambda i: (i, 0)),
                  pl.BlockSpec((1, D), lambda i: (0, 0)),
                  pl.BlockSpec((D, LANES), lambda i: (0, 0))],
        out_specs=[pl.BlockSpec((tr, LANES), lambda i: (i, 0)),
                   pl.BlockSpec((tr, LANES), lambda i: (i, 0))],
        out_shape=[jax.ShapeDtypeStruct((T, LANES), jnp.int32),
                   jax.ShapeDtypeStruct((T, LANES), F32)],
        compiler_params=_cparams(1),
        name="moe_router",
    )(x, g.reshape(1, D), wpad)


def _dispatch_kernel(tok_ref, nxt_ref, used_ref, x_hbm, g_ref, o_ref, buf_ref, sem):
    i = pl.program_id(0)
    nsteps = pl.num_programs(0)
    n = buf_ref.shape[1]
    slot = i % 2

    def row_copy(t_ref, r, s):
        return pltpu.make_async_copy(x_hbm.at[pl.ds(t_ref[0, 0, r], 1)],
                                     buf_ref.at[s, pl.ds(r, 1)], sem.at[s])

    def start_rows(t_ref, s):
        def body(r, carry):
            row_copy(t_ref, r, s).start()
            return carry

        lax.fori_loop(0, n, body, 0)

    @pl.when(jnp.logical_and(i == 0, used_ref[0] > 0))
    def _():
        start_rows(tok_ref, 0)

    @pl.when(jnp.logical_and(i + 1 < nsteps, used_ref[jnp.minimum(i + 1, nsteps - 1)] > 0))
    def _():
        start_rows(nxt_ref, 1 - slot)

    @pl.when(used_ref[i] > 0)
    def _():
        def wait(r, carry):
            row_copy(tok_ref, r, slot).wait()
            return carry

        lax.fori_loop(0, n, wait, 0)
        o_ref[...] = _rms(buf_ref[slot], g_ref[...]).astype(o_ref.dtype)

    @pl.when(used_ref[i] == 0)
    def _():
        o_ref[...] = jnp.zeros_like(o_ref)


def _dispatch(x, g, slot_tok, sub_used):
    T, D = x.shape
    nsub = sub_used.shape[0]
    SB = MOE_SUB
    tok3 = slot_tok.reshape(nsub, 1, SB)
    return pl.pallas_call(
        _dispatch_kernel,
        grid_spec=pltpu.PrefetchScalarGridSpec(
            num_scalar_prefetch=0,
            grid=(nsub,),
            in_specs=[pl.BlockSpec((1, 1, SB), lambda i: (i, 0, 0), memory_space=pltpu.SMEM),
                      pl.BlockSpec((1, 1, SB), lambda i: (jnp.minimum(i + 1, nsub - 1), 0, 0),
                                   memory_space=pltpu.SMEM),
                      pl.BlockSpec(memory_space=pltpu.SMEM),
                      pl.BlockSpec(memory_space=pl.ANY),
                      pl.BlockSpec((1, D), lambda i: (0, 0))],
            out_specs=pl.BlockSpec((SB, D), lambda i: (i, 0)),
            scratch_shapes=[pltpu.VMEM((2, SB, D), F32), pltpu.SemaphoreType.DMA((2,))],
        ),
        out_shape=jax.ShapeDtypeStruct((nsub * SB, D), BF16),
        compiler_params=_cparams(1),
        name="moe_dispatch",
    )(tok3, tok3, sub_used, x, g.reshape(1, D))


def _row_pieces(rows, total, compute, skip):
    for s in range(0, total, MOE_BIG):
        size = min(MOE_BIG, total - s)
        whole = rows > s + size - MOE_SMALL

        @pl.when(whole)
        def _():
            compute(s, size)

        @pl.when(jnp.logical_not(whole))
        def _():
            for t in range(s, s + size, MOE_SMALL):
                @pl.when(rows > t)
                def _():
                    compute(t, MOE_SMALL)

                @pl.when(rows <= t)
                def _():
                    skip(t, MOE_SMALL)


def _moe_up_kernel(ce_ref, cs_ref, cr_ref, nu_ref, x_ref, wg_ref, wu_ref, o_ref):
    c = pl.program_id(0)

    @pl.when(c < nu_ref[0])
    def _():
        wg = wg_ref[0].astype(BF16)
        wu = wu_ref[0].astype(BF16)

        def compute(s, n):
            x = x_ref[pl.ds(s, n), :]
            o_ref[pl.ds(s, n), :] = (_silu(_dot(x, wg)) * _dot(x, wu)).astype(o_ref.dtype)

        def skip(s, n):
            o_ref[pl.ds(s, n), :] = jnp.zeros((n, o_ref.shape[1]), o_ref.dtype)

        _row_pieces(cr_ref[c], x_ref.shape[0], compute, skip)

    @pl.when(c >= nu_ref[0])
    def _():
        o_ref[...] = jnp.zeros_like(o_ref)


def _moe_up(xs, w_gu, meta, R, tf):
    ce, cs, cr, nu = meta
    nch = ce.shape[0]
    D = xs.shape[1]
    F = w_gu.shape[2] // 2
    nf = F // tf

    def fidx(c, f, nu):
        return jnp.where(c < nu[0], f, nf - 1)

    return pl.pallas_call(
        _moe_up_kernel,
        grid_spec=pltpu.PrefetchScalarGridSpec(
            num_scalar_prefetch=4,
            grid=(nch, nf),
            in_specs=[pl.BlockSpec((R, D), lambda c, f, ce, cs, cr, nu: (cs[c], 0)),
                      pl.BlockSpec((1, D, tf), lambda c, f, ce, cs, cr, nu: (ce[c], 0, fidx(c, f, nu))),
                      pl.BlockSpec((1, D, tf), lambda c, f, ce, cs, cr, nu: (ce[c], 0, nf + fidx(c, f, nu)))],
            out_specs=pl.BlockSpec((R, tf), lambda c, f, ce, cs, cr, nu: (c, f)),
        ),
        out_shape=jax.ShapeDtypeStruct((nch * R, F), BF16),
        compiler_params=_cparams(2),
        name="moe_gate_up",
    )(ce, cs, cr, nu, xs, w_gu, w_gu)


def _moe_down_kernel(ce_ref, cs_ref, cr_ref, nu_ref, h_ref, w_ref, o_ref):
    c = pl.program_id(0)
    k = pl.program_id(2)

    @pl.when(c < nu_ref[0])
    def _():
        w = w_ref[0].astype(BF16)

        def compute(s, n):
            part = _dot(h_ref[pl.ds(s, n), :], w)

            @pl.when(k == 0)
            def _():
                o_ref[pl.ds(s, n), :] = part

            @pl.when(k > 0)
            def _():
                o_ref[pl.ds(s, n), :] += part

        def skip(s, n):
            @pl.when(k == 0)
            def _():
                o_ref[pl.ds(s, n), :] = jnp.zeros((n, o_ref.shape[1]), F32)

        _row_pieces(cr_ref[c], h_ref.shape[0], compute, skip)

    @pl.when(jnp.logical_and(c >= nu_ref[0], k == 0))
    def _():
        o_ref[...] = jnp.zeros_like(o_ref)


def _moe_down(hid, w_down, meta, R, tn, tk):
    ce, cs, cr, nu = meta
    nch = ce.shape[0]
    F = hid.shape[1]
    D = w_down.shape[2]
    nn, nk = D // tn, F // tk

    def last_if_unused(c, v, nu, last):
        return jnp.where(c < nu[0], v, last)

    return pl.pallas_call(
        _moe_down_kernel,
        grid_spec=pltpu.PrefetchScalarGridSpec(
            num_scalar_prefetch=4,
            grid=(nch, nn, nk),
            in_specs=[pl.BlockSpec((R, tk), lambda c, n, k, ce, cs, cr, nu:
                                   (cs[c], last_if_unused(c, k, nu, nk - 1))),
                      pl.BlockSpec((1, tk, tn), lambda c, n, k, ce, cs, cr, nu:
                                   (ce[c], last_if_unused(c, k, nu, nk - 1),
                                    last_if_unused(c, n, nu, nn - 1)))],
            out_specs=pl.BlockSpec((R, tn), lambda c, n, k, ce, cs, cr, nu: (c, n)),
        ),
        out_shape=jax.ShapeDtypeStruct((nch * R, D), F32),
        compiler_params=_cparams(3),
        name="moe_down",
    )(ce, cs, cr, nu, hid, w_down)


def _combine_kernel(p1_ref, p2_ref, n1_ref, n2_ref, x_ref, g1_ref, g2_ref, fg_ref, ys_hbm,
                    o_ref, buf_ref, sem):
    i = pl.program_id(0)
    nsteps = pl.num_programs(0)
    n = x_ref.shape[0]
    slot = i % 2

    def copies(a_ref, b_ref, r, s):
        return (pltpu.make_async_copy(ys_hbm.at[pl.ds(a_ref[0, 0, r], 1)],
                                      buf_ref.at[s, 0, pl.ds(r, 1)], sem.at[s]),
                pltpu.make_async_copy(ys_hbm.at[pl.ds(b_ref[0, 0, r], 1)],
                                      buf_ref.at[s, 1, pl.ds(r, 1)], sem.at[s]))

    def start_rows(a_ref, b_ref, s):
        def body(r, carry):
            ca, cb = copies(a_ref, b_ref, r, s)
            ca.start()
            cb.start()
            return carry

        lax.fori_loop(0, n, body, 0)

    @pl.when(i == 0)
    def _():
        start_rows(p1_ref, p2_ref, 0)

    @pl.when(i + 1 < nsteps)
    def _():
        start_rows(n1_ref, n2_ref, 1 - slot)

    def wait(r, carry):
        ca, cb = copies(p1_ref, p2_ref, r, slot)
        ca.wait()
        cb.wait()
        return carry

    lax.fori_loop(0, n, wait, 0)
    y = x_ref[...] + (g1_ref[...] * buf_ref[slot, 0] + g2_ref[...] * buf_ref[slot, 1])
    o_ref[...] = _rms(y, fg_ref[...])


def _combine(x, ys, pos1, pos2, g1, g2, final_g):
    T, D = x.shape
    tr = 128
    assert T % tr == 0
    nb = T // tr
    cur = pl.BlockSpec((1, 1, tr), lambda i: (i, 0, 0), memory_space=pltpu.SMEM)
    nxt = pl.BlockSpec((1, 1, tr), lambda i: (jnp.minimum(i + 1, nb - 1), 0, 0),
                       memory_space=pltpu.SMEM)
    p1 = pos1.reshape(nb, 1, tr)
    p2 = pos2.reshape(nb, 1, tr)
    return pl.pallas_call(
        _combine_kernel,
        grid_spec=pltpu.PrefetchScalarGridSpec(
            num_scalar_prefetch=0,
            grid=(nb,),
            in_specs=[cur, cur, nxt, nxt,
                      pl.BlockSpec((tr, D), lambda i: (i, 0)),
                      pl.BlockSpec((tr, 1), lambda i: (i, 0)),
                      pl.BlockSpec((tr, 1), lambda i: (i, 0)),
                      pl.BlockSpec((1, D), lambda i: (0, 0)),
                      pl.BlockSpec(memory_space=pl.ANY)],
            out_specs=pl.BlockSpec((tr, D), lambda i: (i, 0)),
            scratch_shapes=[pltpu.VMEM((2, 2, tr, D), F32), pltpu.SemaphoreType.DMA((2,))],
        ),
        out_shape=jax.ShapeDtypeStruct((T, D), F32),
        compiler_params=_cparams(1),
        name="moe_combine_final_norm",
    )(p1, p2, p1, p2, x, g1, g2, final_g.reshape(1, D), ys)


def _moe_plan(idx, T):
    M = T * TOP_K
    R = -(-int(M / N_EXPERTS * 1.1) // MOE_SUB) * MOE_SUB
    nch = -(-M // R) + N_EXPERTS
    flat_e = idx.reshape(M)
    onehot = (flat_e[:, None] == jnp.arange(N_EXPERTS, dtype=jnp.int32)[None, :]).astype(jnp.int32)
    csum = jnp.cumsum(onehot, axis=0)
    rank = jnp.sum((csum - onehot) * onehot, axis=1)
    counts = csum[-1]
    e_chunks = (counts + R - 1) // R
    e_first = jnp.cumsum(e_chunks) - e_chunks
    n_used = jnp.sum(e_chunks).astype(jnp.int32)
    slot = (e_first[flat_e] + rank // R) * R + rank % R
    flat_tok = jnp.arange(M, dtype=jnp.int32) // TOP_K
    slot_tok = jnp.zeros((nch * R,), jnp.int32).at[slot].set(flat_tok)
    cidx = jnp.arange(nch, dtype=jnp.int32)
    c_e = jnp.clip(jnp.searchsorted(jnp.cumsum(e_chunks), cidx, side='right'), 0, N_EXPERTS - 1).astype(jnp.int32)
    c_rows = jnp.clip(counts[c_e] - (cidx - e_first[c_e]) * R, 0, R).astype(jnp.int32)
    used = cidx < n_used
    last = jnp.maximum(n_used - 1, 0)
    c_src = jnp.where(used, cidx, last).astype(jnp.int32)
    c_e = jnp.where(used, c_e, c_e[last]).astype(jnp.int32)
    c_rows = jnp.where(used, c_rows, 0).astype(jnp.int32)
    spc = R // MOE_SUB
    sub = jnp.arange(nch * spc, dtype=jnp.int32)
    sub_used = ((sub % spc) * MOE_SUB < c_rows[sub // spc]).astype(jnp.int32)
    pos = slot.reshape(T, TOP_K).astype(jnp.int32)
    meta = (c_e, c_src, c_rows, n_used.reshape(1))
    return R, slot_tok, sub_used, meta, pos[:, 0], pos[:, 1]


def kernel(x_prompt, x_sample, state_gla, state_conv, ev_norm_mix, ev_w_in, ev_sgu_norm, ev_sgu_w, ev_sgu_b, ev_gla_w_alpha, ev_gla_b_alpha, ev_gla_norm, ev_w_out, ev_norm_ffn, ev_ffn_w_gu, ev_ffn_w_down, od_norm_mix, od_conv_w_in, od_conv_w, od_conv_w_out, od_norm_ffn, od_router_w, od_moe_w_gu, od_moe_w_down, final_norm):
    nb, seq, D = x_prompt.shape
    B = x_sample.shape[0]
    assert x_sample.shape[1] == 1
    Tp = nb * seq
    T = Tp + B
    x0 = jnp.concatenate([x_prompt.reshape(Tp, D), x_sample.reshape(B, D)], axis=0)

    sw = ev_sgu_norm.shape[1]
    kw = ev_gla_w_alpha.shape[2]
    rank = ev_gla_w_alpha.shape[1]
    vw = GLA_HEADS * ev_gla_norm.shape[1]
    n_main = 2 * sw + 2 * kw + 2 * vw
    assert ev_w_in.shape[2] == n_main + rank
    tm = _pick_tile(T, 1664)

    h = _norm_bf16(x0, ev_norm_mix[0])
    P = _matmul(h, ev_w_in[0], n_main, tm, 512, name="mix_in_proj")
    log_a = _log_decay(h, ev_w_in[0][:, n_main:], ev_gla_w_alpha[0], ev_gla_b_alpha[0], tm)
    a_out, v_sample = _sgu(P, Tp, B, sw, ev_sgu_norm[0], ev_sgu_w[0], ev_sgu_b[0])
    col_q = 2 * sw
    o_p, gla_p = _gla_prompt(P, log_a, nb, seq, col_q, kw, vw, ev_gla_norm[0])
    Ps = P[Tp:]
    q_s = Ps[:, col_q:col_q + kw]
    k_s = Ps[:, col_q + kw:col_q + 2 * kw]
    v_s = Ps[:, col_q + 2 * kw:col_q + 2 * kw + vw]
    r_s = Ps[:, col_q + 2 * kw + vw:]
    gla_s, o_s = _gla_sample(state_gla[0], log_a[Tp:], k_s, q_s, v_s, r_s, ev_gla_norm[0])
    o_all = jnp.concatenate([o_p, o_s.reshape(B, vw).astype(BF16)], axis=0)
    mix = jnp.concatenate([a_out, o_all], axis=1)
    x1 = _matmul(mix, ev_w_out[0], D, tm, 512, res=x0, name="mix_out_proj")

    h = _norm_bf16(x1, ev_norm_ffn[0])
    hid = _swiglu_up(h, ev_ffn_w_gu[0], tm, 512)
    x2 = _matmul(hid, ev_ffn_w_down[0], D, _pick_tile(T, 832), 256, res=x1, name="ffn_down")

    h = _norm_bf16(x2, od_norm_mix[0])
    bgate, z = _conv_in(h, od_conv_w_in[0], tm, 256)
    yc = _short_conv(z, bgate, od_conv_w[0], state_conv[0][:, 0], state_conv[0][:, 1], Tp, seq)
    x3 = _matmul(yc, od_conv_w_out[0], D, tm, 512, res=x2, name="conv_out_proj")
    zp = z[:Tp].reshape(nb, seq, D)
    conv_p = zp[:, seq - 2:]
    conv_s = jnp.concatenate([state_conv[0][:, 1:], z[Tp:].reshape(B, 1, D)], axis=1)

    idx, gates = _router(x3, od_norm_ffn[0], od_router_w[0])
    R, slot_tok, sub_used, meta, pos1, pos2 = _moe_plan(idx[:, :TOP_K], T)
    xs = _dispatch(x3, od_norm_ffn[0], slot_tok, sub_used)
    FE = od_moe_w_down.shape[2]
    hid = _moe_up(xs, od_moe_w_gu[0], meta, R, _pick_tile(FE, 512, LANES))
    ys = _moe_down(hid, od_moe_w_down[0], meta, R, 1024, _pick_tile(FE, 1024, LANES))
    y = _combine(x3, ys, pos1, pos2, gates[:, 0:1], gates[:, 1:2], final_norm)

    y_prompt = y[:Tp].reshape(nb, seq, D)
    y_sample = y[Tp:].reshape(B, 1, D)
    return (y_prompt, y_sample, gla_p[None], gla_s[None],
            v_sample.reshape(1, B, 1, sw), conv_p[None], conv_s[None])
```

```python
import functools
import math

import numpy as np
import jax
import jax.numpy as jnp
from jax import lax
from jax.experimental import pallas as pl
from jax.experimental.pallas import tpu as pltpu

F32 = jnp.float32
BF16 = jnp.bfloat16
EPS = 1e-6

SGU_CHUNK = 128
SGU_GROUPS = 4
GLA_HEADS = 4
GLA_TAU = 16.0
GLA_CHUNK = 128
N_EXPERTS = 8
TOP_K = 2
MOE_SUB = 256
MOE_BIG = 1024
MOE_SMALL = 128
LANES = 128
VMEM_LIMIT = 52 * 1024 * 1024


def _cparams(n_axes, vmem=VMEM_LIMIT):
    return pltpu.CompilerParams(dimension_semantics=("arbitrary",) * n_axes,
                                vmem_limit_bytes=vmem)


def _pick_tile(n, cap, mult=16):
    best = None
    for t in range(mult, min(n, cap) + 1, mult):
        if n % t == 0:
            best = t
    assert best is not None, (n, cap, mult)
    return best


def _rms(x, g):
    ms = jnp.mean(x * x, axis=-1, keepdims=True)
    return (x * lax.rsqrt(ms + EPS)) * g


def _gelu(x):
    c = math.sqrt(2.0 / math.pi)
    return x * (0.5 * (1.0 + jnp.tanh(c * (x + 0.044715 * (x * x * x)))))


def _silu(x):
    return x * (1.0 / (1.0 + jnp.exp(-x)))


def _dot(a, b):
    return jnp.dot(a, b, preferred_element_type=F32)


def _dot_nt(a, b):
    return lax.dot_general(a, b, (((1,), (1,)), ((), ())), preferred_element_type=F32)


def _norm_kernel(x_ref, g_ref, o_ref):
    o_ref[...] = _rms(x_ref[...], g_ref[...]).astype(BF16)


def _norm_bf16(x, g):
    T, D = x.shape
    tr = _pick_tile(T, 512)
    return pl.pallas_call(
        _norm_kernel,
        grid=(T // tr,),
        in_specs=[pl.BlockSpec((tr, D), lambda i: (i, 0)),
                  pl.BlockSpec((1, D), lambda i: (0, 0))],
        out_specs=pl.BlockSpec((tr, D), lambda i: (i, 0)),
        out_shape=jax.ShapeDtypeStruct((T, D), BF16),
        compiler_params=_cparams(1),
        name="rmsnorm_bf16",
    )(x, g.reshape(1, D))


def _norm2_kernel(n_prompt_blocks, xp_ref, xs_ref, g_ref, o_ref):
    i = pl.program_id(0)

    @pl.when(i < n_prompt_blocks)
    def _():
        o_ref[...] = _rms(xp_ref[...], g_ref[...]).astype(BF16)

    @pl.when(i >= n_prompt_blocks)
    def _():
        o_ref[...] = _rms(xs_ref[...], g_ref[...]).astype(BF16)


def _norm_bf16_stacked(xp, xs, g):
    Tp, D = xp.shape
    B = xs.shape[0]
    tr = B
    assert Tp % tr == 0
    npb = Tp // tr
    return pl.pallas_call(
        functools.partial(_norm2_kernel, npb),
        grid=(npb + 1,),
        in_specs=[pl.BlockSpec((tr, D), lambda i: (jnp.minimum(i, npb - 1), 0)),
                  pl.BlockSpec((tr, D), lambda i: (0, 0)),
                  pl.BlockSpec((1, D), lambda i: (0, 0))],
        out_specs=pl.BlockSpec((tr, D), lambda i: (i, 0)),
        out_shape=jax.ShapeDtypeStruct((Tp + B, D), BF16),
        compiler_params=_cparams(1),
        name="rmsnorm_stack_bf16",
    )(xp, xs, g.reshape(1, D))


def _mm_kernel(x_ref, w_ref, o_ref):
    o_ref[...] = _dot(x_ref[...], w_ref[...].astype(BF16)).astype(o_ref.dtype)


def _mm_res_kernel(x_ref, w_ref, r_ref, o_ref):
    o_ref[...] = r_ref[...] + _dot(x_ref[...], w_ref[...].astype(BF16))


def _matmul(x, w, n_cols, tm, tn, res=None, name="matmul"):
    T, K = x.shape
    grid = (T // tm, n_cols // tn)
    in_specs = [pl.BlockSpec((tm, K), lambda i, j: (i, 0)),
                pl.BlockSpec((K, tn), lambda i, j: (0, j))]
    args = [x, w]
    kern = _mm_kernel
    if res is not None:
        in_specs.append(pl.BlockSpec((tm, tn), lambda i, j: (i, j)))
        args.append(res)
        kern = _mm_res_kernel
    return pl.pallas_call(
        kern, grid=grid, in_specs=in_specs,
        out_specs=pl.BlockSpec((tm, tn), lambda i, j: (i, j)),
        out_shape=jax.ShapeDtypeStruct((T, n_cols), F32),
        compiler_params=_cparams(2),
        name=name,
    )(*args)


def _mix_out_kernel(n_full, n_tail, a_ref, op_ref, os_ref, rp_ref, rs_ref, w1_ref, w2_ref, o_ref):
    i = pl.program_id(0)
    w1 = w1_ref[...].astype(BF16)
    w2 = w2_ref[...].astype(BF16)

    @pl.when(i < n_full)
    def _():
        o_ref[...] = rp_ref[...] + _dot(a_ref[...], w1) + _dot(op_ref[...], w2)

    @pl.when(i >= n_full)
    def _():
        o_ref[0:n_tail, :] = (rs_ref[...] + _dot(a_ref[0:n_tail, :], w1) + _dot(os_ref[...], w2))


def _mix_out(a, o_p, o_s, res_p, res_s, w, tn):
    T, sw = a.shape
    Tp, vw = o_p.shape
    B = o_s.shape[0]
    D = w.shape[1]
    assert sw == vw and T == Tp + B
    tm = _pick_tile(Tp, 1024)
    assert B <= tm
    n_full = Tp // tm
    clamp = lambda i: jnp.minimum(i, n_full - 1)
    return pl.pallas_call(
        functools.partial(_mix_out_kernel, n_full, B),
        grid=(n_full + 1, D // tn),
        in_specs=[pl.BlockSpec((tm, sw), lambda i, j: (i, 0)),
                  pl.BlockSpec((tm, vw), lambda i, j: (clamp(i), 0)),
                  pl.BlockSpec((B, vw), lambda i, j: (0, 0)),
                  pl.BlockSpec((tm, tn), lambda i, j: (clamp(i), j)),
                  pl.BlockSpec((B, tn), lambda i, j: (0, j)),
                  pl.BlockSpec((sw, tn), lambda i, j: (0, j)),
                  pl.BlockSpec((vw, tn), lambda i, j: (1, j))],
        out_specs=pl.BlockSpec((tm, tn), lambda i, j: (i, j)),
        out_shape=jax.ShapeDtypeStruct((T, D), F32),
        compiler_params=_cparams(2),
        name="mix_out_proj",
    )(a, o_p, o_s, res_p, res_s, w, w)


def _swiglu_kernel(x_ref, wg_ref, wu_ref, o_ref):
    x = x_ref[...]
    g = _dot(x, wg_ref[...].astype(BF16))
    u = _dot(x, wu_ref[...].astype(BF16))
    o_ref[...] = (_silu(g) * u).astype(o_ref.dtype)


def _swiglu_up(x, w_gu, tm, tn):
    T, K = x.shape
    F = w_gu.shape[1] // 2
    nj = F // tn
    return pl.pallas_call(
        _swiglu_kernel,
        grid=(T // tm, nj),
        in_specs=[pl.BlockSpec((tm, K), lambda i, j: (i, 0)),
                  pl.BlockSpec((K, tn), lambda i, j: (0, j)),
                  pl.BlockSpec((K, tn), lambda i, j: (0, j + nj))],
        out_specs=pl.BlockSpec((tm, tn), lambda i, j: (i, j)),
        out_shape=jax.ShapeDtypeStruct((T, F), BF16),
        compiler_params=_cparams(2),
        name="ffn_gate_up",
    )(x, w_gu, w_gu)


def _conv_in_kernel(x_ref, wb_ref, wc_ref, wu_ref, b_ref, z_ref):
    x = x_ref[...]
    b_ref[...] = _dot(x, wb_ref[...].astype(BF16))
    c = _dot(x, wc_ref[...].astype(BF16))
    u = _dot(x, wu_ref[...].astype(BF16))
    z_ref[...] = c * u


def _conv_in(x, w, tm, tn):
    T, K = x.shape
    C = w.shape[1] // 3
    nj = C // tn
    return pl.pallas_call(
        _conv_in_kernel,
        grid=(T // tm, nj),
        in_specs=[pl.BlockSpec((tm, K), lambda i, j: (i, 0)),
                  pl.BlockSpec((K, tn), lambda i, j: (0, j)),
                  pl.BlockSpec((K, tn), lambda i, j: (0, j + nj)),
                  pl.BlockSpec((K, tn), lambda i, j: (0, j + 2 * nj))],
        out_specs=[pl.BlockSpec((tm, tn), lambda i, j: (i, j)),
                   pl.BlockSpec((tm, tn), lambda i, j: (i, j))],
        out_shape=[jax.ShapeDtypeStruct((T, C), F32),
                   jax.ShapeDtypeStruct((T, C), F32)],
        compiler_params=_cparams(2),
        name="conv_in_proj",
    )(x, w, w, w)


def _loga_kernel(x_ref, wl_ref, wa_ref, b_ref, o_ref):
    g = _dot(x_ref[...], wl_ref[...].astype(BF16))
    z = _dot(g.astype(BF16), wa_ref[...].astype(BF16)) + b_ref[...]
    ls = jnp.minimum(z, 0.0) - jnp.log1p(jnp.exp(-jnp.abs(z)))
    o_ref[...] = ls * (1.0 / GLA_TAU)


def _log_decay(x, w_lr, w_alpha, b_alpha, tm):
    T, K = x.shape
    R, KW = w_alpha.shape
    return pl.pallas_call(
        _loga_kernel,
        grid=(T // tm,),
        in_specs=[pl.BlockSpec((tm, K), lambda i: (i, 0)),
                  pl.BlockSpec((K, R), lambda i: (0, 0)),
                  pl.BlockSpec((R, KW), lambda i: (0, 0)),
                  pl.BlockSpec((1, KW), lambda i: (0, 0))],
        out_specs=pl.BlockSpec((tm, KW), lambda i: (i, 0)),
        out_shape=jax.ShapeDtypeStruct((T, KW), F32),
        compiler_params=_cparams(1),
        name="gla_log_decay",
    )(x, w_lr, w_alpha, b_alpha.reshape(1, KW))


def _sgu_kernel(n_prompt_blocks, u_ref, v_ref, ng_ref, w_ref, bt_ref, o_ref, vs_ref):
    i = pl.program_id(0)
    is_sample = i >= n_prompt_blocks
    C = SGU_CHUNK
    gu = _gelu(u_ref[...])
    vn = _rms(_gelu(v_ref[...]), ng_ref[...])

    @pl.when(is_sample)
    def _():
        vs_ref[...] = vn

    gd = vn.shape[1] // SGU_GROUPS
    row = lax.broadcasted_iota(jnp.int32, (C, C), 0)
    col = lax.broadcasted_iota(jnp.int32, (C, C), 1)
    vb = vn.astype(BF16)
    for g in range(SGU_GROUPS):
        w = w_ref[g]
        w_prompt = jnp.where(col <= row, w, 0.0)
        w_sample = jnp.where(col == row, w_ref[g, 0:1, 0:1], 0.0)
        w_eff = jnp.where(is_sample, w_sample, w_prompt).astype(BF16)
        bias = jnp.where(is_sample, bt_ref[0:1, g:g + 1], bt_ref[:, g:g + 1])
        mixed = _dot(w_eff, vb[:, g * gd:(g + 1) * gd]) + bias
        o_ref[:, g * gd:(g + 1) * gd] = (gu[:, g * gd:(g + 1) * gd] * mixed).astype(o_ref.dtype)


def _sgu(P, n_prompt, n_sample, width, sgu_norm, sgu_w, sgu_b):
    C = SGU_CHUNK
    assert n_prompt % C == 0 and n_sample == C
    T = n_prompt + n_sample
    npb = n_prompt // C
    return pl.pallas_call(
        functools.partial(_sgu_kernel, npb),
        grid=(T // C,),
        in_specs=[pl.BlockSpec((C, width), lambda i: (i, 0)),
                  pl.BlockSpec((C, width), lambda i: (i, 1)),
                  pl.BlockSpec((1, width), lambda i: (0, 0)),
                  pl.BlockSpec((SGU_GROUPS, C, C), lambda i: (0, 0, 0)),
                  pl.BlockSpec((C, SGU_GROUPS), lambda i: (0, 0))],
        out_specs=[pl.BlockSpec((C, width), lambda i: (i, 0)),
                   pl.BlockSpec((C, width), lambda i: (0, 0))],
        out_shape=[jax.ShapeDtypeStruct((T, width), BF16),
                   jax.ShapeDtypeStruct((n_sample, width), F32)],
        compiler_params=_cparams(1),
        name="sgu_mix",
    )(P, P, sgu_norm.reshape(1, width), sgu_w, sgu_b.T)


def _gla_constants():
    C = GLA_CHUNK
    s = np.arange(C)[None, :]
    i = np.arange(C)[:, None]
    gs = [s <= i, s > i]
    masks = []
    b = C // 2
    while b >= 1:
        upper = (i % (2 * b)) >= b
        m = (i // (2 * b)) * (2 * b) + b - 1
        gs.append(np.where(upper, (s > m) & (s <= i), (s > i) & (s <= m)))
        same = (i // (2 * b)) == (s // (2 * b))
        masks.append(upper & same & ~upper.T)
        b //= 2
    masks.append(np.eye(C, dtype=bool))
    G = np.concatenate(gs, axis=0).astype(np.float32)
    M = np.stack(masks).astype(np.float32)
    return jnp.asarray(G, BF16), jnp.asarray(M, F32)


def _gla_kernel(q_ref, k_ref, v_ref, r_ref, la_ref, G_ref, M_ref, gn_ref,
                o_ref, s_out_ref, st_ref):
    c = pl.program_id(1)
    C = GLA_CHUNK
    dk = q_ref.shape[1] // GLA_HEADS
    dv = v_ref.shape[1] // GLA_HEADS
    nlev = M_ref.shape[0] - 1

    @pl.when(c == 0)
    def _():
        st_ref[...] = jnp.zeros_like(st_ref)

    la = la_ref[...]
    hi = la.astype(BF16)
    r1 = la - hi.astype(F32)
    mid = r1.astype(BF16)
    lo = (r1 - mid.astype(F32)).astype(BF16)
    G = G_ref[...]
    X = jnp.exp(_dot(G, hi) + _dot(G, mid) + _dot(G, lo))

    scale = dk ** -0.5
    for h in range(GLA_HEADS):
        Xh = X[:, h * dk:(h + 1) * dk]
        qh = q_ref[:, h * dk:(h + 1) * dk] * scale
        kh = k_ref[:, h * dk:(h + 1) * dk]
        vh = v_ref[:, h * dv:(h + 1) * dv]
        x_cum = Xh[0:C]
        x_rest = Xh[C:2 * C]
        st_old = st_ref[h]
        o = _dot_nt((qh * x_cum).astype(BF16), st_old.astype(BF16))
        qb = qh.astype(BF16)
        kb = kh.astype(BF16)
        att = _dot_nt(qb, kb) * M_ref[nlev]
        for l in range(nlev):
            xl = Xh[(2 + l) * C:(3 + l) * C]
            att = att + _dot_nt((qh * xl).astype(BF16), (kh * xl).astype(BF16)) * M_ref[l]
        vb = vh.astype(BF16)
        o = o + _dot(att.astype(BF16), vb)
        vt = vh.T.astype(BF16)
        st_new = st_old * x_cum[C - 1:C, :] + _dot(vt, (kh * x_rest).astype(BF16))
        st_ref[h] = st_new
        on = _rms(o, gn_ref[...]) * _silu(r_ref[:, h * dv:(h + 1) * dv])
        o_ref[:, h * dv:(h + 1) * dv] = on.astype(o_ref.dtype)

    @pl.when(c == pl.num_programs(1) - 1)
    def _():
        for h in range(GLA_HEADS):
            s_out_ref[0, h] = st_ref[h].T


def _gla_prompt(P, log_a, n_seq, seq, col_q, kw, vw, gla_norm):
    C = GLA_CHUNK
    assert seq % C == 0
    nc = seq // C
    dk = kw // GLA_HEADS
    dv = vw // GLA_HEADS
    qb, kb = col_q // kw, col_q // kw + 1
    vb = (col_q + 2 * kw) // vw
    rb = vb + 1
    assert col_q % kw == 0 and (col_q + 2 * kw) % vw == 0
    G, M = _gla_constants()
    rows = lambda n, c: n * nc + c
    return pl.pallas_call(
        _gla_kernel,
        grid=(n_seq, nc),
        in_specs=[pl.BlockSpec((C, kw), lambda n, c: (rows(n, c), qb)),
                  pl.BlockSpec((C, kw), lambda n, c: (rows(n, c), kb)),
                  pl.BlockSpec((C, vw), lambda n, c: (rows(n, c), vb)),
                  pl.BlockSpec((C, vw), lambda n, c: (rows(n, c), rb)),
                  pl.BlockSpec((C, kw), lambda n, c: (rows(n, c), 0)),
                  pl.BlockSpec(G.shape, lambda n, c: (0, 0)),
                  pl.BlockSpec(M.shape, lambda n, c: (0, 0, 0)),
                  pl.BlockSpec((1, dv), lambda n, c: (0, 0))],
        out_specs=[pl.BlockSpec((C, vw), lambda n, c: (rows(n, c), 0)),
                   pl.BlockSpec((1, GLA_HEADS, dk, dv), lambda n, c: (n, 0, 0, 0))],
        out_shape=[jax.ShapeDtypeStruct((n_seq * seq, vw), BF16),
                   jax.ShapeDtypeStruct((n_seq, GLA_HEADS, dk, dv), F32)],
        scratch_shapes=[pltpu.VMEM((GLA_HEADS, dv, dk), F32)],
        compiler_params=_cparams(2),
        name="gla_prompt",
    )(P, P, P, P, log_a, G, M, gla_norm.reshape(1, dv))


def _gla_step_kernel(s_ref, a_ref, k_ref, q_ref, v_ref, r_ref, gn_ref, sn_ref, o_ref):
    dk = s_ref.shape[2]
    s_new = jnp.exp(a_ref[...]) * s_ref[...] + k_ref[...] * v_ref[...]
    sn_ref[...] = s_new
    o = jnp.sum((q_ref[...] * (dk ** -0.5)) * s_new, axis=2, keepdims=True)
    o_ref[...] = _rms(o, gn_ref[...]) * _silu(r_ref[...])


def _gla_sample(state, la, k, q, v, r, gla_norm):
    B, H, dk, dv = state.shape
    bb = 4
    col = lambda t: t.reshape(B, H, dk, 1)
    rowv = lambda t: t.reshape(B, H, 1, dv)
    cspec = pl.BlockSpec((bb, H, dk, 1), lambda i: (i, 0, 0, 0))
    rspec = pl.BlockSpec((bb, H, 1, dv), lambda i: (i, 0, 0, 0))
    sspec = pl.BlockSpec((bb, H, dk, dv), lambda i: (i, 0, 0, 0))
    return pl.pallas_call(
        _gla_step_kernel,
        grid=(B // bb,),
        in_specs=[sspec, cspec, cspec, cspec, rspec, rspec,
                  pl.BlockSpec((1, 1, 1, dv), lambda i: (0, 0, 0, 0))],
        out_specs=[sspec, rspec],
        out_shape=[jax.ShapeDtypeStruct((B, H, dk, dv), F32),
                   jax.ShapeDtypeStruct((B, H, 1, dv), F32)],
        compiler_params=_cparams(1),
        name="gla_sample_step",
    )(state, col(la), col(k), col(q), rowv(v), rowv(r), gla_norm.reshape(1, 1, 1, dv))


def _conv_kernel(n_prompt_blocks, blocks_per_seq, z_ref, halo_ref, b_ref, w_ref,
                 h0_ref, h1_ref, o_ref):
    i = pl.program_id(0)
    z = z_ref[...]
    w0, w1, w2 = w_ref[0:1, :], w_ref[1:2, :], w_ref[2:3, :]

    @pl.when(i < n_prompt_blocks)
    def _():
        row = lax.broadcasted_iota(jnp.int32, z.shape, 0)
        halo = jnp.where(i % blocks_per_seq == 0, 0.0, halo_ref[...])
        z1 = jnp.where(row == 0, halo[7:8, :], pltpu.roll(z, 1, 0))
        z2 = pltpu.roll(z, 2, 0)
        z2 = jnp.where(row == 0, halo[6:7, :], jnp.where(row == 1, halo[7:8, :], z2))
        o_ref[...] = (b_ref[...] * (w0 * z2 + w1 * z1 + w2 * z)).astype(o_ref.dtype)

    @pl.when(i >= n_prompt_blocks)
    def _():
        conv = w0 * h0_ref[...] + w1 * h1_ref[...] + w2 * z
        o_ref[...] = (b_ref[...] * conv).astype(o_ref.dtype)


def _short_conv(z, bgate, conv_w, hist0, hist1, n_prompt, seq):
    T, Cd = z.shape
    tr = 128
    n_sample = T - n_prompt
    assert n_sample == tr and seq % tr == 0
    npb = n_prompt // tr
    return pl.pallas_call(
        functools.partial(_conv_kernel, npb, seq // tr),
        grid=(T // tr,),
        in_specs=[pl.BlockSpec((tr, Cd), lambda i: (i, 0)),
                  pl.BlockSpec((8, Cd), lambda i: (jnp.maximum(i * (tr // 8) - 1, 0), 0)),
                  pl.BlockSpec((tr, Cd), lambda i: (i, 0)),
                  pl.BlockSpec(conv_w.shape, lambda i: (0, 0)),
                  pl.BlockSpec((tr, Cd), lambda i: (0, 0)),
                  pl.BlockSpec((tr, Cd), lambda i: (0, 0))],
        out_specs=pl.BlockSpec((tr, Cd), lambda i: (i, 0)),
        out_shape=jax.ShapeDtypeStruct((T, Cd), BF16),
        compiler_params=_cparams(1),
        name="short_conv",
    )(z, z, bgate, conv_w, hist0, hist1)


def _router_kernel(x_ref, g_ref, w_ref, idx_ref, gate_ref):
    h = _rms(x_ref[...], g_ref[...])
    logits = jnp.dot(h, w_ref[...], preferred_element_type=F32,
                     precision=lax.Precision.HIGHEST)
    lane = lax.broadcasted_iota(jnp.int32, logits.shape, 1)
    neg = jnp.float32(-jnp.inf)
    logits = jnp.where(lane < N_EXPERTS, logits, neg)
    v1 = jnp.max(logits, axis=-1, keepdims=True)
    i1 = jnp.min(jnp.where(logits == v1, lane, LANES), axis=-1, keepdims=True)
    rest = jnp.where(lane == i1, neg, logits)
    v2 = jnp.max(rest, axis=-1, keepdims=True)
    i2 = jnp.min(jnp.where(rest == v2, lane, LANES), axis=-1, keepdims=True)
    e2 = jnp.exp(v2 - v1)
    g1 = 1.0 / (1.0 + e2)
    g2 = e2 / (1.0 + e2)
    idx_ref[...] = jnp.where(lane == 0, i1, jnp.where(lane == 1, i2, 0))
    gate_ref[...] = jnp.where(lane == 0, g1, jnp.where(lane == 1, g2, 0.0))


def _router(x, g, router_w):
    T, D = x.shape
    tr = _pick_tile(T, 512)
    wpad = jnp.zeros((D, LANES), F32).at[:, :N_EXPERTS].set(router_w)
    return pl.pallas_call(
        _router_kernel,
        grid=(T // tr,),
        in_specs=[pl.BlockSpec((tr, D), lambda i: (i, 0)),
                  pl.BlockSpec((1, D), lambda i: (0, 0)),
                  pl.BlockSpec((D, LANES), lambda i: (0, 0))],
        out_specs=[pl.BlockSpec((tr, LANES), lambda i: (i, 0)),
                   pl.BlockSpec((tr, LANES), lambda i: (i, 0))],
        out_shape=[jax.ShapeDtypeStruct((T, LANES), jnp.int32),
                   jax.ShapeDtypeStruct((T, LANES), F32)],
        compiler_params=_cparams(1),
        name="moe_router",
    )(x, g.reshape(1, D), wpad)


def _dispatch_kernel(tok_ref, nxt_ref, used_ref, x_hbm, g_ref, o_ref, buf_ref, sem):
    i = pl.program_id(0)
    nsteps = pl.num_programs(0)
    n = o_ref.shape[0]
    ns = x_hbm.shape[1]
    slot = i % 2

    def row_copy(t_ref, r, s):
        return pltpu.make_async_copy(x_hbm.at[t_ref[0, 0, r]],
                                     buf_ref.at[s, pl.ds(pl.multiple_of(r * ns, ns), ns)], sem.at[s])

    def start_rows(t_ref, s):
        def body(r, carry):
            row_copy(t_ref, r, s).start()
            return carry

        lax.fori_loop(0, n, body, 0, unroll=8)

    @pl.when(jnp.logical_and(i == 0, used_ref[0] > 0))
    def _():
        start_rows(tok_ref, 0)

    @pl.when(jnp.logical_and(i + 1 < nsteps, used_ref[jnp.minimum(i + 1, nsteps - 1)] > 0))
    def _():
        start_rows(nxt_ref, 1 - slot)

    @pl.when(used_ref[i] > 0)
    def _():
        pltpu.make_async_copy(buf_ref.at[slot], buf_ref.at[slot], sem.at[slot]).wait()

        def lane_block(s):
            return buf_ref[slot, pl.ds(s, n, stride=ns), :]

        ssq = jnp.zeros((n, LANES), F32)
        for s in range(ns):
            v = lane_block(s)
            ssq = ssq + v * v
        inv = lax.rsqrt(jnp.sum(ssq, axis=-1, keepdims=True) * (1.0 / (ns * LANES)) + EPS)
        for s in range(ns):
            cols = slice(s * LANES, (s + 1) * LANES)
            o_ref[:, cols] = ((lane_block(s) * inv) * g_ref[:, cols]).astype(o_ref.dtype)

    @pl.when(used_ref[i] == 0)
    def _():
        o_ref[...] = jnp.zeros_like(o_ref)


def _dispatch(x, g, slot_tok, sub_used):
    T, D = x.shape
    nsub = sub_used.shape[0]
    SB = MOE_SUB
    tok3 = slot_tok.reshape(nsub, 1, SB)
    return pl.pallas_call(
        _dispatch_kernel,
        grid_spec=pltpu.PrefetchScalarGridSpec(
            num_scalar_prefetch=0,
            grid=(nsub,),
            in_specs=[pl.BlockSpec((1, 1, SB), lambda i: (i, 0, 0), memory_space=pltpu.SMEM),
                      pl.BlockSpec((1, 1, SB), lambda i: (jnp.minimum(i + 1, nsub - 1), 0, 0),
                                   memory_space=pltpu.SMEM),
                      pl.BlockSpec(memory_space=pltpu.SMEM),
                      pl.BlockSpec(memory_space=pl.ANY),
                      pl.BlockSpec((1, D), lambda i: (0, 0))],
            out_specs=pl.BlockSpec((SB, D), lambda i: (i, 0)),
            scratch_shapes=[pltpu.VMEM((2, SB * (D // LANES), LANES), F32),
                            pltpu.SemaphoreType.DMA((2,))],
        ),
        out_shape=jax.ShapeDtypeStruct((nsub * SB, D), BF16),
        compiler_params=_cparams(1),
        name="moe_dispatch",
    )(tok3, tok3, sub_used, x.reshape(T, D // LANES, LANES), g.reshape(1, D))


def _row_pieces(rows, total, compute, skip):
    for s in range(0, total, MOE_BIG):
        size = min(MOE_BIG, total - s)
        whole = rows > s + size - MOE_SMALL

        @pl.when(whole)
        def _():
            compute(s, size)

        @pl.when(jnp.logical_not(whole))
        def _():
            for t in range(s, s + size, MOE_SMALL):
                @pl.when(rows > t)
                def _():
                    compute(t, MOE_SMALL)

                @pl.when(rows <= t)
                def _():
                    skip(t, MOE_SMALL)


def _moe_up_kernel(ce_ref, cs_ref, cr_ref, nu_ref, x_ref, wg_ref, wu_ref, o_ref):
    c = pl.program_id(0)

    @pl.when(c < nu_ref[0])
    def _():
        wg = wg_ref[0].astype(BF16)
        wu = wu_ref[0].astype(BF16)

        def compute(s, n):
            x = x_ref[pl.ds(s, n), :]
            o_ref[pl.ds(s, n), :] = (_silu(_dot(x, wg)) * _dot(x, wu)).astype(o_ref.dtype)

        def skip(s, n):
            o_ref[pl.ds(s, n), :] = jnp.zeros((n, o_ref.shape[1]), o_ref.dtype)

        _row_pieces(cr_ref[c], x_ref.shape[0], compute, skip)

    @pl.when(c >= nu_ref[0])
    def _():
        o_ref[...] = jnp.zeros_like(o_ref)


def _moe_up(xs, w_gu, meta, R, tf):
    ce, cs, cr, nu = meta
    nch = ce.shape[0]
    D = xs.shape[1]
    F = w_gu.shape[2] // 2
    nf = F // tf

    def fidx(c, f, nu):
        return jnp.where(c < nu[0], f, nf - 1)

    return pl.pallas_call(
        _moe_up_kernel,
        grid_spec=pltpu.PrefetchScalarGridSpec(
            num_scalar_prefetch=4,
            grid=(nch, nf),
            in_specs=[pl.BlockSpec((R, D), lambda c, f, ce, cs, cr, nu: (cs[c], 0)),
                      pl.BlockSpec((1, D, tf), lambda c, f, ce, cs, cr, nu: (ce[c], 0, fidx(c, f, nu))),
                      pl.BlockSpec((1, D, tf), lambda c, f, ce, cs, cr, nu: (ce[c], 0, nf + fidx(c, f, nu)))],
            out_specs=pl.BlockSpec((R, tf), lambda c, f, ce, cs, cr, nu: (c, f)),
        ),
        out_shape=jax.ShapeDtypeStruct((nch * R, F), BF16),
        compiler_params=_cparams(2),
        name="moe_gate_up",
    )(ce, cs, cr, nu, xs, w_gu, w_gu)


def _moe_down_kernel(ce_ref, cs_ref, cr_ref, nu_ref, h_ref, w_ref, o_ref):
    c = pl.program_id(0)
    k = pl.program_id(2)

    @pl.when(c < nu_ref[0])
    def _():
        w = w_ref[0].astype(BF16)

        def compute(s, n):
            @pl.when(k == 0)
            def _():
                o_ref[pl.ds(s, n), :] = _dot(h_ref[pl.ds(s, n), :], w)

            @pl.when(k > 0)
            def _():
                o_ref[pl.ds(s, n), :] = o_ref[pl.ds(s, n), :] + _dot(h_ref[pl.ds(s, n), :], w)

        def skip(s, n):
            @pl.when(k == 0)
            def _():
                o_ref[pl.ds(s, n), :] = jnp.zeros((n, o_ref.shape[1]), F32)

        _row_pieces(cr_ref[c], h_ref.shape[0], compute, skip)

    @pl.when(jnp.logical_and(c >= nu_ref[0], k == 0))
    def _():
        o_ref[...] = jnp.zeros_like(o_ref)


def _moe_down(hid, w_down, meta, R, tn, tk):
    ce, cs, cr, nu = meta
    nch = ce.shape[0]
    F = hid.shape[1]
    D = w_down.shape[2]
    nn, nk = D // tn, F // tk

    def last_if_unused(c, v, nu, last):
        return jnp.where(c < nu[0], v, last)

    return pl.pallas_call(
        _moe_down_kernel,
        grid_spec=pltpu.PrefetchScalarGridSpec(
            num_scalar_prefetch=4,
            grid=(nch, nn, nk),
            in_specs=[pl.BlockSpec((R, tk), lambda c, n, k, ce, cs, cr, nu:
                                   (cs[c], last_if_unused(c, k, nu, nk - 1))),
                      pl.BlockSpec((1, tk, tn), lambda c, n, k, ce, cs, cr, nu:
                                   (ce[c], last_if_unused(c, k, nu, nk - 1),
                                    last_if_unused(c, n, nu, nn - 1)))],
            out_specs=pl.BlockSpec((R, tn), lambda c, n, k, ce, cs, cr, nu: (c, n)),
        ),
        out_shape=jax.ShapeDtypeStruct((nch * R, D), F32),
        compiler_params=_cparams(3),
        name="moe_down",
    )(ce, cs, cr, nu, hid, w_down)


def _combine_kernel(n_prompt_blocks, p1_ref, p2_ref, n1_ref, n2_ref, x_ref, g1_ref, g2_ref,
                    fg_ref, ys_hbm, op_ref, os_ref, buf_ref, sem):
    i = pl.program_id(0)
    nsteps = pl.num_programs(0)
    n = x_ref.shape[0]
    slot = i % 2

    def copies(a_ref, b_ref, r, s):
        return (pltpu.make_async_copy(ys_hbm.at[pl.ds(a_ref[0, 0, r], 1)],
                                      buf_ref.at[s, 0, pl.ds(r, 1)], sem.at[s]),
                pltpu.make_async_copy(ys_hbm.at[pl.ds(b_ref[0, 0, r], 1)],
                                      buf_ref.at[s, 1, pl.ds(r, 1)], sem.at[s]))

    def start_rows(a_ref, b_ref, s):
        def body(r, carry):
            ca, cb = copies(a_ref, b_ref, r, s)
            ca.start()
            cb.start()
            return carry

        lax.fori_loop(0, n, body, 0, unroll=8)

    @pl.when(i == 0)
    def _():
        start_rows(p1_ref, p2_ref, 0)

    @pl.when(i + 1 < nsteps)
    def _():
        start_rows(n1_ref, n2_ref, 1 - slot)

    pltpu.make_async_copy(buf_ref.at[slot], buf_ref.at[slot], sem.at[slot]).wait()
    y = x_ref[...] + (g1_ref[...] * buf_ref[slot, 0] + g2_ref[...] * buf_ref[slot, 1])
    yn = _rms(y, fg_ref[...])

    @pl.when(i < n_prompt_blocks)
    def _():
        op_ref[...] = yn

    @pl.when(i >= n_prompt_blocks)
    def _():
        os_ref[...] = yn


def _combine(x, ys, pos1, pos2, g1, g2, final_g, n_prompt):
    T, D = x.shape
    tr = T - n_prompt
    assert n_prompt % tr == 0
    nb = T // tr
    npb = n_prompt // tr
    cur = pl.BlockSpec((1, 1, tr), lambda i: (i, 0, 0), memory_space=pltpu.SMEM)
    nxt = pl.BlockSpec((1, 1, tr), lambda i: (jnp.minimum(i + 1, nb - 1), 0, 0),
                       memory_space=pltpu.SMEM)
    p1 = pos1.reshape(nb, 1, tr)
    p2 = pos2.reshape(nb, 1, tr)
    return pl.pallas_call(
        functools.partial(_combine_kernel, npb),
        grid_spec=pltpu.PrefetchScalarGridSpec(
            num_scalar_prefetch=0,
            grid=(nb,),
            in_specs=[cur, cur, nxt, nxt,
                      pl.BlockSpec((tr, D), lambda i: (i, 0)),
                      pl.BlockSpec((tr, 1), lambda i: (i, 0)),
                      pl.BlockSpec((tr, 1), lambda i: (i, 0)),
                      pl.BlockSpec((1, D), lambda i: (0, 0)),
                      pl.BlockSpec(memory_space=pl.ANY)],
            out_specs=[pl.BlockSpec((tr, D), lambda i: (jnp.minimum(i, npb - 1), 0)),
                       pl.BlockSpec((tr, D), lambda i: (0, 0))],
            scratch_shapes=[pltpu.VMEM((2, 2, tr, D), F32), pltpu.SemaphoreType.DMA((2,))],
        ),
        out_shape=[jax.ShapeDtypeStruct((n_prompt, D), F32),
                   jax.ShapeDtypeStruct((tr, D), F32)],
        compiler_params=_cparams(1),
        name="moe_combine_final_norm",
    )(p1, p2, p1, p2, x, g1, g2, final_g.reshape(1, D), ys)


def _moe_plan(idx, T):
    M = T * TOP_K
    R = -(-int(M / N_EXPERTS * 1.1) // MOE_SUB) * MOE_SUB
    nch = -(-M // R) + N_EXPERTS
    flat_e = idx.reshape(M)
    onehot = (flat_e[:, None] == jnp.arange(N_EXPERTS, dtype=jnp.int32)[None, :]).astype(jnp.int32)
    csum = jnp.cumsum(onehot, axis=0)
    rank = jnp.sum((csum - onehot) * onehot, axis=1)
    counts = csum[-1]
    e_chunks = (counts + R - 1) // R
    e_first = jnp.cumsum(e_chunks) - e_chunks
    n_used = jnp.sum(e_chunks).astype(jnp.int32)
    slot = (e_first[flat_e] + rank // R) * R + rank % R
    flat_tok = jnp.arange(M, dtype=jnp.int32) // TOP_K
    slot_tok = jnp.zeros((nch * R,), jnp.int32).at[slot].set(flat_tok)
    cidx = jnp.arange(nch, dtype=jnp.int32)
    c_e = jnp.clip(jnp.searchsorted(jnp.cumsum(e_chunks), cidx, side='right'), 0, N_EXPERTS - 1).astype(jnp.int32)
    c_rows = jnp.clip(counts[c_e] - (cidx - e_first[c_e]) * R, 0, R).astype(jnp.int32)
    used = cidx < n_used
    last = jnp.maximum(n_used - 1, 0)
    c_src = jnp.where(used, cidx, last).astype(jnp.int32)
    c_e = jnp.where(used, c_e, c_e[last]).astype(jnp.int32)
    c_rows = jnp.where(used, c_rows, 0).astype(jnp.int32)
    spc = R // MOE_SUB
    sub = jnp.arange(nch * spc, dtype=jnp.int32)
    sub_used = ((sub % spc) * MOE_SUB < c_rows[sub // spc]).astype(jnp.int32)
    pos = slot.reshape(T, TOP_K).astype(jnp.int32)
    meta = (c_e, c_src, c_rows, n_used.reshape(1))
    return R, slot_tok, sub_used, meta, pos[:, 0], pos[:, 1]


def kernel(x_prompt, x_sample, state_gla, state_conv, ev_norm_mix, ev_w_in, ev_sgu_norm, ev_sgu_w, ev_sgu_b, ev_gla_w_alpha, ev_gla_b_alpha, ev_gla_norm, ev_w_out, ev_norm_ffn, ev_ffn_w_gu, ev_ffn_w_down, od_norm_mix, od_conv_w_in, od_conv_w, od_conv_w_out, od_norm_ffn, od_router_w, od_moe_w_gu, od_moe_w_down, final_norm):
    nb, seq, D = x_prompt.shape
    B = x_sample.shape[0]
    assert x_sample.shape[1] == 1
    Tp = nb * seq
    T = Tp + B
    xp0 = x_prompt.reshape(Tp, D)
    xs0 = x_sample.reshape(B, D)

    sw = ev_sgu_norm.shape[1]
    kw = ev_gla_w_alpha.shape[2]
    rank = ev_gla_w_alpha.shape[1]
    vw = GLA_HEADS * ev_gla_norm.shape[1]
    n_main = 2 * sw + 2 * kw + 2 * vw
    assert ev_w_in.shape[2] == n_main + rank
    tm = _pick_tile(T, 1664)

    h = _norm_bf16_stacked(xp0, xs0, ev_norm_mix[0])
    P = _matmul(h, ev_w_in[0], n_main, tm, 512, name="mix_in_proj")
    log_a = _log_decay(h, ev_w_in[0][:, n_main:], ev_gla_w_alpha[0], ev_gla_b_alpha[0], tm)
    a_out, v_sample = _sgu(P, Tp, B, sw, ev_sgu_norm[0], ev_sgu_w[0], ev_sgu_b[0])
    col_q = 2 * sw
    o_p, gla_p = _gla_prompt(P, log_a, nb, seq, col_q, kw, vw, ev_gla_norm[0])
    Ps = P[Tp:]
    q_s = Ps[:, col_q:col_q + kw]
    k_s = Ps[:, col_q + kw:col_q + 2 * kw]
    v_s = Ps[:, col_q + 2 * kw:col_q + 2 * kw + vw]
    r_s = Ps[:, col_q + 2 * kw + vw:]
    gla_s, o_s = _gla_sample(state_gla[0], log_a[Tp:], k_s, q_s, v_s, r_s, ev_gla_norm[0])
    x1 = _mix_out(a_out, o_p, o_s.reshape(B, vw).astype(BF16), xp0, xs0, ev_w_out[0], 512)

    h = _norm_bf16(x1, ev_norm_ffn[0])
    hid = _swiglu_up(h, ev_ffn_w_gu[0], tm, 512)
    x2 = _matmul(hid, ev_ffn_w_down[0], D, _pick_tile(T, 832), 256, res=x1, name="ffn_down")

    h = _norm_bf16(x2, od_norm_mix[0])
    bgate, z = _conv_in(h, od_conv_w_in[0], tm, 256)
    yc = _short_conv(z, bgate, od_conv_w[0], state_conv[0][:, 0], state_conv[0][:, 1], Tp, seq)
    x3 = _matmul(yc, od_conv_w_out[0], D, tm, 512, res=x2, name="conv_out_proj")
    conv_p = jnp.stack([z[(n + 1) * seq - 2:(n + 1) * seq] for n in range(nb)])
    conv_s = jnp.concatenate([state_conv[0][:, 1:], z[Tp:].reshape(B, 1, D)], axis=1)

    idx, gates = _router(x3, od_norm_ffn[0], od_router_w[0])
    R, slot_tok, sub_used, meta, pos1, pos2 = _moe_plan(idx[:, :TOP_K], T)
    xs = _dispatch(x3, od_norm_ffn[0], slot_tok, sub_used)
    FE = od_moe_w_down.shape[2]
    hid = _moe_up(xs, od_moe_w_gu[0], meta, R, _pick_tile(FE, 512, LANES))
    ys = _moe_down(hid, od_moe_w_down[0], meta, R, 1024, _pick_tile(FE, 1024, LANES))
    y_p, y_s = _combine(x3, ys, pos1, pos2, gates[:, 0:1], gates[:, 1:2], final_norm, Tp)

    y_prompt = y_p.reshape(nb, seq, D)
    y_sample = y_s.reshape(B, 1, D)
    return (y_prompt, y_sample, gla_p[None], gla_s[None],
            v_sample.reshape(1, B, 1, sw), conv_p[None], conv_s[None])
```

```python
import functools
import math

import numpy as np
import jax
import jax.numpy as jnp
from jax import lax
from jax.experimental import pallas as pl
from jax.experimental.pallas import tpu as pltpu

F32 = jnp.float32
BF16 = jnp.bfloat16
EPS = 1e-6

SGU_CHUNK = 128
SGU_GROUPS = 4
GLA_HEADS = 4
GLA_TAU = 16.0
GLA_CHUNK = 128
N_EXPERTS = 8
TOP_K = 2
MOE_SUB = 256
MOE_BIG = 1024
MOE_SMALL = 128
LANES = 128
VMEM_LIMIT = 52 * 1024 * 1024


def _cparams(n_axes, vmem=VMEM_LIMIT):
    return pltpu.CompilerParams(dimension_semantics=("arbitrary",) * n_axes,
                                vmem_limit_bytes=vmem)


def _pick_tile(n, cap, mult=16):
    best = None
    for t in range(mult, min(n, cap) + 1, mult):
        if n % t == 0:
            best = t
    assert best is not None, (n, cap, mult)
    return best


def _rms(x, g):
    ms = jnp.mean(x * x, axis=-1, keepdims=True)
    return (x * lax.rsqrt(ms + EPS)) * g


def _gelu(x):
    c = math.sqrt(2.0 / math.pi)
    return x * (0.5 * (1.0 + jnp.tanh(c * (x + 0.044715 * (x * x * x)))))


def _silu(x):
    return x * (1.0 / (1.0 + jnp.exp(-x)))


def _dot(a, b):
    return jnp.dot(a, b, preferred_element_type=F32)


def _dot_nt(a, b):
    return lax.dot_general(a, b, (((1,), (1,)), ((), ())), preferred_element_type=F32)


def _norm_kernel(x_ref, g_ref, o_ref):
    o_ref[...] = _rms(x_ref[...], g_ref[...]).astype(BF16)


def _norm_bf16(x, g):
    T, D = x.shape
    tr = _pick_tile(T, 512)
    return pl.pallas_call(
        _norm_kernel,
        grid=(T // tr,),
        in_specs=[pl.BlockSpec((tr, D), lambda i: (i, 0)),
                  pl.BlockSpec((1, D), lambda i: (0, 0))],
        out_specs=pl.BlockSpec((tr, D), lambda i: (i, 0)),
        out_shape=jax.ShapeDtypeStruct((T, D), BF16),
        compiler_params=_cparams(1),
        name="rmsnorm_bf16",
    )(x, g.reshape(1, D))


def _norm2_kernel(n_prompt_blocks, xp_ref, xs_ref, g_ref, o_ref):
    i = pl.program_id(0)

    @pl.when(i < n_prompt_blocks)
    def _():
        o_ref[...] = _rms(xp_ref[...], g_ref[...]).astype(BF16)

    @pl.when(i >= n_prompt_blocks)
    def _():
        o_ref[...] = _rms(xs_ref[...], g_ref[...]).astype(BF16)


def _norm_bf16_stacked(xp, xs, g):
    Tp, D = xp.shape
    B = xs.shape[0]
    tr = B
    assert Tp % tr == 0
    npb = Tp // tr
    return pl.pallas_call(
        functools.partial(_norm2_kernel, npb),
        grid=(npb + 1,),
        in_specs=[pl.BlockSpec((tr, D), lambda i: (jnp.minimum(i, npb - 1), 0)),
                  pl.BlockSpec((tr, D), lambda i: (0, 0)),
                  pl.BlockSpec((1, D), lambda i: (0, 0))],
        out_specs=pl.BlockSpec((tr, D), lambda i: (i, 0)),
        out_shape=jax.ShapeDtypeStruct((Tp + B, D), BF16),
        compiler_params=_cparams(1),
        name="rmsnorm_stack_bf16",
    )(xp, xs, g.reshape(1, D))


def _mm_kernel(x_ref, w_ref, o_ref):
    o_ref[...] = _dot(x_ref[...], w_ref[...].astype(BF16)).astype(o_ref.dtype)


def _mm_res_kernel(x_ref, w_ref, r_ref, o_ref):
    o_ref[...] = r_ref[...] + _dot(x_ref[...], w_ref[...].astype(BF16))


def _matmul(x, w, n_cols, tm, tn, res=None, name="matmul"):
    T, K = x.shape
    grid = (T // tm, n_cols // tn)
    in_specs = [pl.BlockSpec((tm, K), lambda i, j: (i, 0)),
                pl.BlockSpec((K, tn), lambda i, j: (0, j))]
    args = [x, w]
    kern = _mm_kernel
    if res is not None:
        in_specs.append(pl.BlockSpec((tm, tn), lambda i, j: (i, j)))
        args.append(res)
        kern = _mm_res_kernel
    return pl.pallas_call(
        kern, grid=grid, in_specs=in_specs,
        out_specs=pl.BlockSpec((tm, tn), lambda i, j: (i, j)),
        out_shape=jax.ShapeDtypeStruct((T, n_cols), F32),
        compiler_params=_cparams(2),
        name=name,
    )(*args)


def _mix_out_kernel(n_full, n_tail, a_ref, op_ref, os_ref, rp_ref, rs_ref, w_ref, ng_ref,
                    x_ref, h_ref):
    i = pl.program_id(0)
    sw = a_ref.shape[1]

    def finish(rows, a, o, res):
        y = res + _dot(a, w_ref[0:sw, :]) + _dot(o, w_ref[sw:, :])
        x_ref[0:rows, :] = y
        h_ref[0:rows, :] = _rms(y, ng_ref[...]).astype(BF16)

    @pl.when(i < n_full)
    def _():
        finish(a_ref.shape[0], a_ref[...], op_ref[...], rp_ref[...])

    @pl.when(i >= n_full)
    def _():
        finish(n_tail, a_ref[0:n_tail, :], os_ref[...], rs_ref[...])


def _mix_out(a, o_p, o_s, res_p, res_s, w, norm_g):
    T, sw = a.shape
    Tp, vw = o_p.shape
    B = o_s.shape[0]
    D = w.shape[1]
    assert T == Tp + B
    tm = _pick_tile(Tp, 256)
    assert B <= tm
    n_full = Tp // tm
    clamp = lambda i: (jnp.minimum(i, n_full - 1), 0)
    row = lambda i: (i, 0)
    fixed = lambda i: (0, 0)
    return pl.pallas_call(
        functools.partial(_mix_out_kernel, n_full, B),
        grid=(n_full + 1,),
        in_specs=[pl.BlockSpec((tm, sw), row),
                  pl.BlockSpec((tm, vw), clamp),
                  pl.BlockSpec((B, vw), fixed),
                  pl.BlockSpec((tm, D), clamp),
                  pl.BlockSpec((B, D), fixed),
                  pl.BlockSpec((sw + vw, D), fixed),
                  pl.BlockSpec((1, D), fixed)],
        out_specs=[pl.BlockSpec((tm, D), row),
                   pl.BlockSpec((tm, D), row)],
        out_shape=[jax.ShapeDtypeStruct((T, D), F32),
                   jax.ShapeDtypeStruct((T, D), BF16)],
        compiler_params=_cparams(1),
        name="mix_out_proj",
    )(a, o_p, o_s, res_p, res_s, w.astype(BF16), norm_g.reshape(1, D))


def _swiglu_kernel(x_ref, wg_ref, wu_ref, o_ref):
    x = x_ref[...]
    g = _dot(x, wg_ref[...].astype(BF16))
    u = _dot(x, wu_ref[...].astype(BF16))
    o_ref[...] = (_silu(g) * u).astype(o_ref.dtype)


def _swiglu_up(x, w_gu, tm, tn):
    T, K = x.shape
    F = w_gu.shape[1] // 2
    nj = F // tn
    return pl.pallas_call(
        _swiglu_kernel,
        grid=(T // tm, nj),
        in_specs=[pl.BlockSpec((tm, K), lambda i, j: (i, 0)),
                  pl.BlockSpec((K, tn), lambda i, j: (0, j)),
                  pl.BlockSpec((K, tn), lambda i, j: (0, j + nj))],
        out_specs=pl.BlockSpec((tm, tn), lambda i, j: (i, j)),
        out_shape=jax.ShapeDtypeStruct((T, F), BF16),
        compiler_params=_cparams(2),
        name="ffn_gate_up",
    )(x, w_gu, w_gu)


def _conv_in_kernel(x_ref, wb_ref, wc_ref, wu_ref, b_ref, z_ref):
    x = x_ref[...]
    b_ref[...] = _dot(x, wb_ref[...].astype(BF16))
    c = _dot(x, wc_ref[...].astype(BF16))
    u = _dot(x, wu_ref[...].astype(BF16))
    z_ref[...] = c * u


def _conv_in(x, w, tm, tn):
    T, K = x.shape
    C = w.shape[1] // 3
    nj = C // tn
    return pl.pallas_call(
        _conv_in_kernel,
        grid=(T // tm, nj),
        in_specs=[pl.BlockSpec((tm, K), lambda i, j: (i, 0)),
                  pl.BlockSpec((K, tn), lambda i, j: (0, j)),
                  pl.BlockSpec((K, tn), lambda i, j: (0, j + nj)),
                  pl.BlockSpec((K, tn), lambda i, j: (0, j + 2 * nj))],
        out_specs=[pl.BlockSpec((tm, tn), lambda i, j: (i, j)),
                   pl.BlockSpec((tm, tn), lambda i, j: (i, j))],
        out_shape=[jax.ShapeDtypeStruct((T, C), F32),
                   jax.ShapeDtypeStruct((T, C), F32)],
        compiler_params=_cparams(2),
        name="conv_in_proj",
    )(x, w, w, w)


def _loga_kernel(x_ref, wl_ref, wa_ref, b_ref, o_ref):
    g = _dot(x_ref[...], wl_ref[...].astype(BF16))
    z = _dot(g.astype(BF16), wa_ref[...].astype(BF16)) + b_ref[...]
    ls = jnp.minimum(z, 0.0) - jnp.log1p(jnp.exp(-jnp.abs(z)))
    o_ref[...] = ls * (1.0 / GLA_TAU)


def _log_decay(x, w_lr, w_alpha, b_alpha, tm):
    T, K = x.shape
    R, KW = w_alpha.shape
    return pl.pallas_call(
        _loga_kernel,
        grid=(T // tm,),
        in_specs=[pl.BlockSpec((tm, K), lambda i: (i, 0)),
                  pl.BlockSpec((K, R), lambda i: (0, 0)),
                  pl.BlockSpec((R, KW), lambda i: (0, 0)),
                  pl.BlockSpec((1, KW), lambda i: (0, 0))],
        out_specs=pl.BlockSpec((tm, KW), lambda i: (i, 0)),
        out_shape=jax.ShapeDtypeStruct((T, KW), F32),
        compiler_params=_cparams(1),
        name="gla_log_decay",
    )(x, w_lr, w_alpha, b_alpha.reshape(1, KW))


def _sgu_kernel(n_prompt_blocks, u_ref, v_ref, ng_ref, w_ref, bt_ref, o_ref, vs_ref):
    i = pl.program_id(0)
    is_sample = i >= n_prompt_blocks
    C = SGU_CHUNK
    gu = _gelu(u_ref[...])
    vn = _rms(_gelu(v_ref[...]), ng_ref[...])

    @pl.when(is_sample)
    def _():
        vs_ref[...] = vn

    gd = vn.shape[1] // SGU_GROUPS
    row = lax.broadcasted_iota(jnp.int32, (C, C), 0)
    col = lax.broadcasted_iota(jnp.int32, (C, C), 1)
    vb = vn.astype(BF16)
    for g in range(SGU_GROUPS):
        w = w_ref[g]
        w_prompt = jnp.where(col <= row, w, 0.0)
        w_sample = jnp.where(col == row, w_ref[g, 0:1, 0:1], 0.0)
        w_eff = jnp.where(is_sample, w_sample, w_prompt).astype(BF16)
        bias = jnp.where(is_sample, bt_ref[0:1, g:g + 1], bt_ref[:, g:g + 1])
        mixed = _dot(w_eff, vb[:, g * gd:(g + 1) * gd]) + bias
        o_ref[:, g * gd:(g + 1) * gd] = (gu[:, g * gd:(g + 1) * gd] * mixed).astype(o_ref.dtype)


def _sgu(P, n_prompt, n_sample, width, sgu_norm, sgu_w, sgu_b):
    C = SGU_CHUNK
    assert n_prompt % C == 0 and n_sample == C
    T = n_prompt + n_sample
    npb = n_prompt // C
    return pl.pallas_call(
        functools.partial(_sgu_kernel, npb),
        grid=(T // C,),
        in_specs=[pl.BlockSpec((C, width), lambda i: (i, 0)),
                  pl.BlockSpec((C, width), lambda i: (i, 1)),
                  pl.BlockSpec((1, width), lambda i: (0, 0)),
                  pl.BlockSpec((SGU_GROUPS, C, C), lambda i: (0, 0, 0)),
                  pl.BlockSpec((C, SGU_GROUPS), lambda i: (0, 0))],
        out_specs=[pl.BlockSpec((C, width), lambda i: (i, 0)),
                   pl.BlockSpec((C, width), lambda i: (0, 0))],
        out_shape=[jax.ShapeDtypeStruct((T, width), BF16),
                   jax.ShapeDtypeStruct((n_sample, width), F32)],
        compiler_params=_cparams(1),
        name="sgu_mix",
    )(P, P, sgu_norm.reshape(1, width), sgu_w, sgu_b.T)


def _gla_constants():
    C = GLA_CHUNK
    s = np.arange(C)[None, :]
    i = np.arange(C)[:, None]
    gs = [s <= i, s > i]
    masks = []
    b = C // 2
    while b >= 1:
        upper = (i % (2 * b)) >= b
        m = (i // (2 * b)) * (2 * b) + b - 1
        gs.append(np.where(upper, (s > m) & (s <= i), (s > i) & (s <= m)))
        same = (i // (2 * b)) == (s // (2 * b))
        masks.append(upper & same & ~upper.T)
        b //= 2
    masks.append(np.eye(C, dtype=bool))
    G = np.concatenate(gs, axis=0).astype(np.float32)
    M = np.stack(masks).astype(np.float32)
    return jnp.asarray(G, BF16), jnp.asarray(M, F32)


def _gla_kernel(q_ref, k_ref, v_ref, r_ref, la_ref, G_ref, M_ref, gn_ref,
                o_ref, s_out_ref, st_ref):
    c = pl.program_id(1)
    C = GLA_CHUNK
    dk = q_ref.shape[1] // GLA_HEADS
    dv = v_ref.shape[1] // GLA_HEADS
    nlev = M_ref.shape[0] - 1

    @pl.when(c == 0)
    def _():
        st_ref[...] = jnp.zeros_like(st_ref)

    la = la_ref[...]
    hi = la.astype(BF16)
    r1 = la - hi.astype(F32)
    mid = r1.astype(BF16)
    lo = (r1 - mid.astype(F32)).astype(BF16)
    G = G_ref[...]
    X = jnp.exp(_dot(G, hi) + _dot(G, mid) + _dot(G, lo))

    scale = dk ** -0.5
    for h in range(GLA_HEADS):
        Xh = X[:, h * dk:(h + 1) * dk]
        qh = q_ref[:, h * dk:(h + 1) * dk] * scale
        kh = k_ref[:, h * dk:(h + 1) * dk]
        vh = v_ref[:, h * dv:(h + 1) * dv]
        x_cum = Xh[0:C]
        x_rest = Xh[C:2 * C]
        st_old = st_ref[h]
        o = _dot_nt((qh * x_cum).astype(BF16), st_old.astype(BF16))
        qb = qh.astype(BF16)
        kb = kh.astype(BF16)
        att = _dot_nt(qb, kb) * M_ref[nlev]
        for l in range(nlev):
            xl = Xh[(2 + l) * C:(3 + l) * C]
            att = att + _dot_nt((qh * xl).astype(BF16), (kh * xl).astype(BF16)) * M_ref[l]
        vb = vh.astype(BF16)
        o = o + _dot(att.astype(BF16), vb)
        vt = vh.T.astype(BF16)
        st_new = st_old * x_cum[C - 1:C, :] + _dot(vt, (kh * x_rest).astype(BF16))
        st_ref[h] = st_new
        on = _rms(o, gn_ref[...]) * _silu(r_ref[:, h * dv:(h + 1) * dv])
        o_ref[:, h * dv:(h + 1) * dv] = on.astype(o_ref.dtype)

    @pl.when(c == pl.num_programs(1) - 1)
    def _():
        for h in range(GLA_HEADS):
            s_out_ref[0, h] = st_ref[h].T


def _gla_prompt(P, log_a, n_seq, seq, col_q, kw, vw, gla_norm):
    C = GLA_CHUNK
    assert seq % C == 0
    nc = seq // C
    dk = kw // GLA_HEADS
    dv = vw // GLA_HEADS
    qb, kb = col_q // kw, col_q // kw + 1
    vb = (col_q + 2 * kw) // vw
    rb = vb + 1
    assert col_q % kw == 0 and (col_q + 2 * kw) % vw == 0
    G, M = _gla_constants()
    rows = lambda n, c: n * nc + c
    return pl.pallas_call(
        _gla_kernel,
        grid=(n_seq, nc),
        in_specs=[pl.BlockSpec((C, kw), lambda n, c: (rows(n, c), qb)),
                  pl.BlockSpec((C, kw), lambda n, c: (rows(n, c), kb)),
                  pl.BlockSpec((C, vw), lambda n, c: (rows(n, c), vb)),
                  pl.BlockSpec((C, vw), lambda n, c: (rows(n, c), rb)),
                  pl.BlockSpec((C, kw), lambda n, c: (rows(n, c), 0)),
                  pl.BlockSpec(G.shape, lambda n, c: (0, 0)),
                  pl.BlockSpec(M.shape, lambda n, c: (0, 0, 0)),
                  pl.BlockSpec((1, dv), lambda n, c: (0, 0))],
        out_specs=[pl.BlockSpec((C, vw), lambda n, c: (rows(n, c), 0)),
                   pl.BlockSpec((1, GLA_HEADS, dk, dv), lambda n, c: (n, 0, 0, 0))],
        out_shape=[jax.ShapeDtypeStruct((n_seq * seq, vw), BF16),
                   jax.ShapeDtypeStruct((n_seq, GLA_HEADS, dk, dv), F32)],
        scratch_shapes=[pltpu.VMEM((GLA_HEADS, dv, dk), F32)],
        compiler_params=_cparams(2),
        name="gla_prompt",
    )(P, P, P, P, log_a, G, M, gla_norm.reshape(1, dv))


def _gla_step_kernel(s_ref, at_ref, kt_ref, qt_ref, v_ref, r_ref, gn_ref, sn_ref, o_ref):
    i = pl.program_id(0)
    bb, H, dk, dv = s_ref.shape
    B = at_ref.shape[1]
    lane = lax.broadcasted_iota(jnp.int32, (dk, B), 1)
    scale = dk ** -0.5
    for j in range(bb):
        pick = lane == i * bb + j

        def column(t_ref, h):
            return jnp.sum(jnp.where(pick, t_ref[h * dk:(h + 1) * dk, :], 0.0), axis=1, keepdims=True)

        for h in range(H):
            s_new = jnp.exp(column(at_ref, h)) * s_ref[j, h] + column(kt_ref, h) * v_ref[j, h]
            sn_ref[j, h] = s_new
            o = jnp.sum((column(qt_ref, h) * scale) * s_new, axis=0, keepdims=True)
            o_ref[j, h] = _rms(o, gn_ref[...]) * _silu(r_ref[j, h])


def _gla_sample(state, la, k, q, v, r, gla_norm):
    B, H, dk, dv = state.shape
    bb = 8
    rowv = lambda t: t.reshape(B, H, 1, dv)
    tspec = pl.BlockSpec((H * dk, B), lambda i: (0, 0))
    rspec = pl.BlockSpec((bb, H, 1, dv), lambda i: (i, 0, 0, 0))
    sspec = pl.BlockSpec((bb, H, dk, dv), lambda i: (i, 0, 0, 0))
    return pl.pallas_call(
        _gla_step_kernel,
        grid=(B // bb,),
        in_specs=[sspec, tspec, tspec, tspec, rspec, rspec,
                  pl.BlockSpec((1, dv), lambda i: (0, 0))],
        out_specs=[sspec, rspec],
        out_shape=[jax.ShapeDtypeStruct((B, H, dk, dv), F32),
                   jax.ShapeDtypeStruct((B, H, 1, dv), F32)],
        compiler_params=_cparams(1),
        name="gla_sample_step",
    )(state, la.T, k.T, q.T, rowv(v), rowv(r), gla_norm.reshape(1, dv))


def _route(logits):
    lane = lax.broadcasted_iota(jnp.int32, logits.shape, 1)
    neg = jnp.float32(-jnp.inf)
    logits = jnp.where(lane < N_EXPERTS, logits, neg)
    v1 = jnp.max(logits, axis=-1, keepdims=True)
    i1 = jnp.min(jnp.where(logits == v1, lane, LANES), axis=-1, keepdims=True)
    rest = jnp.where(lane == i1, neg, logits)
    v2 = jnp.max(rest, axis=-1, keepdims=True)
    i2 = jnp.min(jnp.where(rest == v2, lane, LANES), axis=-1, keepdims=True)
    e2 = jnp.exp(v2 - v1)
    g1 = 1.0 / (1.0 + e2)
    g2 = e2 / (1.0 + e2)
    idx = jnp.where(lane == 0, i1, jnp.where(lane == 1, i2, 0))
    gate = jnp.where(lane == 0, g1, jnp.where(lane == 1, g2, 0.0))
    return idx, gate


def _conv_out_kernel(n_prompt_blocks, blocks_per_seq, n_tail, z_ref, halo_ref, b_ref, cw_ref,
                     h0_ref, h1_ref, res_ref, w_ref, ng_ref, rw_ref, x_ref, idx_ref, gate_ref):
    i = pl.program_id(0)
    w0, w1, w2 = cw_ref[0:1, :], cw_ref[1:2, :], cw_ref[2:3, :]

    def finish(rows, yc, res):
        y = res + _dot(yc.astype(BF16), w_ref[...])
        x_ref[0:rows, :] = y
        hn = _rms(y, ng_ref[...])
        h_hi = hn.astype(BF16)
        h_lo = (hn - h_hi.astype(F32)).astype(BF16)
        rw = rw_ref[...]
        r_hi = rw.astype(BF16)
        r_lo = (rw - r_hi.astype(F32)).astype(BF16)
        logits = _dot(h_hi, r_hi) + (_dot(h_hi, r_lo) + _dot(h_lo, r_hi))
        idx, gate = _route(logits)
        idx_ref[0:rows, :] = idx
        gate_ref[0:rows, :] = gate

    @pl.when(i < n_prompt_blocks)
    def _():
        z = z_ref[...]
        row = lax.broadcasted_iota(jnp.int32, z.shape, 0)
        halo = jnp.where(i % blocks_per_seq == 0, 0.0, halo_ref[...])
        z1 = jnp.where(row == 0, halo[7:8, :], pltpu.roll(z, 1, 0))
        z2 = pltpu.roll(z, 2, 0)
        z2 = jnp.where(row == 0, halo[6:7, :], jnp.where(row == 1, halo[7:8, :], z2))
        finish(z.shape[0], b_ref[...] * (w0 * z2 + w1 * z1 + w2 * z), res_ref[...])

    @pl.when(i >= n_prompt_blocks)
    def _():
        conv = w0 * h0_ref[...] + w1 * h1_ref[...] + w2 * z_ref[0:n_tail, :]
        finish(n_tail, b_ref[0:n_tail, :] * conv, res_ref[0:n_tail, :])


def _conv_out_route(z, bgate, conv_w, hist0, hist1, res, w_out, norm_g, router_w, n_prompt, seq):
    T, Cd = z.shape
    D = w_out.shape[1]
    B = T - n_prompt
    tm = _pick_tile(math.gcd(seq, n_prompt), 256)
    assert B <= tm and hist0.shape == (B, Cd)
    npb = n_prompt // tm
    wpad = jnp.zeros((D, LANES), F32).at[:, :N_EXPERTS].set(router_w)
    row = lambda i: (i, 0)
    fixed = lambda i: (0, 0)
    return pl.pallas_call(
        functools.partial(_conv_out_kernel, npb, seq // tm, B),
        grid=(npb + 1,),
        in_specs=[pl.BlockSpec((tm, Cd), row),
                  pl.BlockSpec((8, Cd), lambda i: (jnp.maximum(i * (tm // 8) - 1, 0), 0)),
                  pl.BlockSpec((tm, Cd), row),
                  pl.BlockSpec(conv_w.shape, fixed),
                  pl.BlockSpec((B, Cd), fixed),
                  pl.BlockSpec((B, Cd), fixed),
                  pl.BlockSpec((tm, D), row),
                  pl.BlockSpec((Cd, D), fixed),
                  pl.BlockSpec((1, D), fixed),
                  pl.BlockSpec((D, LANES), fixed)],
        out_specs=[pl.BlockSpec((tm, D), row),
                   pl.BlockSpec((tm, LANES), row),
                   pl.BlockSpec((tm, LANES), row)],
        out_shape=[jax.ShapeDtypeStruct((T, D), F32),
                   jax.ShapeDtypeStruct((T, LANES), jnp.int32),
                   jax.ShapeDtypeStruct((T, LANES), F32)],
        compiler_params=_cparams(1),
        name="conv_out_route",
    )(z, z, bgate, conv_w, hist0, hist1, res, w_out.astype(BF16), norm_g.reshape(1, D), wpad)


def _dispatch_kernel(tok_ref, nxt_ref, used_ref, x_hbm, g_ref, o_ref, buf_ref, sem):
    i = pl.program_id(0)
    nsteps = pl.num_programs(0)
    n = o_ref.shape[0]
    ns = x_hbm.shape[1]
    slot = i % 2

    def row_copy(t_ref, r, s):
        return pltpu.make_async_copy(x_hbm.at[t_ref[0, 0, r]],
                                     buf_ref.at[s, pl.ds(pl.multiple_of(r * ns, ns), ns)], sem.at[s])

    def start_rows(t_ref, s):
        def body(r, carry):
            row_copy(t_ref, r, s).start()
            return carry

        lax.fori_loop(0, n, body, 0, unroll=8)

    @pl.when(jnp.logical_and(i == 0, used_ref[0] > 0))
    def _():
        start_rows(tok_ref, 0)

    @pl.when(jnp.logical_and(i + 1 < nsteps, used_ref[jnp.minimum(i + 1, nsteps - 1)] > 0))
    def _():
        start_rows(nxt_ref, 1 - slot)

    @pl.when(used_ref[i] > 0)
    def _():
        pltpu.make_async_copy(buf_ref.at[slot], buf_ref.at[slot], sem.at[slot]).wait()

        def lane_block(s):
            return buf_ref[slot, pl.ds(s, n, stride=ns), :]

        ssq = jnp.zeros((n, LANES), F32)
        for s in range(ns):
            v = lane_block(s)
            ssq = ssq + v * v
        inv = lax.rsqrt(jnp.sum(ssq, axis=-1, keepdims=True) * (1.0 / (ns * LANES)) + EPS)
        for s in range(ns):
            cols = slice(s * LANES, (s + 1) * LANES)
            o_ref[:, cols] = ((lane_block(s) * inv) * g_ref[:, cols]).astype(o_ref.dtype)

    @pl.when(used_ref[i] == 0)
    def _():
        o_ref[...] = jnp.zeros_like(o_ref)


def _dispatch(x, g, slot_tok, sub_used):
    T, D = x.shape
    nsub = sub_used.shape[0]
    SB = MOE_SUB
    tok3 = slot_tok.reshape(nsub, 1, SB)
    return pl.pallas_call(
        _dispatch_kernel,
        grid_spec=pltpu.PrefetchScalarGridSpec(
            num_scalar_prefetch=0,
            grid=(nsub,),
            in_specs=[pl.BlockSpec((1, 1, SB), lambda i: (i, 0, 0), memory_space=pltpu.SMEM),
                      pl.BlockSpec((1, 1, SB), lambda i: (jnp.minimum(i + 1, nsub - 1), 0, 0),
                                   memory_space=pltpu.SMEM),
                      pl.BlockSpec(memory_space=pltpu.SMEM),
                      pl.BlockSpec(memory_space=pl.ANY),
                      pl.BlockSpec((1, D), lambda i: (0, 0))],
            out_specs=pl.BlockSpec((SB, D), lambda i: (i, 0)),
            scratch_shapes=[pltpu.VMEM((2, SB * (D // LANES), LANES), F32),
                            pltpu.SemaphoreType.DMA((2,))],
        ),
        out_shape=jax.ShapeDtypeStruct((nsub * SB, D), BF16),
        compiler_params=_cparams(1),
        name="moe_dispatch",
    )(tok3, tok3, sub_used, x.reshape(T, D // LANES, LANES), g.reshape(1, D))


def _row_pieces(rows, total, compute, skip):
    for s in range(0, total, MOE_BIG):
        size = min(MOE_BIG, total - s)
        whole = rows > s + size - MOE_SMALL

        @pl.when(whole)
        def _():
            compute(s, size)

        @pl.when(jnp.logical_not(whole))
        def _():
            for t in range(s, s + size, MOE_SMALL):
                @pl.when(rows > t)
                def _():
                    compute(t, MOE_SMALL)

                @pl.when(rows <= t)
                def _():
                    skip(t, MOE_SMALL)


def _moe_up_kernel(ce_ref, cs_ref, cr_ref, nu_ref, x_ref, wg_ref, wu_ref, o_ref):
    c = pl.program_id(0)

    @pl.when(c < nu_ref[0])
    def _():
        wg = wg_ref[0].astype(BF16)
        wu = wu_ref[0].astype(BF16)

        def compute(s, n):
            x = x_ref[pl.ds(s, n), :]
            o_ref[pl.ds(s, n), :] = (_silu(_dot(x, wg)) * _dot(x, wu)).astype(o_ref.dtype)

        def skip(s, n):
            o_ref[pl.ds(s, n), :] = jnp.zeros((n, o_ref.shape[1]), o_ref.dtype)

        _row_pieces(cr_ref[c], x_ref.shape[0], compute, skip)

    @pl.when(c >= nu_ref[0])
    def _():
        o_ref[...] = jnp.zeros_like(o_ref)


def _moe_up(xs, w_gu, meta, R, tf):
    ce, cs, cr, nu = meta
    nch = ce.shape[0]
    D = xs.shape[1]
    F = w_gu.shape[2] // 2
    nf = F // tf

    def fidx(c, f, nu):
        return jnp.where(c < nu[0], f, nf - 1)

    return pl.pallas_call(
        _moe_up_kernel,
        grid_spec=pltpu.PrefetchScalarGridSpec(
            num_scalar_prefetch=4,
            grid=(nch, nf),
            in_specs=[pl.BlockSpec((R, D), lambda c, f, ce, cs, cr, nu: (cs[c], 0)),
                      pl.BlockSpec((1, D, tf), lambda c, f, ce, cs, cr, nu: (ce[c], 0, fidx(c, f, nu))),
                      pl.BlockSpec((1, D, tf), lambda c, f, ce, cs, cr, nu: (ce[c], 0, nf + fidx(c, f, nu)))],
            out_specs=pl.BlockSpec((R, tf), lambda c, f, ce, cs, cr, nu: (c, f)),
        ),
        out_shape=jax.ShapeDtypeStruct((nch * R, F), BF16),
        compiler_params=_cparams(2),
        name="moe_gate_up",
    )(ce, cs, cr, nu, xs, w_gu, w_gu)


def _moe_down_kernel(ce_ref, cs_ref, cr_ref, nu_ref, h_ref, w_ref, o_ref):
    c = pl.program_id(0)
    k = pl.program_id(2)

    @pl.when(c < nu_ref[0])
    def _():
        w = w_ref[0].astype(BF16)

        def compute(s, n):
            @pl.when(k == 0)
            def _():
                o_ref[pl.ds(s, n), :] = _dot(h_ref[pl.ds(s, n), :], w)

            @pl.when(k > 0)
            def _():
                o_ref[pl.ds(s, n), :] = o_ref[pl.ds(s, n), :] + _dot(h_ref[pl.ds(s, n), :], w)

        def skip(s, n):
            @pl.when(k == 0)
            def _():
                o_ref[pl.ds(s, n), :] = jnp.zeros((n, o_ref.shape[1]), F32)

        _row_pieces(cr_ref[c], h_ref.shape[0], compute, skip)

    @pl.when(jnp.logical_and(c >= nu_ref[0], k == 0))
    def _():
        o_ref[...] = jnp.zeros_like(o_ref)


def _moe_down(hid, w_down, meta, R, tn, tk):
    ce, cs, cr, nu = meta
    nch = ce.shape[0]
    F = hid.shape[1]
    D = w_down.shape[2]
    nn, nk = D // tn, F // tk

    def last_if_unused(c, v, nu, last):
        return jnp.where(c < nu[0], v, last)

    return pl.pallas_call(
        _moe_down_kernel,
        grid_spec=pltpu.PrefetchScalarGridSpec(
            num_scalar_prefetch=4,
            grid=(nch, nn, nk),
            in_specs=[pl.BlockSpec((R, tk), lambda c, n, k, ce, cs, cr, nu:
                                   (cs[c], last_if_unused(c, k, nu, nk - 1))),
                      pl.BlockSpec((1, tk, tn), lambda c, n, k, ce, cs, cr, nu:
                                   (ce[c], last_if_unused(c, k, nu, nk - 1),
                                    last_if_unused(c, n, nu, nn - 1)))],
            out_specs=pl.BlockSpec((R, tn), lambda c, n, k, ce, cs, cr, nu: (c, n)),
        ),
        out_shape=jax.ShapeDtypeStruct((nch * R, D), F32),
        compiler_params=_cparams(3),
        name="moe_down",
    )(ce, cs, cr, nu, hid, w_down)


def _combine_kernel(n_prompt_blocks, p1_ref, p2_ref, n1_ref, n2_ref, x_ref, g1_ref, g2_ref,
                    fg_ref, ys_hbm, op_ref, os_ref, buf_ref, sem):
    i = pl.program_id(0)
    nsteps = pl.num_programs(0)
    n = x_ref.shape[0]
    slot = i % 2

    def copies(a_ref, b_ref, r, s):
        return (pltpu.make_async_copy(ys_hbm.at[pl.ds(a_ref[0, 0, r], 1)],
                                      buf_ref.at[s, 0, pl.ds(r, 1)], sem.at[s]),
                pltpu.make_async_copy(ys_hbm.at[pl.ds(b_ref[0, 0, r], 1)],
                                      buf_ref.at[s, 1, pl.ds(r, 1)], sem.at[s]))

    def start_rows(a_ref, b_ref, s):
        def body(r, carry):
            ca, cb = copies(a_ref, b_ref, r, s)
            ca.start()
            cb.start()
            return carry

        lax.fori_loop(0, n, body, 0, unroll=8)

    @pl.when(i == 0)
    def _():
        start_rows(p1_ref, p2_ref, 0)

    @pl.when(i + 1 < nsteps)
    def _():
        start_rows(n1_ref, n2_ref, 1 - slot)

    pltpu.make_async_copy(buf_ref.at[slot], buf_ref.at[slot], sem.at[slot]).wait()
    y = x_ref[...] + (g1_ref[...] * buf_ref[slot, 0] + g2_ref[...] * buf_ref[slot, 1])
    yn = _rms(y, fg_ref[...])

    @pl.when(i < n_prompt_blocks)
    def _():
        op_ref[...] = yn

    @pl.when(i >= n_prompt_blocks)
    def _():
        os_ref[...] = yn


def _combine(x, ys, pos1, pos2, g1, g2, final_g, n_prompt):
    T, D = x.shape
    tr = T - n_prompt
    assert n_prompt % tr == 0
    nb = T // tr
    npb = n_prompt // tr
    cur = pl.BlockSpec((1, 1, tr), lambda i: (i, 0, 0), memory_space=pltpu.SMEM)
    nxt = pl.BlockSpec((1, 1, tr), lambda i: (jnp.minimum(i + 1, nb - 1), 0, 0),
                       memory_space=pltpu.SMEM)
    p1 = pos1.reshape(nb, 1, tr)
    p2 = pos2.reshape(nb, 1, tr)
    return pl.pallas_call(
        functools.partial(_combine_kernel, npb),
        grid_spec=pltpu.PrefetchScalarGridSpec(
            num_scalar_prefetch=0,
            grid=(nb,),
            in_specs=[cur, cur, nxt, nxt,
                      pl.BlockSpec((tr, D), lambda i: (i, 0)),
                      pl.BlockSpec((tr, 1), lambda i: (i, 0)),
                      pl.BlockSpec((tr, 1), lambda i: (i, 0)),
                      pl.BlockSpec((1, D), lambda i: (0, 0)),
                      pl.BlockSpec(memory_space=pl.ANY)],
            out_specs=[pl.BlockSpec((tr, D), lambda i: (jnp.minimum(i, npb - 1), 0)),
                       pl.BlockSpec((tr, D), lambda i: (0, 0))],
            scratch_shapes=[pltpu.VMEM((2, 2, tr, D), F32), pltpu.SemaphoreType.DMA((2,))],
        ),
        out_shape=[jax.ShapeDtypeStruct((n_prompt, D), F32),
                   jax.ShapeDtypeStruct((tr, D), F32)],
        compiler_params=_cparams(1),
        name="moe_combine_final_norm",
    )(p1, p2, p1, p2, x, g1, g2, final_g.reshape(1, D), ys)


def _moe_plan(idx, T):
    M = T * TOP_K
    R = -(-int(M / N_EXPERTS * 1.1) // MOE_SUB) * MOE_SUB
    nch = -(-M // R) + N_EXPERTS
    flat_e = idx.reshape(M)
    onehot = (flat_e[:, None] == jnp.arange(N_EXPERTS, dtype=jnp.int32)[None, :]).astype(jnp.int32)
    csum = jnp.cumsum(onehot, axis=0)
    rank = jnp.sum((csum - onehot) * onehot, axis=1)
    counts = csum[-1]
    e_chunks = (counts + R - 1) // R
    e_first = jnp.cumsum(e_chunks) - e_chunks
    n_used = jnp.sum(e_chunks).astype(jnp.int32)
    slot = (e_first[flat_e] + rank // R) * R + rank % R
    flat_tok = jnp.arange(M, dtype=jnp.int32) // TOP_K
    slot_tok = jnp.zeros((nch * R,), jnp.int32).at[slot].set(flat_tok)
    cidx = jnp.arange(nch, dtype=jnp.int32)
    c_e = jnp.clip(jnp.searchsorted(jnp.cumsum(e_chunks), cidx, side='right'), 0, N_EXPERTS - 1).astype(jnp.int32)
    c_rows = jnp.clip(counts[c_e] - (cidx - e_first[c_e]) * R, 0, R).astype(jnp.int32)
    used = cidx < n_used
    last = jnp.maximum(n_used - 1, 0)
    c_src = jnp.where(used, cidx, last).astype(jnp.int32)
    c_e = jnp.where(used, c_e, c_e[last]).astype(jnp.int32)
    c_rows = jnp.where(used, c_rows, 0).astype(jnp.int32)
    spc = R // MOE_SUB
    sub = jnp.arange(nch * spc, dtype=jnp.int32)
    sub_used = ((sub % spc) * MOE_SUB < c_rows[sub // spc]).astype(jnp.int32)
    pos = slot.reshape(T, TOP_K).astype(jnp.int32)
    meta = (c_e, c_src, c_rows, n_used.reshape(1))
    return R, slot_tok, sub_used, meta, pos[:, 0], pos[:, 1]


def kernel(x_prompt, x_sample, state_gla, state_conv, ev_norm_mix, ev_w_in, ev_sgu_norm, ev_sgu_w, ev_sgu_b, ev_gla_w_alpha, ev_gla_b_alpha, ev_gla_norm, ev_w_out, ev_norm_ffn, ev_ffn_w_gu, ev_ffn_w_down, od_norm_mix, od_conv_w_in, od_conv_w, od_conv_w_out, od_norm_ffn, od_router_w, od_moe_w_gu, od_moe_w_down, final_norm):
    nb, seq, D = x_prompt.shape
    B = x_sample.shape[0]
    assert x_sample.shape[1] == 1
    Tp = nb * seq
    T = Tp + B
    xp0 = x_prompt.reshape(Tp, D)
    xs0 = x_sample.reshape(B, D)

    sw = ev_sgu_norm.shape[1]
    kw = ev_gla_w_alpha.shape[2]
    rank = ev_gla_w_alpha.shape[1]
    vw = GLA_HEADS * ev_gla_norm.shape[1]
    n_main = 2 * sw + 2 * kw + 2 * vw
    assert ev_w_in.shape[2] == n_main + rank
    tm = _pick_tile(T, 1664)

    h = _norm_bf16_stacked(xp0, xs0, ev_norm_mix[0])
    P = _matmul(h, ev_w_in[0], n_main, tm, 512, name="mix_in_proj")
    log_a = _log_decay(h, ev_w_in[0][:, n_main:], ev_gla_w_alpha[0], ev_gla_b_alpha[0], tm)
    a_out, v_sample = _sgu(P, Tp, B, sw, ev_sgu_norm[0], ev_sgu_w[0], ev_sgu_b[0])
    col_q = 2 * sw
    o_p, gla_p = _gla_prompt(P, log_a, nb, seq, col_q, kw, vw, ev_gla_norm[0])
    Ps = P[Tp:]
    q_s = Ps[:, col_q:col_q + kw]
    k_s = Ps[:, col_q + kw:col_q + 2 * kw]
    v_s = Ps[:, col_q + 2 * kw:col_q + 2 * kw + vw]
    r_s = Ps[:, col_q + 2 * kw + vw:]
    gla_s, o_s = _gla_sample(state_gla[0], log_a[Tp:], k_s, q_s, v_s, r_s, ev_gla_norm[0])
    x1, h = _mix_out(a_out, o_p, o_s.reshape(B, vw).astype(BF16), xp0, xs0, ev_w_out[0],
                     ev_norm_ffn[0])

    hid = _swiglu_up(h, ev_ffn_w_gu[0], tm, 512)
    x2 = _matmul(hid, ev_ffn_w_down[0], D, _pick_tile(T, 832), 256, res=x1, name="ffn_down")

    h = _norm_bf16(x2, od_norm_mix[0])
    bgate, z = _conv_in(h, od_conv_w_in[0], tm, 256)
    x3, idx, gates = _conv_out_route(z, bgate, od_conv_w[0], state_conv[0][:, 0], state_conv[0][:, 1],
                                     x2, od_conv_w_out[0], od_norm_ffn[0], od_router_w[0], Tp, seq)
    conv_p = jnp.stack([z[(n + 1) * seq - 2:(n + 1) * seq] for n in range(nb)])
    conv_s = jnp.concatenate([state_conv[0][:, 1:], z[Tp:].reshape(B, 1, D)], axis=1)

    R, slot_tok, sub_used, meta, pos1, pos2 = _moe_plan(idx[:, :TOP_K], T)
    xs = _dispatch(x3, od_norm_ffn[0], slot_tok, sub_used)
    FE = od_moe_w_down.shape[2]
    hid = _moe_up(xs, od_moe_w_gu[0], meta, R, _pick_tile(FE, 512, LANES))
    ys = _moe_down(hid, od_moe_w_down[0], meta, R, 1024, _pick_tile(FE, 1024, LANES))
    y_p, y_s = _combine(x3, ys, pos1, pos2, gates[:, 0:1], gates[:, 1:2], final_norm, Tp)

    y_prompt = y_p.reshape(nb, seq, D)
    y_sample = y_s.reshape(B, 1, D)
    return (y_prompt, y_sample, gla_p[None], gla_s[None],
            v_sample.reshape(1, B, 1, sw), conv_p[None], conv_s[None])
```

```python
import functools
import math

import numpy as np
import jax
import jax.numpy as jnp
from jax import lax
from jax.experimental import pallas as pl
from jax.experimental.pallas import tpu as pltpu

F32 = jnp.float32
BF16 = jnp.bfloat16
EPS = 1e-6

SGU_CHUNK = 128
SGU_GROUPS = 4
GLA_HEADS = 4
GLA_TAU = 16.0
GLA_CHUNK = 128
N_EXPERTS = 8
TOP_K = 2
MOE_SUB = 256
MOE_BIG = 1024
MOE_SMALL = 128
LANES = 128
VMEM_LIMIT = 52 * 1024 * 1024


def _cparams(n_axes, vmem=VMEM_LIMIT):
    return pltpu.CompilerParams(dimension_semantics=("arbitrary",) * n_axes,
                                vmem_limit_bytes=vmem)


def _pick_tile(n, cap, mult=16):
    best = None
    for t in range(mult, min(n, cap) + 1, mult):
        if n % t == 0:
            best = t
    assert best is not None, (n, cap, mult)
    return best


def _rms(x, g):
    ms = jnp.mean(x * x, axis=-1, keepdims=True)
    return (x * lax.rsqrt(ms + EPS)) * g


def _gelu(x):
    c = math.sqrt(2.0 / math.pi)
    return x * (0.5 * (1.0 + jnp.tanh(c * (x + 0.044715 * (x * x * x)))))


def _silu(x):
    return x * (1.0 / (1.0 + jnp.exp(-x)))


def _dot(a, b):
    return jnp.dot(a, b, preferred_element_type=F32)


def _dot_nt(a, b):
    return lax.dot_general(a, b, (((1,), (1,)), ((), ())), preferred_element_type=F32)


def _norm_kernel(x_ref, g_ref, o_ref):
    o_ref[...] = _rms(x_ref[...], g_ref[...]).astype(BF16)


def _norm_bf16(x, g):
    T, D = x.shape
    tr = _pick_tile(T, 512)
    return pl.pallas_call(
        _norm_kernel,
        grid=(T // tr,),
        in_specs=[pl.BlockSpec((tr, D), lambda i: (i, 0)),
                  pl.BlockSpec((1, D), lambda i: (0, 0))],
        out_specs=pl.BlockSpec((tr, D), lambda i: (i, 0)),
        out_shape=jax.ShapeDtypeStruct((T, D), BF16),
        compiler_params=_cparams(1),
        name="rmsnorm_bf16",
    )(x, g.reshape(1, D))


def _norm2_kernel(n_prompt_blocks, xp_ref, xs_ref, g_ref, o_ref):
    i = pl.program_id(0)

    @pl.when(i < n_prompt_blocks)
    def _():
        o_ref[...] = _rms(xp_ref[...], g_ref[...]).astype(BF16)

    @pl.when(i >= n_prompt_blocks)
    def _():
        o_ref[...] = _rms(xs_ref[...], g_ref[...]).astype(BF16)


def _norm_bf16_stacked(xp, xs, g):
    Tp, D = xp.shape
    B = xs.shape[0]
    tr = B
    assert Tp % tr == 0
    npb = Tp // tr
    return pl.pallas_call(
        functools.partial(_norm2_kernel, npb),
        grid=(npb + 1,),
        in_specs=[pl.BlockSpec((tr, D), lambda i: (jnp.minimum(i, npb - 1), 0)),
                  pl.BlockSpec((tr, D), lambda i: (0, 0)),
                  pl.BlockSpec((1, D), lambda i: (0, 0))],
        out_specs=pl.BlockSpec((tr, D), lambda i: (i, 0)),
        out_shape=jax.ShapeDtypeStruct((Tp + B, D), BF16),
        compiler_params=_cparams(1),
        name="rmsnorm_stack_bf16",
    )(xp, xs, g.reshape(1, D))


def _mm_kernel(x_ref, w_ref, o_ref):
    o_ref[...] = _dot(x_ref[...], w_ref[...].astype(BF16)).astype(o_ref.dtype)


def _mm_res_kernel(x_ref, w_ref, r_ref, o_ref):
    o_ref[...] = r_ref[...] + _dot(x_ref[...], w_ref[...].astype(BF16))


def _matmul(x, w, n_cols, tm, tn, res=None, name="matmul"):
    T, K = x.shape
    grid = (T // tm, n_cols // tn)
    in_specs = [pl.BlockSpec((tm, K), lambda i, j: (i, 0)),
                pl.BlockSpec((K, tn), lambda i, j: (0, j))]
    args = [x, w]
    kern = _mm_kernel
    if res is not None:
        in_specs.append(pl.BlockSpec((tm, tn), lambda i, j: (i, j)))
        args.append(res)
        kern = _mm_res_kernel
    return pl.pallas_call(
        kern, grid=grid, in_specs=in_specs,
        out_specs=pl.BlockSpec((tm, tn), lambda i, j: (i, j)),
        out_shape=jax.ShapeDtypeStruct((T, n_cols), F32),
        compiler_params=_cparams(2),
        name=name,
    )(*args)


def _mix_out_kernel(n_full, n_tail, a_ref, op_ref, os_ref, rp_ref, rs_ref, w_ref, ng_ref,
                    x_ref, h_ref):
    i = pl.program_id(0)
    sw = a_ref.shape[1]

    def finish(rows, a, o, res):
        y = res + _dot(a, w_ref[0:sw, :]) + _dot(o, w_ref[sw:, :])
        x_ref[0:rows, :] = y
        h_ref[0:rows, :] = _rms(y, ng_ref[...]).astype(BF16)

    @pl.when(i < n_full)
    def _():
        finish(a_ref.shape[0], a_ref[...], op_ref[...], rp_ref[...])

    @pl.when(i >= n_full)
    def _():
        finish(n_tail, a_ref[0:n_tail, :], os_ref[...], rs_ref[...])


def _mix_out(a, o_p, o_s, res_p, res_s, w, norm_g):
    T, sw = a.shape
    Tp, vw = o_p.shape
    B = o_s.shape[0]
    D = w.shape[1]
    assert T == Tp + B
    tm = _pick_tile(Tp, 256)
    assert B <= tm
    n_full = Tp // tm
    clamp = lambda i: (jnp.minimum(i, n_full - 1), 0)
    row = lambda i: (i, 0)
    fixed = lambda i: (0, 0)
    return pl.pallas_call(
        functools.partial(_mix_out_kernel, n_full, B),
        grid=(n_full + 1,),
        in_specs=[pl.BlockSpec((tm, sw), row),
                  pl.BlockSpec((tm, vw), clamp),
                  pl.BlockSpec((B, vw), fixed),
                  pl.BlockSpec((tm, D), clamp),
                  pl.BlockSpec((B, D), fixed),
                  pl.BlockSpec((sw + vw, D), fixed),
                  pl.BlockSpec((1, D), fixed)],
        out_specs=[pl.BlockSpec((tm, D), row),
                   pl.BlockSpec((tm, D), row)],
        out_shape=[jax.ShapeDtypeStruct((T, D), F32),
                   jax.ShapeDtypeStruct((T, D), BF16)],
        compiler_params=_cparams(1),
        name="mix_out_proj",
    )(a, o_p, o_s, res_p, res_s, w.astype(BF16), norm_g.reshape(1, D))


def _swiglu_kernel(x_ref, wg_ref, wu_ref, o_ref):
    x = x_ref[...]
    g = _dot(x, wg_ref[...].astype(BF16))
    u = _dot(x, wu_ref[...].astype(BF16))
    o_ref[...] = (_silu(g) * u).astype(o_ref.dtype)


def _swiglu_up(x, w_gu, tm, tn):
    T, K = x.shape
    F = w_gu.shape[1] // 2
    nj = F // tn
    return pl.pallas_call(
        _swiglu_kernel,
        grid=(T // tm, nj),
        in_specs=[pl.BlockSpec((tm, K), lambda i, j: (i, 0)),
                  pl.BlockSpec((K, tn), lambda i, j: (0, j)),
                  pl.BlockSpec((K, tn), lambda i, j: (0, j + nj))],
        out_specs=pl.BlockSpec((tm, tn), lambda i, j: (i, j)),
        out_shape=jax.ShapeDtypeStruct((T, F), BF16),
        compiler_params=_cparams(2),
        name="ffn_gate_up",
    )(x, w_gu, w_gu)


def _conv_in_kernel(x_ref, wb_ref, wc_ref, wu_ref, b_ref, z_ref):
    x = x_ref[...]
    b_ref[...] = _dot(x, wb_ref[...].astype(BF16))
    c = _dot(x, wc_ref[...].astype(BF16))
    u = _dot(x, wu_ref[...].astype(BF16))
    z_ref[...] = c * u


def _conv_in(x, w, tm, tn):
    T, K = x.shape
    C = w.shape[1] // 3
    nj = C // tn
    return pl.pallas_call(
        _conv_in_kernel,
        grid=(T // tm, nj),
        in_specs=[pl.BlockSpec((tm, K), lambda i, j: (i, 0)),
                  pl.BlockSpec((K, tn), lambda i, j: (0, j)),
                  pl.BlockSpec((K, tn), lambda i, j: (0, j + nj)),
                  pl.BlockSpec((K, tn), lambda i, j: (0, j + 2 * nj))],
        out_specs=[pl.BlockSpec((tm, tn), lambda i, j: (i, j)),
                   pl.BlockSpec((tm, tn), lambda i, j: (i, j))],
        out_shape=[jax.ShapeDtypeStruct((T, C), F32),
                   jax.ShapeDtypeStruct((T, C), F32)],
        compiler_params=_cparams(2),
        name="conv_in_proj",
    )(x, w, w, w)


def _loga_kernel(x_ref, wl_ref, wa_ref, b_ref, o_ref):
    g = _dot(x_ref[...], wl_ref[...].astype(BF16))
    z = _dot(g.astype(BF16), wa_ref[...].astype(BF16)) + b_ref[...]
    ls = jnp.minimum(z, 0.0) - jnp.log1p(jnp.exp(-jnp.abs(z)))
    o_ref[...] = ls * (1.0 / GLA_TAU)


def _log_decay(x, w_lr, w_alpha, b_alpha, tm):
    T, K = x.shape
    R, KW = w_alpha.shape
    return pl.pallas_call(
        _loga_kernel,
        grid=(T // tm,),
        in_specs=[pl.BlockSpec((tm, K), lambda i: (i, 0)),
                  pl.BlockSpec((K, R), lambda i: (0, 0)),
                  pl.BlockSpec((R, KW), lambda i: (0, 0)),
                  pl.BlockSpec((1, KW), lambda i: (0, 0))],
        out_specs=pl.BlockSpec((tm, KW), lambda i: (i, 0)),
        out_shape=jax.ShapeDtypeStruct((T, KW), F32),
        compiler_params=_cparams(1),
        name="gla_log_decay",
    )(x, w_lr, w_alpha, b_alpha.reshape(1, KW))


def _sgu_kernel(n_prompt_blocks, u_ref, v_ref, ng_ref, w_ref, bt_ref, o_ref, vs_ref):
    i = pl.program_id(0)
    is_sample = i >= n_prompt_blocks
    C = SGU_CHUNK
    gu = _gelu(u_ref[...])
    vn = _rms(_gelu(v_ref[...]), ng_ref[...])

    @pl.when(is_sample)
    def _():
        vs_ref[...] = vn

    gd = vn.shape[1] // SGU_GROUPS
    row = lax.broadcasted_iota(jnp.int32, (C, C), 0)
    col = lax.broadcasted_iota(jnp.int32, (C, C), 1)
    vb = vn.astype(BF16)
    for g in range(SGU_GROUPS):
        w = w_ref[g]
        w_prompt = jnp.where(col <= row, w, 0.0)
        w_sample = jnp.where(col == row, w_ref[g, 0:1, 0:1], 0.0)
        w_eff = jnp.where(is_sample, w_sample, w_prompt).astype(BF16)
        bias = jnp.where(is_sample, bt_ref[0:1, g:g + 1], bt_ref[:, g:g + 1])
        mixed = _dot(w_eff, vb[:, g * gd:(g + 1) * gd]) + bias
        o_ref[:, g * gd:(g + 1) * gd] = (gu[:, g * gd:(g + 1) * gd] * mixed).astype(o_ref.dtype)


def _sgu(P, n_prompt, n_sample, width, sgu_norm, sgu_w, sgu_b):
    C = SGU_CHUNK
    assert n_prompt % C == 0 and n_sample == C
    T = n_prompt + n_sample
    npb = n_prompt // C
    return pl.pallas_call(
        functools.partial(_sgu_kernel, npb),
        grid=(T // C,),
        in_specs=[pl.BlockSpec((C, width), lambda i: (i, 0)),
                  pl.BlockSpec((C, width), lambda i: (i, 1)),
                  pl.BlockSpec((1, width), lambda i: (0, 0)),
                  pl.BlockSpec((SGU_GROUPS, C, C), lambda i: (0, 0, 0)),
                  pl.BlockSpec((C, SGU_GROUPS), lambda i: (0, 0))],
        out_specs=[pl.BlockSpec((C, width), lambda i: (i, 0)),
                   pl.BlockSpec((C, width), lambda i: (0, 0))],
        out_shape=[jax.ShapeDtypeStruct((T, width), BF16),
                   jax.ShapeDtypeStruct((n_sample, width), F32)],
        compiler_params=_cparams(1),
        name="sgu_mix",
    )(P, P, sgu_norm.reshape(1, width), sgu_w, sgu_b.T)


def _gla_constants():
    C = GLA_CHUNK
    s = np.arange(C)[None, :]
    i = np.arange(C)[:, None]
    gs = [s <= i, s > i]
    masks = []
    b = C // 2
    while b >= 1:
        upper = (i % (2 * b)) >= b
        m = (i // (2 * b)) * (2 * b) + b - 1
        gs.append(np.where(upper, (s > m) & (s <= i), (s > i) & (s <= m)))
        same = (i // (2 * b)) == (s // (2 * b))
        masks.append(upper & same & ~upper.T)
        b //= 2
    masks.append(np.eye(C, dtype=bool))
    G = np.concatenate(gs, axis=0).astype(np.float32)
    M = np.stack(masks).astype(np.float32)
    return jnp.asarray(G, BF16), jnp.asarray(M, F32)


def _gla_kernel(q_ref, k_ref, v_ref, r_ref, la_ref, G_ref, M_ref, gn_ref,
                o_ref, s_out_ref, st_ref):
    c = pl.program_id(1)
    C = GLA_CHUNK
    dk = q_ref.shape[1] // GLA_HEADS
    dv = v_ref.shape[1] // GLA_HEADS
    nlev = M_ref.shape[0] - 1

    @pl.when(c == 0)
    def _():
        st_ref[...] = jnp.zeros_like(st_ref)

    la = la_ref[...]
    hi = la.astype(BF16)
    r1 = la - hi.astype(F32)
    mid = r1.astype(BF16)
    lo = (r1 - mid.astype(F32)).astype(BF16)
    G = G_ref[...]
    X = jnp.exp(_dot(G, hi) + _dot(G, mid) + _dot(G, lo))

    scale = dk ** -0.5
    for h in range(GLA_HEADS):
        Xh = X[:, h * dk:(h + 1) * dk]
        qh = q_ref[:, h * dk:(h + 1) * dk] * scale
        kh = k_ref[:, h * dk:(h + 1) * dk]
        vh = v_ref[:, h * dv:(h + 1) * dv]
        x_cum = Xh[0:C]
        x_rest = Xh[C:2 * C]
        st_old = st_ref[h]
        o = _dot_nt((qh * x_cum).astype(BF16), st_old.astype(BF16))
        qb = qh.astype(BF16)
        kb = kh.astype(BF16)
        att = _dot_nt(qb, kb) * M_ref[nlev]
        for l in range(nlev):
            xl = Xh[(2 + l) * C:(3 + l) * C]
            att = att + _dot_nt((qh * xl).astype(BF16), (kh * xl).astype(BF16)) * M_ref[l]
        vb = vh.astype(BF16)
        o = o + _dot(att.astype(BF16), vb)
        vt = vh.T.astype(BF16)
        st_new = st_old * x_cum[C - 1:C, :] + _dot(vt, (kh * x_rest).astype(BF16))
        st_ref[h] = st_new
        on = _rms(o, gn_ref[...]) * _silu(r_ref[:, h * dv:(h + 1) * dv])
        o_ref[:, h * dv:(h + 1) * dv] = on.astype(o_ref.dtype)

    @pl.when(c == pl.num_programs(1) - 1)
    def _():
        for h in range(GLA_HEADS):
            s_out_ref[0, h] = st_ref[h].T


def _gla_prompt(P, log_a, n_seq, seq, col_q, kw, vw, gla_norm):
    C = GLA_CHUNK
    assert seq % C == 0
    nc = seq // C
    dk = kw // GLA_HEADS
    dv = vw // GLA_HEADS
    qb, kb = col_q // kw, col_q // kw + 1
    vb = (col_q + 2 * kw) // vw
    rb = vb + 1
    assert col_q % kw == 0 and (col_q + 2 * kw) % vw == 0
    G, M = _gla_constants()
    rows = lambda n, c: n * nc + c
    return pl.pallas_call(
        _gla_kernel,
        grid=(n_seq, nc),
        in_specs=[pl.BlockSpec((C, kw), lambda n, c: (rows(n, c), qb)),
                  pl.BlockSpec((C, kw), lambda n, c: (rows(n, c), kb)),
                  pl.BlockSpec((C, vw), lambda n, c: (rows(n, c), vb)),
                  pl.BlockSpec((C, vw), lambda n, c: (rows(n, c), rb)),
                  pl.BlockSpec((C, kw), lambda n, c: (rows(n, c), 0)),
                  pl.BlockSpec(G.shape, lambda n, c: (0, 0)),
                  pl.BlockSpec(M.shape, lambda n, c: (0, 0, 0)),
                  pl.BlockSpec((1, dv), lambda n, c: (0, 0))],
        out_specs=[pl.BlockSpec((C, vw), lambda n, c: (rows(n, c), 0)),
                   pl.BlockSpec((1, GLA_HEADS, dk, dv), lambda n, c: (n, 0, 0, 0))],
        out_shape=[jax.ShapeDtypeStruct((n_seq * seq, vw), BF16),
                   jax.ShapeDtypeStruct((n_seq, GLA_HEADS, dk, dv), F32)],
        scratch_shapes=[pltpu.VMEM((GLA_HEADS, dv, dk), F32)],
        compiler_params=_cparams(2),
        name="gla_prompt",
    )(P, P, P, P, log_a, G, M, gla_norm.reshape(1, dv))


def _gla_step_kernel(s_ref, at_ref, kt_ref, qt_ref, v_ref, r_ref, gn_ref, sn_ref, o_ref):
    i = pl.program_id(0)
    bb, H, dk, dv = s_ref.shape
    B = at_ref.shape[1]
    lane = lax.broadcasted_iota(jnp.int32, (dk, B), 1)
    scale = dk ** -0.5
    for j in range(bb):
        pick = lane == i * bb + j

        def column(t_ref, h):
            return jnp.sum(jnp.where(pick, t_ref[h * dk:(h + 1) * dk, :], 0.0), axis=1, keepdims=True)

        for h in range(H):
            s_new = jnp.exp(column(at_ref, h)) * s_ref[j, h] + column(kt_ref, h) * v_ref[j, h]
            sn_ref[j, h] = s_new
            o = jnp.sum((column(qt_ref, h) * scale) * s_new, axis=0, keepdims=True)
            o_ref[j, h] = _rms(o, gn_ref[...]) * _silu(r_ref[j, h])


def _gla_sample(state, la, k, q, v, r, gla_norm):
    B, H, dk, dv = state.shape
    bb = 8
    rowv = lambda t: t.reshape(B, H, 1, dv)
    tspec = pl.BlockSpec((H * dk, B), lambda i: (0, 0))
    rspec = pl.BlockSpec((bb, H, 1, dv), lambda i: (i, 0, 0, 0))
    sspec = pl.BlockSpec((bb, H, dk, dv), lambda i: (i, 0, 0, 0))
    return pl.pallas_call(
        _gla_step_kernel,
        grid=(B // bb,),
        in_specs=[sspec, tspec, tspec, tspec, rspec, rspec,
                  pl.BlockSpec((1, dv), lambda i: (0, 0))],
        out_specs=[sspec, rspec],
        out_shape=[jax.ShapeDtypeStruct((B, H, dk, dv), F32),
                   jax.ShapeDtypeStruct((B, H, 1, dv), F32)],
        compiler_params=_cparams(1),
        name="gla_sample_step",
    )(state, la.T, k.T, q.T, rowv(v), rowv(r), gla_norm.reshape(1, dv))


def _route(logits):
    lane = lax.broadcasted_iota(jnp.int32, logits.shape, 1)
    neg = jnp.float32(-jnp.inf)
    logits = jnp.where(lane < N_EXPERTS, logits, neg)
    v1 = jnp.max(logits, axis=-1, keepdims=True)
    i1 = jnp.min(jnp.where(logits == v1, lane, LANES), axis=-1, keepdims=True)
    rest = jnp.where(lane == i1, neg, logits)
    v2 = jnp.max(rest, axis=-1, keepdims=True)
    i2 = jnp.min(jnp.where(rest == v2, lane, LANES), axis=-1, keepdims=True)
    e2 = jnp.exp(v2 - v1)
    g1 = 1.0 / (1.0 + e2)
    g2 = e2 / (1.0 + e2)
    idx = jnp.where(lane == 0, i1, jnp.where(lane == 1, i2, 0))
    gate = jnp.where(lane == 0, g1, jnp.where(lane == 1, g2, 0.0))
    return idx, gate


def _conv_out_kernel(n_prompt_blocks, blocks_per_seq, n_tail, z_ref, halo_ref, b_ref, cw_ref,
                     h0_ref, h1_ref, res_ref, w_ref, ng_ref, rw_ref, x_ref, xslab_ref, idx_ref,
                     gate_ref):
    i = pl.program_id(0)
    w0, w1, w2 = cw_ref[0:1, :], cw_ref[1:2, :], cw_ref[2:3, :]
    ns = x_ref.shape[1] // LANES

    def finish(rows, yc, res):
        y = res + _dot(yc.astype(BF16), w_ref[...])
        x_ref[0:rows, :] = y
        for s in range(ns):
            xslab_ref[pl.ds(s, rows, stride=ns), :] = y[:, s * LANES:(s + 1) * LANES]
        hn = _rms(y, ng_ref[...])
        h_hi = hn.astype(BF16)
        h_lo = (hn - h_hi.astype(F32)).astype(BF16)
        rw = rw_ref[...]
        r_hi = rw.astype(BF16)
        r_lo = (rw - r_hi.astype(F32)).astype(BF16)
        logits = _dot(h_hi, r_hi) + (_dot(h_hi, r_lo) + _dot(h_lo, r_hi))
        idx, gate = _route(logits)
        idx_ref[0:rows, :] = idx
        gate_ref[0:rows, :] = gate

    @pl.when(i < n_prompt_blocks)
    def _():
        z = z_ref[...]
        row = lax.broadcasted_iota(jnp.int32, z.shape, 0)
        halo = jnp.where(i % blocks_per_seq == 0, 0.0, halo_ref[...])
        z1 = jnp.where(row == 0, halo[7:8, :], pltpu.roll(z, 1, 0))
        z2 = pltpu.roll(z, 2, 0)
        z2 = jnp.where(row == 0, halo[6:7, :], jnp.where(row == 1, halo[7:8, :], z2))
        finish(z.shape[0], b_ref[...] * (w0 * z2 + w1 * z1 + w2 * z), res_ref[...])

    @pl.when(i >= n_prompt_blocks)
    def _():
        conv = w0 * h0_ref[...] + w1 * h1_ref[...] + w2 * z_ref[0:n_tail, :]
        finish(n_tail, b_ref[0:n_tail, :] * conv, res_ref[0:n_tail, :])


def _conv_out_route(z, bgate, conv_w, hist0, hist1, res, w_out, norm_g, router_w, n_prompt, seq):
    T, Cd = z.shape
    D = w_out.shape[1]
    B = T - n_prompt
    tm = _pick_tile(math.gcd(seq, n_prompt), 256)
    assert B <= tm and hist0.shape == (B, Cd)
    npb = n_prompt // tm
    wpad = jnp.zeros((D, LANES), F32).at[:, :N_EXPERTS].set(router_w)
    row = lambda i: (i, 0)
    fixed = lambda i: (0, 0)
    return pl.pallas_call(
        functools.partial(_conv_out_kernel, npb, seq // tm, B),
        grid=(npb + 1,),
        in_specs=[pl.BlockSpec((tm, Cd), row),
                  pl.BlockSpec((8, Cd), lambda i: (jnp.maximum(i * (tm // 8) - 1, 0), 0)),
                  pl.BlockSpec((tm, Cd), row),
                  pl.BlockSpec(conv_w.shape, fixed),
                  pl.BlockSpec((B, Cd), fixed),
                  pl.BlockSpec((B, Cd), fixed),
                  pl.BlockSpec((tm, D), row),
                  pl.BlockSpec((Cd, D), fixed),
                  pl.BlockSpec((1, D), fixed),
                  pl.BlockSpec((D, LANES), fixed)],
        out_specs=[pl.BlockSpec((tm, D), row),
                   pl.BlockSpec((tm * (D // LANES), LANES), row),
                   pl.BlockSpec((tm, LANES), row),
                   pl.BlockSpec((tm, LANES), row)],
        out_shape=[jax.ShapeDtypeStruct((T, D), F32),
                   jax.ShapeDtypeStruct((T * (D // LANES), LANES), F32),
                   jax.ShapeDtypeStruct((T, LANES), jnp.int32),
                   jax.ShapeDtypeStruct((T, LANES), F32)],
        compiler_params=_cparams(1),
        name="conv_out_route",
    )(z, z, bgate, conv_w, hist0, hist1, res, w_out.astype(BF16), norm_g.reshape(1, D), wpad)


def _dispatch_kernel(tok_ref, nxt_ref, used_ref, x_hbm, g_ref, o_ref, buf_ref, sem):
    i = pl.program_id(0)
    nsteps = pl.num_programs(0)
    n = o_ref.shape[0]
    ns = x_hbm.shape[1]
    slot = i % 2

    def row_copy(t_ref, r, s):
        return pltpu.make_async_copy(x_hbm.at[t_ref[0, 0, r]],
                                     buf_ref.at[s, pl.ds(pl.multiple_of(r * ns, ns), ns)], sem.at[s])

    def start_rows(t_ref, s):
        def body(r, carry):
            row_copy(t_ref, r, s).start()
            return carry

        lax.fori_loop(0, n, body, 0, unroll=8)

    @pl.when(jnp.logical_and(i == 0, used_ref[0] > 0))
    def _():
        start_rows(tok_ref, 0)

    @pl.when(jnp.logical_and(i + 1 < nsteps, used_ref[jnp.minimum(i + 1, nsteps - 1)] > 0))
    def _():
        start_rows(nxt_ref, 1 - slot)

    @pl.when(used_ref[i] > 0)
    def _():
        pltpu.make_async_copy(buf_ref.at[slot], buf_ref.at[slot], sem.at[slot]).wait()

        def lane_block(s):
            return buf_ref[slot, pl.ds(s, n, stride=ns), :]

        ssq = jnp.zeros((n, LANES), F32)
        for s in range(ns):
            v = lane_block(s)
            ssq = ssq + v * v
        inv = lax.rsqrt(jnp.sum(ssq, axis=-1, keepdims=True) * (1.0 / (ns * LANES)) + EPS)
        for s in range(ns):
            cols = slice(s * LANES, (s + 1) * LANES)
            o_ref[:, cols] = ((lane_block(s) * inv) * g_ref[:, cols]).astype(o_ref.dtype)

    @pl.when(used_ref[i] == 0)
    def _():
        o_ref[...] = jnp.zeros_like(o_ref)


def _dispatch(x_slab, g, slot_tok, sub_used):
    D = x_slab.shape[1] * x_slab.shape[2]
    nsub = sub_used.shape[0]
    SB = MOE_SUB
    tok3 = slot_tok.reshape(nsub, 1, SB)
    return pl.pallas_call(
        _dispatch_kernel,
        grid_spec=pltpu.PrefetchScalarGridSpec(
            num_scalar_prefetch=0,
            grid=(nsub,),
            in_specs=[pl.BlockSpec((1, 1, SB), lambda i: (i, 0, 0), memory_space=pltpu.SMEM),
                      pl.BlockSpec((1, 1, SB), lambda i: (jnp.minimum(i + 1, nsub - 1), 0, 0),
                                   memory_space=pltpu.SMEM),
                      pl.BlockSpec(memory_space=pltpu.SMEM),
                      pl.BlockSpec(memory_space=pl.ANY),
                      pl.BlockSpec((1, D), lambda i: (0, 0))],
            out_specs=pl.BlockSpec((SB, D), lambda i: (i, 0)),
            scratch_shapes=[pltpu.VMEM((2, SB * (D // LANES), LANES), F32),
                            pltpu.SemaphoreType.DMA((2,))],
        ),
        out_shape=jax.ShapeDtypeStruct((nsub * SB, D), BF16),
        compiler_params=_cparams(1),
        name="moe_dispatch",
    )(tok3, tok3, sub_used, x_slab, g.reshape(1, D))


def _row_pieces(rows, total, compute, skip):
    for s in range(0, total, MOE_BIG):
        size = min(MOE_BIG, total - s)
        whole = rows > s + size - MOE_SMALL

        @pl.when(whole)
        def _():
            compute(s, size)

        @pl.when(jnp.logical_not(whole))
        def _():
            for t in range(s, s + size, MOE_SMALL):
                @pl.when(rows > t)
                def _():
                    compute(t, MOE_SMALL)

                @pl.when(rows <= t)
                def _():
                    skip(t, MOE_SMALL)


def _moe_up_kernel(ce_ref, cs_ref, cr_ref, nu_ref, x_ref, wg_ref, wu_ref, o_ref):
    c = pl.program_id(0)

    @pl.when(c < nu_ref[0])
    def _():
        wg = wg_ref[0].astype(BF16)
        wu = wu_ref[0].astype(BF16)

        def compute(s, n):
            x = x_ref[pl.ds(s, n), :]
            o_ref[pl.ds(s, n), :] = (_silu(_dot(x, wg)) * _dot(x, wu)).astype(o_ref.dtype)

        def skip(s, n):
            o_ref[pl.ds(s, n), :] = jnp.zeros((n, o_ref.shape[1]), o_ref.dtype)

        _row_pieces(cr_ref[c], x_ref.shape[0], compute, skip)

    @pl.when(c >= nu_ref[0])
    def _():
        o_ref[...] = jnp.zeros_like(o_ref)


def _moe_up(xs, w_gu, meta, R, tf, direct):
    ce, cs, cr, nu = meta
    nch = ce.shape[0]
    D = xs.shape[1]
    F = w_gu.shape[2] // 2
    nf = F // tf

    def fidx(c, f, nu):
        return jnp.where(c < nu[0], f, nf - 1)

    if direct:
        in_specs = [pl.BlockSpec((R, D), lambda c, f, *_: (c, 0)),
                    pl.BlockSpec((1, D, tf), lambda c, f, *_: (c, 0, f)),
                    pl.BlockSpec((1, D, tf), lambda c, f, *_: (c, 0, nf + f))]
    else:
        in_specs = [pl.BlockSpec((R, D), lambda c, f, ce, cs, cr, nu: (cs[c], 0)),
                    pl.BlockSpec((1, D, tf), lambda c, f, ce, cs, cr, nu: (ce[c], 0, fidx(c, f, nu))),
                    pl.BlockSpec((1, D, tf), lambda c, f, ce, cs, cr, nu: (ce[c], 0, nf + fidx(c, f, nu)))]
    return pl.pallas_call(
        _moe_up_kernel,
        grid_spec=pltpu.PrefetchScalarGridSpec(
            num_scalar_prefetch=4,
            grid=(nch, nf),
            in_specs=in_specs,
            out_specs=pl.BlockSpec((R, tf), lambda c, f, ce, cs, cr, nu: (c, f)),
        ),
        out_shape=jax.ShapeDtypeStruct((nch * R, F), BF16),
        compiler_params=_cparams(2),
        name="moe_gate_up",
    )(ce, cs, cr, nu, xs, w_gu, w_gu)


def _moe_down_kernel(ce_ref, cs_ref, cr_ref, nu_ref, h_ref, w_ref, o_ref):
    c = pl.program_id(0)
    k = pl.program_id(2)

    @pl.when(c < nu_ref[0])
    def _():
        w = w_ref[0].astype(BF16)

        def compute(s, n):
            @pl.when(k == 0)
            def _():
                o_ref[pl.ds(s, n), :] = _dot(h_ref[pl.ds(s, n), :], w)

            @pl.when(k > 0)
            def _():
                o_ref[pl.ds(s, n), :] = o_ref[pl.ds(s, n), :] + _dot(h_ref[pl.ds(s, n), :], w)

        def skip(s, n):
            @pl.when(k == 0)
            def _():
                o_ref[pl.ds(s, n), :] = jnp.zeros((n, o_ref.shape[1]), F32)

        _row_pieces(cr_ref[c], h_ref.shape[0], compute, skip)

    @pl.when(jnp.logical_and(c >= nu_ref[0], k == 0))
    def _():
        o_ref[...] = jnp.zeros_like(o_ref)


def _moe_down(hid, w_down, meta, R, tn, tk, direct):
    ce, cs, cr, nu = meta
    nch = ce.shape[0]
    F = hid.shape[1]
    D = w_down.shape[2]
    nn, nk = D // tn, F // tk

    def last_if_unused(c, v, nu, last):
        return jnp.where(c < nu[0], v, last)

    if direct:
        in_specs = [pl.BlockSpec((R, tk), lambda c, n, k, *_: (c, k)),
                    pl.BlockSpec((1, tk, tn), lambda c, n, k, *_: (c, k, n))]
    else:
        in_specs = [pl.BlockSpec((R, tk), lambda c, n, k, ce, cs, cr, nu:
                                 (cs[c], last_if_unused(c, k, nu, nk - 1))),
                    pl.BlockSpec((1, tk, tn), lambda c, n, k, ce, cs, cr, nu:
                                 (ce[c], last_if_unused(c, k, nu, nk - 1),
                                  last_if_unused(c, n, nu, nn - 1)))]
    return pl.pallas_call(
        _moe_down_kernel,
        grid_spec=pltpu.PrefetchScalarGridSpec(
            num_scalar_prefetch=4,
            grid=(nch, nn, nk),
            in_specs=in_specs,
            out_specs=pl.BlockSpec((R, tn), lambda c, n, k, ce, cs, cr, nu: (c, n)),
        ),
        out_shape=jax.ShapeDtypeStruct((nch * R, D), F32),
        compiler_params=_cparams(3),
        name="moe_down",
    )(ce, cs, cr, nu, hid, w_down)


def _combine_kernel(n_prompt_blocks, has_overflow, p1_ref, p2_ref, n1_ref, n2_ref, x_ref, g1_ref,
                    g2_ref, fg_ref, ys_hbm, *rest):
    yo_hbm = rest[0] if has_overflow else None
    op_ref, os_ref, buf_ref, sem = rest[-4:]
    i = pl.program_id(0)
    nsteps = pl.num_programs(0)
    n = x_ref.shape[0]
    slot = i % 2
    split = ys_hbm.shape[0]

    def start_row(pos, k, r, s):
        dst = buf_ref.at[s, k, pl.ds(r, 1)]
        if yo_hbm is None:
            pltpu.make_async_copy(ys_hbm.at[pl.ds(pos, 1)], dst, sem.at[s]).start()
        else:
            @pl.when(pos < split)
            def _():
                pltpu.make_async_copy(ys_hbm.at[pl.ds(pos, 1)], dst, sem.at[s]).start()

            @pl.when(pos >= split)
            def _():
                pltpu.make_async_copy(yo_hbm.at[pl.ds(pos - split, 1)], dst, sem.at[s]).start()

    def start_rows(a_ref, b_ref, s):
        def body(r, carry):
            start_row(a_ref[0, 0, r], 0, r, s)
            start_row(b_ref[0, 0, r], 1, r, s)
            return carry

        lax.fori_loop(0, n, body, 0, unroll=8)

    @pl.when(i == 0)
    def _():
        start_rows(p1_ref, p2_ref, 0)

    @pl.when(i + 1 < nsteps)
    def _():
        start_rows(n1_ref, n2_ref, 1 - slot)

    pltpu.make_async_copy(buf_ref.at[slot], buf_ref.at[slot], sem.at[slot]).wait()
    y = x_ref[...] + (g1_ref[...] * buf_ref[slot, 0] + g2_ref[...] * buf_ref[slot, 1])
    yn = _rms(y, fg_ref[...])

    @pl.when(i < n_prompt_blocks)
    def _():
        op_ref[...] = yn

    @pl.when(i >= n_prompt_blocks)
    def _():
        os_ref[...] = yn


def _combine(x, ys, ys_over, pos1, pos2, g1, g2, final_g, n_prompt):
    T, D = x.shape
    srcs = [ys] if ys_over is None else [ys, ys_over]
    tr = T - n_prompt
    assert n_prompt % tr == 0
    nb = T // tr
    npb = n_prompt // tr
    cur = pl.BlockSpec((1, 1, tr), lambda i: (i, 0, 0), memory_space=pltpu.SMEM)
    nxt = pl.BlockSpec((1, 1, tr), lambda i: (jnp.minimum(i + 1, nb - 1), 0, 0),
                       memory_space=pltpu.SMEM)
    p1 = pos1.reshape(nb, 1, tr)
    p2 = pos2.reshape(nb, 1, tr)
    return pl.pallas_call(
        functools.partial(_combine_kernel, npb, ys_over is not None),
        grid_spec=pltpu.PrefetchScalarGridSpec(
            num_scalar_prefetch=0,
            grid=(nb,),
            in_specs=[cur, cur, nxt, nxt,
                      pl.BlockSpec((tr, D), lambda i: (i, 0)),
                      pl.BlockSpec((tr, 1), lambda i: (i, 0)),
                      pl.BlockSpec((tr, 1), lambda i: (i, 0)),
                      pl.BlockSpec((1, D), lambda i: (0, 0))]
                     + [pl.BlockSpec(memory_space=pl.ANY)] * len(srcs),
            out_specs=[pl.BlockSpec((tr, D), lambda i: (jnp.minimum(i, npb - 1), 0)),
                       pl.BlockSpec((tr, D), lambda i: (0, 0))],
            scratch_shapes=[pltpu.VMEM((2, 2, tr, D), F32), pltpu.SemaphoreType.DMA((2,))],
        ),
        out_shape=[jax.ShapeDtypeStruct((n_prompt, D), F32),
                   jax.ShapeDtypeStruct((tr, D), F32)],
        compiler_params=_cparams(1),
        name="moe_combine_final_norm",
    )(p1, p2, p1, p2, x, g1, g2, final_g.reshape(1, D), *srcs)


def _moe_plan(idx, T):
    M = T * TOP_K
    E = N_EXPERTS
    R = -(-int(M / E * 1.1) // MOE_SUB) * MOE_SUB
    n_over = -(-M // R)
    i32 = jnp.int32
    flat_e = idx.reshape(M)
    onehot = (flat_e[:, None] == jnp.arange(E, dtype=i32)[None, :]).astype(i32)
    csum = jnp.cumsum(onehot, axis=0)
    rank = jnp.sum((csum - onehot) * onehot, axis=1)
    counts = csum[-1]
    o_chunks = jnp.maximum((counts + R - 1) // R - 1, 0)
    o_first = jnp.cumsum(o_chunks) - o_chunks
    n_used = jnp.sum(o_chunks).astype(i32)
    over = rank - R
    slot = jnp.where(rank < R, flat_e * R + rank,
                     (E + o_first[flat_e] + over // R) * R + over % R)
    flat_tok = jnp.arange(M, dtype=i32) // TOP_K
    slot_tok = jnp.zeros(((E + n_over) * R,), i32).at[slot].set(flat_tok)
    spc = R // MOE_SUB

    def sub_used(c_rows):
        sub = jnp.arange(c_rows.shape[0] * spc, dtype=i32)
        return ((sub % spc) * MOE_SUB < c_rows[sub // spc]).astype(i32)

    m_rows = jnp.minimum(counts, R).astype(i32)
    eidx = jnp.arange(E, dtype=i32)
    main = dict(tok=slot_tok[:E * R], used=sub_used(m_rows),
                meta=(eidx, eidx, m_rows, jnp.full((1,), E, i32)))
    cidx = jnp.arange(n_over, dtype=i32)
    c_e = jnp.clip(jnp.searchsorted(jnp.cumsum(o_chunks), cidx, side='right'), 0, E - 1).astype(i32)
    c_rows = jnp.clip(counts[c_e] - R - (cidx - o_first[c_e]) * R, 0, R).astype(i32)
    used = cidx < n_used
    last = jnp.maximum(n_used - 1, 0)
    c_src = jnp.where(used, cidx, last).astype(i32)
    c_e = jnp.where(used, c_e, c_e[last]).astype(i32)
    c_rows = jnp.where(used, c_rows, 0).astype(i32)
    overflow = dict(tok=slot_tok[E * R:], used=sub_used(c_rows),
                    meta=(c_e, c_src, c_rows, n_used.reshape(1)))
    pos = slot.reshape(T, TOP_K).astype(i32)
    return R, main, overflow, n_used, pos[:, 0], pos[:, 1]


def kernel(x_prompt, x_sample, state_gla, state_conv, ev_norm_mix, ev_w_in, ev_sgu_norm, ev_sgu_w, ev_sgu_b, ev_gla_w_alpha, ev_gla_b_alpha, ev_gla_norm, ev_w_out, ev_norm_ffn, ev_ffn_w_gu, ev_ffn_w_down, od_norm_mix, od_conv_w_in, od_conv_w, od_conv_w_out, od_norm_ffn, od_router_w, od_moe_w_gu, od_moe_w_down, final_norm):
    nb, seq, D = x_prompt.shape
    B = x_sample.shape[0]
    assert x_sample.shape[1] == 1
    Tp = nb * seq
    T = Tp + B
    xp0 = x_prompt.reshape(Tp, D)
    xs0 = x_sample.reshape(B, D)

    sw = ev_sgu_norm.shape[1]
    kw = ev_gla_w_alpha.shape[2]
    rank = ev_gla_w_alpha.shape[1]
    vw = GLA_HEADS * ev_gla_norm.shape[1]
    n_main = 2 * sw + 2 * kw + 2 * vw
    assert ev_w_in.shape[2] == n_main + rank
    tm = _pick_tile(T, 1664)

    h = _norm_bf16_stacked(xp0, xs0, ev_norm_mix[0])
    P = _matmul(h, ev_w_in[0], n_main, tm, 512, name="mix_in_proj")
    log_a = _log_decay(h, ev_w_in[0][:, n_main:], ev_gla_w_alpha[0], ev_gla_b_alpha[0], tm)
    a_out, v_sample = _sgu(P, Tp, B, sw, ev_sgu_norm[0], ev_sgu_w[0], ev_sgu_b[0])
    col_q = 2 * sw
    o_p, gla_p = _gla_prompt(P, log_a, nb, seq, col_q, kw, vw, ev_gla_norm[0])
    Ps = P[Tp:]
    q_s = Ps[:, col_q:col_q + kw]
    k_s = Ps[:, col_q + kw:col_q + 2 * kw]
    v_s = Ps[:, col_q + 2 * kw:col_q + 2 * kw + vw]
    r_s = Ps[:, col_q + 2 * kw + vw:]
    gla_s, o_s = _gla_sample(state_gla[0], log_a[Tp:], k_s, q_s, v_s, r_s, ev_gla_norm[0])
    x1, h = _mix_out(a_out, o_p, o_s.reshape(B, vw).astype(BF16), xp0, xs0, ev_w_out[0],
                     ev_norm_ffn[0])

    hid = _swiglu_up(h, ev_ffn_w_gu[0], tm, 512)
    x2 = _matmul(hid, ev_ffn_w_down[0], D, _pick_tile(T, 832), 256, res=x1, name="ffn_down")

    h = _norm_bf16(x2, od_norm_mix[0])
    bgate, z = _conv_in(h, od_conv_w_in[0], tm, 256)
    x3, x3_slab, idx, gates = _conv_out_route(
        z, bgate, od_conv_w[0], state_conv[0][:, 0], state_conv[0][:, 1],
        x2, od_conv_w_out[0], od_norm_ffn[0], od_router_w[0], Tp, seq)
    x3_slab = x3_slab.reshape(T, D // LANES, LANES)
    conv_p = jnp.stack([z[(n + 1) * seq - 2:(n + 1) * seq] for n in range(nb)])
    conv_s = jnp.concatenate([state_conv[0][:, 1:], z[Tp:].reshape(B, 1, D)], axis=1)

    R, main, overflow, n_overflow, pos1, pos2 = _moe_plan(idx[:, :TOP_K], T)
    FE = od_moe_w_down.shape[2]
    tf, tk = _pick_tile(FE, 512, LANES), _pick_tile(FE, 1024, LANES)

    def experts(part, direct):
        xs = _dispatch(x3_slab, od_norm_ffn[0], part["tok"], part["used"])
        hid = _moe_up(xs, od_moe_w_gu[0], part["meta"], R, tf, direct)
        return _moe_down(hid, od_moe_w_down[0], part["meta"], R, 1024, tk, direct)

    ys = experts(main, True)
    g1, g2 = gates[:, 0:1], gates[:, 1:2]
    y_p, y_s = lax.cond(
        n_overflow > 0,
        lambda: _combine(x3, ys, experts(overflow, False), pos1, pos2, g1, g2, final_norm, Tp),
        lambda: _combine(x3, ys, None, pos1, pos2, g1, g2, final_norm, Tp))

    y_prompt = y_p.reshape(nb, seq, D)
    y_sample = y_s.reshape(B, 1, D)
    return (y_prompt, y_sample, gla_p[None], gla_s[None],
            v_sample.reshape(1, B, 1, sw), conv_p[None], conv_s[None])
```

```python
import functools
import math

import numpy as np
import jax
import jax.numpy as jnp
from jax import lax
from jax.experimental import pallas as pl
from jax.experimental.pallas import tpu as pltpu

F32 = jnp.float32
BF16 = jnp.bfloat16
EPS = 1e-6

SGU_CHUNK = 128
SGU_GROUPS = 4
GLA_HEADS = 4
GLA_TAU = 16.0
GLA_CHUNK = 128
N_EXPERTS = 8
TOP_K = 2
MOE_SUB = 256
MOE_CAPACITY = 1.35
MOE_BIG = 1024
MOE_SMALL = 128
LANES = 128
VMEM_LIMIT = 52 * 1024 * 1024


def _cparams(n_axes, vmem=VMEM_LIMIT):
    return pltpu.CompilerParams(dimension_semantics=("arbitrary",) * n_axes,
                                vmem_limit_bytes=vmem)


def _pick_tile(n, cap, mult=16):
    best = None
    for t in range(mult, min(n, cap) + 1, mult):
        if n % t == 0:
            best = t
    assert best is not None, (n, cap, mult)
    return best


def _rms(x, g):
    ms = jnp.mean(x * x, axis=-1, keepdims=True)
    return (x * lax.rsqrt(ms + EPS)) * g


def _gelu(x):
    c = math.sqrt(2.0 / math.pi)
    return x * (0.5 * (1.0 + jnp.tanh(c * (x + 0.044715 * (x * x * x)))))


def _silu(x):
    return x * (1.0 / (1.0 + jnp.exp(-x)))


def _dot(a, b):
    return jnp.dot(a, b, preferred_element_type=F32)


def _dot_nt(a, b):
    return lax.dot_general(a, b, (((1,), (1,)), ((), ())), preferred_element_type=F32)


def _norm_kernel(x_ref, g_ref, o_ref):
    o_ref[...] = _rms(x_ref[...], g_ref[...]).astype(BF16)


def _norm_bf16(x, g):
    T, D = x.shape
    tr = _pick_tile(T, 512)
    return pl.pallas_call(
        _norm_kernel,
        grid=(T // tr,),
        in_specs=[pl.BlockSpec((tr, D), lambda i: (i, 0)),
                  pl.BlockSpec((1, D), lambda i: (0, 0))],
        out_specs=pl.BlockSpec((tr, D), lambda i: (i, 0)),
        out_shape=jax.ShapeDtypeStruct((T, D), BF16),
        compiler_params=_cparams(1),
        name="rmsnorm_bf16",
    )(x, g.reshape(1, D))


def _norm2_kernel(n_prompt_blocks, xp_ref, xs_ref, g_ref, o_ref):
    i = pl.program_id(0)

    @pl.when(i < n_prompt_blocks)
    def _():
        o_ref[...] = _rms(xp_ref[...], g_ref[...]).astype(BF16)

    @pl.when(i >= n_prompt_blocks)
    def _():
        o_ref[0:xs_ref.shape[0], :] = _rms(xs_ref[...], g_ref[...]).astype(BF16)


def _norm_bf16_stacked(xp, xs, g):
    Tp, D = xp.shape
    B = xs.shape[0]
    tr = _pick_tile(Tp, 512)
    assert B <= tr
    npb = Tp // tr
    return pl.pallas_call(
        functools.partial(_norm2_kernel, npb),
        grid=(npb + 1,),
        in_specs=[pl.BlockSpec((tr, D), lambda i: (jnp.minimum(i, npb - 1), 0)),
                  pl.BlockSpec((B, D), lambda i: (0, 0)),
                  pl.BlockSpec((1, D), lambda i: (0, 0))],
        out_specs=pl.BlockSpec((tr, D), lambda i: (i, 0)),
        out_shape=jax.ShapeDtypeStruct((Tp + B, D), BF16),
        compiler_params=_cparams(1),
        name="rmsnorm_stack_bf16",
    )(xp, xs, g.reshape(1, D))


def _mm_kernel(x_ref, w_ref, o_ref):
    o_ref[...] = _dot(x_ref[...], w_ref[...].astype(BF16)).astype(o_ref.dtype)


def _mm_res_kernel(x_ref, w_ref, r_ref, o_ref):
    o_ref[...] = r_ref[...] + _dot(x_ref[...], w_ref[...].astype(BF16))


def _matmul(x, w, n_cols, tm, tn, res=None, name="matmul"):
    T, K = x.shape
    grid = (T // tm, n_cols // tn)
    in_specs = [pl.BlockSpec((tm, K), lambda i, j: (i, 0)),
                pl.BlockSpec((K, tn), lambda i, j: (0, j))]
    args = [x, w]
    kern = _mm_kernel
    if res is not None:
        in_specs.append(pl.BlockSpec((tm, tn), lambda i, j: (i, j)))
        args.append(res)
        kern = _mm_res_kernel
    return pl.pallas_call(
        kern, grid=grid, in_specs=in_specs,
        out_specs=pl.BlockSpec((tm, tn), lambda i, j: (i, j)),
        out_shape=jax.ShapeDtypeStruct((T, n_cols), F32),
        compiler_params=_cparams(2),
        name=name,
    )(*args)


def _mix_out_kernel(n_full, n_tail, a_ref, op_ref, os_ref, rp_ref, rs_ref, w_ref, ng_ref,
                    x_ref, h_ref):
    i = pl.program_id(0)
    sw = a_ref.shape[1]

    def finish(rows, a, o, res):
        y = res + _dot(a, w_ref[0:sw, :]) + _dot(o, w_ref[sw:, :])
        x_ref[0:rows, :] = y
        h_ref[0:rows, :] = _rms(y, ng_ref[...]).astype(BF16)

    @pl.when(i < n_full)
    def _():
        finish(a_ref.shape[0], a_ref[...], op_ref[...], rp_ref[...])

    @pl.when(i >= n_full)
    def _():
        finish(n_tail, a_ref[0:n_tail, :], os_ref[...], rs_ref[...])


def _mix_out(a, o_p, o_s, res_p, res_s, w, norm_g):
    T, sw = a.shape
    Tp, vw = o_p.shape
    B = o_s.shape[0]
    D = w.shape[1]
    assert T == Tp + B
    tm = _pick_tile(Tp, 256)
    assert B <= tm
    n_full = Tp // tm
    clamp = lambda i: (jnp.minimum(i, n_full - 1), 0)
    row = lambda i: (i, 0)
    fixed = lambda i: (0, 0)
    return pl.pallas_call(
        functools.partial(_mix_out_kernel, n_full, B),
        grid=(n_full + 1,),
        in_specs=[pl.BlockSpec((tm, sw), row),
                  pl.BlockSpec((tm, vw), clamp),
                  pl.BlockSpec((B, vw), fixed),
                  pl.BlockSpec((tm, D), clamp),
                  pl.BlockSpec((B, D), fixed),
                  pl.BlockSpec((sw + vw, D), fixed),
                  pl.BlockSpec((1, D), fixed)],
        out_specs=[pl.BlockSpec((tm, D), row),
                   pl.BlockSpec((tm, D), row)],
        out_shape=[jax.ShapeDtypeStruct((T, D), F32),
                   jax.ShapeDtypeStruct((T, D), BF16)],
        compiler_params=_cparams(1),
        name="mix_out_proj",
    )(a, o_p, o_s, res_p, res_s, w.astype(BF16), norm_g.reshape(1, D))


def _swiglu_kernel(x_ref, wg_ref, wu_ref, o_ref):
    x = x_ref[...]
    g = _dot(x, wg_ref[...].astype(BF16))
    u = _dot(x, wu_ref[...].astype(BF16))
    o_ref[...] = (_silu(g) * u).astype(o_ref.dtype)


def _swiglu_up(x, w_gu, tm, tn):
    T, K = x.shape
    F = w_gu.shape[1] // 2
    nj = F // tn
    return pl.pallas_call(
        _swiglu_kernel,
        grid=(T // tm, nj),
        in_specs=[pl.BlockSpec((tm, K), lambda i, j: (i, 0)),
                  pl.BlockSpec((K, tn), lambda i, j: (0, j)),
                  pl.BlockSpec((K, tn), lambda i, j: (0, j + nj))],
        out_specs=pl.BlockSpec((tm, tn), lambda i, j: (i, j)),
        out_shape=jax.ShapeDtypeStruct((T, F), BF16),
        compiler_params=_cparams(2),
        name="ffn_gate_up",
    )(x, w_gu, w_gu)


def _conv_in_kernel(x_ref, wb_ref, wc_ref, wu_ref, b_ref, z_ref):
    x = x_ref[...]
    b_ref[...] = _dot(x, wb_ref[...].astype(BF16))
    c = _dot(x, wc_ref[...].astype(BF16))
    u = _dot(x, wu_ref[...].astype(BF16))
    z_ref[...] = c * u


def _conv_in(x, w, tm, tn):
    T, K = x.shape
    C = w.shape[1] // 3
    nj = C // tn
    return pl.pallas_call(
        _conv_in_kernel,
        grid=(T // tm, nj),
        in_specs=[pl.BlockSpec((tm, K), lambda i, j: (i, 0)),
                  pl.BlockSpec((K, tn), lambda i, j: (0, j)),
                  pl.BlockSpec((K, tn), lambda i, j: (0, j + nj)),
                  pl.BlockSpec((K, tn), lambda i, j: (0, j + 2 * nj))],
        out_specs=[pl.BlockSpec((tm, tn), lambda i, j: (i, j)),
                   pl.BlockSpec((tm, tn), lambda i, j: (i, j))],
        out_shape=[jax.ShapeDtypeStruct((T, C), F32),
                   jax.ShapeDtypeStruct((T, C), F32)],
        compiler_params=_cparams(2),
        name="conv_in_proj",
    )(x, w, w, w)


def _loga_kernel(x_ref, wl_ref, wa_ref, b_ref, o_ref):
    rank = wa_ref.shape[0]
    g = _dot(x_ref[...], wl_ref[:, 0:rank].astype(BF16))
    z = _dot(g.astype(BF16), wa_ref[...].astype(BF16)) + b_ref[...]
    ls = jnp.minimum(z, 0.0) - jnp.log1p(jnp.exp(-jnp.abs(z)))
    o_ref[...] = ls * (1.0 / GLA_TAU)


def _log_decay(x, w_in, col, w_alpha, b_alpha, tm):
    T, K = x.shape
    R, KW = w_alpha.shape
    assert col % LANES == 0 and R <= LANES
    return pl.pallas_call(
        _loga_kernel,
        grid=(T // tm,),
        in_specs=[pl.BlockSpec((tm, K), lambda i: (i, 0)),
                  pl.BlockSpec((K, LANES), lambda i: (0, col // LANES)),
                  pl.BlockSpec((R, KW), lambda i: (0, 0)),
                  pl.BlockSpec((1, KW), lambda i: (0, 0))],
        out_specs=pl.BlockSpec((tm, KW), lambda i: (i, 0)),
        out_shape=jax.ShapeDtypeStruct((T, KW), F32),
        compiler_params=_cparams(1),
        name="gla_log_decay",
    )(x, w_in, w_alpha, b_alpha.reshape(1, KW))


def _sgu_kernel(n_prompt_blocks, u_ref, v_ref, ng_ref, w_ref, bt_ref, o_ref, vs_ref):
    i = pl.program_id(0)
    is_sample = i >= n_prompt_blocks
    C = SGU_CHUNK
    gu = _gelu(u_ref[...])
    vn = _rms(_gelu(v_ref[...]), ng_ref[...])

    @pl.when(is_sample)
    def _():
        vs_ref[...] = vn

    gd = vn.shape[1] // SGU_GROUPS
    row = lax.broadcasted_iota(jnp.int32, (C, C), 0)
    col = lax.broadcasted_iota(jnp.int32, (C, C), 1)
    vb = vn.astype(BF16)
    for g in range(SGU_GROUPS):
        w = w_ref[g]
        w_prompt = jnp.where(col <= row, w, 0.0)
        w_sample = jnp.where(col == row, w_ref[g, 0:1, 0:1], 0.0)
        w_eff = jnp.where(is_sample, w_sample, w_prompt).astype(BF16)
        bias = jnp.where(is_sample, bt_ref[0:1, g:g + 1], bt_ref[:, g:g + 1])
        mixed = _dot(w_eff, vb[:, g * gd:(g + 1) * gd]) + bias
        o_ref[:, g * gd:(g + 1) * gd] = (gu[:, g * gd:(g + 1) * gd] * mixed).astype(o_ref.dtype)


def _sgu(P, n_prompt, n_sample, width, sgu_norm, sgu_w, sgu_b):
    C = SGU_CHUNK
    assert n_prompt % C == 0 and n_sample == C
    T = n_prompt + n_sample
    npb = n_prompt // C
    return pl.pallas_call(
        functools.partial(_sgu_kernel, npb),
        grid=(T // C,),
        in_specs=[pl.BlockSpec((C, width), lambda i: (i, 0)),
                  pl.BlockSpec((C, width), lambda i: (i, 1)),
                  pl.BlockSpec((1, width), lambda i: (0, 0)),
                  pl.BlockSpec((SGU_GROUPS, C, C), lambda i: (0, 0, 0)),
                  pl.BlockSpec((C, SGU_GROUPS), lambda i: (0, 0))],
        out_specs=[pl.BlockSpec((C, width), lambda i: (i, 0)),
                   pl.BlockSpec((C, width), lambda i: (0, 0))],
        out_shape=[jax.ShapeDtypeStruct((T, width), BF16),
                   jax.ShapeDtypeStruct((n_sample, width), F32)],
        compiler_params=_cparams(1),
        name="sgu_mix",
    )(P, P, sgu_norm.reshape(1, width), sgu_w, sgu_b.T)


def _gla_constants():
    C = GLA_CHUNK
    s = np.arange(C)[None, :]
    i = np.arange(C)[:, None]
    gs = [s <= i, s > i]
    masks = []
    b = C // 2
    while b >= 1:
        upper = (i % (2 * b)) >= b
        m = (i // (2 * b)) * (2 * b) + b - 1
        gs.append(np.where(upper, (s > m) & (s <= i), (s > i) & (s <= m)))
        same = (i // (2 * b)) == (s // (2 * b))
        masks.append(upper & same & ~upper.T)
        b //= 2
    masks.append(np.eye(C, dtype=bool))
    G = np.concatenate(gs, axis=0).astype(np.float32)
    M = np.stack(masks).astype(np.float32)
    return jnp.asarray(G, BF16), jnp.asarray(M, F32)


def _gla_kernel(q_ref, k_ref, v_ref, r_ref, la_ref, G_ref, M_ref, gn_ref,
                o_ref, s_out_ref, st_ref):
    c = pl.program_id(1)
    C = GLA_CHUNK
    dk = q_ref.shape[1] // GLA_HEADS
    dv = v_ref.shape[1] // GLA_HEADS
    nlev = M_ref.shape[0] - 1

    @pl.when(c == 0)
    def _():
        st_ref[...] = jnp.zeros_like(st_ref)

    la = la_ref[...]
    hi = la.astype(BF16)
    r1 = la - hi.astype(F32)
    mid = r1.astype(BF16)
    lo = (r1 - mid.astype(F32)).astype(BF16)
    G = G_ref[...]
    X = jnp.exp(_dot(G, hi) + _dot(G, mid) + _dot(G, lo))

    scale = dk ** -0.5
    for h in range(GLA_HEADS):
        Xh = X[:, h * dk:(h + 1) * dk]
        qh = q_ref[:, h * dk:(h + 1) * dk] * scale
        kh = k_ref[:, h * dk:(h + 1) * dk]
        vh = v_ref[:, h * dv:(h + 1) * dv]
        x_cum = Xh[0:C]
        x_rest = Xh[C:2 * C]
        st_old = st_ref[h]
        o = _dot_nt((qh * x_cum).astype(BF16), st_old.astype(BF16))
        qb = qh.astype(BF16)
        kb = kh.astype(BF16)
        att = _dot_nt(qb, kb) * M_ref[nlev]
        for l in range(nlev):
            xl = Xh[(2 + l) * C:(3 + l) * C]
            att = att + _dot_nt((qh * xl).astype(BF16), (kh * xl).astype(BF16)) * M_ref[l]
        vb = vh.astype(BF16)
        o = o + _dot(att.astype(BF16), vb)
        vt = vh.T.astype(BF16)
        st_new = st_old * x_cum[C - 1:C, :] + _dot(vt, (kh * x_rest).astype(BF16))
        st_ref[h] = st_new
        on = _rms(o, gn_ref[...]) * _silu(r_ref[:, h * dv:(h + 1) * dv])
        o_ref[:, h * dv:(h + 1) * dv] = on.astype(o_ref.dtype)

    @pl.when(c == pl.num_programs(1) - 1)
    def _():
        for h in range(GLA_HEADS):
            s_out_ref[0, h] = st_ref[h].T


def _gla_prompt(P, log_a, n_seq, seq, col_q, kw, vw, gla_norm):
    C = GLA_CHUNK
    assert seq % C == 0
    nc = seq // C
    dk = kw // GLA_HEADS
    dv = vw // GLA_HEADS
    qb, kb = col_q // kw, col_q // kw + 1
    vb = (col_q + 2 * kw) // vw
    rb = vb + 1
    assert col_q % kw == 0 and (col_q + 2 * kw) % vw == 0
    G, M = _gla_constants()
    rows = lambda n, c: n * nc + c
    return pl.pallas_call(
        _gla_kernel,
        grid=(n_seq, nc),
        in_specs=[pl.BlockSpec((C, kw), lambda n, c: (rows(n, c), qb)),
                  pl.BlockSpec((C, kw), lambda n, c: (rows(n, c), kb)),
                  pl.BlockSpec((C, vw), lambda n, c: (rows(n, c), vb)),
                  pl.BlockSpec((C, vw), lambda n, c: (rows(n, c), rb)),
                  pl.BlockSpec((C, kw), lambda n, c: (rows(n, c), 0)),
                  pl.BlockSpec(G.shape, lambda n, c: (0, 0)),
                  pl.BlockSpec(M.shape, lambda n, c: (0, 0, 0)),
                  pl.BlockSpec((1, dv), lambda n, c: (0, 0))],
        out_specs=[pl.BlockSpec((C, vw), lambda n, c: (rows(n, c), 0)),
                   pl.BlockSpec((1, GLA_HEADS, dk, dv), lambda n, c: (n, 0, 0, 0))],
        out_shape=[jax.ShapeDtypeStruct((n_seq * seq, vw), BF16),
                   jax.ShapeDtypeStruct((n_seq, GLA_HEADS, dk, dv), F32)],
        scratch_shapes=[pltpu.VMEM((GLA_HEADS, dv, dk), F32)],
        compiler_params=_cparams(2),
        name="gla_prompt",
    )(P, P, P, P, log_a, G, M, gla_norm.reshape(1, dv))


def _gla_step_kernel(s_ref, at_ref, kt_ref, qt_ref, v_ref, r_ref, gn_ref, sn_ref, o_ref):
    i = pl.program_id(0)
    bb, H, dk, dv = s_ref.shape
    B = at_ref.shape[1]
    lane = lax.broadcasted_iota(jnp.int32, (dk, B), 1)
    scale = dk ** -0.5
    for j in range(bb):
        pick = lane == i * bb + j

        def column(t_ref, h):
            return jnp.sum(jnp.where(pick, t_ref[h * dk:(h + 1) * dk, :], 0.0), axis=1, keepdims=True)

        for h in range(H):
            s_new = jnp.exp(column(at_ref, h)) * s_ref[j, h] + column(kt_ref, h) * v_ref[j, h]
            sn_ref[j, h] = s_new
            o = jnp.sum((column(qt_ref, h) * scale) * s_new, axis=0, keepdims=True)
            o_ref[j, h] = _rms(o, gn_ref[...]) * _silu(r_ref[j, h])


def _gla_sample(state, la, k, q, v, r, gla_norm):
    B, H, dk, dv = state.shape
    bb = 8
    rowv = lambda t: t.reshape(B, H, 1, dv)
    tspec = pl.BlockSpec((H * dk, B), lambda i: (0, 0))
    rspec = pl.BlockSpec((bb, H, 1, dv), lambda i: (i, 0, 0, 0))
    sspec = pl.BlockSpec((bb, H, dk, dv), lambda i: (i, 0, 0, 0))
    return pl.pallas_call(
        _gla_step_kernel,
        grid=(B // bb,),
        in_specs=[sspec, tspec, tspec, tspec, rspec, rspec,
                  pl.BlockSpec((1, dv), lambda i: (0, 0))],
        out_specs=[sspec, rspec],
        out_shape=[jax.ShapeDtypeStruct((B, H, dk, dv), F32),
                   jax.ShapeDtypeStruct((B, H, 1, dv), F32)],
        compiler_params=_cparams(1),
        name="gla_sample_step",
    )(state, la.T, k.T, q.T, rowv(v), rowv(r), gla_norm.reshape(1, dv))


def _route(logits):
    lane = lax.broadcasted_iota(jnp.int32, logits.shape, 1)
    neg = jnp.float32(-jnp.inf)
    logits = jnp.where(lane < N_EXPERTS, logits, neg)
    v1 = jnp.max(logits, axis=-1, keepdims=True)
    i1 = jnp.min(jnp.where(logits == v1, lane, LANES), axis=-1, keepdims=True)
    rest = jnp.where(lane == i1, neg, logits)
    v2 = jnp.max(rest, axis=-1, keepdims=True)
    i2 = jnp.min(jnp.where(rest == v2, lane, LANES), axis=-1, keepdims=True)
    e2 = jnp.exp(v2 - v1)
    g1 = 1.0 / (1.0 + e2)
    g2 = e2 / (1.0 + e2)
    idx = jnp.where(lane == 0, i1, jnp.where(lane == 1, i2, 0))
    gate = jnp.where(lane == 0, g1, jnp.where(lane == 1, g2, 0.0))
    return idx, gate


def _conv_out_kernel(n_prompt_blocks, blocks_per_seq, n_tail, z_ref, halo_ref, b_ref, cw_ref,
                     h0_ref, h1_ref, res_ref, w_ref, ng_ref, rw_ref, x_ref, xslab_ref, idx_ref,
                     gate_ref):
    i = pl.program_id(0)
    w0, w1, w2 = cw_ref[0:1, :], cw_ref[1:2, :], cw_ref[2:3, :]
    ns = x_ref.shape[1] // LANES

    def finish(rows, yc, res):
        y = res + _dot(yc.astype(BF16), w_ref[...])
        x_ref[0:rows, :] = y
        hn = _rms(y, ng_ref[...])
        for s in range(ns):
            xslab_ref[pl.ds(s, rows, stride=ns), :] = hn[:, s * LANES:(s + 1) * LANES]
        h_hi = hn.astype(BF16)
        h_lo = (hn - h_hi.astype(F32)).astype(BF16)
        rw = rw_ref[...]
        r_hi = rw.astype(BF16)
        r_lo = (rw - r_hi.astype(F32)).astype(BF16)
        logits = _dot(h_hi, r_hi) + (_dot(h_hi, r_lo) + _dot(h_lo, r_hi))
        idx, gate = _route(logits)
        idx_ref[0:rows, :] = idx
        gate_ref[0:rows, :] = gate

    @pl.when(i < n_prompt_blocks)
    def _():
        z = z_ref[...]
        row = lax.broadcasted_iota(jnp.int32, z.shape, 0)
        halo = jnp.where(i % blocks_per_seq == 0, 0.0, halo_ref[...])
        z1 = jnp.where(row == 0, halo[7:8, :], pltpu.roll(z, 1, 0))
        z2 = pltpu.roll(z, 2, 0)
        z2 = jnp.where(row == 0, halo[6:7, :], jnp.where(row == 1, halo[7:8, :], z2))
        finish(z.shape[0], b_ref[...] * (w0 * z2 + w1 * z1 + w2 * z), res_ref[...])

    @pl.when(i >= n_prompt_blocks)
    def _():
        conv = w0 * h0_ref[...] + w1 * h1_ref[...] + w2 * z_ref[0:n_tail, :]
        finish(n_tail, b_ref[0:n_tail, :] * conv, res_ref[0:n_tail, :])


def _conv_out_route(z, bgate, conv_w, hist0, hist1, res, w_out, norm_g, router_w, n_prompt, seq):
    T, Cd = z.shape
    D = w_out.shape[1]
    B = T - n_prompt
    tm = _pick_tile(math.gcd(seq, n_prompt), 256)
    assert B <= tm and hist0.shape == (B, Cd)
    npb = n_prompt // tm
    wpad = jnp.zeros((D, LANES), F32).at[:, :N_EXPERTS].set(router_w)
    row = lambda i: (i, 0)
    fixed = lambda i: (0, 0)
    return pl.pallas_call(
        functools.partial(_conv_out_kernel, npb, seq // tm, B),
        grid=(npb + 1,),
        in_specs=[pl.BlockSpec((tm, Cd), row),
                  pl.BlockSpec((8, Cd), lambda i: (jnp.maximum(i * (tm // 8) - 1, 0), 0)),
                  pl.BlockSpec((tm, Cd), row),
                  pl.BlockSpec(conv_w.shape, fixed),
                  pl.BlockSpec((B, Cd), fixed),
                  pl.BlockSpec((B, Cd), fixed),
                  pl.BlockSpec((tm, D), row),
                  pl.BlockSpec((Cd, D), fixed),
                  pl.BlockSpec((1, D), fixed),
                  pl.BlockSpec((D, LANES), fixed)],
        out_specs=[pl.BlockSpec((tm, D), row),
                   pl.BlockSpec((tm * (D // LANES), LANES), row),
                   pl.BlockSpec((tm, LANES), row),
                   pl.BlockSpec((tm, LANES), row)],
        out_shape=[jax.ShapeDtypeStruct((T, D), F32),
                   jax.ShapeDtypeStruct((T * (D // LANES), LANES), F32),
                   jax.ShapeDtypeStruct((T, LANES), jnp.int32),
                   jax.ShapeDtypeStruct((T, LANES), F32)],
        compiler_params=_cparams(1),
        name="conv_out_route",
    )(z, z, bgate, conv_w, hist0, hist1, res, w_out.astype(BF16), norm_g.reshape(1, D), wpad)


def _dispatch_kernel(tok_ref, nxt_ref, used_ref, x_hbm, o_ref, buf_ref, sem):
    i = pl.program_id(0)
    nsteps = pl.num_programs(0)
    n = o_ref.shape[0]
    ns = x_hbm.shape[1]
    slot = i % 2

    def row_copy(t_ref, r, s):
        return pltpu.make_async_copy(x_hbm.at[t_ref[0, 0, r]],
                                     buf_ref.at[s, pl.ds(pl.multiple_of(r * ns, ns), ns)], sem.at[s])

    def start_rows(t_ref, s):
        def body(r, carry):
            row_copy(t_ref, r, s).start()
            return carry

        lax.fori_loop(0, n, body, 0, unroll=8)

    @pl.when(jnp.logical_and(i == 0, used_ref[0] > 0))
    def _():
        start_rows(tok_ref, 0)

    @pl.when(jnp.logical_and(i + 1 < nsteps, used_ref[jnp.minimum(i + 1, nsteps - 1)] > 0))
    def _():
        start_rows(nxt_ref, 1 - slot)

    @pl.when(used_ref[i] > 0)
    def _():
        pltpu.make_async_copy(buf_ref.at[slot], buf_ref.at[slot], sem.at[slot]).wait()

        for s in range(ns):
            o_ref[:, s * LANES:(s + 1) * LANES] = (
                buf_ref[slot, pl.ds(s, n, stride=ns), :].astype(o_ref.dtype))

    @pl.when(used_ref[i] == 0)
    def _():
        o_ref[...] = jnp.zeros_like(o_ref)


def _dispatch(x_slab, slot_tok, sub_used):
    D = x_slab.shape[1] * x_slab.shape[2]
    nsub = sub_used.shape[0]
    SB = MOE_SUB
    tok3 = slot_tok.reshape(nsub, 1, SB)
    return pl.pallas_call(
        _dispatch_kernel,
        grid_spec=pltpu.PrefetchScalarGridSpec(
            num_scalar_prefetch=0,
            grid=(nsub,),
            in_specs=[pl.BlockSpec((1, 1, SB), lambda i: (i, 0, 0), memory_space=pltpu.SMEM),
                      pl.BlockSpec((1, 1, SB), lambda i: (jnp.minimum(i + 1, nsub - 1), 0, 0),
                                   memory_space=pltpu.SMEM),
                      pl.BlockSpec(memory_space=pltpu.SMEM),
                      pl.BlockSpec(memory_space=pl.ANY)],
            out_specs=pl.BlockSpec((SB, D), lambda i: (i, 0)),
            scratch_shapes=[pltpu.VMEM((2, SB * (D // LANES), LANES), F32),
                            pltpu.SemaphoreType.DMA((2,))],
        ),
        out_shape=jax.ShapeDtypeStruct((nsub * SB, D), BF16),
        compiler_params=_cparams(1),
        name="moe_dispatch",
    )(tok3, tok3, sub_used, x_slab)


def _row_pieces(rows, total, compute, skip):
    for s in range(0, total, MOE_BIG):
        size = min(MOE_BIG, total - s)
        whole = rows > s + size - MOE_SMALL

        @pl.when(whole)
        def _():
            compute(s, size)

        @pl.when(jnp.logical_not(whole))
        def _():
            for t in range(s, s + size, MOE_SMALL):
                @pl.when(rows > t)
                def _():
                    compute(t, MOE_SMALL)

                @pl.when(rows <= t)
                def _():
                    skip(t, MOE_SMALL)


def _moe_up_kernel(ce_ref, cs_ref, cr_ref, nu_ref, x_ref, wg_ref, wu_ref, o_ref):
    c = pl.program_id(0)

    @pl.when(c < nu_ref[0])
    def _():
        wg = wg_ref[0].astype(BF16)
        wu = wu_ref[0].astype(BF16)

        def compute(s, n):
            x = x_ref[pl.ds(s, n), :]
            o_ref[pl.ds(s, n), :] = (_silu(_dot(x, wg)) * _dot(x, wu)).astype(o_ref.dtype)

        def skip(s, n):
            o_ref[pl.ds(s, n), :] = jnp.zeros((n, o_ref.shape[1]), o_ref.dtype)

        _row_pieces(cr_ref[c], x_ref.shape[0], compute, skip)

    @pl.when(c >= nu_ref[0])
    def _():
        o_ref[...] = jnp.zeros_like(o_ref)


def _moe_up(xs, w_gu, meta, R, tf, direct):
    ce, cs, cr, nu = meta
    nch = ce.shape[0]
    D = xs.shape[1]
    F = w_gu.shape[2] // 2
    nf = F // tf

    def fidx(c, f, nu):
        return jnp.where(c < nu[0], f, nf - 1)

    if direct:
        in_specs = [pl.BlockSpec((R, D), lambda c, f, *_: (c, 0), pipeline_mode=pl.Buffered(1)),
                    pl.BlockSpec((1, D, tf), lambda c, f, *_: (c, 0, f)),
                    pl.BlockSpec((1, D, tf), lambda c, f, *_: (c, 0, nf + f))]
    else:
        in_specs = [pl.BlockSpec((R, D), lambda c, f, ce, cs, cr, nu: (cs[c], 0),
                                 pipeline_mode=pl.Buffered(1)),
                    pl.BlockSpec((1, D, tf), lambda c, f, ce, cs, cr, nu: (ce[c], 0, fidx(c, f, nu))),
                    pl.BlockSpec((1, D, tf), lambda c, f, ce, cs, cr, nu: (ce[c], 0, nf + fidx(c, f, nu)))]
    return pl.pallas_call(
        _moe_up_kernel,
        grid_spec=pltpu.PrefetchScalarGridSpec(
            num_scalar_prefetch=4,
            grid=(nch, nf),
            in_specs=in_specs,
            out_specs=pl.BlockSpec((R, tf), lambda c, f, ce, cs, cr, nu: (c, f)),
        ),
        out_shape=jax.ShapeDtypeStruct((nch * R, F), BF16),
        compiler_params=_cparams(2),
        name="moe_gate_up",
    )(ce, cs, cr, nu, xs, w_gu, w_gu)


def _moe_down_kernel(ce_ref, cs_ref, cr_ref, nu_ref, h_ref, w_ref, o_ref):
    c = pl.program_id(0)
    k = pl.program_id(2)

    @pl.when(c < nu_ref[0])
    def _():
        w = w_ref[0].astype(BF16)

        def compute(s, n):
            @pl.when(k == 0)
            def _():
                o_ref[pl.ds(s, n), :] = _dot(h_ref[pl.ds(s, n), :], w)

            @pl.when(k > 0)
            def _():
                o_ref[pl.ds(s, n), :] = o_ref[pl.ds(s, n), :] + _dot(h_ref[pl.ds(s, n), :], w)

        def skip(s, n):
            @pl.when(k == 0)
            def _():
                o_ref[pl.ds(s, n), :] = jnp.zeros((n, o_ref.shape[1]), F32)

        _row_pieces(cr_ref[c], h_ref.shape[0], compute, skip)

    @pl.when(jnp.logical_and(c >= nu_ref[0], k == 0))
    def _():
        o_ref[...] = jnp.zeros_like(o_ref)


def _moe_down(hid, w_down, meta, R, tn, tk, direct):
    ce, cs, cr, nu = meta
    nch = ce.shape[0]
    F = hid.shape[1]
    D = w_down.shape[2]
    nn, nk = D // tn, F // tk

    def last_if_unused(c, v, nu, last):
        return jnp.where(c < nu[0], v, last)

    if direct:
        in_specs = [pl.BlockSpec((R, tk), lambda c, n, k, *_: (c, k)),
                    pl.BlockSpec((1, tk, tn), lambda c, n, k, *_: (c, k, n))]
    else:
        in_specs = [pl.BlockSpec((R, tk), lambda c, n, k, ce, cs, cr, nu:
                                 (cs[c], last_if_unused(c, k, nu, nk - 1))),
                    pl.BlockSpec((1, tk, tn), lambda c, n, k, ce, cs, cr, nu:
                                 (ce[c], last_if_unused(c, k, nu, nk - 1),
                                  last_if_unused(c, n, nu, nn - 1)))]
    return pl.pallas_call(
        _moe_down_kernel,
        grid_spec=pltpu.PrefetchScalarGridSpec(
            num_scalar_prefetch=4,
            grid=(nch, nn, nk),
            in_specs=in_specs,
            out_specs=pl.BlockSpec((R, tn), lambda c, n, k, ce, cs, cr, nu: (c, n)),
        ),
        out_shape=jax.ShapeDtypeStruct((nch * R, D), F32),
        compiler_params=_cparams(3),
        name="moe_down",
    )(ce, cs, cr, nu, hid, w_down)


def _combine_kernel(n_prompt_blocks, has_overflow, p1_ref, p2_ref, n1_ref, n2_ref, x_ref, g1_ref,
                    g2_ref, fg_ref, ys_hbm, *rest):
    yo_hbm = rest[0] if has_overflow else None
    op_ref, os_ref, buf_ref, sem = rest[-4:]
    i = pl.program_id(0)
    nsteps = pl.num_programs(0)
    n = x_ref.shape[0]
    slot = i % 2
    split = ys_hbm.shape[0]

    def start_row(pos, k, r, s):
        dst = buf_ref.at[s, k, pl.ds(r, 1)]
        if yo_hbm is None:
            pltpu.make_async_copy(ys_hbm.at[pl.ds(pos, 1)], dst, sem.at[s]).start()
        else:
            @pl.when(pos < split)
            def _():
                pltpu.make_async_copy(ys_hbm.at[pl.ds(pos, 1)], dst, sem.at[s]).start()

            @pl.when(pos >= split)
            def _():
                pltpu.make_async_copy(yo_hbm.at[pl.ds(pos - split, 1)], dst, sem.at[s]).start()

    def start_rows(a_ref, b_ref, s):
        def body(r, carry):
            start_row(a_ref[0, 0, r], 0, r, s)
            start_row(b_ref[0, 0, r], 1, r, s)
            return carry

        lax.fori_loop(0, n, body, 0, unroll=8)

    @pl.when(i == 0)
    def _():
        start_rows(p1_ref, p2_ref, 0)

    @pl.when(i + 1 < nsteps)
    def _():
        start_rows(n1_ref, n2_ref, 1 - slot)

    pltpu.make_async_copy(buf_ref.at[slot], buf_ref.at[slot], sem.at[slot]).wait()
    y = x_ref[...] + (g1_ref[...] * buf_ref[slot, 0] + g2_ref[...] * buf_ref[slot, 1])
    yn = _rms(y, fg_ref[...])

    @pl.when(i < n_prompt_blocks)
    def _():
        op_ref[...] = yn

    @pl.when(i >= n_prompt_blocks)
    def _():
        os_ref[...] = yn


def _combine(x, ys, ys_over, pos1, pos2, g1, g2, final_g, n_prompt):
    T, D = x.shape
    srcs = [ys] if ys_over is None else [ys, ys_over]
    tr = T - n_prompt
    assert n_prompt % tr == 0
    nb = T // tr
    npb = n_prompt // tr
    cur = pl.BlockSpec((1, 1, tr), lambda i: (i, 0, 0), memory_space=pltpu.SMEM)
    nxt = pl.BlockSpec((1, 1, tr), lambda i: (jnp.minimum(i + 1, nb - 1), 0, 0),
                       memory_space=pltpu.SMEM)
    p1 = pos1.reshape(nb, 1, tr)
    p2 = pos2.reshape(nb, 1, tr)
    return pl.pallas_call(
        functools.partial(_combine_kernel, npb, ys_over is not None),
        grid_spec=pltpu.PrefetchScalarGridSpec(
            num_scalar_prefetch=0,
            grid=(nb,),
            in_specs=[cur, cur, nxt, nxt,
                      pl.BlockSpec((tr, D), lambda i: (i, 0)),
                      pl.BlockSpec((tr, 1), lambda i: (i, 0)),
                      pl.BlockSpec((tr, 1), lambda i: (i, 0)),
                      pl.BlockSpec((1, D), lambda i: (0, 0))]
                     + [pl.BlockSpec(memory_space=pl.ANY)] * len(srcs),
            out_specs=[pl.BlockSpec((tr, D), lambda i: (jnp.minimum(i, npb - 1), 0)),
                       pl.BlockSpec((tr, D), lambda i: (0, 0))],
            scratch_shapes=[pltpu.VMEM((2, 2, tr, D), F32), pltpu.SemaphoreType.DMA((2,))],
        ),
        out_shape=[jax.ShapeDtypeStruct((n_prompt, D), F32),
                   jax.ShapeDtypeStruct((tr, D), F32)],
        compiler_params=_cparams(1),
        name="moe_combine_final_norm",
    )(p1, p2, p1, p2, x, g1, g2, final_g.reshape(1, D), *srcs)


def _moe_plan(idx, T):
    M = T * TOP_K
    E = N_EXPERTS
    R = -(-int(M / E * MOE_CAPACITY) // MOE_SUB) * MOE_SUB
    n_over = -(-M // R)
    i32 = jnp.int32
    flat_e = idx.reshape(M)
    onehot = (flat_e[:, None] == jnp.arange(E, dtype=i32)[None, :]).astype(i32)
    csum = jnp.cumsum(onehot, axis=0)
    rank = jnp.sum((csum - onehot) * onehot, axis=1)
    counts = csum[-1]
    o_chunks = jnp.maximum((counts + R - 1) // R - 1, 0)
    o_first = jnp.cumsum(o_chunks) - o_chunks
    n_used = jnp.sum(o_chunks).astype(i32)
    over = rank - R
    slot = jnp.where(rank < R, flat_e * R + rank,
                     (E + o_first[flat_e] + over // R) * R + over % R)
    flat_tok = jnp.arange(M, dtype=i32) // TOP_K
    slot_tok = jnp.zeros(((E + n_over) * R,), i32).at[slot].set(flat_tok)
    spc = R // MOE_SUB

    def sub_used(c_rows):
        sub = jnp.arange(c_rows.shape[0] * spc, dtype=i32)
        return ((sub % spc) * MOE_SUB < c_rows[sub // spc]).astype(i32)

    m_rows = jnp.minimum(counts, R).astype(i32)
    eidx = jnp.arange(E, dtype=i32)
    main = dict(tok=slot_tok[:E * R], used=sub_used(m_rows),
                meta=(eidx, eidx, m_rows, jnp.full((1,), E, i32)))
    cidx = jnp.arange(n_over, dtype=i32)
    c_e = jnp.clip(jnp.searchsorted(jnp.cumsum(o_chunks), cidx, side='right'), 0, E - 1).astype(i32)
    c_rows = jnp.clip(counts[c_e] - R - (cidx - o_first[c_e]) * R, 0, R).astype(i32)
    used = cidx < n_used
    last = jnp.maximum(n_used - 1, 0)
    c_src = jnp.where(used, cidx, last).astype(i32)
    c_e = jnp.where(used, c_e, c_e[last]).astype(i32)
    c_rows = jnp.where(used, c_rows, 0).astype(i32)
    overflow = dict(tok=slot_tok[E * R:], used=sub_used(c_rows),
                    meta=(c_e, c_src, c_rows, n_used.reshape(1)))
    pos = slot.reshape(T, TOP_K).astype(i32)
    return R, main, overflow, n_used, pos[:, 0], pos[:, 1]


def kernel(x_prompt, x_sample, state_gla, state_conv, ev_norm_mix, ev_w_in, ev_sgu_norm, ev_sgu_w, ev_sgu_b, ev_gla_w_alpha, ev_gla_b_alpha, ev_gla_norm, ev_w_out, ev_norm_ffn, ev_ffn_w_gu, ev_ffn_w_down, od_norm_mix, od_conv_w_in, od_conv_w, od_conv_w_out, od_norm_ffn, od_router_w, od_moe_w_gu, od_moe_w_down, final_norm):
    nb, seq, D = x_prompt.shape
    B = x_sample.shape[0]
    assert x_sample.shape[1] == 1
    Tp = nb * seq
    T = Tp + B
    xp0 = x_prompt.reshape(Tp, D)
    xs0 = x_sample.reshape(B, D)

    sw = ev_sgu_norm.shape[1]
    kw = ev_gla_w_alpha.shape[2]
    rank = ev_gla_w_alpha.shape[1]
    vw = GLA_HEADS * ev_gla_norm.shape[1]
    n_main = 2 * sw + 2 * kw + 2 * vw
    assert ev_w_in.shape[2] == n_main + rank
    tm = _pick_tile(T, 1664)

    h = _norm_bf16_stacked(xp0, xs0, ev_norm_mix[0])
    P = _matmul(h, ev_w_in[0], n_main, tm, 512, name="mix_in_proj")
    log_a = _log_decay(h, ev_w_in[0], n_main, ev_gla_w_alpha[0], ev_gla_b_alpha[0], tm)
    a_out, v_sample = _sgu(P, Tp, B, sw, ev_sgu_norm[0], ev_sgu_w[0], ev_sgu_b[0])
    col_q = 2 * sw
    o_p, gla_p = _gla_prompt(P, log_a, nb, seq, col_q, kw, vw, ev_gla_norm[0])
    Ps = P[Tp:]
    q_s = Ps[:, col_q:col_q + kw]
    k_s = Ps[:, col_q + kw:col_q + 2 * kw]
    v_s = Ps[:, col_q + 2 * kw:col_q + 2 * kw + vw]
    r_s = Ps[:, col_q + 2 * kw + vw:]
    gla_s, o_s = _gla_sample(state_gla[0], log_a[Tp:], k_s, q_s, v_s, r_s, ev_gla_norm[0])
    x1, h = _mix_out(a_out, o_p, o_s.reshape(B, vw).astype(BF16), xp0, xs0, ev_w_out[0],
                     ev_norm_ffn[0])

    hid = _swiglu_up(h, ev_ffn_w_gu[0], tm, 512)
    x2 = _matmul(hid, ev_ffn_w_down[0], D, _pick_tile(T, 832), 256, res=x1, name="ffn_down")

    h = _norm_bf16(x2, od_norm_mix[0])
    bgate, z = _conv_in(h, od_conv_w_in[0], tm, 256)
    x3, h3_slab, idx, gates = _conv_out_route(
        z, bgate, od_conv_w[0], state_conv[0][:, 0], state_conv[0][:, 1],
        x2, od_conv_w_out[0], od_norm_ffn[0], od_router_w[0], Tp, seq)
    h3_slab = h3_slab.reshape(T, D // LANES, LANES)
    conv_p = jnp.stack([z[(n + 1) * seq - 2:(n + 1) * seq] for n in range(nb)])
    conv_s = jnp.concatenate([state_conv[0][:, 1:], z[Tp:].reshape(B, 1, D)], axis=1)

    R, main, overflow, n_overflow, pos1, pos2 = _moe_plan(idx[:, :TOP_K], T)
    FE = od_moe_w_down.shape[2]
    tf, tk = _pick_tile(FE, 512, LANES), _pick_tile(FE, 1024, LANES)

    def experts(part, direct):
        xs = _dispatch(h3_slab, part["tok"], part["used"])
        hid = _moe_up(xs, od_moe_w_gu[0], part["meta"], R, tf, direct)
        return _moe_down(hid, od_moe_w_down[0], part["meta"], R, 1024, tk, direct)

    ys = experts(main, True)
    g1, g2 = gates[:, 0:1], gates[:, 1:2]
    y_p, y_s = lax.cond(
        n_overflow > 0,
        lambda: _combine(x3, ys, experts(overflow, False), pos1, pos2, g1, g2, final_norm, Tp),
        lambda: _combine(x3, ys, None, pos1, pos2, g1, g2, final_norm, Tp))

    y_prompt = y_p.reshape(nb, seq, D)
    y_sample = y_s.reshape(B, 1, D)
    return (y_prompt, y_sample, gla_p[None], gla_s[None],
            v_sample.reshape(1, B, 1, sw), conv_p[None], conv_s[None])
```

```python
import functools
import math

import numpy as np
import jax
import jax.numpy as jnp
from jax import lax
from jax.experimental import pallas as pl
from jax.experimental.pallas import tpu as pltpu

F32 = jnp.float32
BF16 = jnp.bfloat16
EPS = 1e-6

SGU_CHUNK = 128
SGU_GROUPS = 4
GLA_HEADS = 4
GLA_TAU = 16.0
GLA_CHUNK = 128
N_EXPERTS = 8
TOP_K = 2
MOE_SUB = 256
MOE_CAPACITY = 1.35
MOE_BIG = 1024
MOE_SMALL = 128
LANES = 128
VMEM_LIMIT = 52 * 1024 * 1024


def _cparams(n_axes, vmem=VMEM_LIMIT):
    return pltpu.CompilerParams(dimension_semantics=("arbitrary",) * n_axes,
                                vmem_limit_bytes=vmem)


def _pick_tile(n, cap, mult=16):
    best = None
    for t in range(mult, min(n, cap) + 1, mult):
        if n % t == 0:
            best = t
    assert best is not None, (n, cap, mult)
    return best


def _rms(x, g):
    ms = jnp.mean(x * x, axis=-1, keepdims=True)
    return (x * lax.rsqrt(ms + EPS)) * g


def _gelu(x):
    c = math.sqrt(2.0 / math.pi)
    return x * (0.5 * (1.0 + jnp.tanh(c * (x + 0.044715 * (x * x * x)))))


def _silu(x):
    return x * (1.0 / (1.0 + jnp.exp(-x)))


def _dot(a, b):
    return jnp.dot(a, b, preferred_element_type=F32)


def _dot_nt(a, b):
    return lax.dot_general(a, b, (((1,), (1,)), ((), ())), preferred_element_type=F32)


def _norm_kernel(x_ref, g_ref, o_ref):
    o_ref[...] = _rms(x_ref[...], g_ref[...]).astype(BF16)


def _norm_bf16(x, g):
    T, D = x.shape
    tr = _pick_tile(T, 512)
    return pl.pallas_call(
        _norm_kernel,
        grid=(T // tr,),
        in_specs=[pl.BlockSpec((tr, D), lambda i: (i, 0)),
                  pl.BlockSpec((1, D), lambda i: (0, 0))],
        out_specs=pl.BlockSpec((tr, D), lambda i: (i, 0)),
        out_shape=jax.ShapeDtypeStruct((T, D), BF16),
        compiler_params=_cparams(1),
        name="rmsnorm_bf16",
    )(x, g.reshape(1, D))


def _norm2_kernel(n_prompt_blocks, xp_ref, xs_ref, g_ref, o_ref):
    i = pl.program_id(0)

    @pl.when(i < n_prompt_blocks)
    def _():
        o_ref[...] = _rms(xp_ref[...], g_ref[...]).astype(BF16)

    @pl.when(i >= n_prompt_blocks)
    def _():
        o_ref[0:xs_ref.shape[0], :] = _rms(xs_ref[...], g_ref[...]).astype(BF16)


def _norm_bf16_stacked(xp, xs, g):
    Tp, D = xp.shape
    B = xs.shape[0]
    tr = _pick_tile(Tp, 512)
    assert B <= tr
    npb = Tp // tr
    return pl.pallas_call(
        functools.partial(_norm2_kernel, npb),
        grid=(npb + 1,),
        in_specs=[pl.BlockSpec((tr, D), lambda i: (jnp.minimum(i, npb - 1), 0)),
                  pl.BlockSpec((B, D), lambda i: (0, 0)),
                  pl.BlockSpec((1, D), lambda i: (0, 0))],
        out_specs=pl.BlockSpec((tr, D), lambda i: (i, 0)),
        out_shape=jax.ShapeDtypeStruct((Tp + B, D), BF16),
        compiler_params=_cparams(1),
        name="rmsnorm_stack_bf16",
    )(xp, xs, g.reshape(1, D))


def _mm_kernel(x_ref, w_ref, o_ref):
    o_ref[...] = _dot(x_ref[...], w_ref[...].astype(BF16)).astype(o_ref.dtype)


def _mm_res_kernel(x_ref, w_ref, r_ref, o_ref):
    o_ref[...] = r_ref[...] + _dot(x_ref[...], w_ref[...].astype(BF16))


def _matmul(x, w, n_cols, tm, tn, res=None, name="matmul"):
    T, K = x.shape
    grid = (T // tm, n_cols // tn)
    if w.ndim == 3:
        w_spec = pl.BlockSpec((None, K, tn), lambda i, j: (0, 0, j))
    else:
        w_spec = pl.BlockSpec((K, tn), lambda i, j: (0, j))
    in_specs = [pl.BlockSpec((tm, K), lambda i, j: (i, 0)), w_spec]
    args = [x, w]
    kern = _mm_kernel
    if res is not None:
        in_specs.append(pl.BlockSpec((tm, tn), lambda i, j: (i, j)))
        args.append(res)
        kern = _mm_res_kernel
    return pl.pallas_call(
        kern, grid=grid, in_specs=in_specs,
        out_specs=pl.BlockSpec((tm, tn), lambda i, j: (i, j)),
        out_shape=jax.ShapeDtypeStruct((T, n_cols), F32),
        compiler_params=_cparams(2),
        name=name,
    )(*args)


def _mix_out_kernel(n_full, n_tail, a_ref, op_ref, os_ref, rp_ref, rs_ref, w_ref, ng_ref,
                    x_ref, h_ref):
    i = pl.program_id(0)
    sw = a_ref.shape[1]

    def finish(rows, a, o, res):
        y = res + _dot(a, w_ref[0:sw, :]) + _dot(o, w_ref[sw:, :])
        x_ref[0:rows, :] = y
        h_ref[0:rows, :] = _rms(y, ng_ref[...]).astype(BF16)

    @pl.when(i < n_full)
    def _():
        finish(a_ref.shape[0], a_ref[...], op_ref[...], rp_ref[...])

    @pl.when(i >= n_full)
    def _():
        finish(n_tail, a_ref[0:n_tail, :], os_ref[...], rs_ref[...])


def _mix_out(a, o_p, o_s, res_p, res_s, w, norm_g):
    T, sw = a.shape
    Tp, vw = o_p.shape
    B = o_s.shape[0]
    D = w.shape[1]
    assert T == Tp + B
    tm = _pick_tile(Tp, 256)
    assert B <= tm
    n_full = Tp // tm
    clamp = lambda i: (jnp.minimum(i, n_full - 1), 0)
    row = lambda i: (i, 0)
    fixed = lambda i: (0, 0)
    return pl.pallas_call(
        functools.partial(_mix_out_kernel, n_full, B),
        grid=(n_full + 1,),
        in_specs=[pl.BlockSpec((tm, sw), row),
                  pl.BlockSpec((tm, vw), clamp),
                  pl.BlockSpec((B, vw), fixed),
                  pl.BlockSpec((tm, D), clamp),
                  pl.BlockSpec((B, D), fixed),
                  pl.BlockSpec((sw + vw, D), fixed),
                  pl.BlockSpec((1, D), fixed)],
        out_specs=[pl.BlockSpec((tm, D), row),
                   pl.BlockSpec((tm, D), row)],
        out_shape=[jax.ShapeDtypeStruct((T, D), F32),
                   jax.ShapeDtypeStruct((T, D), BF16)],
        compiler_params=_cparams(1),
        name="mix_out_proj",
    )(a, o_p, o_s, res_p, res_s, w.astype(BF16), norm_g.reshape(1, D))


def _swiglu_kernel(x_ref, wg_ref, wu_ref, o_ref):
    x = x_ref[...]
    g = _dot(x, wg_ref[...].astype(BF16))
    u = _dot(x, wu_ref[...].astype(BF16))
    o_ref[...] = (_silu(g) * u).astype(o_ref.dtype)


def _swiglu_up(x, w_gu, tm, tn):
    T, K = x.shape
    F = w_gu.shape[1] // 2
    nj = F // tn
    return pl.pallas_call(
        _swiglu_kernel,
        grid=(T // tm, nj),
        in_specs=[pl.BlockSpec((tm, K), lambda i, j: (i, 0)),
                  pl.BlockSpec((K, tn), lambda i, j: (0, j)),
                  pl.BlockSpec((K, tn), lambda i, j: (0, j + nj))],
        out_specs=pl.BlockSpec((tm, tn), lambda i, j: (i, j)),
        out_shape=jax.ShapeDtypeStruct((T, F), BF16),
        compiler_params=_cparams(2),
        name="ffn_gate_up",
    )(x, w_gu, w_gu)


def _conv_in_kernel(x_ref, wb_ref, wc_ref, wu_ref, b_ref, z_ref):
    x = x_ref[...]
    b_ref[...] = _dot(x, wb_ref[...].astype(BF16))
    c = _dot(x, wc_ref[...].astype(BF16))
    u = _dot(x, wu_ref[...].astype(BF16))
    z_ref[...] = c * u


def _conv_in(x, w, tm, tn):
    T, K = x.shape
    C = w.shape[1] // 3
    nj = C // tn
    return pl.pallas_call(
        _conv_in_kernel,
        grid=(T // tm, nj),
        in_specs=[pl.BlockSpec((tm, K), lambda i, j: (i, 0)),
                  pl.BlockSpec((K, tn), lambda i, j: (0, j)),
                  pl.BlockSpec((K, tn), lambda i, j: (0, j + nj)),
                  pl.BlockSpec((K, tn), lambda i, j: (0, j + 2 * nj))],
        out_specs=[pl.BlockSpec((tm, tn), lambda i, j: (i, j)),
                   pl.BlockSpec((tm, tn), lambda i, j: (i, j))],
        out_shape=[jax.ShapeDtypeStruct((T, C), F32),
                   jax.ShapeDtypeStruct((T, C), F32)],
        compiler_params=_cparams(2),
        name="conv_in_proj",
    )(x, w, w, w)


def _loga_kernel(x_ref, wl_ref, wa_ref, b_ref, o_ref):
    rank = wa_ref.shape[0]
    g = _dot(x_ref[...], wl_ref[:, 0:rank].astype(BF16))
    z = _dot(g.astype(BF16), wa_ref[...].astype(BF16)) + b_ref[...]
    ls = jnp.minimum(z, 0.0) - jnp.log1p(jnp.exp(-jnp.abs(z)))
    o_ref[...] = ls * (1.0 / GLA_TAU)


def _log_decay(x, w_in, col, w_alpha, b_alpha, tm):
    T, K = x.shape
    R, KW = w_alpha.shape
    assert col % LANES == 0 and R <= LANES
    return pl.pallas_call(
        _loga_kernel,
        grid=(T // tm,),
        in_specs=[pl.BlockSpec((tm, K), lambda i: (i, 0)),
                  pl.BlockSpec((None, K, LANES), lambda i: (0, 0, col // LANES)),
                  pl.BlockSpec((R, KW), lambda i: (0, 0)),
                  pl.BlockSpec((1, KW), lambda i: (0, 0))],
        out_specs=pl.BlockSpec((tm, KW), lambda i: (i, 0)),
        out_shape=jax.ShapeDtypeStruct((T, KW), F32),
        compiler_params=_cparams(1),
        name="gla_log_decay",
    )(x, w_in, w_alpha, b_alpha.reshape(1, KW))


def _sgu_kernel(n_prompt_blocks, u_ref, v_ref, ng_ref, w_ref, bt_ref, o_ref, vs_ref):
    i = pl.program_id(0)
    is_sample = i >= n_prompt_blocks
    C = SGU_CHUNK
    gu = _gelu(u_ref[...])
    vn = _rms(_gelu(v_ref[...]), ng_ref[...])

    @pl.when(is_sample)
    def _():
        vs_ref[...] = vn

    gd = vn.shape[1] // SGU_GROUPS
    row = lax.broadcasted_iota(jnp.int32, (C, C), 0)
    col = lax.broadcasted_iota(jnp.int32, (C, C), 1)
    vb = vn.astype(BF16)
    for g in range(SGU_GROUPS):
        w = w_ref[g]
        w_prompt = jnp.where(col <= row, w, 0.0)
        w_sample = jnp.where(col == row, w_ref[g, 0:1, 0:1], 0.0)
        w_eff = jnp.where(is_sample, w_sample, w_prompt).astype(BF16)
        bias = jnp.where(is_sample, bt_ref[0:1, g:g + 1], bt_ref[:, g:g + 1])
        mixed = _dot(w_eff, vb[:, g * gd:(g + 1) * gd]) + bias
        o_ref[:, g * gd:(g + 1) * gd] = (gu[:, g * gd:(g + 1) * gd] * mixed).astype(o_ref.dtype)


def _sgu(P, n_prompt, n_sample, width, sgu_norm, sgu_w, sgu_b):
    C = SGU_CHUNK
    assert n_prompt % C == 0 and n_sample == C
    T = n_prompt + n_sample
    npb = n_prompt // C
    return pl.pallas_call(
        functools.partial(_sgu_kernel, npb),
        grid=(T // C,),
        in_specs=[pl.BlockSpec((C, width), lambda i: (i, 0)),
                  pl.BlockSpec((C, width), lambda i: (i, 1)),
                  pl.BlockSpec((1, width), lambda i: (0, 0)),
                  pl.BlockSpec((SGU_GROUPS, C, C), lambda i: (0, 0, 0)),
                  pl.BlockSpec((C, SGU_GROUPS), lambda i: (0, 0))],
        out_specs=[pl.BlockSpec((C, width), lambda i: (i, 0)),
                   pl.BlockSpec((C, width), lambda i: (0, 0))],
        out_shape=[jax.ShapeDtypeStruct((T, width), BF16),
                   jax.ShapeDtypeStruct((n_sample, width), F32)],
        compiler_params=_cparams(1),
        name="sgu_mix",
    )(P, P, sgu_norm.reshape(1, width), sgu_w, sgu_b.T)


def _gla_constants():
    C = GLA_CHUNK
    s = np.arange(C)[None, :]
    i = np.arange(C)[:, None]
    gs = [s <= i, s > i]
    masks = []
    b = C // 2
    while b >= 1:
        upper = (i % (2 * b)) >= b
        m = (i // (2 * b)) * (2 * b) + b - 1
        gs.append(np.where(upper, (s > m) & (s <= i), (s > i) & (s <= m)))
        same = (i // (2 * b)) == (s // (2 * b))
        masks.append(upper & same & ~upper.T)
        b //= 2
    masks.append(np.eye(C, dtype=bool))
    G = np.concatenate(gs, axis=0).astype(np.float32)
    M = np.stack(masks).astype(np.float32)
    return jnp.asarray(G, BF16), jnp.asarray(M, F32)


def _gla_kernel(q_ref, k_ref, v_ref, r_ref, la_ref, G_ref, M_ref, gn_ref,
                o_ref, s_out_ref, st_ref):
    c = pl.program_id(1)
    C = GLA_CHUNK
    dk = q_ref.shape[1] // GLA_HEADS
    dv = v_ref.shape[1] // GLA_HEADS
    nlev = M_ref.shape[0] - 1

    @pl.when(c == 0)
    def _():
        st_ref[...] = jnp.zeros_like(st_ref)

    la = la_ref[...]
    hi = la.astype(BF16)
    r1 = la - hi.astype(F32)
    mid = r1.astype(BF16)
    lo = (r1 - mid.astype(F32)).astype(BF16)
    G = G_ref[...]
    X = jnp.exp(_dot(G, hi) + _dot(G, mid) + _dot(G, lo))

    scale = dk ** -0.5
    for h in range(GLA_HEADS):
        Xh = X[:, h * dk:(h + 1) * dk]
        qh = q_ref[:, h * dk:(h + 1) * dk] * scale
        kh = k_ref[:, h * dk:(h + 1) * dk]
        vh = v_ref[:, h * dv:(h + 1) * dv]
        x_cum = Xh[0:C]
        x_rest = Xh[C:2 * C]
        st_old = st_ref[h]
        o = _dot_nt((qh * x_cum).astype(BF16), st_old.astype(BF16))
        qb = qh.astype(BF16)
        kb = kh.astype(BF16)
        att = _dot_nt(qb, kb) * M_ref[nlev]
        for l in range(nlev):
            xl = Xh[(2 + l) * C:(3 + l) * C]
            att = att + _dot_nt((qh * xl).astype(BF16), (kh * xl).astype(BF16)) * M_ref[l]
        vb = vh.astype(BF16)
        o = o + _dot(att.astype(BF16), vb)
        vt = vh.T.astype(BF16)
        st_new = st_old * x_cum[C - 1:C, :] + _dot(vt, (kh * x_rest).astype(BF16))
        st_ref[h] = st_new
        on = _rms(o, gn_ref[...]) * _silu(r_ref[:, h * dv:(h + 1) * dv])
        o_ref[:, h * dv:(h + 1) * dv] = on.astype(o_ref.dtype)

    @pl.when(c == pl.num_programs(1) - 1)
    def _():
        for h in range(GLA_HEADS):
            s_out_ref[0, h] = st_ref[h].T


def _gla_prompt(P, log_a, n_seq, seq, col_q, kw, vw, gla_norm):
    C = GLA_CHUNK
    assert seq % C == 0
    nc = seq // C
    dk = kw // GLA_HEADS
    dv = vw // GLA_HEADS
    qb, kb = col_q // kw, col_q // kw + 1
    vb = (col_q + 2 * kw) // vw
    rb = vb + 1
    assert col_q % kw == 0 and (col_q + 2 * kw) % vw == 0
    G, M = _gla_constants()
    rows = lambda n, c: n * nc + c
    return pl.pallas_call(
        _gla_kernel,
        grid=(n_seq, nc),
        in_specs=[pl.BlockSpec((C, kw), lambda n, c: (rows(n, c), qb)),
                  pl.BlockSpec((C, kw), lambda n, c: (rows(n, c), kb)),
                  pl.BlockSpec((C, vw), lambda n, c: (rows(n, c), vb)),
                  pl.BlockSpec((C, vw), lambda n, c: (rows(n, c), rb)),
                  pl.BlockSpec((C, kw), lambda n, c: (rows(n, c), 0)),
                  pl.BlockSpec(G.shape, lambda n, c: (0, 0)),
                  pl.BlockSpec(M.shape, lambda n, c: (0, 0, 0)),
                  pl.BlockSpec((1, dv), lambda n, c: (0, 0))],
        out_specs=[pl.BlockSpec((C, vw), lambda n, c: (rows(n, c), 0)),
                   pl.BlockSpec((1, GLA_HEADS, dk, dv), lambda n, c: (n, 0, 0, 0))],
        out_shape=[jax.ShapeDtypeStruct((n_seq * seq, vw), BF16),
                   jax.ShapeDtypeStruct((n_seq, GLA_HEADS, dk, dv), F32)],
        scratch_shapes=[pltpu.VMEM((GLA_HEADS, dv, dk), F32)],
        compiler_params=_cparams(2),
        name="gla_prompt",
    )(P, P, P, P, log_a, G, M, gla_norm.reshape(1, dv))


def _gla_step_kernel(s_ref, at_ref, kt_ref, qt_ref, v_ref, r_ref, gn_ref, sn_ref, o_ref):
    i = pl.program_id(0)
    bb, H, dk, dv = s_ref.shape
    B = at_ref.shape[1]
    lane = lax.broadcasted_iota(jnp.int32, (dk, B), 1)
    scale = dk ** -0.5
    for j in range(bb):
        pick = lane == i * bb + j

        def column(t_ref, h):
            return jnp.sum(jnp.where(pick, t_ref[h * dk:(h + 1) * dk, :], 0.0), axis=1, keepdims=True)

        for h in range(H):
            s_new = jnp.exp(column(at_ref, h)) * s_ref[j, h] + column(kt_ref, h) * v_ref[j, h]
            sn_ref[j, h] = s_new
            o = jnp.sum((column(qt_ref, h) * scale) * s_new, axis=0, keepdims=True)
            o_ref[j, h] = _rms(o, gn_ref[...]) * _silu(r_ref[j, h])


def _gla_sample(state, la, k, q, v, r, gla_norm):
    B, H, dk, dv = state.shape
    bb = 8
    rowv = lambda t: t.reshape(B, H, 1, dv)
    tspec = pl.BlockSpec((H * dk, B), lambda i: (0, 0))
    rspec = pl.BlockSpec((bb, H, 1, dv), lambda i: (i, 0, 0, 0))
    sspec = pl.BlockSpec((bb, H, dk, dv), lambda i: (i, 0, 0, 0))
    return pl.pallas_call(
        _gla_step_kernel,
        grid=(B // bb,),
        in_specs=[sspec, tspec, tspec, tspec, rspec, rspec,
                  pl.BlockSpec((1, dv), lambda i: (0, 0))],
        out_specs=[sspec, rspec],
        out_shape=[jax.ShapeDtypeStruct((B, H, dk, dv), F32),
                   jax.ShapeDtypeStruct((B, H, 1, dv), F32)],
        compiler_params=_cparams(1),
        name="gla_sample_step",
    )(state, la.T, k.T, q.T, rowv(v), rowv(r), gla_norm.reshape(1, dv))


def _route(logits):
    lane = lax.broadcasted_iota(jnp.int32, logits.shape, 1)
    neg = jnp.float32(-jnp.inf)
    logits = jnp.where(lane < N_EXPERTS, logits, neg)
    v1 = jnp.max(logits, axis=-1, keepdims=True)
    i1 = jnp.min(jnp.where(logits == v1, lane, LANES), axis=-1, keepdims=True)
    rest = jnp.where(lane == i1, neg, logits)
    v2 = jnp.max(rest, axis=-1, keepdims=True)
    i2 = jnp.min(jnp.where(rest == v2, lane, LANES), axis=-1, keepdims=True)
    e2 = jnp.exp(v2 - v1)
    g1 = 1.0 / (1.0 + e2)
    g2 = e2 / (1.0 + e2)
    idx = jnp.where(lane == 0, i1, jnp.where(lane == 1, i2, 0))
    gate = jnp.where(lane == 0, g1, jnp.where(lane == 1, g2, 0.0))
    return idx, gate


def _conv_out_kernel(n_prompt_blocks, blocks_per_seq, n_tail, z_ref, halo_ref, b_ref, cw_ref,
                     h0_ref, h1_ref, res_ref, w_ref, ng_ref, rw_ref, x_ref, xslab_ref, idx_ref,
                     gate_ref):
    i = pl.program_id(0)
    w0, w1, w2 = cw_ref[0:1, :], cw_ref[1:2, :], cw_ref[2:3, :]
    ns = x_ref.shape[1] // LANES

    def finish(rows, yc, res):
        y = res + _dot(yc.astype(BF16), w_ref[...])
        x_ref[0:rows, :] = y
        hn = _rms(y, ng_ref[...])
        for s in range(ns):
            xslab_ref[pl.ds(s, rows, stride=ns), :] = hn[:, s * LANES:(s + 1) * LANES]
        h_hi = hn.astype(BF16)
        h_lo = (hn - h_hi.astype(F32)).astype(BF16)
        rw = rw_ref[...]
        r_hi = rw.astype(BF16)
        r_lo = (rw - r_hi.astype(F32)).astype(BF16)
        logits = _dot(h_hi, r_hi) + (_dot(h_hi, r_lo) + _dot(h_lo, r_hi))
        idx, gate = _route(logits)
        idx_ref[0:rows, :] = idx
        gate_ref[0:rows, :] = gate

    @pl.when(i < n_prompt_blocks)
    def _():
        z = z_ref[...]
        row = lax.broadcasted_iota(jnp.int32, z.shape, 0)
        halo = jnp.where(i % blocks_per_seq == 0, 0.0, halo_ref[...])
        z1 = jnp.where(row == 0, halo[7:8, :], pltpu.roll(z, 1, 0))
        z2 = pltpu.roll(z, 2, 0)
        z2 = jnp.where(row == 0, halo[6:7, :], jnp.where(row == 1, halo[7:8, :], z2))
        finish(z.shape[0], b_ref[...] * (w0 * z2 + w1 * z1 + w2 * z), res_ref[...])

    @pl.when(i >= n_prompt_blocks)
    def _():
        conv = w0 * h0_ref[...] + w1 * h1_ref[...] + w2 * z_ref[0:n_tail, :]
        finish(n_tail, b_ref[0:n_tail, :] * conv, res_ref[0:n_tail, :])


def _conv_out_route(z, bgate, conv_w, hist0, hist1, res, w_out, norm_g, router_w, n_prompt, seq):
    T, Cd = z.shape
    D = w_out.shape[1]
    B = T - n_prompt
    tm = _pick_tile(math.gcd(seq, n_prompt), 256)
    assert B <= tm and hist0.shape == (B, Cd)
    npb = n_prompt // tm
    wpad = jnp.zeros((D, LANES), F32).at[:, :N_EXPERTS].set(router_w)
    row = lambda i: (i, 0)
    fixed = lambda i: (0, 0)
    return pl.pallas_call(
        functools.partial(_conv_out_kernel, npb, seq // tm, B),
        grid=(npb + 1,),
        in_specs=[pl.BlockSpec((tm, Cd), row),
                  pl.BlockSpec((8, Cd), lambda i: (jnp.maximum(i * (tm // 8) - 1, 0), 0)),
                  pl.BlockSpec((tm, Cd), row),
                  pl.BlockSpec(conv_w.shape, fixed),
                  pl.BlockSpec((B, Cd), fixed),
                  pl.BlockSpec((B, Cd), fixed),
                  pl.BlockSpec((tm, D), row),
                  pl.BlockSpec((Cd, D), fixed),
                  pl.BlockSpec((1, D), fixed),
                  pl.BlockSpec((D, LANES), fixed)],
        out_specs=[pl.BlockSpec((tm, D), row),
                   pl.BlockSpec((tm * (D // LANES), LANES), row),
                   pl.BlockSpec((tm, LANES), row),
                   pl.BlockSpec((tm, LANES), row)],
        out_shape=[jax.ShapeDtypeStruct((T, D), F32),
                   jax.ShapeDtypeStruct((T * (D // LANES), LANES), F32),
                   jax.ShapeDtypeStruct((T, LANES), jnp.int32),
                   jax.ShapeDtypeStruct((T, LANES), F32)],
        compiler_params=_cparams(1),
        name="conv_out_route",
    )(z, z, bgate, conv_w, hist0, hist1, res, w_out.astype(BF16), norm_g.reshape(1, D), wpad)


def _dispatch_kernel(tok_ref, nxt_ref, used_ref, x_hbm, o_ref, buf_ref, sem):
    i = pl.program_id(0)
    nsteps = pl.num_programs(0)
    n = o_ref.shape[0]
    ns = x_hbm.shape[1]
    slot = i % 2

    def row_copy(t_ref, r, s):
        return pltpu.make_async_copy(x_hbm.at[t_ref[0, 0, r]],
                                     buf_ref.at[s, pl.ds(pl.multiple_of(r * ns, ns), ns)], sem.at[s])

    def start_rows(t_ref, s):
        def body(r, carry):
            row_copy(t_ref, r, s).start()
            return carry

        lax.fori_loop(0, n, body, 0, unroll=8)

    @pl.when(jnp.logical_and(i == 0, used_ref[0] > 0))
    def _():
        start_rows(tok_ref, 0)

    @pl.when(jnp.logical_and(i + 1 < nsteps, used_ref[jnp.minimum(i + 1, nsteps - 1)] > 0))
    def _():
        start_rows(nxt_ref, 1 - slot)

    @pl.when(used_ref[i] > 0)
    def _():
        pltpu.make_async_copy(buf_ref.at[slot], buf_ref.at[slot], sem.at[slot]).wait()

        for s in range(ns):
            o_ref[:, s * LANES:(s + 1) * LANES] = (
                buf_ref[slot, pl.ds(s, n, stride=ns), :].astype(o_ref.dtype))

    @pl.when(used_ref[i] == 0)
    def _():
        o_ref[...] = jnp.zeros_like(o_ref)


def _dispatch(x_slab, slot_tok, sub_used):
    D = x_slab.shape[1] * x_slab.shape[2]
    nsub = sub_used.shape[0]
    SB = MOE_SUB
    tok3 = slot_tok.reshape(nsub, 1, SB)
    return pl.pallas_call(
        _dispatch_kernel,
        grid_spec=pltpu.PrefetchScalarGridSpec(
            num_scalar_prefetch=0,
            grid=(nsub,),
            in_specs=[pl.BlockSpec((1, 1, SB), lambda i: (i, 0, 0), memory_space=pltpu.SMEM),
                      pl.BlockSpec((1, 1, SB), lambda i: (jnp.minimum(i + 1, nsub - 1), 0, 0),
                                   memory_space=pltpu.SMEM),
                      pl.BlockSpec(memory_space=pltpu.SMEM),
                      pl.BlockSpec(memory_space=pl.ANY)],
            out_specs=pl.BlockSpec((SB, D), lambda i: (i, 0)),
            scratch_shapes=[pltpu.VMEM((2, SB * (D // LANES), LANES), F32),
                            pltpu.SemaphoreType.DMA((2,))],
        ),
        out_shape=jax.ShapeDtypeStruct((nsub * SB, D), BF16),
        compiler_params=_cparams(1),
        name="moe_dispatch",
    )(tok3, tok3, sub_used, x_slab)


def _row_pieces(rows, total, compute, skip):
    for s in range(0, total, MOE_BIG):
        size = min(MOE_BIG, total - s)
        whole = rows > s + size - MOE_SMALL

        @pl.when(whole)
        def _():
            compute(s, size)

        @pl.when(jnp.logical_not(whole))
        def _():
            for t in range(s, s + size, MOE_SMALL):
                @pl.when(rows > t)
                def _():
                    compute(t, MOE_SMALL)

                @pl.when(rows <= t)
                def _():
                    skip(t, MOE_SMALL)


def _moe_up_kernel(ce_ref, cs_ref, cr_ref, nu_ref, x_ref, wg_ref, wu_ref, o_ref, wbf_ref):
    c = pl.program_id(0)

    @pl.when(c < nu_ref[0])
    def _():
        def compute(s, n):
            if s == 0:
                wg = wg_ref[0].astype(BF16)
                wu = wu_ref[0].astype(BF16)
                wbf_ref[0] = wg
                wbf_ref[1] = wu
            else:
                wg = wbf_ref[0]
                wu = wbf_ref[1]
            x = x_ref[pl.ds(s, n), :]
            o_ref[pl.ds(s, n), :] = (_silu(_dot(x, wg)) * _dot(x, wu)).astype(o_ref.dtype)

        def skip(s, n):
            o_ref[pl.ds(s, n), :] = jnp.zeros((n, o_ref.shape[1]), o_ref.dtype)

        _row_pieces(cr_ref[c], x_ref.shape[0], compute, skip)

    @pl.when(c >= nu_ref[0])
    def _():
        o_ref[...] = jnp.zeros_like(o_ref)


def _moe_up(xs, w_gu, meta, R, tf, direct):
    ce, cs, cr, nu = meta
    nch = ce.shape[0]
    D = xs.shape[1]
    F = w_gu.shape[2] // 2
    nf = F // tf

    def fidx(c, f, nu):
        return jnp.where(c < nu[0], f, nf - 1)

    if direct:
        in_specs = [pl.BlockSpec((R, D), lambda c, f, *_: (c, 0), pipeline_mode=pl.Buffered(1)),
                    pl.BlockSpec((1, D, tf), lambda c, f, *_: (c, 0, f)),
                    pl.BlockSpec((1, D, tf), lambda c, f, *_: (c, 0, nf + f))]
    else:
        in_specs = [pl.BlockSpec((R, D), lambda c, f, ce, cs, cr, nu: (cs[c], 0),
                                 pipeline_mode=pl.Buffered(1)),
                    pl.BlockSpec((1, D, tf), lambda c, f, ce, cs, cr, nu: (ce[c], 0, fidx(c, f, nu))),
                    pl.BlockSpec((1, D, tf), lambda c, f, ce, cs, cr, nu: (ce[c], 0, nf + fidx(c, f, nu)))]
    return pl.pallas_call(
        _moe_up_kernel,
        grid_spec=pltpu.PrefetchScalarGridSpec(
            num_scalar_prefetch=4,
            grid=(nch, nf),
            in_specs=in_specs,
            out_specs=pl.BlockSpec((R, tf), lambda c, f, ce, cs, cr, nu: (c, f)),
            scratch_shapes=[pltpu.VMEM((2, D, tf), BF16)],
        ),
        out_shape=jax.ShapeDtypeStruct((nch * R, F), BF16),
        compiler_params=_cparams(2),
        name="moe_gate_up",
    )(ce, cs, cr, nu, xs, w_gu, w_gu)


def _moe_down_kernel(ce_ref, cs_ref, cr_ref, nu_ref, h_ref, w_ref, o_ref, wbf_ref):
    c = pl.program_id(0)
    k = pl.program_id(2)

    @pl.when(c < nu_ref[0])
    def _():
        def weight(s):
            if s == 0:
                w = w_ref[0].astype(BF16)
                wbf_ref[...] = w
                return w
            return wbf_ref[...]

        def compute(s, n):
            @pl.when(k == 0)
            def _():
                o_ref[pl.ds(s, n), :] = _dot(h_ref[pl.ds(s, n), :], weight(s))

            @pl.when(k > 0)
            def _():
                o_ref[pl.ds(s, n), :] = (o_ref[pl.ds(s, n), :]
                                         + _dot(h_ref[pl.ds(s, n), :], weight(s)))

        def skip(s, n):
            @pl.when(k == 0)
            def _():
                o_ref[pl.ds(s, n), :] = jnp.zeros((n, o_ref.shape[1]), F32)

        _row_pieces(cr_ref[c], h_ref.shape[0], compute, skip)

    @pl.when(jnp.logical_and(c >= nu_ref[0], k == 0))
    def _():
        o_ref[...] = jnp.zeros_like(o_ref)


def _moe_down(hid, w_down, meta, R, tn, tk, direct):
    ce, cs, cr, nu = meta
    nch = ce.shape[0]
    F = hid.shape[1]
    D = w_down.shape[2]
    nn, nk = D // tn, F // tk

    def last_if_unused(c, v, nu, last):
        return jnp.where(c < nu[0], v, last)

    if direct:
        in_specs = [pl.BlockSpec((R, tk), lambda c, n, k, *_: (c, k)),
                    pl.BlockSpec((1, tk, tn), lambda c, n, k, *_: (c, k, n))]
    else:
        in_specs = [pl.BlockSpec((R, tk), lambda c, n, k, ce, cs, cr, nu:
                                 (cs[c], last_if_unused(c, k, nu, nk - 1))),
                    pl.BlockSpec((1, tk, tn), lambda c, n, k, ce, cs, cr, nu:
                                 (ce[c], last_if_unused(c, k, nu, nk - 1),
                                  last_if_unused(c, n, nu, nn - 1)))]
    return pl.pallas_call(
        _moe_down_kernel,
        grid_spec=pltpu.PrefetchScalarGridSpec(
            num_scalar_prefetch=4,
            grid=(nch, nn, nk),
            in_specs=in_specs,
            out_specs=pl.BlockSpec((R, tn), lambda c, n, k, ce, cs, cr, nu: (c, n)),
            scratch_shapes=[pltpu.VMEM((tk, tn), BF16)],
        ),
        out_shape=jax.ShapeDtypeStruct((nch * R, D), F32),
        compiler_params=_cparams(3),
        name="moe_down",
    )(ce, cs, cr, nu, hid, w_down)


def _combine_kernel(n_prompt_blocks, has_overflow, p1_ref, p2_ref, n1_ref, n2_ref, x_ref, g1_ref,
                    g2_ref, fg_ref, ys_hbm, *rest):
    yo_hbm = rest[0] if has_overflow else None
    op_ref, os_ref, buf_ref, sem = rest[-4:]
    i = pl.program_id(0)
    nsteps = pl.num_programs(0)
    n = x_ref.shape[0]
    slot = i % 2
    split = ys_hbm.shape[0]

    def start_row(pos, k, r, s):
        dst = buf_ref.at[s, k, pl.ds(r, 1)]
        if yo_hbm is None:
            pltpu.make_async_copy(ys_hbm.at[pl.ds(pos, 1)], dst, sem.at[s]).start()
        else:
            @pl.when(pos < split)
            def _():
                pltpu.make_async_copy(ys_hbm.at[pl.ds(pos, 1)], dst, sem.at[s]).start()

            @pl.when(pos >= split)
            def _():
                pltpu.make_async_copy(yo_hbm.at[pl.ds(pos - split, 1)], dst, sem.at[s]).start()

    def start_rows(a_ref, b_ref, s):
        def body(r, carry):
            start_row(a_ref[0, 0, r], 0, r, s)
            start_row(b_ref[0, 0, r], 1, r, s)
            return carry

        lax.fori_loop(0, n, body, 0, unroll=8)

    @pl.when(i == 0)
    def _():
        start_rows(p1_ref, p2_ref, 0)

    @pl.when(i + 1 < nsteps)
    def _():
        start_rows(n1_ref, n2_ref, 1 - slot)

    pltpu.make_async_copy(buf_ref.at[slot], buf_ref.at[slot], sem.at[slot]).wait()
    y = x_ref[...] + (g1_ref[...] * buf_ref[slot, 0] + g2_ref[...] * buf_ref[slot, 1])
    yn = _rms(y, fg_ref[...])

    @pl.when(i < n_prompt_blocks)
    def _():
        op_ref[...] = yn

    @pl.when(i >= n_prompt_blocks)
    def _():
        os_ref[...] = yn


def _combine(x, ys, ys_over, pos1, pos2, g1, g2, final_g, n_prompt):
    T, D = x.shape
    srcs = [ys] if ys_over is None else [ys, ys_over]
    tr = T - n_prompt
    assert n_prompt % tr == 0
    nb = T // tr
    npb = n_prompt // tr
    cur = pl.BlockSpec((1, 1, tr), lambda i: (i, 0, 0), memory_space=pltpu.SMEM)
    nxt = pl.BlockSpec((1, 1, tr), lambda i: (jnp.minimum(i + 1, nb - 1), 0, 0),
                       memory_space=pltpu.SMEM)
    p1 = pos1.reshape(nb, 1, tr)
    p2 = pos2.reshape(nb, 1, tr)
    return pl.pallas_call(
        functools.partial(_combine_kernel, npb, ys_over is not None),
        grid_spec=pltpu.PrefetchScalarGridSpec(
            num_scalar_prefetch=0,
            grid=(nb,),
            in_specs=[cur, cur, nxt, nxt,
                      pl.BlockSpec((tr, D), lambda i: (i, 0)),
                      pl.BlockSpec((tr, 1), lambda i: (i, 0)),
                      pl.BlockSpec((tr, 1), lambda i: (i, 0)),
                      pl.BlockSpec((1, D), lambda i: (0, 0))]
                     + [pl.BlockSpec(memory_space=pl.ANY)] * len(srcs),
            out_specs=[pl.BlockSpec((tr, D), lambda i: (jnp.minimum(i, npb - 1), 0)),
                       pl.BlockSpec((tr, D), lambda i: (0, 0))],
            scratch_shapes=[pltpu.VMEM((2, 2, tr, D), F32), pltpu.SemaphoreType.DMA((2,))],
        ),
        out_shape=[jax.ShapeDtypeStruct((n_prompt, D), F32),
                   jax.ShapeDtypeStruct((tr, D), F32)],
        compiler_params=_cparams(1),
        name="moe_combine_final_norm",
    )(p1, p2, p1, p2, x, g1, g2, final_g.reshape(1, D), *srcs)


def _moe_plan(idx, T):
    M = T * TOP_K
    E = N_EXPERTS
    R = -(-int(M / E * MOE_CAPACITY) // MOE_SUB) * MOE_SUB
    n_over = -(-M // R)
    i32 = jnp.int32
    flat_e = idx.reshape(M)
    onehot = (flat_e[:, None] == jnp.arange(E, dtype=i32)[None, :]).astype(i32)
    csum = jnp.cumsum(onehot, axis=0)
    rank = jnp.sum((csum - onehot) * onehot, axis=1)
    counts = csum[-1]
    o_chunks = jnp.maximum((counts + R - 1) // R - 1, 0)
    o_first = jnp.cumsum(o_chunks) - o_chunks
    n_used = jnp.sum(o_chunks).astype(i32)
    over = rank - R
    slot = jnp.where(rank < R, flat_e * R + rank,
                     (E + o_first[flat_e] + over // R) * R + over % R)
    flat_tok = jnp.arange(M, dtype=i32) // TOP_K
    slot_tok = jnp.zeros(((E + n_over) * R,), i32).at[slot].set(flat_tok)
    spc = R // MOE_SUB

    def sub_used(c_rows):
        sub = jnp.arange(c_rows.shape[0] * spc, dtype=i32)
        return ((sub % spc) * MOE_SUB < c_rows[sub // spc]).astype(i32)

    m_rows = jnp.minimum(counts, R).astype(i32)
    eidx = jnp.arange(E, dtype=i32)
    main = dict(tok=slot_tok[:E * R], used=sub_used(m_rows),
                meta=(eidx, eidx, m_rows, jnp.full((1,), E, i32)))
    cidx = jnp.arange(n_over, dtype=i32)
    c_e = jnp.clip(jnp.searchsorted(jnp.cumsum(o_chunks), cidx, side='right'), 0, E - 1).astype(i32)
    c_rows = jnp.clip(counts[c_e] - R - (cidx - o_first[c_e]) * R, 0, R).astype(i32)
    used = cidx < n_used
    last = jnp.maximum(n_used - 1, 0)
    c_src = jnp.where(used, cidx, last).astype(i32)
    c_e = jnp.where(used, c_e, c_e[last]).astype(i32)
    c_rows = jnp.where(used, c_rows, 0).astype(i32)
    overflow = dict(tok=slot_tok[E * R:], used=sub_used(c_rows),
                    meta=(c_e, c_src, c_rows, n_used.reshape(1)))
    pos = slot.reshape(T, TOP_K).astype(i32)
    return R, main, overflow, n_used, pos[:, 0], pos[:, 1]


def kernel(x_prompt, x_sample, state_gla, state_conv, ev_norm_mix, ev_w_in, ev_sgu_norm, ev_sgu_w, ev_sgu_b, ev_gla_w_alpha, ev_gla_b_alpha, ev_gla_norm, ev_w_out, ev_norm_ffn, ev_ffn_w_gu, ev_ffn_w_down, od_norm_mix, od_conv_w_in, od_conv_w, od_conv_w_out, od_norm_ffn, od_router_w, od_moe_w_gu, od_moe_w_down, final_norm):
    nb, seq, D = x_prompt.shape
    B = x_sample.shape[0]
    assert x_sample.shape[1] == 1
    Tp = nb * seq
    T = Tp + B
    xp0 = x_prompt.reshape(Tp, D)
    xs0 = x_sample.reshape(B, D)

    sw = ev_sgu_norm.shape[1]
    kw = ev_gla_w_alpha.shape[2]
    rank = ev_gla_w_alpha.shape[1]
    vw = GLA_HEADS * ev_gla_norm.shape[1]
    n_main = 2 * sw + 2 * kw + 2 * vw
    assert ev_w_in.shape[2] == n_main + rank
    tm = _pick_tile(T, 1664)

    h = _norm_bf16_stacked(xp0, xs0, ev_norm_mix[0])
    P = _matmul(h, ev_w_in, n_main, tm, 512, name="mix_in_proj")
    log_a = _log_decay(h, ev_w_in, n_main, ev_gla_w_alpha[0], ev_gla_b_alpha[0], tm)
    a_out, v_sample = _sgu(P, Tp, B, sw, ev_sgu_norm[0], ev_sgu_w[0], ev_sgu_b[0])
    col_q = 2 * sw
    o_p, gla_p = _gla_prompt(P, log_a, nb, seq, col_q, kw, vw, ev_gla_norm[0])
    Ps = P[Tp:]
    q_s = Ps[:, col_q:col_q + kw]
    k_s = Ps[:, col_q + kw:col_q + 2 * kw]
    v_s = Ps[:, col_q + 2 * kw:col_q + 2 * kw + vw]
    r_s = Ps[:, col_q + 2 * kw + vw:]
    gla_s, o_s = _gla_sample(state_gla[0], log_a[Tp:], k_s, q_s, v_s, r_s, ev_gla_norm[0])
    x1, h = _mix_out(a_out, o_p, o_s.reshape(B, vw).astype(BF16), xp0, xs0, ev_w_out[0],
                     ev_norm_ffn[0])

    hid = _swiglu_up(h, ev_ffn_w_gu[0], tm, 512)
    x2 = _matmul(hid, ev_ffn_w_down[0], D, _pick_tile(T, 832), 256, res=x1, name="ffn_down")

    h = _norm_bf16(x2, od_norm_mix[0])
    bgate, z = _conv_in(h, od_conv_w_in[0], tm, 256)
    x3, h3_slab, idx, gates = _conv_out_route(
        z, bgate, od_conv_w[0], state_conv[0][:, 0], state_conv[0][:, 1],
        x2, od_conv_w_out[0], od_norm_ffn[0], od_router_w[0], Tp, seq)
    h3_slab = h3_slab.reshape(T, D // LANES, LANES)
    conv_p = jnp.stack([z[(n + 1) * seq - 2:(n + 1) * seq] for n in range(nb)])
    conv_s = jnp.concatenate([state_conv[0][:, 1:], z[Tp:].reshape(B, 1, D)], axis=1)

    R, main, overflow, n_overflow, pos1, pos2 = _moe_plan(idx[:, :TOP_K], T)
    FE = od_moe_w_down.shape[2]
    tf, tk = _pick_tile(FE, 512, LANES), _pick_tile(FE, 1024, LANES)

    def experts(part, direct):
        xs = _dispatch(h3_slab, part["tok"], part["used"])
        hid = _moe_up(xs, od_moe_w_gu[0], part["meta"], R, tf, direct)
        return _moe_down(hid, od_moe_w_down[0], part["meta"], R, 1024, tk, direct)

    ys = experts(main, True)
    g1, g2 = gates[:, 0:1], gates[:, 1:2]
    y_p, y_s = lax.cond(
        n_overflow > 0,
        lambda: _combine(x3, ys, experts(overflow, False), pos1, pos2, g1, g2, final_norm, Tp),
        lambda: _combine(x3, ys, None, pos1, pos2, g1, g2, final_norm, Tp))

    y_prompt = y_p.reshape(nb, seq, D)
    y_sample = y_s.reshape(B, 1, D)
    return (y_prompt, y_sample, gla_p[None], gla_s[None],
            v_sample.reshape(1, B, 1, sw), conv_p[None], conv_s[None])
```

```python
import functools
import math

import numpy as np
import jax
import jax.numpy as jnp
from jax import lax
from jax.experimental import pallas as pl
from jax.experimental.pallas import tpu as pltpu

F32 = jnp.float32
BF16 = jnp.bfloat16
EPS = 1e-6

SGU_CHUNK = 128
SGU_GROUPS = 4
GLA_HEADS = 4
GLA_TAU = 16.0
GLA_CHUNK = 128
N_EXPERTS = 8
TOP_K = 2
MOE_SUB = 256
MOE_CAPACITY = 1.35
MOE_BIG = 1024
MOE_SMALL = 128
LANES = 128
VMEM_LIMIT = 52 * 1024 * 1024


def _cparams(n_axes, vmem=VMEM_LIMIT):
    return pltpu.CompilerParams(dimension_semantics=("arbitrary",) * n_axes,
                                vmem_limit_bytes=vmem)


def _pick_tile(n, cap, mult=16):
    best = None
    for t in range(mult, min(n, cap) + 1, mult):
        if n % t == 0:
            best = t
    assert best is not None, (n, cap, mult)
    return best


def _rms(x, g):
    ms = jnp.mean(x * x, axis=-1, keepdims=True)
    return (x * lax.rsqrt(ms + EPS)) * g


def _gelu(x):
    c = math.sqrt(2.0 / math.pi)
    return x * (0.5 * (1.0 + jnp.tanh(c * (x + 0.044715 * (x * x * x)))))


def _silu(x):
    return x * (1.0 / (1.0 + jnp.exp(-x)))


def _dot(a, b):
    return jnp.dot(a, b, preferred_element_type=F32)


def _dot_nt(a, b):
    return lax.dot_general(a, b, (((1,), (1,)), ((), ())), preferred_element_type=F32)


def _norm_kernel(x_ref, g_ref, o_ref):
    o_ref[...] = _rms(x_ref[...], g_ref[...]).astype(BF16)


def _norm_bf16(x, g):
    T, D = x.shape
    tr = _pick_tile(T, 512)
    return pl.pallas_call(
        _norm_kernel,
        grid=(T // tr,),
        in_specs=[pl.BlockSpec((tr, D), lambda i: (i, 0)),
                  pl.BlockSpec((1, D), lambda i: (0, 0))],
        out_specs=pl.BlockSpec((tr, D), lambda i: (i, 0)),
        out_shape=jax.ShapeDtypeStruct((T, D), BF16),
        compiler_params=_cparams(1),
        name="rmsnorm_bf16",
    )(x, g.reshape(1, D))


def _norm2_kernel(n_prompt_blocks, xp_ref, xs_ref, g_ref, o_ref):
    i = pl.program_id(0)

    @pl.when(i < n_prompt_blocks)
    def _():
        o_ref[...] = _rms(xp_ref[...], g_ref[...]).astype(BF16)

    @pl.when(i >= n_prompt_blocks)
    def _():
        o_ref[0:xs_ref.shape[0], :] = _rms(xs_ref[...], g_ref[...]).astype(BF16)


def _norm_bf16_stacked(xp, xs, g):
    Tp, D = xp.shape
    B = xs.shape[0]
    tr = _pick_tile(Tp, 512)
    assert B <= tr
    npb = Tp // tr
    return pl.pallas_call(
        functools.partial(_norm2_kernel, npb),
        grid=(npb + 1,),
        in_specs=[pl.BlockSpec((tr, D), lambda i: (jnp.minimum(i, npb - 1), 0)),
                  pl.BlockSpec((B, D), lambda i: (0, 0)),
                  pl.BlockSpec((1, D), lambda i: (0, 0))],
        out_specs=pl.BlockSpec((tr, D), lambda i: (i, 0)),
        out_shape=jax.ShapeDtypeStruct((Tp + B, D), BF16),
        compiler_params=_cparams(1),
        name="rmsnorm_stack_bf16",
    )(xp, xs, g.reshape(1, D))


def _mm_kernel(x_ref, w_ref, o_ref):
    o_ref[...] = _dot(x_ref[...], w_ref[...].astype(BF16)).astype(o_ref.dtype)


def _mm_res_kernel(x_ref, w_ref, r_ref, o_ref):
    o_ref[...] = r_ref[...] + _dot(x_ref[...], w_ref[...].astype(BF16))


def _matmul(x, w, n_cols, tm, tn, res=None, name="matmul"):
    T, K = x.shape
    grid = (T // tm, n_cols // tn)
    if w.ndim == 3:
        w_spec = pl.BlockSpec((None, K, tn), lambda i, j: (0, 0, j))
    else:
        w_spec = pl.BlockSpec((K, tn), lambda i, j: (0, j))
    in_specs = [pl.BlockSpec((tm, K), lambda i, j: (i, 0)), w_spec]
    args = [x, w]
    kern = _mm_kernel
    if res is not None:
        in_specs.append(pl.BlockSpec((tm, tn), lambda i, j: (i, j)))
        args.append(res)
        kern = _mm_res_kernel
    return pl.pallas_call(
        kern, grid=grid, in_specs=in_specs,
        out_specs=pl.BlockSpec((tm, tn), lambda i, j: (i, j)),
        out_shape=jax.ShapeDtypeStruct((T, n_cols), F32),
        compiler_params=_cparams(2),
        name=name,
    )(*args)


def _mix_out_kernel(n_full, n_tail, a_ref, op_ref, os_ref, rp_ref, rs_ref, w_ref, ng_ref,
                    x_ref, h_ref):
    i = pl.program_id(0)
    sw = a_ref.shape[1]

    def finish(rows, a, o, res):
        y = res + _dot(a, w_ref[0:sw, :]) + _dot(o, w_ref[sw:, :])
        x_ref[0:rows, :] = y
        h_ref[0:rows, :] = _rms(y, ng_ref[...]).astype(BF16)

    @pl.when(i < n_full)
    def _():
        finish(a_ref.shape[0], a_ref[...], op_ref[...], rp_ref[...])

    @pl.when(i >= n_full)
    def _():
        finish(n_tail, a_ref[0:n_tail, :], os_ref[...], rs_ref[...])


def _mix_out(a, o_p, o_s, res_p, res_s, w, norm_g):
    T, sw = a.shape
    Tp, vw = o_p.shape
    B = o_s.shape[0]
    D = w.shape[1]
    assert T == Tp + B
    tm = _pick_tile(Tp, 256)
    assert B <= tm
    n_full = Tp // tm
    clamp = lambda i: (jnp.minimum(i, n_full - 1), 0)
    row = lambda i: (i, 0)
    fixed = lambda i: (0, 0)
    return pl.pallas_call(
        functools.partial(_mix_out_kernel, n_full, B),
        grid=(n_full + 1,),
        in_specs=[pl.BlockSpec((tm, sw), row),
                  pl.BlockSpec((tm, vw), clamp),
                  pl.BlockSpec((B, vw), fixed),
                  pl.BlockSpec((tm, D), clamp),
                  pl.BlockSpec((B, D), fixed),
                  pl.BlockSpec((sw + vw, D), fixed),
                  pl.BlockSpec((1, D), fixed)],
        out_specs=[pl.BlockSpec((tm, D), row),
                   pl.BlockSpec((tm, D), row)],
        out_shape=[jax.ShapeDtypeStruct((T, D), F32),
                   jax.ShapeDtypeStruct((T, D), BF16)],
        compiler_params=_cparams(1),
        name="mix_out_proj",
    )(a, o_p, o_s, res_p, res_s, w.astype(BF16), norm_g.reshape(1, D))


def _swiglu_kernel(x_ref, wg_ref, wu_ref, o_ref):
    x = x_ref[...]
    g = _dot(x, wg_ref[...].astype(BF16))
    u = _dot(x, wu_ref[...].astype(BF16))
    o_ref[...] = (_silu(g) * u).astype(o_ref.dtype)


def _swiglu_up(x, w_gu, tm, tn):
    T, K = x.shape
    F = w_gu.shape[1] // 2
    nj = F // tn
    return pl.pallas_call(
        _swiglu_kernel,
        grid=(T // tm, nj),
        in_specs=[pl.BlockSpec((tm, K), lambda i, j: (i, 0)),
                  pl.BlockSpec((K, tn), lambda i, j: (0, j)),
                  pl.BlockSpec((K, tn), lambda i, j: (0, j + nj))],
        out_specs=pl.BlockSpec((tm, tn), lambda i, j: (i, j)),
        out_shape=jax.ShapeDtypeStruct((T, F), BF16),
        compiler_params=_cparams(2),
        name="ffn_gate_up",
    )(x, w_gu, w_gu)


def _conv_in_kernel(x_ref, wb_ref, wc_ref, wu_ref, b_ref, z_ref):
    x = x_ref[...]
    b_ref[...] = _dot(x, wb_ref[...].astype(BF16))
    c = _dot(x, wc_ref[...].astype(BF16))
    u = _dot(x, wu_ref[...].astype(BF16))
    z_ref[...] = c * u


def _conv_in(x, w, tm, tn):
    T, K = x.shape
    C = w.shape[1] // 3
    nj = C // tn
    return pl.pallas_call(
        _conv_in_kernel,
        grid=(T // tm, nj),
        in_specs=[pl.BlockSpec((tm, K), lambda i, j: (i, 0)),
                  pl.BlockSpec((K, tn), lambda i, j: (0, j)),
                  pl.BlockSpec((K, tn), lambda i, j: (0, j + nj)),
                  pl.BlockSpec((K, tn), lambda i, j: (0, j + 2 * nj))],
        out_specs=[pl.BlockSpec((tm, tn), lambda i, j: (i, j)),
                   pl.BlockSpec((tm, tn), lambda i, j: (i, j))],
        out_shape=[jax.ShapeDtypeStruct((T, C), F32),
                   jax.ShapeDtypeStruct((T, C), F32)],
        compiler_params=_cparams(2),
        name="conv_in_proj",
    )(x, w, w, w)


def _loga_kernel(x_ref, wl_ref, wa_ref, b_ref, o_ref):
    rank = wa_ref.shape[0]
    g = _dot(x_ref[...], wl_ref[:, 0:rank].astype(BF16))
    z = _dot(g.astype(BF16), wa_ref[...].astype(BF16)) + b_ref[...]
    ls = jnp.minimum(z, 0.0) - jnp.log1p(jnp.exp(-jnp.abs(z)))
    o_ref[...] = ls * (1.0 / GLA_TAU)


def _log_decay(x, w_in, col, w_alpha, b_alpha, tm):
    T, K = x.shape
    R, KW = w_alpha.shape
    assert col % LANES == 0 and R <= LANES
    return pl.pallas_call(
        _loga_kernel,
        grid=(T // tm,),
        in_specs=[pl.BlockSpec((tm, K), lambda i: (i, 0)),
                  pl.BlockSpec((None, K, LANES), lambda i: (0, 0, col // LANES)),
                  pl.BlockSpec((R, KW), lambda i: (0, 0)),
                  pl.BlockSpec((1, KW), lambda i: (0, 0))],
        out_specs=pl.BlockSpec((tm, KW), lambda i: (i, 0)),
        out_shape=jax.ShapeDtypeStruct((T, KW), F32),
        compiler_params=_cparams(1),
        name="gla_log_decay",
    )(x, w_in, w_alpha, b_alpha.reshape(1, KW))


def _sgu_kernel(n_prompt_tiles, u_ref, v_ref, ng_ref, w_ref, bt_ref, o_ref, vs_ref):
    i = pl.program_id(0)
    C = SGU_CHUNK
    gd = u_ref.shape[1] // SGU_GROUPS
    row = lax.broadcasted_iota(jnp.int32, (C, C), 0)
    col = lax.broadcasted_iota(jnp.int32, (C, C), 1)

    def chunk(r0, weights, biases, keep_v):
        gu = _gelu(u_ref[r0:r0 + C, :])
        vn = _rms(_gelu(v_ref[r0:r0 + C, :]), ng_ref[...])
        if keep_v:
            vs_ref[...] = vn
        vb = vn.astype(BF16)
        for g in range(SGU_GROUPS):
            cols = slice(g * gd, (g + 1) * gd)
            mixed = _dot(weights[g], vb[:, cols]) + biases[g]
            o_ref[r0:r0 + C, cols] = (gu[:, cols] * mixed).astype(o_ref.dtype)

    @pl.when(i < n_prompt_tiles)
    def _():
        weights = [jnp.where(col <= row, w_ref[g], 0.0).astype(BF16) for g in range(SGU_GROUPS)]
        biases = [bt_ref[:, g:g + 1] for g in range(SGU_GROUPS)]
        for r0 in range(0, u_ref.shape[0], C):
            chunk(r0, weights, biases, False)

    @pl.when(i >= n_prompt_tiles)
    def _():
        weights = [jnp.where(col == row, w_ref[g, 0:1, 0:1], 0.0).astype(BF16)
                   for g in range(SGU_GROUPS)]
        biases = [bt_ref[0:1, g:g + 1] for g in range(SGU_GROUPS)]
        chunk(0, weights, biases, True)


def _sgu(P, n_prompt, n_sample, width, sgu_norm, sgu_w, sgu_b):
    C = SGU_CHUNK
    assert n_prompt % C == 0 and n_sample == C
    T = n_prompt + n_sample
    tr = C * math.gcd(n_prompt // C, 4)
    npt = n_prompt // tr
    return pl.pallas_call(
        functools.partial(_sgu_kernel, npt),
        grid=(npt + 1,),
        in_specs=[pl.BlockSpec((tr, width), lambda i: (i, 0)),
                  pl.BlockSpec((tr, width), lambda i: (i, 1)),
                  pl.BlockSpec((1, width), lambda i: (0, 0)),
                  pl.BlockSpec((SGU_GROUPS, C, C), lambda i: (0, 0, 0)),
                  pl.BlockSpec((C, SGU_GROUPS), lambda i: (0, 0))],
        out_specs=[pl.BlockSpec((tr, width), lambda i: (i, 0)),
                   pl.BlockSpec((C, width), lambda i: (0, 0))],
        out_shape=[jax.ShapeDtypeStruct((T, width), BF16),
                   jax.ShapeDtypeStruct((n_sample, width), F32)],
        compiler_params=_cparams(1),
        name="sgu_mix",
    )(P, P, sgu_norm.reshape(1, width), sgu_w, sgu_b.T)


def _gla_constants():
    C = GLA_CHUNK
    s = np.arange(C)[None, :]
    i = np.arange(C)[:, None]
    gs = [s <= i, s > i]
    masks = []
    b = C // 2
    while b >= 1:
        upper = (i % (2 * b)) >= b
        m = (i // (2 * b)) * (2 * b) + b - 1
        gs.append(np.where(upper, (s > m) & (s <= i), (s > i) & (s <= m)))
        same = (i // (2 * b)) == (s // (2 * b))
        masks.append(upper & same & ~upper.T)
        b //= 2
    masks.append(np.eye(C, dtype=bool))
    G = np.concatenate(gs, axis=0).astype(np.float32)
    M = np.stack(masks).astype(np.float32)
    return jnp.asarray(G, BF16), jnp.asarray(M, F32)


def _gla_kernel(nseq, *refs):
    ins = [refs[5 * s:5 * s + 5] for s in range(nseq)]
    G_ref, M_ref, gn_ref, o_ref, s_out_ref, st_ref = refs[5 * nseq:]
    c = pl.program_id(1)
    C = GLA_CHUNK
    hk = ins[0][0].shape[1]
    dk = hk // GLA_HEADS
    dv = ins[0][2].shape[1] // GLA_HEADS
    nlev = M_ref.shape[0] - 1

    @pl.when(c == 0)
    def _():
        st_ref[...] = jnp.zeros_like(st_ref)

    parts = []
    for s in range(nseq):
        la = ins[s][4][...]
        hi = la.astype(BF16)
        r1 = la - hi.astype(F32)
        mid = r1.astype(BF16)
        lo = (r1 - mid.astype(F32)).astype(BF16)
        parts.append(jnp.concatenate([hi, mid, lo], axis=0))
    X = jnp.exp(_dot(G_ref[...], jnp.concatenate(parts, axis=1)))

    scale = dk ** -0.5
    for s in range(nseq):
        q_ref, k_ref, v_ref, r_ref, _ = ins[s]
        for h in range(GLA_HEADS):
            Xh = X[:, s * hk + h * dk:s * hk + (h + 1) * dk]
            qh = q_ref[:, h * dk:(h + 1) * dk] * scale
            kh = k_ref[:, h * dk:(h + 1) * dk]
            vh = v_ref[:, h * dv:(h + 1) * dv]
            x_cum = Xh[0:C]
            x_rest = Xh[C:2 * C]
            st_old = st_ref[s, h]
            o = _dot_nt((qh * x_cum).astype(BF16), st_old.astype(BF16))
            att = _dot_nt(qh.astype(BF16), kh.astype(BF16)) * M_ref[nlev]
            for l in range(nlev):
                xl = Xh[(2 + l) * C:(3 + l) * C]
                att = att + _dot_nt((qh * xl).astype(BF16), (kh * xl).astype(BF16)) * M_ref[l]
            o = o + _dot(att.astype(BF16), vh.astype(BF16))
            vt = vh.T.astype(BF16)
            st_ref[s, h] = st_old * x_cum[C - 1:C, :] + _dot(vt, (kh * x_rest).astype(BF16))
            on = _rms(o, gn_ref[...]) * _silu(r_ref[:, h * dv:(h + 1) * dv])
            o_ref[s, :, h * dv:(h + 1) * dv] = on.astype(o_ref.dtype)

    @pl.when(c == pl.num_programs(1) - 1)
    def _():
        for s in range(nseq):
            for h in range(GLA_HEADS):
                s_out_ref[s, h] = st_ref[s, h].T


def _gla_prompt(P, log_a, n_seq, seq, col_q, kw, vw, gla_norm):
    C = GLA_CHUNK
    assert seq % C == 0
    nc = seq // C
    dk = kw // GLA_HEADS
    dv = vw // GLA_HEADS
    qb, kb = col_q // kw, col_q // kw + 1
    vb = (col_q + 2 * kw) // vw
    rb = vb + 1
    assert col_q % kw == 0 and (col_q + 2 * kw) % vw == 0
    nseq = 2 if n_seq % 2 == 0 else 1
    G, M = _gla_constants()
    G3 = jnp.concatenate([G, G, G], axis=1)

    def seq_specs(s):
        rows = lambda n, c: (n * nseq + s) * nc + c
        return [pl.BlockSpec((C, kw), lambda n, c: (rows(n, c), qb)),
                pl.BlockSpec((C, kw), lambda n, c: (rows(n, c), kb)),
                pl.BlockSpec((C, vw), lambda n, c: (rows(n, c), vb)),
                pl.BlockSpec((C, vw), lambda n, c: (rows(n, c), rb)),
                pl.BlockSpec((C, kw), lambda n, c: (rows(n, c), 0))]

    in_specs = [sp for s in range(nseq) for sp in seq_specs(s)]
    in_specs += [pl.BlockSpec(G3.shape, lambda n, c: (0, 0)),
                 pl.BlockSpec(M.shape, lambda n, c: (0, 0, 0)),
                 pl.BlockSpec((1, dv), lambda n, c: (0, 0))]
    o, states = pl.pallas_call(
        functools.partial(_gla_kernel, nseq),
        grid=(n_seq // nseq, nc),
        in_specs=in_specs,
        out_specs=[pl.BlockSpec((None, nseq, C, vw), lambda n, c: (n, 0, c, 0)),
                   pl.BlockSpec((nseq, GLA_HEADS, dk, dv), lambda n, c: (n, 0, 0, 0))],
        out_shape=[jax.ShapeDtypeStruct((n_seq // nseq, nseq, seq, vw), BF16),
                   jax.ShapeDtypeStruct((n_seq, GLA_HEADS, dk, dv), F32)],
        scratch_shapes=[pltpu.VMEM((nseq, GLA_HEADS, dv, dk), F32)],
        compiler_params=_cparams(2),
        name="gla_prompt",
    )(*([P, P, P, P, log_a] * nseq), G3, M, gla_norm.reshape(1, dv))
    return o.reshape(n_seq * seq, vw), states


def _gla_step_kernel(s_ref, at_ref, kt_ref, qt_ref, v_ref, r_ref, gn_ref, sn_ref, o_ref):
    i = pl.program_id(0)
    bb, H, dk, dv = s_ref.shape
    B = at_ref.shape[1]
    lane = lax.broadcasted_iota(jnp.int32, (dk, B), 1)
    scale = dk ** -0.5
    for j in range(bb):
        pick = lane == i * bb + j

        def column(t_ref, h):
            return jnp.sum(jnp.where(pick, t_ref[h * dk:(h + 1) * dk, :], 0.0), axis=1, keepdims=True)

        for h in range(H):
            s_new = jnp.exp(column(at_ref, h)) * s_ref[j, h] + column(kt_ref, h) * v_ref[j, h]
            sn_ref[j, h] = s_new
            o = jnp.sum((column(qt_ref, h) * scale) * s_new, axis=0, keepdims=True)
            o_ref[j, h] = _rms(o, gn_ref[...]) * _silu(r_ref[j, h])


def _gla_sample(state, la, k, q, v, r, gla_norm):
    B, H, dk, dv = state.shape
    bb = 8
    rowv = lambda t: t.reshape(B, H, 1, dv)
    tspec = pl.BlockSpec((H * dk, B), lambda i: (0, 0))
    rspec = pl.BlockSpec((bb, H, 1, dv), lambda i: (i, 0, 0, 0))
    sspec = pl.BlockSpec((bb, H, dk, dv), lambda i: (i, 0, 0, 0))
    return pl.pallas_call(
        _gla_step_kernel,
        grid=(B // bb,),
        in_specs=[sspec, tspec, tspec, tspec, rspec, rspec,
                  pl.BlockSpec((1, dv), lambda i: (0, 0))],
        out_specs=[sspec, rspec],
        out_shape=[jax.ShapeDtypeStruct((B, H, dk, dv), F32),
                   jax.ShapeDtypeStruct((B, H, 1, dv), F32)],
        compiler_params=_cparams(1),
        name="gla_sample_step",
    )(state, la.T, k.T, q.T, rowv(v), rowv(r), gla_norm.reshape(1, dv))


def _route(logits):
    lane = lax.broadcasted_iota(jnp.int32, logits.shape, 1)
    neg = jnp.float32(-jnp.inf)
    logits = jnp.where(lane < N_EXPERTS, logits, neg)
    v1 = jnp.max(logits, axis=-1, keepdims=True)
    i1 = jnp.min(jnp.where(logits == v1, lane, LANES), axis=-1, keepdims=True)
    rest = jnp.where(lane == i1, neg, logits)
    v2 = jnp.max(rest, axis=-1, keepdims=True)
    i2 = jnp.min(jnp.where(rest == v2, lane, LANES), axis=-1, keepdims=True)
    e2 = jnp.exp(v2 - v1)
    g1 = 1.0 / (1.0 + e2)
    g2 = e2 / (1.0 + e2)
    idx = jnp.where(lane == 0, i1, jnp.where(lane == 1, i2, 0))
    gate = jnp.where(lane == 0, g1, jnp.where(lane == 1, g2, 0.0))
    return idx, gate


def _conv_out_kernel(n_prompt_blocks, blocks_per_seq, n_tail, z_ref, halo_ref, b_ref, cw_ref,
                     h0_ref, h1_ref, res_ref, w_ref, ng_ref, rw_ref, x_ref, xslab_ref, idx_ref,
                     gate_ref):
    i = pl.program_id(0)
    w0, w1, w2 = cw_ref[0:1, :], cw_ref[1:2, :], cw_ref[2:3, :]
    ns = x_ref.shape[1] // LANES

    def finish(rows, yc, res):
        y = res + _dot(yc.astype(BF16), w_ref[...])
        x_ref[0:rows, :] = y
        hn = _rms(y, ng_ref[...])
        for s in range(ns):
            xslab_ref[pl.ds(s, rows, stride=ns), :] = hn[:, s * LANES:(s + 1) * LANES]
        h_hi = hn.astype(BF16)
        h_lo = (hn - h_hi.astype(F32)).astype(BF16)
        rw = rw_ref[...]
        r_hi = rw.astype(BF16)
        r_lo = (rw - r_hi.astype(F32)).astype(BF16)
        logits = _dot(h_hi, r_hi) + (_dot(h_hi, r_lo) + _dot(h_lo, r_hi))
        idx, gate = _route(logits)
        idx_ref[0:rows, :] = idx
        gate_ref[0:rows, :] = gate

    @pl.when(i < n_prompt_blocks)
    def _():
        z = z_ref[...]
        row = lax.broadcasted_iota(jnp.int32, z.shape, 0)
        halo = jnp.where(i % blocks_per_seq == 0, 0.0, halo_ref[...])
        z1 = jnp.where(row == 0, halo[7:8, :], pltpu.roll(z, 1, 0))
        z2 = pltpu.roll(z, 2, 0)
        z2 = jnp.where(row == 0, halo[6:7, :], jnp.where(row == 1, halo[7:8, :], z2))
        finish(z.shape[0], b_ref[...] * (w0 * z2 + w1 * z1 + w2 * z), res_ref[...])

    @pl.when(i >= n_prompt_blocks)
    def _():
        conv = w0 * h0_ref[...] + w1 * h1_ref[...] + w2 * z_ref[0:n_tail, :]
        finish(n_tail, b_ref[0:n_tail, :] * conv, res_ref[0:n_tail, :])


def _conv_out_route(z, bgate, conv_w, hist0, hist1, res, w_out, norm_g, router_w, n_prompt, seq):
    T, Cd = z.shape
    D = w_out.shape[1]
    B = T - n_prompt
    tm = _pick_tile(math.gcd(seq, n_prompt), 256)
    assert B <= tm and hist0.shape == (B, Cd)
    npb = n_prompt // tm
    wpad = jnp.zeros((D, LANES), F32).at[:, :N_EXPERTS].set(router_w)
    row = lambda i: (i, 0)
    fixed = lambda i: (0, 0)
    return pl.pallas_call(
        functools.partial(_conv_out_kernel, npb, seq // tm, B),
        grid=(npb + 1,),
        in_specs=[pl.BlockSpec((tm, Cd), row),
                  pl.BlockSpec((8, Cd), lambda i: (jnp.maximum(i * (tm // 8) - 1, 0), 0)),
                  pl.BlockSpec((tm, Cd), row),
                  pl.BlockSpec(conv_w.shape, fixed),
                  pl.BlockSpec((B, Cd), fixed),
                  pl.BlockSpec((B, Cd), fixed),
                  pl.BlockSpec((tm, D), row),
                  pl.BlockSpec((Cd, D), fixed),
                  pl.BlockSpec((1, D), fixed),
                  pl.BlockSpec((D, LANES), fixed)],
        out_specs=[pl.BlockSpec((tm, D), row),
                   pl.BlockSpec((tm * (D // LANES), LANES), row),
                   pl.BlockSpec((tm, LANES), row),
                   pl.BlockSpec((tm, LANES), row)],
        out_shape=[jax.ShapeDtypeStruct((T, D), F32),
                   jax.ShapeDtypeStruct((T * (D // LANES), LANES), F32),
                   jax.ShapeDtypeStruct((T, LANES), jnp.int32),
                   jax.ShapeDtypeStruct((T, LANES), F32)],
        compiler_params=_cparams(1),
        name="conv_out_route",
    )(z, z, bgate, conv_w, hist0, hist1, res, w_out.astype(BF16), norm_g.reshape(1, D), wpad)


def _dispatch_kernel(tok_ref, nxt_ref, used_ref, x_hbm, o_ref, buf_ref, sem):
    i = pl.program_id(0)
    nsteps = pl.num_programs(0)
    n = o_ref.shape[0]
    ns = x_hbm.shape[1]
    slot = i % 2

    def row_copy(t_ref, r, s):
        return pltpu.make_async_copy(x_hbm.at[t_ref[0, 0, r]],
                                     buf_ref.at[s, pl.ds(pl.multiple_of(r * ns, ns), ns)], sem.at[s])

    def start_rows(t_ref, s):
        def body(r, carry):
            row_copy(t_ref, r, s).start()
            return carry

        lax.fori_loop(0, n, body, 0, unroll=8)

    @pl.when(jnp.logical_and(i == 0, used_ref[0] > 0))
    def _():
        start_rows(tok_ref, 0)

    @pl.when(jnp.logical_and(i + 1 < nsteps, used_ref[jnp.minimum(i + 1, nsteps - 1)] > 0))
    def _():
        start_rows(nxt_ref, 1 - slot)

    @pl.when(used_ref[i] > 0)
    def _():
        pltpu.make_async_copy(buf_ref.at[slot], buf_ref.at[slot], sem.at[slot]).wait()

        for s in range(ns):
            o_ref[:, s * LANES:(s + 1) * LANES] = (
                buf_ref[slot, pl.ds(s, n, stride=ns), :].astype(o_ref.dtype))

    @pl.when(used_ref[i] == 0)
    def _():
        o_ref[...] = jnp.zeros_like(o_ref)


def _dispatch(x_slab, slot_tok, sub_used):
    D = x_slab.shape[1] * x_slab.shape[2]
    nsub = sub_used.shape[0]
    SB = MOE_SUB
    tok3 = slot_tok.reshape(nsub, 1, SB)
    return pl.pallas_call(
        _dispatch_kernel,
        grid_spec=pltpu.PrefetchScalarGridSpec(
            num_scalar_prefetch=0,
            grid=(nsub,),
            in_specs=[pl.BlockSpec((1, 1, SB), lambda i: (i, 0, 0), memory_space=pltpu.SMEM),
                      pl.BlockSpec((1, 1, SB), lambda i: (jnp.minimum(i + 1, nsub - 1), 0, 0),
                                   memory_space=pltpu.SMEM),
                      pl.BlockSpec(memory_space=pltpu.SMEM),
                      pl.BlockSpec(memory_space=pl.ANY)],
            out_specs=pl.BlockSpec((SB, D), lambda i: (i, 0)),
            scratch_shapes=[pltpu.VMEM((2, SB * (D // LANES), LANES), F32),
                            pltpu.SemaphoreType.DMA((2,))],
        ),
        out_shape=jax.ShapeDtypeStruct((nsub * SB, D), BF16),
        compiler_params=_cparams(1),
        name="moe_dispatch",
    )(tok3, tok3, sub_used, x_slab)


def _row_pieces(rows, total, compute, skip):
    for s in range(0, total, MOE_BIG):
        size = min(MOE_BIG, total - s)
        whole = rows > s + size - MOE_SMALL

        @pl.when(whole)
        def _():
            compute(s, size)

        @pl.when(jnp.logical_not(whole))
        def _():
            for t in range(s, s + size, MOE_SMALL):
                @pl.when(rows > t)
                def _():
                    compute(t, MOE_SMALL)

                @pl.when(rows <= t)
                def _():
                    skip(t, MOE_SMALL)


def _moe_up_kernel(ce_ref, cs_ref, cr_ref, nu_ref, x_ref, wg_ref, wu_ref, o_ref, wbf_ref):
    c = pl.program_id(0)

    @pl.when(c < nu_ref[0])
    def _():
        def compute(s, n):
            if s == 0:
                wg = wg_ref[0].astype(BF16)
                wu = wu_ref[0].astype(BF16)
                wbf_ref[0] = wg
                wbf_ref[1] = wu
            else:
                wg = wbf_ref[0]
                wu = wbf_ref[1]
            x = x_ref[pl.ds(s, n), :]
            o_ref[pl.ds(s, n), :] = (_silu(_dot(x, wg)) * _dot(x, wu)).astype(o_ref.dtype)

        def skip(s, n):
            o_ref[pl.ds(s, n), :] = jnp.zeros((n, o_ref.shape[1]), o_ref.dtype)

        _row_pieces(cr_ref[c], x_ref.shape[0], compute, skip)

    @pl.when(c >= nu_ref[0])
    def _():
        o_ref[...] = jnp.zeros_like(o_ref)


def _moe_up(xs, w_gu, meta, R, tf, direct):
    ce, cs, cr, nu = meta
    nch = ce.shape[0]
    D = xs.shape[1]
    F = w_gu.shape[2] // 2
    nf = F // tf

    def fidx(c, f, nu):
        return jnp.where(c < nu[0], f, nf - 1)

    if direct:
        in_specs = [pl.BlockSpec((R, D), lambda c, f, *_: (c, 0), pipeline_mode=pl.Buffered(1)),
                    pl.BlockSpec((1, D, tf), lambda c, f, *_: (c, 0, f)),
                    pl.BlockSpec((1, D, tf), lambda c, f, *_: (c, 0, nf + f))]
    else:
        in_specs = [pl.BlockSpec((R, D), lambda c, f, ce, cs, cr, nu: (cs[c], 0),
                                 pipeline_mode=pl.Buffered(1)),
                    pl.BlockSpec((1, D, tf), lambda c, f, ce, cs, cr, nu: (ce[c], 0, fidx(c, f, nu))),
                    pl.BlockSpec((1, D, tf), lambda c, f, ce, cs, cr, nu: (ce[c], 0, nf + fidx(c, f, nu)))]
    return pl.pallas_call(
        _moe_up_kernel,
        grid_spec=pltpu.PrefetchScalarGridSpec(
            num_scalar_prefetch=4,
            grid=(nch, nf),
            in_specs=in_specs,
            out_specs=pl.BlockSpec((R, tf), lambda c, f, ce, cs, cr, nu: (c, f)),
            scratch_shapes=[pltpu.VMEM((2, D, tf), BF16)],
        ),
        out_shape=jax.ShapeDtypeStruct((nch * R, F), BF16),
        compiler_params=_cparams(2),
        name="moe_gate_up",
    )(ce, cs, cr, nu, xs, w_gu, w_gu)


def _moe_down_kernel(ce_ref, cs_ref, cr_ref, nu_ref, h_ref, w_ref, o_ref, wbf_ref):
    c = pl.program_id(0)
    k = pl.program_id(2)

    @pl.when(c < nu_ref[0])
    def _():
        def weight(s):
            if s == 0:
                w = w_ref[0].astype(BF16)
                wbf_ref[...] = w
                return w
            return wbf_ref[...]

        def compute(s, n):
            @pl.when(k == 0)
            def _():
                o_ref[pl.ds(s, n), :] = _dot(h_ref[pl.ds(s, n), :], weight(s))

            @pl.when(k > 0)
            def _():
                o_ref[pl.ds(s, n), :] = (o_ref[pl.ds(s, n), :]
                                         + _dot(h_ref[pl.ds(s, n), :], weight(s)))

        def skip(s, n):
            @pl.when(k == 0)
            def _():
                o_ref[pl.ds(s, n), :] = jnp.zeros((n, o_ref.shape[1]), F32)

        _row_pieces(cr_ref[c], h_ref.shape[0], compute, skip)

    @pl.when(jnp.logical_and(c >= nu_ref[0], k == 0))
    def _():
        o_ref[...] = jnp.zeros_like(o_ref)


def _moe_down(hid, w_down, meta, R, tn, tk, direct):
    ce, cs, cr, nu = meta
    nch = ce.shape[0]
    F = hid.shape[1]
    D = w_down.shape[2]
    nn, nk = D // tn, F // tk

    def last_if_unused(c, v, nu, last):
        return jnp.where(c < nu[0], v, last)

    if direct:
        in_specs = [pl.BlockSpec((R, tk), lambda c, n, k, *_: (c, k)),
                    pl.BlockSpec((1, tk, tn), lambda c, n, k, *_: (c, k, n))]
    else:
        in_specs = [pl.BlockSpec((R, tk), lambda c, n, k, ce, cs, cr, nu:
                                 (cs[c], last_if_unused(c, k, nu, nk - 1))),
                    pl.BlockSpec((1, tk, tn), lambda c, n, k, ce, cs, cr, nu:
                                 (ce[c], last_if_unused(c, k, nu, nk - 1),
                                  last_if_unused(c, n, nu, nn - 1)))]
    return pl.pallas_call(
        _moe_down_kernel,
        grid_spec=pltpu.PrefetchScalarGridSpec(
            num_scalar_prefetch=4,
            grid=(nch, nn, nk),
            in_specs=in_specs,
            out_specs=pl.BlockSpec((R, tn), lambda c, n, k, ce, cs, cr, nu: (c, n)),
            scratch_shapes=[pltpu.VMEM((tk, tn), BF16)],
        ),
        out_shape=jax.ShapeDtypeStruct((nch * R, D), F32),
        compiler_params=_cparams(3),
        name="moe_down",
    )(ce, cs, cr, nu, hid, w_down)


def _combine_kernel(n_prompt_blocks, has_overflow, p1_ref, p2_ref, n1_ref, n2_ref, x_ref, g1_ref,
                    g2_ref, fg_ref, ys_hbm, *rest):
    yo_hbm = rest[0] if has_overflow else None
    op_ref, os_ref, buf_ref, sem = rest[-4:]
    i = pl.program_id(0)
    nsteps = pl.num_programs(0)
    n = x_ref.shape[0]
    slot = i % 2
    split = ys_hbm.shape[0]

    def start_row(pos, k, r, s):
        dst = buf_ref.at[s, k, pl.ds(r, 1)]
        if yo_hbm is None:
            pltpu.make_async_copy(ys_hbm.at[pl.ds(pos, 1)], dst, sem.at[s]).start()
        else:
            @pl.when(pos < split)
            def _():
                pltpu.make_async_copy(ys_hbm.at[pl.ds(pos, 1)], dst, sem.at[s]).start()

            @pl.when(pos >= split)
            def _():
                pltpu.make_async_copy(yo_hbm.at[pl.ds(pos - split, 1)], dst, sem.at[s]).start()

    def start_rows(a_ref, b_ref, s):
        def body(r, carry):
            start_row(a_ref[0, 0, r], 0, r, s)
            start_row(b_ref[0, 0, r], 1, r, s)
            return carry

        lax.fori_loop(0, n, body, 0, unroll=8)

    @pl.when(i == 0)
    def _():
        start_rows(p1_ref, p2_ref, 0)

    @pl.when(i + 1 < nsteps)
    def _():
        start_rows(n1_ref, n2_ref, 1 - slot)

    pltpu.make_async_copy(buf_ref.at[slot], buf_ref.at[slot], sem.at[slot]).wait()
    y = x_ref[...] + (g1_ref[...] * buf_ref[slot, 0] + g2_ref[...] * buf_ref[slot, 1])
    yn = _rms(y, fg_ref[...])

    @pl.when(i < n_prompt_blocks)
    def _():
        op_ref[...] = yn

    @pl.when(i >= n_prompt_blocks)
    def _():
        os_ref[...] = yn


def _combine(x, ys, ys_over, pos1, pos2, g1, g2, final_g, n_prompt):
    T, D = x.shape
    srcs = [ys] if ys_over is None else [ys, ys_over]
    tr = T - n_prompt
    assert n_prompt % tr == 0
    nb = T // tr
    npb = n_prompt // tr
    cur = pl.BlockSpec((1, 1, tr), lambda i: (i, 0, 0), memory_space=pltpu.SMEM)
    nxt = pl.BlockSpec((1, 1, tr), lambda i: (jnp.minimum(i + 1, nb - 1), 0, 0),
                       memory_space=pltpu.SMEM)
    p1 = pos1.reshape(nb, 1, tr)
    p2 = pos2.reshape(nb, 1, tr)
    return pl.pallas_call(
        functools.partial(_combine_kernel, npb, ys_over is not None),
        grid_spec=pltpu.PrefetchScalarGridSpec(
            num_scalar_prefetch=0,
            grid=(nb,),
            in_specs=[cur, cur, nxt, nxt,
                      pl.BlockSpec((tr, D), lambda i: (i, 0)),
                      pl.BlockSpec((tr, 1), lambda i: (i, 0)),
                      pl.BlockSpec((tr, 1), lambda i: (i, 0)),
                      pl.BlockSpec((1, D), lambda i: (0, 0))]
                     + [pl.BlockSpec(memory_space=pl.ANY)] * len(srcs),
            out_specs=[pl.BlockSpec((tr, D), lambda i: (jnp.minimum(i, npb - 1), 0)),
                       pl.BlockSpec((tr, D), lambda i: (0, 0))],
            scratch_shapes=[pltpu.VMEM((2, 2, tr, D), F32), pltpu.SemaphoreType.DMA((2,))],
        ),
        out_shape=[jax.ShapeDtypeStruct((n_prompt, D), F32),
                   jax.ShapeDtypeStruct((tr, D), F32)],
        compiler_params=_cparams(1),
        name="moe_combine_final_norm",
    )(p1, p2, p1, p2, x, g1, g2, final_g.reshape(1, D), *srcs)


def _moe_plan(idx, T):
    M = T * TOP_K
    E = N_EXPERTS
    R = -(-int(M / E * MOE_CAPACITY) // MOE_SUB) * MOE_SUB
    n_over = -(-M // R)
    i32 = jnp.int32
    flat_e = idx.reshape(M)
    onehot = (flat_e[:, None] == jnp.arange(E, dtype=i32)[None, :]).astype(i32)
    csum = jnp.cumsum(onehot, axis=0)
    rank = jnp.sum((csum - onehot) * onehot, axis=1)
    counts = csum[-1]
    o_chunks = jnp.maximum((counts + R - 1) // R - 1, 0)
    o_first = jnp.cumsum(o_chunks) - o_chunks
    n_used = jnp.sum(o_chunks).astype(i32)
    over = rank - R
    slot = jnp.where(rank < R, flat_e * R + rank,
                     (E + o_first[flat_e] + over // R) * R + over % R)
    flat_tok = jnp.arange(M, dtype=i32) // TOP_K
    slot_tok = jnp.zeros(((E + n_over) * R,), i32).at[slot].set(flat_tok)
    spc = R // MOE_SUB

    def sub_used(c_rows):
        sub = jnp.arange(c_rows.shape[0] * spc, dtype=i32)
        return ((sub % spc) * MOE_SUB < c_rows[sub // spc]).astype(i32)

    m_rows = jnp.minimum(counts, R).astype(i32)
    eidx = jnp.arange(E, dtype=i32)
    main = dict(tok=slot_tok[:E * R], used=sub_used(m_rows),
                meta=(eidx, eidx, m_rows, jnp.full((1,), E, i32)))
    cidx = jnp.arange(n_over, dtype=i32)
    c_e = jnp.clip(jnp.searchsorted(jnp.cumsum(o_chunks), cidx, side='right'), 0, E - 1).astype(i32)
    c_rows = jnp.clip(counts[c_e] - R - (cidx - o_first[c_e]) * R, 0, R).astype(i32)
    used = cidx < n_used
    last = jnp.maximum(n_used - 1, 0)
    c_src = jnp.where(used, cidx, last).astype(i32)
    c_e = jnp.where(used, c_e, c_e[last]).astype(i32)
    c_rows = jnp.where(used, c_rows, 0).astype(i32)
    overflow = dict(tok=slot_tok[E * R:], used=sub_used(c_rows),
                    meta=(c_e, c_src, c_rows, n_used.reshape(1)))
    pos = slot.reshape(T, TOP_K).astype(i32)
    return R, main, overflow, n_used, pos[:, 0], pos[:, 1]


def kernel(x_prompt, x_sample, state_gla, state_conv, ev_norm_mix, ev_w_in, ev_sgu_norm, ev_sgu_w, ev_sgu_b, ev_gla_w_alpha, ev_gla_b_alpha, ev_gla_norm, ev_w_out, ev_norm_ffn, ev_ffn_w_gu, ev_ffn_w_down, od_norm_mix, od_conv_w_in, od_conv_w, od_conv_w_out, od_norm_ffn, od_router_w, od_moe_w_gu, od_moe_w_down, final_norm):
    nb, seq, D = x_prompt.shape
    B = x_sample.shape[0]
    assert x_sample.shape[1] == 1
    Tp = nb * seq
    T = Tp + B
    xp0 = x_prompt.reshape(Tp, D)
    xs0 = x_sample.reshape(B, D)

    sw = ev_sgu_norm.shape[1]
    kw = ev_gla_w_alpha.shape[2]
    rank = ev_gla_w_alpha.shape[1]
    vw = GLA_HEADS * ev_gla_norm.shape[1]
    n_main = 2 * sw + 2 * kw + 2 * vw
    assert ev_w_in.shape[2] == n_main + rank
    tm = _pick_tile(T, 1664)

    h = _norm_bf16_stacked(xp0, xs0, ev_norm_mix[0])
    P = _matmul(h, ev_w_in, n_main, tm, 512, name="mix_in_proj")
    log_a = _log_decay(h, ev_w_in, n_main, ev_gla_w_alpha[0], ev_gla_b_alpha[0], tm)
    a_out, v_sample = _sgu(P, Tp, B, sw, ev_sgu_norm[0], ev_sgu_w[0], ev_sgu_b[0])
    col_q = 2 * sw
    o_p, gla_p = _gla_prompt(P, log_a, nb, seq, col_q, kw, vw, ev_gla_norm[0])
    Ps = P[Tp:]
    q_s = Ps[:, col_q:col_q + kw]
    k_s = Ps[:, col_q + kw:col_q + 2 * kw]
    v_s = Ps[:, col_q + 2 * kw:col_q + 2 * kw + vw]
    r_s = Ps[:, col_q + 2 * kw + vw:]
    gla_s, o_s = _gla_sample(state_gla[0], log_a[Tp:], k_s, q_s, v_s, r_s, ev_gla_norm[0])
    x1, h = _mix_out(a_out, o_p, o_s.reshape(B, vw).astype(BF16), xp0, xs0, ev_w_out[0],
                     ev_norm_ffn[0])

    hid = _swiglu_up(h, ev_ffn_w_gu[0], tm, 512)
    x2 = _matmul(hid, ev_ffn_w_down[0], D, _pick_tile(T, 832), 256, res=x1, name="ffn_down")

    h = _norm_bf16(x2, od_norm_mix[0])
    bgate, z = _conv_in(h, od_conv_w_in[0], tm, 256)
    x3, h3_slab, idx, gates = _conv_out_route(
        z, bgate, od_conv_w[0], state_conv[0][:, 0], state_conv[0][:, 1],
        x2, od_conv_w_out[0], od_norm_ffn[0], od_router_w[0], Tp, seq)
    h3_slab = h3_slab.reshape(T, D // LANES, LANES)
    conv_p = jnp.stack([z[(n + 1) * seq - 2:(n + 1) * seq] for n in range(nb)])
    conv_s = jnp.concatenate([state_conv[0][:, 1:], z[Tp:].reshape(B, 1, D)], axis=1)

    R, main, overflow, n_overflow, pos1, pos2 = _moe_plan(idx[:, :TOP_K], T)
    FE = od_moe_w_down.shape[2]
    tf, tk = _pick_tile(FE, 512, LANES), _pick_tile(FE, 1024, LANES)

    def experts(part, direct):
        xs = _dispatch(h3_slab, part["tok"], part["used"])
        hid = _moe_up(xs, od_moe_w_gu[0], part["meta"], R, tf, direct)
        return _moe_down(hid, od_moe_w_down[0], part["meta"], R, 1024, tk, direct)

    ys = experts(main, True)
    g1, g2 = gates[:, 0:1], gates[:, 1:2]
    y_p, y_s = lax.cond(
        n_overflow > 0,
        lambda: _combine(x3, ys, experts(overflow, False), pos1, pos2, g1, g2, final_norm, Tp),
        lambda: _combine(x3, ys, None, pos1, pos2, g1, g2, final_norm, Tp))

    y_prompt = y_p.reshape(nb, seq, D)
    y_sample = y_s.reshape(B, 1, D)
    return (y_prompt, y_sample, gla_p[None], gla_s[None],
            v_sample.reshape(1, B, 1, sw), conv_p[None], conv_s[None])
```

```python
import functools
import math

import numpy as np
import jax
import jax.numpy as jnp
from jax import lax
from jax.experimental import pallas as pl
from jax.experimental.pallas import tpu as pltpu

F32 = jnp.float32
BF16 = jnp.bfloat16
EPS = 1e-6

SGU_CHUNK = 128
SGU_GROUPS = 4
GLA_HEADS = 4
GLA_TAU = 16.0
GLA_CHUNK = 128
N_EXPERTS = 8
TOP_K = 2
MOE_SUB = 256
MOE_CAPACITY = 1.35
MOE_BIG = 1024
MOE_SMALL = 128
LANES = 128
VMEM_LIMIT = 52 * 1024 * 1024


def _cparams(n_axes, vmem=VMEM_LIMIT):
    return pltpu.CompilerParams(dimension_semantics=("arbitrary",) * n_axes,
                                vmem_limit_bytes=vmem)


def _pick_tile(n, cap, mult=16):
    best = None
    for t in range(mult, min(n, cap) + 1, mult):
        if n % t == 0:
            best = t
    assert best is not None, (n, cap, mult)
    return best


def _rms(x, g):
    ms = jnp.mean(x * x, axis=-1, keepdims=True)
    return (x * lax.rsqrt(ms + EPS)) * g


def _gelu(x):
    c = math.sqrt(2.0 / math.pi)
    return x * (0.5 * (1.0 + jnp.tanh(c * (x + 0.044715 * (x * x * x)))))


def _silu(x):
    return x * (1.0 / (1.0 + jnp.exp(-x)))


def _dot(a, b):
    return jnp.dot(a, b, preferred_element_type=F32)


def _dot_nt(a, b):
    return lax.dot_general(a, b, (((1,), (1,)), ((), ())), preferred_element_type=F32)


def _norm_kernel(x_ref, g_ref, o_ref):
    o_ref[...] = _rms(x_ref[...], g_ref[...]).astype(BF16)


def _norm_bf16(x, g):
    T, D = x.shape
    tr = _pick_tile(T, 512)
    return pl.pallas_call(
        _norm_kernel,
        grid=(T // tr,),
        in_specs=[pl.BlockSpec((tr, D), lambda i: (i, 0)),
                  pl.BlockSpec((1, D), lambda i: (0, 0))],
        out_specs=pl.BlockSpec((tr, D), lambda i: (i, 0)),
        out_shape=jax.ShapeDtypeStruct((T, D), BF16),
        compiler_params=_cparams(1),
        name="rmsnorm_bf16",
    )(x, g.reshape(1, D))


def _norm2_kernel(n_prompt_blocks, xp_ref, xs_ref, g_ref, o_ref):
    i = pl.program_id(0)

    @pl.when(i < n_prompt_blocks)
    def _():
        o_ref[...] = _rms(xp_ref[...], g_ref[...]).astype(BF16)

    @pl.when(i >= n_prompt_blocks)
    def _():
        o_ref[0:xs_ref.shape[0], :] = _rms(xs_ref[...], g_ref[...]).astype(BF16)


def _norm_bf16_stacked(xp, xs, g):
    Tp, D = xp.shape
    B = xs.shape[0]
    tr = _pick_tile(Tp, 512)
    assert B <= tr
    npb = Tp // tr
    return pl.pallas_call(
        functools.partial(_norm2_kernel, npb),
        grid=(npb + 1,),
        in_specs=[pl.BlockSpec((tr, D), lambda i: (jnp.minimum(i, npb - 1), 0)),
                  pl.BlockSpec((B, D), lambda i: (0, 0)),
                  pl.BlockSpec((1, D), lambda i: (0, 0))],
        out_specs=pl.BlockSpec((tr, D), lambda i: (i, 0)),
        out_shape=jax.ShapeDtypeStruct((Tp + B, D), BF16),
        compiler_params=_cparams(1),
        name="rmsnorm_stack_bf16",
    )(xp, xs, g.reshape(1, D))


def _mm_kernel(x_ref, w_ref, o_ref):
    o_ref[...] = _dot(x_ref[...], w_ref[...].astype(BF16)).astype(o_ref.dtype)


def _mm_res_kernel(x_ref, w_ref, r_ref, o_ref):
    o_ref[...] = r_ref[...] + _dot(x_ref[...], w_ref[...].astype(BF16))


def _matmul(x, w, n_cols, tm, tn, res=None, name="matmul"):
    T, K = x.shape
    grid = (T // tm, n_cols // tn)
    if w.ndim == 3:
        w_spec = pl.BlockSpec((None, K, tn), lambda i, j: (0, 0, j))
    else:
        w_spec = pl.BlockSpec((K, tn), lambda i, j: (0, j))
    in_specs = [pl.BlockSpec((tm, K), lambda i, j: (i, 0)), w_spec]
    args = [x, w]
    kern = _mm_kernel
    if res is not None:
        in_specs.append(pl.BlockSpec((tm, tn), lambda i, j: (i, j)))
        args.append(res)
        kern = _mm_res_kernel
    return pl.pallas_call(
        kern, grid=grid, in_specs=in_specs,
        out_specs=pl.BlockSpec((tm, tn), lambda i, j: (i, j)),
        out_shape=jax.ShapeDtypeStruct((T, n_cols), F32),
        compiler_params=_cparams(2),
        name=name,
    )(*args)


def _mix_out_kernel(n_full, n_tail, a_ref, op_ref, os_ref, rp_ref, rs_ref, w_ref, ng_ref,
                    x_ref, h_ref):
    i = pl.program_id(0)
    sw = a_ref.shape[1]

    def finish(rows, a, o, res):
        y = res + _dot(a, w_ref[0:sw, :]) + _dot(o, w_ref[sw:, :])
        x_ref[0:rows, :] = y
        h_ref[0:rows, :] = _rms(y, ng_ref[...]).astype(BF16)

    @pl.when(i < n_full)
    def _():
        finish(a_ref.shape[0], a_ref[...], op_ref[...], rp_ref[...])

    @pl.when(i >= n_full)
    def _():
        finish(n_tail, a_ref[0:n_tail, :], os_ref[...], rs_ref[...])


def _mix_out(a, o_p, o_s, res_p, res_s, w, norm_g):
    T, sw = a.shape
    Tp, vw = o_p.shape
    B = o_s.shape[0]
    D = w.shape[1]
    assert T == Tp + B
    tm = _pick_tile(Tp, 256)
    assert B <= tm
    n_full = Tp // tm
    clamp = lambda i: (jnp.minimum(i, n_full - 1), 0)
    row = lambda i: (i, 0)
    fixed = lambda i: (0, 0)
    return pl.pallas_call(
        functools.partial(_mix_out_kernel, n_full, B),
        grid=(n_full + 1,),
        in_specs=[pl.BlockSpec((tm, sw), row),
                  pl.BlockSpec((tm, vw), clamp),
                  pl.BlockSpec((B, vw), fixed),
                  pl.BlockSpec((tm, D), clamp),
                  pl.BlockSpec((B, D), fixed),
                  pl.BlockSpec((sw + vw, D), fixed),
                  pl.BlockSpec((1, D), fixed)],
        out_specs=[pl.BlockSpec((tm, D), row),
                   pl.BlockSpec((tm, D), row)],
        out_shape=[jax.ShapeDtypeStruct((T, D), F32),
                   jax.ShapeDtypeStruct((T, D), BF16)],
        compiler_params=_cparams(1),
        name="mix_out_proj",
    )(a, o_p, o_s, res_p, res_s, w.astype(BF16), norm_g.reshape(1, D))


def _swiglu_kernel(x_ref, wg_ref, wu_ref, o_ref):
    x = x_ref[...]
    g = _dot(x, wg_ref[...].astype(BF16))
    u = _dot(x, wu_ref[...].astype(BF16))
    o_ref[...] = (_silu(g) * u).astype(o_ref.dtype)


def _swiglu_up(x, w_gu, tm, tn):
    T, K = x.shape
    F = w_gu.shape[1] // 2
    nj = F // tn
    return pl.pallas_call(
        _swiglu_kernel,
        grid=(T // tm, nj),
        in_specs=[pl.BlockSpec((tm, K), lambda i, j: (i, 0)),
                  pl.BlockSpec((K, tn), lambda i, j: (0, j)),
                  pl.BlockSpec((K, tn), lambda i, j: (0, j + nj))],
        out_specs=pl.BlockSpec((tm, tn), lambda i, j: (i, j)),
        out_shape=jax.ShapeDtypeStruct((T, F), BF16),
        compiler_params=_cparams(2),
        name="ffn_gate_up",
    )(x, w_gu, w_gu)


def _conv_in_kernel(x_ref, wb_ref, wc_ref, wu_ref, b_ref, z_ref):
    x = x_ref[...]
    b_ref[...] = _dot(x, wb_ref[...].astype(BF16))
    c = _dot(x, wc_ref[...].astype(BF16))
    u = _dot(x, wu_ref[...].astype(BF16))
    z_ref[...] = c * u


def _conv_in(x, w, tm, tn):
    T, K = x.shape
    C = w.shape[1] // 3
    nj = C // tn
    return pl.pallas_call(
        _conv_in_kernel,
        grid=(T // tm, nj),
        in_specs=[pl.BlockSpec((tm, K), lambda i, j: (i, 0)),
                  pl.BlockSpec((K, tn), lambda i, j: (0, j)),
                  pl.BlockSpec((K, tn), lambda i, j: (0, j + nj)),
                  pl.BlockSpec((K, tn), lambda i, j: (0, j + 2 * nj))],
        out_specs=[pl.BlockSpec((tm, tn), lambda i, j: (i, j)),
                   pl.BlockSpec((tm, tn), lambda i, j: (i, j))],
        out_shape=[jax.ShapeDtypeStruct((T, C), F32),
                   jax.ShapeDtypeStruct((T, C), F32)],
        compiler_params=_cparams(2),
        name="conv_in_proj",
    )(x, w, w, w)


def _loga_kernel(x_ref, wl_ref, wa_ref, b_ref, o_ref):
    rank = wa_ref.shape[0]
    g = _dot(x_ref[...], wl_ref[:, 0:rank].astype(BF16))
    z = _dot(g.astype(BF16), wa_ref[...].astype(BF16)) + b_ref[...]
    ls = jnp.minimum(z, 0.0) - jnp.log1p(jnp.exp(-jnp.abs(z)))
    o_ref[...] = ls * (1.0 / GLA_TAU)


def _log_decay(x, w_in, col, w_alpha, b_alpha, tm):
    T, K = x.shape
    R, KW = w_alpha.shape
    assert col % LANES == 0 and R <= LANES
    return pl.pallas_call(
        _loga_kernel,
        grid=(T // tm,),
        in_specs=[pl.BlockSpec((tm, K), lambda i: (i, 0)),
                  pl.BlockSpec((None, K, LANES), lambda i: (0, 0, col // LANES)),
                  pl.BlockSpec((R, KW), lambda i: (0, 0)),
                  pl.BlockSpec((1, KW), lambda i: (0, 0))],
        out_specs=pl.BlockSpec((tm, KW), lambda i: (i, 0)),
        out_shape=jax.ShapeDtypeStruct((T, KW), F32),
        compiler_params=_cparams(1),
        name="gla_log_decay",
    )(x, w_in, w_alpha, b_alpha.reshape(1, KW))


def _sgu_kernel(n_prompt_tiles, u_ref, v_ref, ng_ref, w_ref, bt_ref, o_ref, vs_ref):
    i = pl.program_id(0)
    C = SGU_CHUNK
    gd = u_ref.shape[1] // SGU_GROUPS
    row = lax.broadcasted_iota(jnp.int32, (C, C), 0)
    col = lax.broadcasted_iota(jnp.int32, (C, C), 1)

    def chunk(r0, weights, biases, keep_v):
        gu = _gelu(u_ref[r0:r0 + C, :])
        vn = _rms(_gelu(v_ref[r0:r0 + C, :]), ng_ref[...])
        if keep_v:
            vs_ref[...] = vn
        vb = vn.astype(BF16)
        for g in range(SGU_GROUPS):
            cols = slice(g * gd, (g + 1) * gd)
            mixed = _dot(weights[g], vb[:, cols]) + biases[g]
            o_ref[r0:r0 + C, cols] = (gu[:, cols] * mixed).astype(o_ref.dtype)

    @pl.when(i < n_prompt_tiles)
    def _():
        weights = [jnp.where(col <= row, w_ref[g], 0.0).astype(BF16) for g in range(SGU_GROUPS)]
        biases = [bt_ref[:, g:g + 1] for g in range(SGU_GROUPS)]
        for r0 in range(0, u_ref.shape[0], C):
            chunk(r0, weights, biases, False)

    @pl.when(i >= n_prompt_tiles)
    def _():
        weights = [jnp.where(col == row, w_ref[g, 0:1, 0:1], 0.0).astype(BF16)
                   for g in range(SGU_GROUPS)]
        biases = [bt_ref[0:1, g:g + 1] for g in range(SGU_GROUPS)]
        chunk(0, weights, biases, True)


def _sgu(P, n_prompt, n_sample, width, sgu_norm, sgu_w, sgu_b):
    C = SGU_CHUNK
    assert n_prompt % C == 0 and n_sample == C
    T = n_prompt + n_sample
    tr = C * math.gcd(n_prompt // C, 4)
    npt = n_prompt // tr
    return pl.pallas_call(
        functools.partial(_sgu_kernel, npt),
        grid=(npt + 1,),
        in_specs=[pl.BlockSpec((tr, width), lambda i: (i, 0)),
                  pl.BlockSpec((tr, width), lambda i: (i, 1)),
                  pl.BlockSpec((1, width), lambda i: (0, 0)),
                  pl.BlockSpec((SGU_GROUPS, C, C), lambda i: (0, 0, 0)),
                  pl.BlockSpec((C, SGU_GROUPS), lambda i: (0, 0))],
        out_specs=[pl.BlockSpec((tr, width), lambda i: (i, 0)),
                   pl.BlockSpec((C, width), lambda i: (0, 0))],
        out_shape=[jax.ShapeDtypeStruct((T, width), BF16),
                   jax.ShapeDtypeStruct((n_sample, width), F32)],
        compiler_params=_cparams(1),
        name="sgu_mix",
    )(P, P, sgu_norm.reshape(1, width), sgu_w, sgu_b.T)


def _gla_constants():
    C = GLA_CHUNK
    s = np.arange(C)[None, :]
    i = np.arange(C)[:, None]
    gs = [s <= i, s > i]
    masks = []
    b = C // 2
    while b >= 1:
        upper = (i % (2 * b)) >= b
        m = (i // (2 * b)) * (2 * b) + b - 1
        gs.append(np.where(upper, (s > m) & (s <= i), (s > i) & (s <= m)))
        same = (i // (2 * b)) == (s // (2 * b))
        masks.append(upper & same & ~upper.T)
        b //= 2
    masks.append(np.eye(C, dtype=bool))
    G = np.concatenate(gs, axis=0).astype(np.float32)
    M = np.stack(masks).astype(np.float32)
    return jnp.asarray(G, BF16), jnp.asarray(M, F32)


def _gla_kernel(nseq, *refs):
    ins = [refs[5 * s:5 * s + 5] for s in range(nseq)]
    G_ref, M_ref, gn_ref, o_ref, s_out_ref, st_ref = refs[5 * nseq:]
    c = pl.program_id(1)
    C = GLA_CHUNK
    hk = ins[0][0].shape[1]
    dk = hk // GLA_HEADS
    dv = ins[0][2].shape[1] // GLA_HEADS
    nlev = M_ref.shape[0] - 1

    @pl.when(c == 0)
    def _():
        st_ref[...] = jnp.zeros_like(st_ref)

    parts = []
    for s in range(nseq):
        la = ins[s][4][...]
        hi = la.astype(BF16)
        r1 = la - hi.astype(F32)
        mid = r1.astype(BF16)
        lo = (r1 - mid.astype(F32)).astype(BF16)
        parts.append(jnp.concatenate([hi, mid, lo], axis=0))
    X = jnp.exp(_dot(G_ref[...], jnp.concatenate(parts, axis=1)))

    scale = dk ** -0.5
    for s in range(nseq):
        q_ref, k_ref, v_ref, r_ref, _ = ins[s]
        for h in range(GLA_HEADS):
            Xh = X[:, s * hk + h * dk:s * hk + (h + 1) * dk]
            qh = q_ref[:, h * dk:(h + 1) * dk] * scale
            kh = k_ref[:, h * dk:(h + 1) * dk]
            vh = v_ref[:, h * dv:(h + 1) * dv]
            x_cum = Xh[0:C]
            x_rest = Xh[C:2 * C]
            st_old = st_ref[s, h]
            o = _dot_nt((qh * x_cum).astype(BF16), st_old.astype(BF16))
            att = _dot_nt(qh.astype(BF16), kh.astype(BF16)) * M_ref[nlev]
            for l in range(nlev):
                xl = Xh[(2 + l) * C:(3 + l) * C]
                att = att + _dot_nt((qh * xl).astype(BF16), (kh * xl).astype(BF16)) * M_ref[l]
            o = o + _dot(att.astype(BF16), vh.astype(BF16))
            vt = vh.T.astype(BF16)
            st_ref[s, h] = st_old * x_cum[C - 1:C, :] + _dot(vt, (kh * x_rest).astype(BF16))
            on = _rms(o, gn_ref[...]) * _silu(r_ref[:, h * dv:(h + 1) * dv])
            o_ref[s, :, h * dv:(h + 1) * dv] = on.astype(o_ref.dtype)

    @pl.when(c == pl.num_programs(1) - 1)
    def _():
        for s in range(nseq):
            for h in range(GLA_HEADS):
                s_out_ref[s, h] = st_ref[s, h].T


def _gla_prompt(P, log_a, n_seq, seq, col_q, kw, vw, gla_norm):
    C = GLA_CHUNK
    assert seq % C == 0
    nc = seq // C
    dk = kw // GLA_HEADS
    dv = vw // GLA_HEADS
    qb, kb = col_q // kw, col_q // kw + 1
    vb = (col_q + 2 * kw) // vw
    rb = vb + 1
    assert col_q % kw == 0 and (col_q + 2 * kw) % vw == 0
    nseq = 2 if n_seq % 2 == 0 else 1
    G, M = _gla_constants()
    G3 = jnp.concatenate([G, G, G], axis=1)

    def seq_specs(s):
        rows = lambda n, c: (n * nseq + s) * nc + c
        return [pl.BlockSpec((C, kw), lambda n, c: (rows(n, c), qb)),
                pl.BlockSpec((C, kw), lambda n, c: (rows(n, c), kb)),
                pl.BlockSpec((C, vw), lambda n, c: (rows(n, c), vb)),
                pl.BlockSpec((C, vw), lambda n, c: (rows(n, c), rb)),
                pl.BlockSpec((C, kw), lambda n, c: (rows(n, c), 0))]

    in_specs = [sp for s in range(nseq) for sp in seq_specs(s)]
    in_specs += [pl.BlockSpec(G3.shape, lambda n, c: (0, 0)),
                 pl.BlockSpec(M.shape, lambda n, c: (0, 0, 0)),
                 pl.BlockSpec((1, dv), lambda n, c: (0, 0))]
    o, states = pl.pallas_call(
        functools.partial(_gla_kernel, nseq),
        grid=(n_seq // nseq, nc),
        in_specs=in_specs,
        out_specs=[pl.BlockSpec((None, nseq, C, vw), lambda n, c: (n, 0, c, 0)),
                   pl.BlockSpec((nseq, GLA_HEADS, dk, dv), lambda n, c: (n, 0, 0, 0))],
        out_shape=[jax.ShapeDtypeStruct((n_seq // nseq, nseq, seq, vw), BF16),
                   jax.ShapeDtypeStruct((n_seq, GLA_HEADS, dk, dv), F32)],
        scratch_shapes=[pltpu.VMEM((nseq, GLA_HEADS, dv, dk), F32)],
        compiler_params=_cparams(2),
        name="gla_prompt",
    )(*([P, P, P, P, log_a] * nseq), G3, M, gla_norm.reshape(1, dv))
    return o.reshape(n_seq * seq, vw), states


def _gla_step_kernel(s_ref, at_ref, kt_ref, qt_ref, v_ref, r_ref, gn_ref, sn_ref, o_ref):
    i = pl.program_id(0)
    bb, H, dk, dv = s_ref.shape
    B = at_ref.shape[1]
    lane = lax.broadcasted_iota(jnp.int32, (dk, B), 1)
    scale = dk ** -0.5
    for j in range(bb):
        pick = lane == i * bb + j

        def column(t_ref, h):
            return jnp.sum(jnp.where(pick, t_ref[h * dk:(h + 1) * dk, :], 0.0), axis=1, keepdims=True)

        for h in range(H):
            s_new = jnp.exp(column(at_ref, h)) * s_ref[j, h] + column(kt_ref, h) * v_ref[j, h]
            sn_ref[j, h] = s_new
            o = jnp.sum((column(qt_ref, h) * scale) * s_new, axis=0, keepdims=True)
            o_ref[j, h] = _rms(o, gn_ref[...]) * _silu(r_ref[j, h])


def _gla_sample(state, la, k, q, v, r, gla_norm):
    B, H, dk, dv = state.shape
    bb = 8
    rowv = lambda t: t.reshape(B, H, 1, dv)
    tspec = pl.BlockSpec((H * dk, B), lambda i: (0, 0))
    rspec = pl.BlockSpec((bb, H, 1, dv), lambda i: (i, 0, 0, 0))
    sspec = pl.BlockSpec((bb, H, dk, dv), lambda i: (i, 0, 0, 0))
    return pl.pallas_call(
        _gla_step_kernel,
        grid=(B // bb,),
        in_specs=[sspec, tspec, tspec, tspec, rspec, rspec,
                  pl.BlockSpec((1, dv), lambda i: (0, 0))],
        out_specs=[sspec, rspec],
        out_shape=[jax.ShapeDtypeStruct((B, H, dk, dv), F32),
                   jax.ShapeDtypeStruct((B, H, 1, dv), F32)],
        compiler_params=_cparams(1),
        name="gla_sample_step",
    )(state, la.T, k.T, q.T, rowv(v), rowv(r), gla_norm.reshape(1, dv))


def _route(logits):
    lane = lax.broadcasted_iota(jnp.int32, logits.shape, 1)
    neg = jnp.float32(-jnp.inf)
    logits = jnp.where(lane < N_EXPERTS, logits, neg)
    v1 = jnp.max(logits, axis=-1, keepdims=True)
    i1 = jnp.min(jnp.where(logits == v1, lane, LANES), axis=-1, keepdims=True)
    rest = jnp.where(lane == i1, neg, logits)
    v2 = jnp.max(rest, axis=-1, keepdims=True)
    i2 = jnp.min(jnp.where(rest == v2, lane, LANES), axis=-1, keepdims=True)
    e2 = jnp.exp(v2 - v1)
    g1 = 1.0 / (1.0 + e2)
    g2 = e2 / (1.0 + e2)
    idx = jnp.where(lane == 0, i1, jnp.where(lane == 1, i2, 0))
    gate = jnp.where(lane == 0, g1, jnp.where(lane == 1, g2, 0.0))
    return idx, gate


def _conv_out_kernel(n_prompt_blocks, blocks_per_seq, n_tail, z_ref, halo_ref, b_ref, cw_ref,
                     h0_ref, h1_ref, res_ref, w_ref, ng_ref, rw_ref, x_ref, xslab_ref, idx_ref,
                     gate_ref):
    i = pl.program_id(0)
    w0, w1, w2 = cw_ref[0:1, :], cw_ref[1:2, :], cw_ref[2:3, :]
    ns = x_ref.shape[1] // LANES

    def finish(rows, yc, res):
        y = res + _dot(yc.astype(BF16), w_ref[...])
        x_ref[0:rows, :] = y
        hn = _rms(y, ng_ref[...])
        for s in range(ns):
            xslab_ref[pl.ds(s, rows, stride=ns), :] = hn[:, s * LANES:(s + 1) * LANES]
        h_hi = hn.astype(BF16)
        h_lo = (hn - h_hi.astype(F32)).astype(BF16)
        rw = rw_ref[...]
        r_hi = rw.astype(BF16)
        r_lo = (rw - r_hi.astype(F32)).astype(BF16)
        logits = _dot(h_hi, r_hi) + (_dot(h_hi, r_lo) + _dot(h_lo, r_hi))
        idx, gate = _route(logits)
        idx_ref[0:rows, :] = idx
        gate_ref[0:rows, :] = gate

    @pl.when(i < n_prompt_blocks)
    def _():
        z = z_ref[...]
        row = lax.broadcasted_iota(jnp.int32, z.shape, 0)
        halo = jnp.where(i % blocks_per_seq == 0, 0.0, halo_ref[...])
        z1 = jnp.where(row == 0, halo[7:8, :], pltpu.roll(z, 1, 0))
        z2 = pltpu.roll(z, 2, 0)
        z2 = jnp.where(row == 0, halo[6:7, :], jnp.where(row == 1, halo[7:8, :], z2))
        finish(z.shape[0], b_ref[...] * (w0 * z2 + w1 * z1 + w2 * z), res_ref[...])

    @pl.when(i >= n_prompt_blocks)
    def _():
        conv = w0 * h0_ref[...] + w1 * h1_ref[...] + w2 * z_ref[0:n_tail, :]
        finish(n_tail, b_ref[0:n_tail, :] * conv, res_ref[0:n_tail, :])


def _conv_out_route(z, bgate, conv_w, hist0, hist1, res, w_out, norm_g, router_w, n_prompt, seq):
    T, Cd = z.shape
    D = w_out.shape[1]
    B = T - n_prompt
    tm = _pick_tile(math.gcd(seq, n_prompt), 256)
    assert B <= tm and hist0.shape == (B, Cd)
    npb = n_prompt // tm
    wpad = jnp.zeros((D, LANES), F32).at[:, :N_EXPERTS].set(router_w)
    row = lambda i: (i, 0)
    fixed = lambda i: (0, 0)
    return pl.pallas_call(
        functools.partial(_conv_out_kernel, npb, seq // tm, B),
        grid=(npb + 1,),
        in_specs=[pl.BlockSpec((tm, Cd), row),
                  pl.BlockSpec((8, Cd), lambda i: (jnp.maximum(i * (tm // 8) - 1, 0), 0)),
                  pl.BlockSpec((tm, Cd), row),
                  pl.BlockSpec(conv_w.shape, fixed),
                  pl.BlockSpec((B, Cd), fixed),
                  pl.BlockSpec((B, Cd), fixed),
                  pl.BlockSpec((tm, D), row),
                  pl.BlockSpec((Cd, D), fixed),
                  pl.BlockSpec((1, D), fixed),
                  pl.BlockSpec((D, LANES), fixed)],
        out_specs=[pl.BlockSpec((tm, D), row),
                   pl.BlockSpec((tm * (D // LANES), LANES), row),
                   pl.BlockSpec((tm, LANES), row),
                   pl.BlockSpec((tm, LANES), row)],
        out_shape=[jax.ShapeDtypeStruct((T, D), F32),
                   jax.ShapeDtypeStruct((T * (D // LANES), LANES), F32),
                   jax.ShapeDtypeStruct((T, LANES), jnp.int32),
                   jax.ShapeDtypeStruct((T, LANES), F32)],
        compiler_params=_cparams(1),
        name="conv_out_route",
    )(z, z, bgate, conv_w, hist0, hist1, res, w_out.astype(BF16), norm_g.reshape(1, D), wpad)


def _dispatch_kernel(tok_ref, nxt_ref, used_ref, x_hbm, o_ref, buf_ref, sem):
    i = pl.program_id(0)
    nsteps = pl.num_programs(0)
    n = o_ref.shape[0]
    ns = x_hbm.shape[1]
    slot = i % 2

    def row_copy(t_ref, r, s):
        return pltpu.make_async_copy(x_hbm.at[t_ref[0, 0, r]],
                                     buf_ref.at[s, pl.ds(pl.multiple_of(r * ns, ns), ns)], sem.at[s])

    def start_rows(t_ref, s):
        def body(r, carry):
            row_copy(t_ref, r, s).start()
            return carry

        lax.fori_loop(0, n, body, 0, unroll=8)

    @pl.when(jnp.logical_and(i == 0, used_ref[0] > 0))
    def _():
        start_rows(tok_ref, 0)

    @pl.when(jnp.logical_and(i + 1 < nsteps, used_ref[jnp.minimum(i + 1, nsteps - 1)] > 0))
    def _():
        start_rows(nxt_ref, 1 - slot)

    @pl.when(used_ref[i] > 0)
    def _():
        pltpu.make_async_copy(buf_ref.at[slot], buf_ref.at[slot], sem.at[slot]).wait()

        for s in range(ns):
            o_ref[:, s * LANES:(s + 1) * LANES] = (
                buf_ref[slot, pl.ds(s, n, stride=ns), :].astype(o_ref.dtype))

    @pl.when(used_ref[i] == 0)
    def _():
        o_ref[...] = jnp.zeros_like(o_ref)


def _dispatch(x_slab, slot_tok, sub_used):
    D = x_slab.shape[1] * x_slab.shape[2]
    nsub = sub_used.shape[0]
    SB = MOE_SUB
    tok3 = slot_tok.reshape(nsub, 1, SB)
    return pl.pallas_call(
        _dispatch_kernel,
        grid_spec=pltpu.PrefetchScalarGridSpec(
            num_scalar_prefetch=0,
            grid=(nsub,),
            in_specs=[pl.BlockSpec((1, 1, SB), lambda i: (i, 0, 0), memory_space=pltpu.SMEM),
                      pl.BlockSpec((1, 1, SB), lambda i: (jnp.minimum(i + 1, nsub - 1), 0, 0),
                                   memory_space=pltpu.SMEM),
                      pl.BlockSpec(memory_space=pltpu.SMEM),
                      pl.BlockSpec(memory_space=pl.ANY)],
            out_specs=pl.BlockSpec((SB, D), lambda i: (i, 0)),
            scratch_shapes=[pltpu.VMEM((2, SB * (D // LANES), LANES), F32),
                            pltpu.SemaphoreType.DMA((2,))],
        ),
        out_shape=jax.ShapeDtypeStruct((nsub * SB, D), BF16),
        compiler_params=_cparams(1),
        name="moe_dispatch",
    )(tok3, tok3, sub_used, x_slab)


def _row_pieces(rows, total, small, compute, skip):
    for s in range(0, total, MOE_BIG):
        size = min(MOE_BIG, total - s)
        assert size % small == 0
        whole = rows > s + size - small

        @pl.when(whole)
        def _():
            compute(s, size)

        @pl.when(jnp.logical_not(whole))
        def _():
            for t in range(s, s + size, small):
                @pl.when(rows > t)
                def _():
                    compute(t, small)

                @pl.when(rows <= t)
                def _():
                    skip(t, small)


def _moe_up_kernel(ce_ref, cs_ref, cr_ref, nu_ref, x_ref, wg_ref, wu_ref, o_ref, wbf_ref):
    c = pl.program_id(0)

    @pl.when(c < nu_ref[0])
    def _():
        def compute(s, n):
            if s == 0:
                wg = wg_ref[0].astype(BF16)
                wu = wu_ref[0].astype(BF16)
                wbf_ref[0] = wg
                wbf_ref[1] = wu
            else:
                wg = wbf_ref[0]
                wu = wbf_ref[1]
            x = x_ref[pl.ds(s, n), :]
            o_ref[pl.ds(s, n), :] = (_silu(_dot(x, wg)) * _dot(x, wu)).astype(o_ref.dtype)

        def skip(s, n):
            o_ref[pl.ds(s, n), :] = jnp.zeros((n, o_ref.shape[1]), o_ref.dtype)

        _row_pieces(cr_ref[c], x_ref.shape[0], MOE_SMALL, compute, skip)

    @pl.when(c >= nu_ref[0])
    def _():
        o_ref[...] = jnp.zeros_like(o_ref)


def _moe_up(xs, w_gu, meta, R, tf, direct):
    ce, cs, cr, nu = meta
    nch = ce.shape[0]
    D = xs.shape[1]
    F = w_gu.shape[2] // 2
    nf = F // tf

    def fidx(c, f, nu):
        return jnp.where(c < nu[0], f, nf - 1)

    if direct:
        in_specs = [pl.BlockSpec((R, D), lambda c, f, *_: (c, 0), pipeline_mode=pl.Buffered(1)),
                    pl.BlockSpec((1, D, tf), lambda c, f, *_: (c, 0, f)),
                    pl.BlockSpec((1, D, tf), lambda c, f, *_: (c, 0, nf + f))]
    else:
        in_specs = [pl.BlockSpec((R, D), lambda c, f, ce, cs, cr, nu: (cs[c], 0),
                                 pipeline_mode=pl.Buffered(1)),
                    pl.BlockSpec((1, D, tf), lambda c, f, ce, cs, cr, nu: (ce[c], 0, fidx(c, f, nu))),
                    pl.BlockSpec((1, D, tf), lambda c, f, ce, cs, cr, nu: (ce[c], 0, nf + fidx(c, f, nu)))]
    return pl.pallas_call(
        _moe_up_kernel,
        grid_spec=pltpu.PrefetchScalarGridSpec(
            num_scalar_prefetch=4,
            grid=(nch, nf),
            in_specs=in_specs,
            out_specs=pl.BlockSpec((R, tf), lambda c, f, ce, cs, cr, nu: (c, f)),
            scratch_shapes=[pltpu.VMEM((2, D, tf), BF16)],
        ),
        out_shape=jax.ShapeDtypeStruct((nch * R, F), BF16),
        compiler_params=_cparams(2),
        name="moe_gate_up",
    )(ce, cs, cr, nu, xs, w_gu, w_gu)


def _moe_down_kernel(ce_ref, cs_ref, cr_ref, nu_ref, h_ref, w_ref, o_ref, wbf_ref):
    c = pl.program_id(0)
    k = pl.program_id(2)

    @pl.when(c < nu_ref[0])
    def _():
        def weight(s):
            if s == 0:
                w = w_ref[0].astype(BF16)
                wbf_ref[...] = w
                return w
            return wbf_ref[...]

        def compute(s, n):
            @pl.when(k == 0)
            def _():
                o_ref[pl.ds(s, n), :] = _dot(h_ref[pl.ds(s, n), :], weight(s))

            @pl.when(k > 0)
            def _():
                o_ref[pl.ds(s, n), :] = (o_ref[pl.ds(s, n), :]
                                         + _dot(h_ref[pl.ds(s, n), :], weight(s)))

        def skip(s, n):
            @pl.when(k == 0)
            def _():
                o_ref[pl.ds(s, n), :] = jnp.zeros((n, o_ref.shape[1]), F32)

        _row_pieces(cr_ref[c], h_ref.shape[0], MOE_SUB, compute, skip)

    @pl.when(jnp.logical_and(c >= nu_ref[0], k == 0))
    def _():
        o_ref[...] = jnp.zeros_like(o_ref)


def _moe_down(hid, w_down, meta, R, tn, tk, direct):
    ce, cs, cr, nu = meta
    nch = ce.shape[0]
    F = hid.shape[1]
    D = w_down.shape[2]
    nn, nk = D // tn, F // tk

    def last_if_unused(c, v, nu, last):
        return jnp.where(c < nu[0], v, last)

    if direct:
        in_specs = [pl.BlockSpec((R, tk), lambda c, n, k, *_: (c, k)),
                    pl.BlockSpec((1, tk, tn), lambda c, n, k, *_: (c, k, n))]
    else:
        in_specs = [pl.BlockSpec((R, tk), lambda c, n, k, ce, cs, cr, nu:
                                 (cs[c], last_if_unused(c, k, nu, nk - 1))),
                    pl.BlockSpec((1, tk, tn), lambda c, n, k, ce, cs, cr, nu:
                                 (ce[c], last_if_unused(c, k, nu, nk - 1),
                                  last_if_unused(c, n, nu, nn - 1)))]
    return pl.pallas_call(
        _moe_down_kernel,
        grid_spec=pltpu.PrefetchScalarGridSpec(
            num_scalar_prefetch=4,
            grid=(nch, nn, nk),
            in_specs=in_specs,
            out_specs=pl.BlockSpec((R, tn), lambda c, n, k, ce, cs, cr, nu: (c, n)),
            scratch_shapes=[pltpu.VMEM((tk, tn), BF16)],
        ),
        out_shape=jax.ShapeDtypeStruct((nch * R, D), F32),
        compiler_params=_cparams(3),
        name="moe_down",
    )(ce, cs, cr, nu, hid, w_down)


def _combine_kernel(n_prompt_blocks, has_overflow, p1_ref, p2_ref, n1_ref, n2_ref, x_ref, g1_ref,
                    g2_ref, fg_ref, ys_hbm, *rest):
    yo_hbm = rest[0] if has_overflow else None
    op_ref, os_ref, buf_ref, sem = rest[-4:]
    i = pl.program_id(0)
    nsteps = pl.num_programs(0)
    n = x_ref.shape[0]
    slot = i % 2
    split = ys_hbm.shape[0]

    def start_row(pos, k, r, s):
        dst = buf_ref.at[s, k, pl.ds(r, 1)]
        if yo_hbm is None:
            pltpu.make_async_copy(ys_hbm.at[pl.ds(pos, 1)], dst, sem.at[s]).start()
        else:
            @pl.when(pos < split)
            def _():
                pltpu.make_async_copy(ys_hbm.at[pl.ds(pos, 1)], dst, sem.at[s]).start()

            @pl.when(pos >= split)
            def _():
                pltpu.make_async_copy(yo_hbm.at[pl.ds(pos - split, 1)], dst, sem.at[s]).start()

    def start_rows(a_ref, b_ref, s):
        def body(r, carry):
            start_row(a_ref[0, 0, r], 0, r, s)
            start_row(b_ref[0, 0, r], 1, r, s)
            return carry

        lax.fori_loop(0, n, body, 0, unroll=8)

    @pl.when(i == 0)
    def _():
        start_rows(p1_ref, p2_ref, 0)

    @pl.when(i + 1 < nsteps)
    def _():
        start_rows(n1_ref, n2_ref, 1 - slot)

    pltpu.make_async_copy(buf_ref.at[slot], buf_ref.at[slot], sem.at[slot]).wait()
    y = x_ref[...] + (g1_ref[...] * buf_ref[slot, 0] + g2_ref[...] * buf_ref[slot, 1])
    yn = _rms(y, fg_ref[...])

    @pl.when(i < n_prompt_blocks)
    def _():
        op_ref[...] = yn

    @pl.when(i >= n_prompt_blocks)
    def _():
        os_ref[...] = yn


def _combine(x, ys, ys_over, pos1, pos2, g1, g2, final_g, n_prompt):
    T, D = x.shape
    srcs = [ys] if ys_over is None else [ys, ys_over]
    tr = T - n_prompt
    assert n_prompt % tr == 0
    nb = T // tr
    npb = n_prompt // tr
    cur = pl.BlockSpec((1, 1, tr), lambda i: (i, 0, 0), memory_space=pltpu.SMEM)
    nxt = pl.BlockSpec((1, 1, tr), lambda i: (jnp.minimum(i + 1, nb - 1), 0, 0),
                       memory_space=pltpu.SMEM)
    p1 = pos1.reshape(nb, 1, tr)
    p2 = pos2.reshape(nb, 1, tr)
    return pl.pallas_call(
        functools.partial(_combine_kernel, npb, ys_over is not None),
        grid_spec=pltpu.PrefetchScalarGridSpec(
            num_scalar_prefetch=0,
            grid=(nb,),
            in_specs=[cur, cur, nxt, nxt,
                      pl.BlockSpec((tr, D), lambda i: (i, 0)),
                      pl.BlockSpec((tr, 1), lambda i: (i, 0)),
                      pl.BlockSpec((tr, 1), lambda i: (i, 0)),
                      pl.BlockSpec((1, D), lambda i: (0, 0))]
                     + [pl.BlockSpec(memory_space=pl.ANY)] * len(srcs),
            out_specs=[pl.BlockSpec((tr, D), lambda i: (jnp.minimum(i, npb - 1), 0)),
                       pl.BlockSpec((tr, D), lambda i: (0, 0))],
            scratch_shapes=[pltpu.VMEM((2, 2, tr, D), F32), pltpu.SemaphoreType.DMA((2,))],
        ),
        out_shape=[jax.ShapeDtypeStruct((n_prompt, D), F32),
                   jax.ShapeDtypeStruct((tr, D), F32)],
        compiler_params=_cparams(1),
        name="moe_combine_final_norm",
    )(p1, p2, p1, p2, x, g1, g2, final_g.reshape(1, D), *srcs)


def _moe_plan(idx, T):
    M = T * TOP_K
    E = N_EXPERTS
    R = -(-int(M / E * MOE_CAPACITY) // MOE_SUB) * MOE_SUB
    n_over = -(-M // R)
    i32 = jnp.int32
    flat_e = idx.reshape(M)
    onehot = (flat_e[:, None] == jnp.arange(E, dtype=i32)[None, :]).astype(i32)
    csum = jnp.cumsum(onehot, axis=0)
    rank = jnp.sum((csum - onehot) * onehot, axis=1)
    counts = csum[-1]
    o_chunks = jnp.maximum((counts + R - 1) // R - 1, 0)
    o_first = jnp.cumsum(o_chunks) - o_chunks
    n_used = jnp.sum(o_chunks).astype(i32)
    over = rank - R
    slot = jnp.where(rank < R, flat_e * R + rank,
                     (E + o_first[flat_e] + over // R) * R + over % R)
    flat_tok = jnp.arange(M, dtype=i32) // TOP_K
    slot_tok = jnp.zeros(((E + n_over) * R,), i32).at[slot].set(flat_tok)
    spc = R // MOE_SUB

    def sub_used(c_rows):
        sub = jnp.arange(c_rows.shape[0] * spc, dtype=i32)
        return ((sub % spc) * MOE_SUB < c_rows[sub // spc]).astype(i32)

    m_rows = jnp.minimum(counts, R).astype(i32)
    eidx = jnp.arange(E, dtype=i32)
    main = dict(tok=slot_tok[:E * R], used=sub_used(m_rows),
                meta=(eidx, eidx, m_rows, jnp.full((1,), E, i32)))
    cidx = jnp.arange(n_over, dtype=i32)
    c_e = jnp.clip(jnp.searchsorted(jnp.cumsum(o_chunks), cidx, side='right'), 0, E - 1).astype(i32)
    c_rows = jnp.clip(counts[c_e] - R - (cidx - o_first[c_e]) * R, 0, R).astype(i32)
    used = cidx < n_used
    last = jnp.maximum(n_used - 1, 0)
    c_src = jnp.where(used, cidx, last).astype(i32)
    c_e = jnp.where(used, c_e, c_e[last]).astype(i32)
    c_rows = jnp.where(used, c_rows, 0).astype(i32)
    overflow = dict(tok=slot_tok[E * R:], used=sub_used(c_rows),
                    meta=(c_e, c_src, c_rows, n_used.reshape(1)))
    pos = slot.reshape(T, TOP_K).astype(i32)
    return R, main, overflow, n_used, pos[:, 0], pos[:, 1]


def kernel(x_prompt, x_sample, state_gla, state_conv, ev_norm_mix, ev_w_in, ev_sgu_norm, ev_sgu_w, ev_sgu_b, ev_gla_w_alpha, ev_gla_b_alpha, ev_gla_norm, ev_w_out, ev_norm_ffn, ev_ffn_w_gu, ev_ffn_w_down, od_norm_mix, od_conv_w_in, od_conv_w, od_conv_w_out, od_norm_ffn, od_router_w, od_moe_w_gu, od_moe_w_down, final_norm):
    nb, seq, D = x_prompt.shape
    B = x_sample.shape[0]
    assert x_sample.shape[1] == 1
    Tp = nb * seq
    T = Tp + B
    xp0 = x_prompt.reshape(Tp, D)
    xs0 = x_sample.reshape(B, D)

    sw = ev_sgu_norm.shape[1]
    kw = ev_gla_w_alpha.shape[2]
    rank = ev_gla_w_alpha.shape[1]
    vw = GLA_HEADS * ev_gla_norm.shape[1]
    n_main = 2 * sw + 2 * kw + 2 * vw
    assert ev_w_in.shape[2] == n_main + rank
    tm = _pick_tile(T, 1664)

    h = _norm_bf16_stacked(xp0, xs0, ev_norm_mix[0])
    P = _matmul(h, ev_w_in, n_main, tm, 512, name="mix_in_proj")
    log_a = _log_decay(h, ev_w_in, n_main, ev_gla_w_alpha[0], ev_gla_b_alpha[0], tm)
    a_out, v_sample = _sgu(P, Tp, B, sw, ev_sgu_norm[0], ev_sgu_w[0], ev_sgu_b[0])
    col_q = 2 * sw
    o_p, gla_p = _gla_prompt(P, log_a, nb, seq, col_q, kw, vw, ev_gla_norm[0])
    Ps = P[Tp:]
    q_s = Ps[:, col_q:col_q + kw]
    k_s = Ps[:, col_q + kw:col_q + 2 * kw]
    v_s = Ps[:, col_q + 2 * kw:col_q + 2 * kw + vw]
    r_s = Ps[:, col_q + 2 * kw + vw:]
    gla_s, o_s = _gla_sample(state_gla[0], log_a[Tp:], k_s, q_s, v_s, r_s, ev_gla_norm[0])
    x1, h = _mix_out(a_out, o_p, o_s.reshape(B, vw).astype(BF16), xp0, xs0, ev_w_out[0],
                     ev_norm_ffn[0])

    hid = _swiglu_up(h, ev_ffn_w_gu[0], tm, 512)
    x2 = _matmul(hid, ev_ffn_w_down[0], D, _pick_tile(T, 832), 256, res=x1, name="ffn_down")

    h = _norm_bf16(x2, od_norm_mix[0])
    bgate, z = _conv_in(h, od_conv_w_in[0], tm, 256)
    x3, h3_slab, idx, gates = _conv_out_route(
        z, bgate, od_conv_w[0], state_conv[0][:, 0], state_conv[0][:, 1],
        x2, od_conv_w_out[0], od_norm_ffn[0], od_router_w[0], Tp, seq)
    h3_slab = h3_slab.reshape(T, D // LANES, LANES)
    conv_p = jnp.stack([z[(n + 1) * seq - 2:(n + 1) * seq] for n in range(nb)])
    conv_s = jnp.concatenate([state_conv[0][:, 1:], z[Tp:].reshape(B, 1, D)], axis=1)

    R, main, overflow, n_overflow, pos1, pos2 = _moe_plan(idx[:, :TOP_K], T)
    FE = od_moe_w_down.shape[2]
    tf, tk = _pick_tile(FE, 512, LANES), _pick_tile(FE, 1024, LANES)

    def experts(part, direct):
        xs = _dispatch(h3_slab, part["tok"], part["used"])
        hid = _moe_up(xs, od_moe_w_gu[0], part["meta"], R, tf, direct)
        return _moe_down(hid, od_moe_w_down[0], part["meta"], R, 1024, tk, direct)

    ys = experts(main, True)
    g1, g2 = gates[:, 0:1], gates[:, 1:2]
    y_p, y_s = lax.cond(
        n_overflow > 0,
        lambda: _combine(x3, ys, experts(overflow, False), pos1, pos2, g1, g2, final_norm, Tp),
        lambda: _combine(x3, ys, None, pos1, pos2, g1, g2, final_norm, Tp))

    y_prompt = y_p.reshape(nb, seq, D)
    y_sample = y_s.reshape(B, 1, D)
    return (y_prompt, y_sample, gla_p[None], gla_s[None],
            v_sample.reshape(1, B, 1, sw), conv_p[None], conv_s[None])
```

```python
import functools
import math

import numpy as np
import jax
import jax.numpy as jnp
from jax import lax
from jax.experimental import pallas as pl
from jax.experimental.pallas import tpu as pltpu

F32 = jnp.float32
BF16 = jnp.bfloat16
EPS = 1e-6

SGU_CHUNK = 128
SGU_GROUPS = 4
GLA_HEADS = 4
GLA_TAU = 16.0
GLA_CHUNK = 128
N_EXPERTS = 8
TOP_K = 2
MOE_SUB = 256
MOE_CAPACITY = 1.35
MOE_BIG = 1024
MOE_SMALL = 128
LANES = 128
VMEM_LIMIT = 52 * 1024 * 1024
VMEM_LIMIT_MAX = 60 * 1024 * 1024


def _cparams(n_axes, vmem=VMEM_LIMIT):
    return pltpu.CompilerParams(dimension_semantics=("arbitrary",) * n_axes,
                                vmem_limit_bytes=vmem)


def _pick_tile(n, cap, mult=16):
    best = None
    for t in range(mult, min(n, cap) + 1, mult):
        if n % t == 0:
            best = t
    assert best is not None, (n, cap, mult)
    return best


def _rms(x, g):
    ms = jnp.mean(x * x, axis=-1, keepdims=True)
    return (x * lax.rsqrt(ms + EPS)) * g


def _gelu(x):
    c = math.sqrt(2.0 / math.pi)
    return x * (0.5 * (1.0 + jnp.tanh(c * (x + 0.044715 * (x * x * x)))))


def _silu(x):
    return x * (1.0 / (1.0 + jnp.exp(-x)))


def _dot(a, b):
    return jnp.dot(a, b, preferred_element_type=F32)


def _dot_nt(a, b):
    return lax.dot_general(a, b, (((1,), (1,)), ((), ())), preferred_element_type=F32)


def _norm_kernel(x_ref, g_ref, o_ref):
    o_ref[...] = _rms(x_ref[...], g_ref[...]).astype(BF16)


def _norm_bf16(x, g):
    T, D = x.shape
    tr = _pick_tile(T, 512)
    return pl.pallas_call(
        _norm_kernel,
        grid=(T // tr,),
        in_specs=[pl.BlockSpec((tr, D), lambda i: (i, 0)),
                  pl.BlockSpec((1, D), lambda i: (0, 0))],
        out_specs=pl.BlockSpec((tr, D), lambda i: (i, 0)),
        out_shape=jax.ShapeDtypeStruct((T, D), BF16),
        compiler_params=_cparams(1),
        name="rmsnorm_bf16",
    )(x, g.reshape(1, D))


def _norm2_kernel(n_prompt_blocks, xp_ref, xs_ref, g_ref, o_ref):
    i = pl.program_id(0)

    @pl.when(i < n_prompt_blocks)
    def _():
        o_ref[...] = _rms(xp_ref[...], g_ref[...]).astype(BF16)

    @pl.when(i >= n_prompt_blocks)
    def _():
        o_ref[0:xs_ref.shape[0], :] = _rms(xs_ref[...], g_ref[...]).astype(BF16)


def _norm_bf16_stacked(xp, xs, g):
    Tp, D = xp.shape
    B = xs.shape[0]
    tr = _pick_tile(Tp, 512)
    assert B <= tr
    npb = Tp // tr
    return pl.pallas_call(
        functools.partial(_norm2_kernel, npb),
        grid=(npb + 1,),
        in_specs=[pl.BlockSpec((tr, D), lambda i: (jnp.minimum(i, npb - 1), 0)),
                  pl.BlockSpec((B, D), lambda i: (0, 0)),
                  pl.BlockSpec((1, D), lambda i: (0, 0))],
        out_specs=pl.BlockSpec((tr, D), lambda i: (i, 0)),
        out_shape=jax.ShapeDtypeStruct((Tp + B, D), BF16),
        compiler_params=_cparams(1),
        name="rmsnorm_stack_bf16",
    )(xp, xs, g.reshape(1, D))


def _mm_kernel(x_ref, w_ref, o_ref):
    o_ref[...] = _dot(x_ref[...], w_ref[...].astype(BF16)).astype(o_ref.dtype)


def _mm_res_kernel(x_ref, w_ref, r_ref, o_ref):
    o_ref[...] = r_ref[...] + _dot(x_ref[...], w_ref[...].astype(BF16))


def _matmul(x, w, n_cols, tm, tn, res=None, name="matmul"):
    T, K = x.shape
    grid = (T // tm, n_cols // tn)
    if w.ndim == 3:
        w_spec = pl.BlockSpec((None, K, tn), lambda i, j: (0, 0, j))
    else:
        w_spec = pl.BlockSpec((K, tn), lambda i, j: (0, j))
    in_specs = [pl.BlockSpec((tm, K), lambda i, j: (i, 0)), w_spec]
    args = [x, w]
    kern = _mm_kernel
    if res is not None:
        in_specs.append(pl.BlockSpec((tm, tn), lambda i, j: (i, j)))
        args.append(res)
        kern = _mm_res_kernel
    return pl.pallas_call(
        kern, grid=grid, in_specs=in_specs,
        out_specs=pl.BlockSpec((tm, tn), lambda i, j: (i, j)),
        out_shape=jax.ShapeDtypeStruct((T, n_cols), F32),
        compiler_params=_cparams(2),
        name=name,
    )(*args)


def _mix_out_kernel(n_full, n_tail, a_ref, op_ref, os_ref, rp_ref, rs_ref, w_ref, ng_ref,
                    x_ref, h_ref):
    i = pl.program_id(0)
    sw = a_ref.shape[1]

    def finish(rows, a, o, res):
        y = res + _dot(a, w_ref[0:sw, :]) + _dot(o, w_ref[sw:, :])
        x_ref[0:rows, :] = y
        h_ref[0:rows, :] = _rms(y, ng_ref[...]).astype(BF16)

    @pl.when(i < n_full)
    def _():
        finish(a_ref.shape[0], a_ref[...], op_ref[...], rp_ref[...])

    @pl.when(i >= n_full)
    def _():
        finish(n_tail, a_ref[0:n_tail, :], os_ref[...], rs_ref[...])


def _mix_out(a, o_p, o_s, res_p, res_s, w, norm_g):
    T, sw = a.shape
    Tp, vw = o_p.shape
    B = o_s.shape[0]
    D = w.shape[1]
    assert T == Tp + B
    tm = _pick_tile(Tp, 512)
    assert B <= tm
    n_full = Tp // tm
    clamp = lambda i: (jnp.minimum(i, n_full - 1), 0)
    row = lambda i: (i, 0)
    fixed = lambda i: (0, 0)
    return pl.pallas_call(
        functools.partial(_mix_out_kernel, n_full, B),
        grid=(n_full + 1,),
        in_specs=[pl.BlockSpec((tm, sw), row),
                  pl.BlockSpec((tm, vw), clamp),
                  pl.BlockSpec((B, vw), fixed),
                  pl.BlockSpec((tm, D), clamp),
                  pl.BlockSpec((B, D), fixed),
                  pl.BlockSpec((sw + vw, D), fixed),
                  pl.BlockSpec((1, D), fixed)],
        out_specs=[pl.BlockSpec((tm, D), row),
                   pl.BlockSpec((tm, D), row)],
        out_shape=[jax.ShapeDtypeStruct((T, D), F32),
                   jax.ShapeDtypeStruct((T, D), BF16)],
        compiler_params=_cparams(1),
        name="mix_out_proj",
    )(a, o_p, o_s, res_p, res_s, w.astype(BF16), norm_g.reshape(1, D))


def _swiglu_kernel(x_ref, wg_ref, wu_ref, o_ref):
    x = x_ref[...]
    g = _dot(x, wg_ref[...].astype(BF16))
    u = _dot(x, wu_ref[...].astype(BF16))
    o_ref[...] = (_silu(g) * u).astype(o_ref.dtype)


def _swiglu_up(x, w_gu, tm, tn):
    T, K = x.shape
    F = w_gu.shape[1] // 2
    nj = F // tn
    return pl.pallas_call(
        _swiglu_kernel,
        grid=(T // tm, nj),
        in_specs=[pl.BlockSpec((tm, K), lambda i, j: (i, 0)),
                  pl.BlockSpec((K, tn), lambda i, j: (0, j)),
                  pl.BlockSpec((K, tn), lambda i, j: (0, j + nj))],
        out_specs=pl.BlockSpec((tm, tn), lambda i, j: (i, j)),
        out_shape=jax.ShapeDtypeStruct((T, F), BF16),
        compiler_params=_cparams(2),
        name="ffn_gate_up",
    )(x, w_gu, w_gu)


def _conv_in_kernel(x_ref, wb_ref, wc_ref, wu_ref, b_ref, z_ref):
    x = x_ref[...]
    b_ref[...] = _dot(x, wb_ref[...].astype(BF16))
    c = _dot(x, wc_ref[...].astype(BF16))
    u = _dot(x, wu_ref[...].astype(BF16))
    z_ref[...] = c * u


def _conv_in(x, w, tm, tn):
    T, K = x.shape
    C = w.shape[1] // 3
    nj = C // tn
    return pl.pallas_call(
        _conv_in_kernel,
        grid=(T // tm, nj),
        in_specs=[pl.BlockSpec((tm, K), lambda i, j: (i, 0)),
                  pl.BlockSpec((K, tn), lambda i, j: (0, j)),
                  pl.BlockSpec((K, tn), lambda i, j: (0, j + nj)),
                  pl.BlockSpec((K, tn), lambda i, j: (0, j + 2 * nj))],
        out_specs=[pl.BlockSpec((tm, tn), lambda i, j: (i, j)),
                   pl.BlockSpec((tm, tn), lambda i, j: (i, j))],
        out_shape=[jax.ShapeDtypeStruct((T, C), F32),
                   jax.ShapeDtypeStruct((T, C), F32)],
        compiler_params=_cparams(2),
        name="conv_in_proj",
    )(x, w, w, w)


def _loga_kernel(x_ref, wl_ref, wa_ref, b_ref, o_ref):
    rank = wa_ref.shape[0]
    g = _dot(x_ref[...], wl_ref[:, 0:rank].astype(BF16))
    z = _dot(g.astype(BF16), wa_ref[...].astype(BF16)) + b_ref[...]
    ls = jnp.minimum(z, 0.0) - jnp.log1p(jnp.exp(-jnp.abs(z)))
    o_ref[...] = ls * (1.0 / GLA_TAU)


def _log_decay(x, w_in, col, w_alpha, b_alpha, tm):
    T, K = x.shape
    R, KW = w_alpha.shape
    assert col % LANES == 0 and R <= LANES
    return pl.pallas_call(
        _loga_kernel,
        grid=(T // tm,),
        in_specs=[pl.BlockSpec((tm, K), lambda i: (i, 0)),
                  pl.BlockSpec((None, K, LANES), lambda i: (0, 0, col // LANES)),
                  pl.BlockSpec((R, KW), lambda i: (0, 0)),
                  pl.BlockSpec((1, KW), lambda i: (0, 0))],
        out_specs=pl.BlockSpec((tm, KW), lambda i: (i, 0)),
        out_shape=jax.ShapeDtypeStruct((T, KW), F32),
        compiler_params=_cparams(1),
        name="gla_log_decay",
    )(x, w_in, w_alpha, b_alpha.reshape(1, KW))


def _sgu_kernel(n_prompt_tiles, u_ref, v_ref, ng_ref, w_ref, bt_ref, o_ref, vs_ref):
    i = pl.program_id(0)
    C = SGU_CHUNK
    gd = u_ref.shape[1] // SGU_GROUPS
    row = lax.broadcasted_iota(jnp.int32, (C, C), 0)
    col = lax.broadcasted_iota(jnp.int32, (C, C), 1)

    def chunk(r0, weights, biases, keep_v):
        gu = _gelu(u_ref[r0:r0 + C, :])
        vn = _rms(_gelu(v_ref[r0:r0 + C, :]), ng_ref[...])
        if keep_v:
            vs_ref[...] = vn
        vb = vn.astype(BF16)
        for g in range(SGU_GROUPS):
            cols = slice(g * gd, (g + 1) * gd)
            mixed = _dot(weights[g], vb[:, cols]) + biases[g]
            o_ref[r0:r0 + C, cols] = (gu[:, cols] * mixed).astype(o_ref.dtype)

    @pl.when(i < n_prompt_tiles)
    def _():
        weights = [jnp.where(col <= row, w_ref[g], 0.0).astype(BF16) for g in range(SGU_GROUPS)]
        biases = [bt_ref[:, g:g + 1] for g in range(SGU_GROUPS)]
        for r0 in range(0, u_ref.shape[0], C):
            chunk(r0, weights, biases, False)

    @pl.when(i >= n_prompt_tiles)
    def _():
        weights = [jnp.where(col == row, w_ref[g, 0:1, 0:1], 0.0).astype(BF16)
                   for g in range(SGU_GROUPS)]
        biases = [bt_ref[0:1, g:g + 1] for g in range(SGU_GROUPS)]
        chunk(0, weights, biases, True)


def _sgu(P, n_prompt, n_sample, width, sgu_norm, sgu_w, sgu_b):
    C = SGU_CHUNK
    assert n_prompt % C == 0 and n_sample == C
    T = n_prompt + n_sample
    tr = C * math.gcd(n_prompt // C, 4)
    npt = n_prompt // tr
    return pl.pallas_call(
        functools.partial(_sgu_kernel, npt),
        grid=(npt + 1,),
        in_specs=[pl.BlockSpec((tr, width), lambda i: (i, 0)),
                  pl.BlockSpec((tr, width), lambda i: (i, 1)),
                  pl.BlockSpec((1, width), lambda i: (0, 0)),
                  pl.BlockSpec((SGU_GROUPS, C, C), lambda i: (0, 0, 0)),
                  pl.BlockSpec((C, SGU_GROUPS), lambda i: (0, 0))],
        out_specs=[pl.BlockSpec((tr, width), lambda i: (i, 0)),
                   pl.BlockSpec((C, width), lambda i: (0, 0))],
        out_shape=[jax.ShapeDtypeStruct((T, width), BF16),
                   jax.ShapeDtypeStruct((n_sample, width), F32)],
        compiler_params=_cparams(1),
        name="sgu_mix",
    )(P, P, sgu_norm.reshape(1, width), sgu_w, sgu_b.T)


def _gla_constants():
    C = GLA_CHUNK
    s = np.arange(C)[None, :]
    i = np.arange(C)[:, None]
    gs = [s <= i, s > i]
    masks = []
    b = C // 2
    while b >= 1:
        upper = (i % (2 * b)) >= b
        m = (i // (2 * b)) * (2 * b) + b - 1
        gs.append(np.where(upper, (s > m) & (s <= i), (s > i) & (s <= m)))
        same = (i // (2 * b)) == (s // (2 * b))
        masks.append(upper & same & ~upper.T)
        b //= 2
    masks.append(np.eye(C, dtype=bool))
    G = np.concatenate(gs, axis=0).astype(np.float32)
    M = np.stack(masks).astype(np.float32)
    return jnp.asarray(G, BF16), jnp.asarray(M, F32)


def _gla_kernel(nseq, *refs):
    ins = [refs[5 * s:5 * s + 5] for s in range(nseq)]
    G_ref, M_ref, gn_ref, o_ref, s_out_ref, st_ref = refs[5 * nseq:]
    c = pl.program_id(1)
    C = GLA_CHUNK
    hk = ins[0][0].shape[1]
    dk = hk // GLA_HEADS
    dv = ins[0][2].shape[1] // GLA_HEADS
    nlev = M_ref.shape[0] - 1

    @pl.when(c == 0)
    def _():
        st_ref[...] = jnp.zeros_like(st_ref)

    parts = []
    for s in range(nseq):
        la = ins[s][4][...]
        hi = la.astype(BF16)
        r1 = la - hi.astype(F32)
        mid = r1.astype(BF16)
        lo = (r1 - mid.astype(F32)).astype(BF16)
        parts.append(jnp.concatenate([hi, mid, lo], axis=0))
    X = jnp.exp(_dot(G_ref[...], jnp.concatenate(parts, axis=1)))

    scale = dk ** -0.5
    for s in range(nseq):
        q_ref, k_ref, v_ref, r_ref, _ = ins[s]
        for h in range(GLA_HEADS):
            Xh = X[:, s * hk + h * dk:s * hk + (h + 1) * dk]
            qh = q_ref[:, h * dk:(h + 1) * dk] * scale
            kh = k_ref[:, h * dk:(h + 1) * dk]
            vh = v_ref[:, h * dv:(h + 1) * dv]
            x_cum = Xh[0:C]
            x_rest = Xh[C:2 * C]
            st_old = st_ref[s, h]
            o = _dot_nt((qh * x_cum).astype(BF16), st_old.astype(BF16))
            att = _dot_nt(qh.astype(BF16), kh.astype(BF16)) * M_ref[nlev]
            for l in range(nlev):
                xl = Xh[(2 + l) * C:(3 + l) * C]
                att = att + _dot_nt((qh * xl).astype(BF16), (kh * xl).astype(BF16)) * M_ref[l]
            o = o + _dot(att.astype(BF16), vh.astype(BF16))
            vt = vh.T.astype(BF16)
            st_ref[s, h] = st_old * x_cum[C - 1:C, :] + _dot(vt, (kh * x_rest).astype(BF16))
            on = _rms(o, gn_ref[...]) * _silu(r_ref[:, h * dv:(h + 1) * dv])
            o_ref[s, :, h * dv:(h + 1) * dv] = on.astype(o_ref.dtype)

    @pl.when(c == pl.num_programs(1) - 1)
    def _():
        for s in range(nseq):
            for h in range(GLA_HEADS):
                s_out_ref[s, h] = st_ref[s, h].T


def _gla_prompt(P, log_a, n_seq, seq, col_q, kw, vw, gla_norm):
    C = GLA_CHUNK
    assert seq % C == 0
    nc = seq // C
    dk = kw // GLA_HEADS
    dv = vw // GLA_HEADS
    qb, kb = col_q // kw, col_q // kw + 1
    vb = (col_q + 2 * kw) // vw
    rb = vb + 1
    assert col_q % kw == 0 and (col_q + 2 * kw) % vw == 0
    nseq = math.gcd(n_seq, 4)
    G, M = _gla_constants()
    G3 = jnp.concatenate([G, G, G], axis=1)

    def seq_specs(s):
        rows = lambda n, c: (n * nseq + s) * nc + c
        return [pl.BlockSpec((C, kw), lambda n, c: (rows(n, c), qb)),
                pl.BlockSpec((C, kw), lambda n, c: (rows(n, c), kb)),
                pl.BlockSpec((C, vw), lambda n, c: (rows(n, c), vb)),
                pl.BlockSpec((C, vw), lambda n, c: (rows(n, c), rb)),
                pl.BlockSpec((C, kw), lambda n, c: (rows(n, c), 0))]

    in_specs = [sp for s in range(nseq) for sp in seq_specs(s)]
    in_specs += [pl.BlockSpec(G3.shape, lambda n, c: (0, 0)),
                 pl.BlockSpec(M.shape, lambda n, c: (0, 0, 0)),
                 pl.BlockSpec((1, dv), lambda n, c: (0, 0))]
    o, states = pl.pallas_call(
        functools.partial(_gla_kernel, nseq),
        grid=(n_seq // nseq, nc),
        in_specs=in_specs,
        out_specs=[pl.BlockSpec((None, nseq, C, vw), lambda n, c: (n, 0, c, 0)),
                   pl.BlockSpec((nseq, GLA_HEADS, dk, dv), lambda n, c: (n, 0, 0, 0))],
        out_shape=[jax.ShapeDtypeStruct((n_seq // nseq, nseq, seq, vw), BF16),
                   jax.ShapeDtypeStruct((n_seq, GLA_HEADS, dk, dv), F32)],
        scratch_shapes=[pltpu.VMEM((nseq, GLA_HEADS, dv, dk), F32)],
        compiler_params=_cparams(2),
        name="gla_prompt",
    )(*([P, P, P, P, log_a] * nseq), G3, M, gla_norm.reshape(1, dv))
    return o.reshape(n_seq * seq, vw), states


def _gla_step_kernel(s_ref, at_ref, kt_ref, qt_ref, v_ref, r_ref, gn_ref, sn_ref, o_ref):
    i = pl.program_id(0)
    bb, H, dk, dv = s_ref.shape
    B = at_ref.shape[1]
    lane = lax.broadcasted_iota(jnp.int32, (dk, B), 1)
    scale = dk ** -0.5
    for j in range(bb):
        pick = lane == i * bb + j

        def column(t_ref, h):
            return jnp.sum(jnp.where(pick, t_ref[h * dk:(h + 1) * dk, :], 0.0), axis=1, keepdims=True)

        for h in range(H):
            s_new = jnp.exp(column(at_ref, h)) * s_ref[j, h] + column(kt_ref, h) * v_ref[j, h]
            sn_ref[j, h] = s_new
            o = jnp.sum((column(qt_ref, h) * scale) * s_new, axis=0, keepdims=True)
            o_ref[j, h] = _rms(o, gn_ref[...]) * _silu(r_ref[j, h])


def _gla_sample(state, la, k, q, v, r, gla_norm):
    B, H, dk, dv = state.shape
    bb = 8
    rowv = lambda t: t.reshape(B, H, 1, dv)
    tspec = pl.BlockSpec((H * dk, B), lambda i: (0, 0))
    rspec = pl.BlockSpec((bb, H, 1, dv), lambda i: (i, 0, 0, 0))
    sspec = pl.BlockSpec((bb, H, dk, dv), lambda i: (i, 0, 0, 0))
    return pl.pallas_call(
        _gla_step_kernel,
        grid=(B // bb,),
        in_specs=[sspec, tspec, tspec, tspec, rspec, rspec,
                  pl.BlockSpec((1, dv), lambda i: (0, 0))],
        out_specs=[sspec, rspec],
        out_shape=[jax.ShapeDtypeStruct((B, H, dk, dv), F32),
                   jax.ShapeDtypeStruct((B, H, 1, dv), F32)],
        compiler_params=_cparams(1),
        name="gla_sample_step",
    )(state, la.T, k.T, q.T, rowv(v), rowv(r), gla_norm.reshape(1, dv))


def _route(logits):
    lane = lax.broadcasted_iota(jnp.int32, logits.shape, 1)
    neg = jnp.float32(-jnp.inf)
    logits = jnp.where(lane < N_EXPERTS, logits, neg)
    v1 = jnp.max(logits, axis=-1, keepdims=True)
    i1 = jnp.min(jnp.where(logits == v1, lane, LANES), axis=-1, keepdims=True)
    rest = jnp.where(lane == i1, neg, logits)
    v2 = jnp.max(rest, axis=-1, keepdims=True)
    i2 = jnp.min(jnp.where(rest == v2, lane, LANES), axis=-1, keepdims=True)
    e2 = jnp.exp(v2 - v1)
    g1 = 1.0 / (1.0 + e2)
    g2 = e2 / (1.0 + e2)
    idx = jnp.where(lane == 0, i1, jnp.where(lane == 1, i2, 0))
    gate = jnp.where(lane == 0, g1, jnp.where(lane == 1, g2, 0.0))
    return idx, gate


def _conv_out_kernel(n_prompt_blocks, blocks_per_seq, n_tail, z_ref, halo_ref, b_ref, cw_ref,
                     h0_ref, h1_ref, res_ref, w_ref, ng_ref, rw_ref, x_ref, xslab_ref, idx_ref,
                     gate_ref):
    i = pl.program_id(0)
    w0, w1, w2 = cw_ref[0:1, :], cw_ref[1:2, :], cw_ref[2:3, :]
    ns = x_ref.shape[1] // LANES

    def finish(rows, yc, res):
        y = res + _dot(yc.astype(BF16), w_ref[...])
        x_ref[0:rows, :] = y
        hn = _rms(y, ng_ref[...])
        for s in range(ns):
            xslab_ref[pl.ds(s, rows, stride=ns), :] = hn[:, s * LANES:(s + 1) * LANES]
        h_hi = hn.astype(BF16)
        h_lo = (hn - h_hi.astype(F32)).astype(BF16)
        rw = rw_ref[...]
        r_hi = rw.astype(BF16)
        r_lo = (rw - r_hi.astype(F32)).astype(BF16)
        logits = _dot(h_hi, r_hi) + (_dot(h_hi, r_lo) + _dot(h_lo, r_hi))
        idx, gate = _route(logits)
        idx_ref[0:rows, :] = idx
        gate_ref[0:rows, :] = gate

    @pl.when(i < n_prompt_blocks)
    def _():
        z = z_ref[...]
        row = lax.broadcasted_iota(jnp.int32, z.shape, 0)
        halo = jnp.where(i % blocks_per_seq == 0, 0.0, halo_ref[...])
        z1 = jnp.where(row == 0, halo[7:8, :], pltpu.roll(z, 1, 0))
        z2 = pltpu.roll(z, 2, 0)
        z2 = jnp.where(row == 0, halo[6:7, :], jnp.where(row == 1, halo[7:8, :], z2))
        finish(z.shape[0], b_ref[...] * (w0 * z2 + w1 * z1 + w2 * z), res_ref[...])

    @pl.when(i >= n_prompt_blocks)
    def _():
        conv = w0 * h0_ref[...] + w1 * h1_ref[...] + w2 * z_ref[0:n_tail, :]
        finish(n_tail, b_ref[0:n_tail, :] * conv, res_ref[0:n_tail, :])


def _conv_out_route(z, bgate, conv_w, hist0, hist1, res, w_out, norm_g, router_w, n_prompt, seq):
    T, Cd = z.shape
    D = w_out.shape[1]
    B = T - n_prompt
    tm = _pick_tile(math.gcd(seq, n_prompt), 256)
    assert B <= tm and hist0.shape == (B, Cd)
    npb = n_prompt // tm
    wpad = jnp.zeros((D, LANES), F32).at[:, :N_EXPERTS].set(router_w)
    row = lambda i: (i, 0)
    fixed = lambda i: (0, 0)
    return pl.pallas_call(
        functools.partial(_conv_out_kernel, npb, seq // tm, B),
        grid=(npb + 1,),
        in_specs=[pl.BlockSpec((tm, Cd), row),
                  pl.BlockSpec((8, Cd), lambda i: (jnp.maximum(i * (tm // 8) - 1, 0), 0)),
                  pl.BlockSpec((tm, Cd), row),
                  pl.BlockSpec(conv_w.shape, fixed),
                  pl.BlockSpec((B, Cd), fixed),
                  pl.BlockSpec((B, Cd), fixed),
                  pl.BlockSpec((tm, D), row),
                  pl.BlockSpec((Cd, D), fixed),
                  pl.BlockSpec((1, D), fixed),
                  pl.BlockSpec((D, LANES), fixed)],
        out_specs=[pl.BlockSpec((tm, D), row),
                   pl.BlockSpec((tm * (D // LANES), LANES), row),
                   pl.BlockSpec((tm, LANES), row),
                   pl.BlockSpec((tm, LANES), row)],
        out_shape=[jax.ShapeDtypeStruct((T, D), F32),
                   jax.ShapeDtypeStruct((T * (D // LANES), LANES), F32),
                   jax.ShapeDtypeStruct((T, LANES), jnp.int32),
                   jax.ShapeDtypeStruct((T, LANES), F32)],
        compiler_params=_cparams(1),
        name="conv_out_route",
    )(z, z, bgate, conv_w, hist0, hist1, res, w_out.astype(BF16), norm_g.reshape(1, D), wpad)


def _dispatch_kernel(tok_ref, nxt_ref, used_ref, x_hbm, o_ref, buf_ref, sem):
    i = pl.program_id(0)
    nsteps = pl.num_programs(0)
    n = o_ref.shape[0]
    ns = x_hbm.shape[1]
    slot = i % 2

    def row_copy(t_ref, r, s):
        return pltpu.make_async_copy(x_hbm.at[t_ref[0, 0, r]],
                                     buf_ref.at[s, pl.ds(pl.multiple_of(r * ns, ns), ns)], sem.at[s])

    def start_rows(t_ref, s):
        def body(r, carry):
            row_copy(t_ref, r, s).start()
            return carry

        lax.fori_loop(0, n, body, 0, unroll=8)

    @pl.when(jnp.logical_and(i == 0, used_ref[0] > 0))
    def _():
        start_rows(tok_ref, 0)

    @pl.when(jnp.logical_and(i + 1 < nsteps, used_ref[jnp.minimum(i + 1, nsteps - 1)] > 0))
    def _():
        start_rows(nxt_ref, 1 - slot)

    @pl.when(used_ref[i] > 0)
    def _():
        pltpu.make_async_copy(buf_ref.at[slot], buf_ref.at[slot], sem.at[slot]).wait()

        for s in range(ns):
            o_ref[:, s * LANES:(s + 1) * LANES] = (
                buf_ref[slot, pl.ds(s, n, stride=ns), :].astype(o_ref.dtype))

    @pl.when(used_ref[i] == 0)
    def _():
        o_ref[...] = jnp.zeros_like(o_ref)


def _dispatch(x_slab, slot_tok, sub_used):
    D = x_slab.shape[1] * x_slab.shape[2]
    nsub = sub_used.shape[0]
    SB = MOE_SUB
    tok3 = slot_tok.reshape(nsub, 1, SB)
    return pl.pallas_call(
        _dispatch_kernel,
        grid_spec=pltpu.PrefetchScalarGridSpec(
            num_scalar_prefetch=0,
            grid=(nsub,),
            in_specs=[pl.BlockSpec((1, 1, SB), lambda i: (i, 0, 0), memory_space=pltpu.SMEM),
                      pl.BlockSpec((1, 1, SB), lambda i: (jnp.minimum(i + 1, nsub - 1), 0, 0),
                                   memory_space=pltpu.SMEM),
                      pl.BlockSpec(memory_space=pltpu.SMEM),
                      pl.BlockSpec(memory_space=pl.ANY)],
            out_specs=pl.BlockSpec((SB, D), lambda i: (i, 0)),
            scratch_shapes=[pltpu.VMEM((2, SB * (D // LANES), LANES), F32),
                            pltpu.SemaphoreType.DMA((2,))],
        ),
        out_shape=jax.ShapeDtypeStruct((nsub * SB, D), BF16),
        compiler_params=_cparams(1),
        name="moe_dispatch",
    )(tok3, tok3, sub_used, x_slab)


def _row_pieces(rows, total, small, compute, skip):
    for s in range(0, total, MOE_BIG):
        size = min(MOE_BIG, total - s)
        assert size % small == 0
        whole = rows > s + size - small

        @pl.when(whole)
        def _():
            compute(s, size)

        @pl.when(jnp.logical_not(whole))
        def _():
            for t in range(s, s + size, small):
                @pl.when(rows > t)
                def _():
                    compute(t, small)

                @pl.when(rows <= t)
                def _():
                    skip(t, small)


def _moe_up_kernel(ce_ref, cs_ref, cr_ref, nu_ref, x_ref, wg_ref, wu_ref, o_ref, wbf_ref):
    c = pl.program_id(0)

    @pl.when(c < nu_ref[0])
    def _():
        def compute(s, n):
            if s == 0:
                wg = wg_ref[0].astype(BF16)
                wu = wu_ref[0].astype(BF16)
                wbf_ref[0] = wg
                wbf_ref[1] = wu
            else:
                wg = wbf_ref[0]
                wu = wbf_ref[1]
            x = x_ref[pl.ds(s, n), :]
            o_ref[pl.ds(s, n), :] = (_silu(_dot(x, wg)) * _dot(x, wu)).astype(o_ref.dtype)

        def skip(s, n):
            o_ref[pl.ds(s, n), :] = jnp.zeros((n, o_ref.shape[1]), o_ref.dtype)

        _row_pieces(cr_ref[c], x_ref.shape[0], MOE_SMALL, compute, skip)

    @pl.when(c >= nu_ref[0])
    def _():
        o_ref[...] = jnp.zeros_like(o_ref)


def _moe_up(xs, w_gu, meta, R, tf, direct):
    ce, cs, cr, nu = meta
    nch = ce.shape[0]
    D = xs.shape[1]
    F = w_gu.shape[2] // 2
    nf = F // tf

    def fidx(c, f, nu):
        return jnp.where(c < nu[0], f, nf - 1)

    if direct:
        in_specs = [pl.BlockSpec((R, D), lambda c, f, *_: (c, 0)),
                    pl.BlockSpec((1, D, tf), lambda c, f, *_: (c, 0, f)),
                    pl.BlockSpec((1, D, tf), lambda c, f, *_: (c, 0, nf + f))]
    else:
        in_specs = [pl.BlockSpec((R, D), lambda c, f, ce, cs, cr, nu: (cs[c], 0),
                                 pipeline_mode=pl.Buffered(1)),
                    pl.BlockSpec((1, D, tf), lambda c, f, ce, cs, cr, nu: (ce[c], 0, fidx(c, f, nu))),
                    pl.BlockSpec((1, D, tf), lambda c, f, ce, cs, cr, nu: (ce[c], 0, nf + fidx(c, f, nu)))]
    return pl.pallas_call(
        _moe_up_kernel,
        grid_spec=pltpu.PrefetchScalarGridSpec(
            num_scalar_prefetch=4,
            grid=(nch, nf),
            in_specs=in_specs,
            out_specs=pl.BlockSpec((R, tf), lambda c, f, ce, cs, cr, nu: (c, f)),
            scratch_shapes=[pltpu.VMEM((2, D, tf), BF16)],
        ),
        out_shape=jax.ShapeDtypeStruct((nch * R, F), BF16),
        compiler_params=_cparams(2, vmem=VMEM_LIMIT_MAX if direct else VMEM_LIMIT),
        name="moe_gate_up",
    )(ce, cs, cr, nu, xs, w_gu, w_gu)


def _moe_down_kernel(ce_ref, cs_ref, cr_ref, nu_ref, h_ref, w_ref, o_ref, wbf_ref):
    c = pl.program_id(0)
    k = pl.program_id(2)

    @pl.when(c < nu_ref[0])
    def _():
        def weight(s):
            if s == 0:
                w = w_ref[0].astype(BF16)
                wbf_ref[...] = w
                return w
            return wbf_ref[...]

        def compute(s, n):
            @pl.when(k == 0)
            def _():
                o_ref[pl.ds(s, n), :] = _dot(h_ref[pl.ds(s, n), :], weight(s))

            @pl.when(k > 0)
            def _():
                o_ref[pl.ds(s, n), :] = (o_ref[pl.ds(s, n), :]
                                         + _dot(h_ref[pl.ds(s, n), :], weight(s)))

        def skip(s, n):
            @pl.when(k == 0)
            def _():
                o_ref[pl.ds(s, n), :] = jnp.zeros((n, o_ref.shape[1]), F32)

        _row_pieces(cr_ref[c], h_ref.shape[0], MOE_SUB, compute, skip)

    @pl.when(jnp.logical_and(c >= nu_ref[0], k == 0))
    def _():
        o_ref[...] = jnp.zeros_like(o_ref)


def _moe_down(hid, w_down, meta, R, tn, tk, direct):
    ce, cs, cr, nu = meta
    nch = ce.shape[0]
    F = hid.shape[1]
    D = w_down.shape[2]
    nn, nk = D // tn, F // tk

    def last_if_unused(c, v, nu, last):
        return jnp.where(c < nu[0], v, last)

    if direct:
        in_specs = [pl.BlockSpec((R, tk), lambda c, n, k, *_: (c, k)),
                    pl.BlockSpec((1, tk, tn), lambda c, n, k, *_: (c, k, n))]
    else:
        in_specs = [pl.BlockSpec((R, tk), lambda c, n, k, ce, cs, cr, nu:
                                 (cs[c], last_if_unused(c, k, nu, nk - 1))),
                    pl.BlockSpec((1, tk, tn), lambda c, n, k, ce, cs, cr, nu:
                                 (ce[c], last_if_unused(c, k, nu, nk - 1),
                                  last_if_unused(c, n, nu, nn - 1)))]
    return pl.pallas_call(
        _moe_down_kernel,
        grid_spec=pltpu.PrefetchScalarGridSpec(
            num_scalar_prefetch=4,
            grid=(nch, nn, nk),
            in_specs=in_specs,
            out_specs=pl.BlockSpec((R, tn), lambda c, n, k, ce, cs, cr, nu: (c, n)),
            scratch_shapes=[pltpu.VMEM((tk, tn), BF16)],
        ),
        out_shape=jax.ShapeDtypeStruct((nch * R, D), F32),
        compiler_params=_cparams(3),
        name="moe_down",
    )(ce, cs, cr, nu, hid, w_down)


def _combine_kernel(n_prompt_blocks, has_overflow, p1_ref, p2_ref, n1_ref, n2_ref, x_ref, g1_ref,
                    g2_ref, fg_ref, ys_hbm, *rest):
    yo_hbm = rest[0] if has_overflow else None
    op_ref, os_ref, buf_ref, sem = rest[-4:]
    i = pl.program_id(0)
    nsteps = pl.num_programs(0)
    n = x_ref.shape[0]
    slot = i % 2
    split = ys_hbm.shape[0]

    def start_row(pos, k, r, s):
        dst = buf_ref.at[s, k, pl.ds(r, 1)]
        if yo_hbm is None:
            pltpu.make_async_copy(ys_hbm.at[pl.ds(pos, 1)], dst, sem.at[s]).start()
        else:
            @pl.when(pos < split)
            def _():
                pltpu.make_async_copy(ys_hbm.at[pl.ds(pos, 1)], dst, sem.at[s]).start()

            @pl.when(pos >= split)
            def _():
                pltpu.make_async_copy(yo_hbm.at[pl.ds(pos - split, 1)], dst, sem.at[s]).start()

    def start_rows(a_ref, b_ref, s):
        def body(r, carry):
            start_row(a_ref[0, 0, r], 0, r, s)
            start_row(b_ref[0, 0, r], 1, r, s)
            return carry

        lax.fori_loop(0, n, body, 0, unroll=8)

    @pl.when(i == 0)
    def _():
        start_rows(p1_ref, p2_ref, 0)

    @pl.when(i + 1 < nsteps)
    def _():
        start_rows(n1_ref, n2_ref, 1 - slot)

    pltpu.make_async_copy(buf_ref.at[slot], buf_ref.at[slot], sem.at[slot]).wait()
    y = x_ref[...] + (g1_ref[...] * buf_ref[slot, 0] + g2_ref[...] * buf_ref[slot, 1])
    yn = _rms(y, fg_ref[...])

    @pl.when(i < n_prompt_blocks)
    def _():
        op_ref[...] = yn

    @pl.when(i >= n_prompt_blocks)
    def _():
        os_ref[...] = yn


def _combine(x, ys, ys_over, pos1, pos2, g1, g2, final_g, n_prompt):
    T, D = x.shape
    srcs = [ys] if ys_over is None else [ys, ys_over]
    tr = T - n_prompt
    assert n_prompt % tr == 0
    nb = T // tr
    npb = n_prompt // tr
    cur = pl.BlockSpec((1, 1, tr), lambda i: (i, 0, 0), memory_space=pltpu.SMEM)
    nxt = pl.BlockSpec((1, 1, tr), lambda i: (jnp.minimum(i + 1, nb - 1), 0, 0),
                       memory_space=pltpu.SMEM)
    p1 = pos1.reshape(nb, 1, tr)
    p2 = pos2.reshape(nb, 1, tr)
    return pl.pallas_call(
        functools.partial(_combine_kernel, npb, ys_over is not None),
        grid_spec=pltpu.PrefetchScalarGridSpec(
            num_scalar_prefetch=0,
            grid=(nb,),
            in_specs=[cur, cur, nxt, nxt,
                      pl.BlockSpec((tr, D), lambda i: (i, 0)),
                      pl.BlockSpec((tr, 1), lambda i: (i, 0)),
                      pl.BlockSpec((tr, 1), lambda i: (i, 0)),
                      pl.BlockSpec((1, D), lambda i: (0, 0))]
                     + [pl.BlockSpec(memory_space=pl.ANY)] * len(srcs),
            out_specs=[pl.BlockSpec((tr, D), lambda i: (jnp.minimum(i, npb - 1), 0)),
                       pl.BlockSpec((tr, D), lambda i: (0, 0))],
            scratch_shapes=[pltpu.VMEM((2, 2, tr, D), F32), pltpu.SemaphoreType.DMA((2,))],
        ),
        out_shape=[jax.ShapeDtypeStruct((n_prompt, D), F32),
                   jax.ShapeDtypeStruct((tr, D), F32)],
        compiler_params=_cparams(1),
        name="moe_combine_final_norm",
    )(p1, p2, p1, p2, x, g1, g2, final_g.reshape(1, D), *srcs)


def _moe_plan(idx, T):
    M = T * TOP_K
    E = N_EXPERTS
    R = -(-int(M / E * MOE_CAPACITY) // MOE_SUB) * MOE_SUB
    n_over = -(-M // R)
    i32 = jnp.int32
    flat_e = idx.reshape(M)
    onehot = (flat_e[:, None] == jnp.arange(E, dtype=i32)[None, :]).astype(i32)
    csum = jnp.cumsum(onehot, axis=0)
    rank = jnp.sum((csum - onehot) * onehot, axis=1)
    counts = csum[-1]
    o_chunks = jnp.maximum((counts + R - 1) // R - 1, 0)
    o_first = jnp.cumsum(o_chunks) - o_chunks
    n_used = jnp.sum(o_chunks).astype(i32)
    over = rank - R
    slot = jnp.where(rank < R, flat_e * R + rank,
                     (E + o_first[flat_e] + over // R) * R + over % R)
    flat_tok = jnp.arange(M, dtype=i32) // TOP_K
    slot_tok = jnp.zeros(((E + n_over) * R,), i32).at[slot].set(flat_tok)
    spc = R // MOE_SUB

    def sub_used(c_rows):
        sub = jnp.arange(c_rows.shape[0] * spc, dtype=i32)
        return ((sub % spc) * MOE_SUB < c_rows[sub // spc]).astype(i32)

    m_rows = jnp.minimum(counts, R).astype(i32)
    eidx = jnp.arange(E, dtype=i32)
    main = dict(tok=slot_tok[:E * R], used=sub_used(m_rows),
                meta=(eidx, eidx, m_rows, jnp.full((1,), E, i32)))
    cidx = jnp.arange(n_over, dtype=i32)
    c_e = jnp.clip(jnp.searchsorted(jnp.cumsum(o_chunks), cidx, side='right'), 0, E - 1).astype(i32)
    c_rows = jnp.clip(counts[c_e] - R - (cidx - o_first[c_e]) * R, 0, R).astype(i32)
    used = cidx < n_used
    last = jnp.maximum(n_used - 1, 0)
    c_src = jnp.where(used, cidx, last).astype(i32)
    c_e = jnp.where(used, c_e, c_e[last]).astype(i32)
    c_rows = jnp.where(used, c_rows, 0).astype(i32)
    overflow = dict(tok=slot_tok[E * R:], used=sub_used(c_rows),
                    meta=(c_e, c_src, c_rows, n_used.reshape(1)))
    pos = slot.reshape(T, TOP_K).astype(i32)
    return R, main, overflow, n_used, pos[:, 0], pos[:, 1]


def kernel(x_prompt, x_sample, state_gla, state_conv, ev_norm_mix, ev_w_in, ev_sgu_norm, ev_sgu_w, ev_sgu_b, ev_gla_w_alpha, ev_gla_b_alpha, ev_gla_norm, ev_w_out, ev_norm_ffn, ev_ffn_w_gu, ev_ffn_w_down, od_norm_mix, od_conv_w_in, od_conv_w, od_conv_w_out, od_norm_ffn, od_router_w, od_moe_w_gu, od_moe_w_down, final_norm):
    nb, seq, D = x_prompt.shape
    B = x_sample.shape[0]
    assert x_sample.shape[1] == 1
    Tp = nb * seq
    T = Tp + B
    xp0 = x_prompt.reshape(Tp, D)
    xs0 = x_sample.reshape(B, D)

    sw = ev_sgu_norm.shape[1]
    kw = ev_gla_w_alpha.shape[2]
    rank = ev_gla_w_alpha.shape[1]
    vw = GLA_HEADS * ev_gla_norm.shape[1]
    n_main = 2 * sw + 2 * kw + 2 * vw
    assert ev_w_in.shape[2] == n_main + rank
    tm = _pick_tile(T, 1664)

    h = _norm_bf16_stacked(xp0, xs0, ev_norm_mix[0])
    P = _matmul(h, ev_w_in, n_main, tm, 512, name="mix_in_proj")
    log_a = _log_decay(h, ev_w_in, n_main, ev_gla_w_alpha[0], ev_gla_b_alpha[0], tm)
    a_out, v_sample = _sgu(P, Tp, B, sw, ev_sgu_norm[0], ev_sgu_w[0], ev_sgu_b[0])
    col_q = 2 * sw
    o_p, gla_p = _gla_prompt(P, log_a, nb, seq, col_q, kw, vw, ev_gla_norm[0])
    Ps = P[Tp:]
    q_s = Ps[:, col_q:col_q + kw]
    k_s = Ps[:, col_q + kw:col_q + 2 * kw]
    v_s = Ps[:, col_q + 2 * kw:col_q + 2 * kw + vw]
    r_s = Ps[:, col_q + 2 * kw + vw:]
    gla_s, o_s = _gla_sample(state_gla[0], log_a[Tp:], k_s, q_s, v_s, r_s, ev_gla_norm[0])
    x1, h = _mix_out(a_out, o_p, o_s.reshape(B, vw).astype(BF16), xp0, xs0, ev_w_out[0],
                     ev_norm_ffn[0])

    hid = _swiglu_up(h, ev_ffn_w_gu[0], tm, 512)
    x2 = _matmul(hid, ev_ffn_w_down[0], D, _pick_tile(T, 832), 256, res=x1, name="ffn_down")

    h = _norm_bf16(x2, od_norm_mix[0])
    bgate, z = _conv_in(h, od_conv_w_in[0], tm, 256)
    x3, h3_slab, idx, gates = _conv_out_route(
        z, bgate, od_conv_w[0], state_conv[0][:, 0], state_conv[0][:, 1],
        x2, od_conv_w_out[0], od_norm_ffn[0], od_router_w[0], Tp, seq)
    h3_slab = h3_slab.reshape(T, D // LANES, LANES)
    conv_p = jnp.stack([z[(n + 1) * seq - 2:(n + 1) * seq] for n in range(nb)])
    conv_s = jnp.concatenate([state_conv[0][:, 1:], z[Tp:].reshape(B, 1, D)], axis=1)

    R, main, overflow, n_overflow, pos1, pos2 = _moe_plan(idx[:, :TOP_K], T)
    FE = od_moe_w_down.shape[2]
    tf, tk = _pick_tile(FE, 512, LANES), _pick_tile(FE, 1024, LANES)

    def experts(part, direct):
        xs = _dispatch(h3_slab, part["tok"], part["used"])
        hid = _moe_up(xs, od_moe_w_gu[0], part["meta"], R, tf, direct)
        return _moe_down(hid, od_moe_w_down[0], part["meta"], R, 1024, tk, direct)

    ys = experts(main, True)
    g1, g2 = gates[:, 0:1], gates[:, 1:2]
    y_p, y_s = lax.cond(
        n_overflow > 0,
        lambda: _combine(x3, ys, experts(overflow, False), pos1, pos2, g1, g2, final_norm, Tp),
        lambda: _combine(x3, ys, None, pos1, pos2, g1, g2, final_norm, Tp))

    y_prompt = y_p.reshape(nb, seq, D)
    y_sample = y_s.reshape(B, 1, D)
    return (y_prompt, y_sample, gla_p[None], gla_s[None],
            v_sample.reshape(1, B, 1, sw), conv_p[None], conv_s[None])
```

```python
import functools
import math

import numpy as np
import jax
import jax.numpy as jnp
from jax import lax
from jax.experimental import pallas as pl
from jax.experimental.pallas import tpu as pltpu

F32 = jnp.float32
BF16 = jnp.bfloat16
EPS = 1e-6

SGU_CHUNK = 128
SGU_GROUPS = 4
GLA_HEADS = 4
GLA_TAU = 16.0
GLA_CHUNK = 128
N_EXPERTS = 8
TOP_K = 2
MOE_SUB = 256
MOE_CAPACITY = 1.35
MOE_BIG = 1024
MOE_SMALL = 128
LANES = 128
VMEM_LIMIT = 52 * 1024 * 1024
VMEM_LIMIT_MAX = 60 * 1024 * 1024


def _cparams(n_axes, vmem=VMEM_LIMIT):
    return pltpu.CompilerParams(dimension_semantics=("arbitrary",) * n_axes,
                                vmem_limit_bytes=vmem)


def _pick_tile(n, cap, mult=16):
    best = None
    for t in range(mult, min(n, cap) + 1, mult):
        if n % t == 0:
            best = t
    assert best is not None, (n, cap, mult)
    return best


def _rms(x, g):
    ms = jnp.mean(x * x, axis=-1, keepdims=True)
    return (x * lax.rsqrt(ms + EPS)) * g


def _gelu(x):
    c = math.sqrt(2.0 / math.pi)
    return x * (0.5 * (1.0 + jnp.tanh(c * (x + 0.044715 * (x * x * x)))))


def _silu(x):
    return x * (1.0 / (1.0 + jnp.exp(-x)))


def _dot(a, b):
    return jnp.dot(a, b, preferred_element_type=F32)


def _dot_nt(a, b):
    return lax.dot_general(a, b, (((1,), (1,)), ((), ())), preferred_element_type=F32)


def _norm_kernel(x_ref, g_ref, o_ref):
    o_ref[...] = _rms(x_ref[...], g_ref[...]).astype(BF16)


def _norm_bf16(x, g):
    T, D = x.shape
    tr = _pick_tile(T, 512)
    return pl.pallas_call(
        _norm_kernel,
        grid=(T // tr,),
        in_specs=[pl.BlockSpec((tr, D), lambda i: (i, 0)),
                  pl.BlockSpec((1, D), lambda i: (0, 0))],
        out_specs=pl.BlockSpec((tr, D), lambda i: (i, 0)),
        out_shape=jax.ShapeDtypeStruct((T, D), BF16),
        compiler_params=_cparams(1),
        name="rmsnorm_bf16",
    )(x, g.reshape(1, D))


def _norm2_kernel(n_prompt_blocks, xp_ref, xs_ref, g_ref, o_ref):
    i = pl.program_id(0)

    @pl.when(i < n_prompt_blocks)
    def _():
        o_ref[...] = _rms(xp_ref[...], g_ref[...]).astype(BF16)

    @pl.when(i >= n_prompt_blocks)
    def _():
        o_ref[0:xs_ref.shape[0], :] = _rms(xs_ref[...], g_ref[...]).astype(BF16)


def _norm_bf16_stacked(xp, xs, g):
    Tp, D = xp.shape
    B = xs.shape[0]
    tr = _pick_tile(Tp, 512)
    assert B <= tr
    npb = Tp // tr
    return pl.pallas_call(
        functools.partial(_norm2_kernel, npb),
        grid=(npb + 1,),
        in_specs=[pl.BlockSpec((tr, D), lambda i: (jnp.minimum(i, npb - 1), 0)),
                  pl.BlockSpec((B, D), lambda i: (0, 0)),
                  pl.BlockSpec((1, D), lambda i: (0, 0))],
        out_specs=pl.BlockSpec((tr, D), lambda i: (i, 0)),
        out_shape=jax.ShapeDtypeStruct((Tp + B, D), BF16),
        compiler_params=_cparams(1),
        name="rmsnorm_stack_bf16",
    )(xp, xs, g.reshape(1, D))


def _mm_kernel(x_ref, w_ref, o_ref):
    o_ref[...] = _dot(x_ref[...], w_ref[...].astype(BF16)).astype(o_ref.dtype)


def _mm_res_kernel(x_ref, w_ref, r_ref, o_ref):
    o_ref[...] = r_ref[...] + _dot(x_ref[...], w_ref[...].astype(BF16))


def _matmul(x, w, n_cols, tm, tn, res=None, name="matmul"):
    T, K = x.shape
    grid = (T // tm, n_cols // tn)
    in_specs = [pl.BlockSpec((tm, K), lambda i, j: (i, 0)),
                pl.BlockSpec((K, tn), lambda i, j: (0, j))]
    args = [x, w]
    kern = _mm_kernel
    if res is not None:
        in_specs.append(pl.BlockSpec((tm, tn), lambda i, j: (i, j)))
        args.append(res)
        kern = _mm_res_kernel
    return pl.pallas_call(
        kern, grid=grid, in_specs=in_specs,
        out_specs=pl.BlockSpec((tm, tn), lambda i, j: (i, j)),
        out_shape=jax.ShapeDtypeStruct((T, n_cols), F32),
        compiler_params=_cparams(2),
        name=name,
    )(*args)


def _mix_out_kernel(n_full, n_tail, a_ref, op_ref, os_ref, rp_ref, rs_ref, w_ref, ng_ref,
                    x_ref, h_ref):
    i = pl.program_id(0)
    sw = a_ref.shape[1]

    def finish(rows, a, o, res):
        y = res + _dot(a, w_ref[0:sw, :]) + _dot(o, w_ref[sw:, :])
        x_ref[0:rows, :] = y
        h_ref[0:rows, :] = _rms(y, ng_ref[...]).astype(BF16)

    @pl.when(i < n_full)
    def _():
        finish(a_ref.shape[0], a_ref[...], op_ref[...], rp_ref[...])

    @pl.when(i >= n_full)
    def _():
        finish(n_tail, a_ref[0:n_tail, :], os_ref[...], rs_ref[...])


def _mix_out(a, o_p, o_s, res_p, res_s, w, norm_g):
    T, sw = a.shape
    Tp, vw = o_p.shape
    B = o_s.shape[0]
    D = w.shape[1]
    assert T == Tp + B
    tm = _pick_tile(Tp, 512)
    assert B <= tm
    n_full = Tp // tm
    clamp = lambda i: (jnp.minimum(i, n_full - 1), 0)
    row = lambda i: (i, 0)
    fixed = lambda i: (0, 0)
    return pl.pallas_call(
        functools.partial(_mix_out_kernel, n_full, B),
        grid=(n_full + 1,),
        in_specs=[pl.BlockSpec((tm, sw), row),
                  pl.BlockSpec((tm, vw), clamp),
                  pl.BlockSpec((B, vw), fixed),
                  pl.BlockSpec((tm, D), clamp),
                  pl.BlockSpec((B, D), fixed),
                  pl.BlockSpec((sw + vw, D), fixed),
                  pl.BlockSpec((1, D), fixed)],
        out_specs=[pl.BlockSpec((tm, D), row),
                   pl.BlockSpec((tm, D), row)],
        out_shape=[jax.ShapeDtypeStruct((T, D), F32),
                   jax.ShapeDtypeStruct((T, D), BF16)],
        compiler_params=_cparams(1),
        name="mix_out_proj",
    )(a, o_p, o_s, res_p, res_s, w.astype(BF16), norm_g.reshape(1, D))


def _swiglu_kernel(x_ref, wg_ref, wu_ref, o_ref):
    x = x_ref[...]
    g = _dot(x, wg_ref[...].astype(BF16))
    u = _dot(x, wu_ref[...].astype(BF16))
    o_ref[...] = (_silu(g) * u).astype(o_ref.dtype)


def _swiglu_up(x, w_gu, tm, tn):
    T, K = x.shape
    F = w_gu.shape[1] // 2
    nj = F // tn
    return pl.pallas_call(
        _swiglu_kernel,
        grid=(T // tm, nj),
        in_specs=[pl.BlockSpec((tm, K), lambda i, j: (i, 0)),
                  pl.BlockSpec((K, tn), lambda i, j: (0, j)),
                  pl.BlockSpec((K, tn), lambda i, j: (0, j + nj))],
        out_specs=pl.BlockSpec((tm, tn), lambda i, j: (i, j)),
        out_shape=jax.ShapeDtypeStruct((T, F), BF16),
        compiler_params=_cparams(2, vmem=VMEM_LIMIT_MAX),
        name="ffn_gate_up",
    )(x, w_gu, w_gu)


def _conv_in_kernel(x_ref, wb_ref, wc_ref, wu_ref, b_ref, z_ref):
    x = x_ref[...]
    b_ref[...] = _dot(x, wb_ref[...].astype(BF16))
    c = _dot(x, wc_ref[...].astype(BF16))
    u = _dot(x, wu_ref[...].astype(BF16))
    z_ref[...] = c * u


def _conv_in(x, w, tm, tn):
    T, K = x.shape
    C = w.shape[1] // 3
    nj = C // tn
    return pl.pallas_call(
        _conv_in_kernel,
        grid=(T // tm, nj),
        in_specs=[pl.BlockSpec((tm, K), lambda i, j: (i, 0)),
                  pl.BlockSpec((K, tn), lambda i, j: (0, j)),
                  pl.BlockSpec((K, tn), lambda i, j: (0, j + nj)),
                  pl.BlockSpec((K, tn), lambda i, j: (0, j + 2 * nj))],
        out_specs=[pl.BlockSpec((tm, tn), lambda i, j: (i, j)),
                   pl.BlockSpec((tm, tn), lambda i, j: (i, j))],
        out_shape=[jax.ShapeDtypeStruct((T, C), F32),
                   jax.ShapeDtypeStruct((T, C), F32)],
        compiler_params=_cparams(2),
        name="conv_in_proj",
    )(x, w, w, w)


def _mix_in_kernel(x_ref, w_ref, wl_ref, wa_ref, b_ref, p_ref, la_ref):
    x = x_ref[...]
    p_ref[...] = _dot(x, w_ref[...].astype(BF16))

    @pl.when(pl.program_id(1) == pl.num_programs(1) - 1)
    def _():
        rank = wa_ref.shape[0]
        g = _dot(x, wl_ref[:, 0:rank].astype(BF16))
        z = _dot(g.astype(BF16), wa_ref[...].astype(BF16)) + b_ref[...]
        ls = jnp.minimum(z, 0.0) - jnp.log1p(jnp.exp(-jnp.abs(z)))
        la_ref[...] = ls * (1.0 / GLA_TAU)


def _mix_in(x, w_in, n_main, w_alpha, b_alpha, tm, tn):
    T, K = x.shape
    R, KW = w_alpha.shape
    assert n_main % LANES == 0 and R <= LANES
    return pl.pallas_call(
        _mix_in_kernel,
        grid=(T // tm, n_main // tn),
        in_specs=[pl.BlockSpec((tm, K), lambda i, j: (i, 0)),
                  pl.BlockSpec((None, K, tn), lambda i, j: (0, 0, j)),
                  pl.BlockSpec((None, K, LANES), lambda i, j: (0, 0, n_main // LANES)),
                  pl.BlockSpec((R, KW), lambda i, j: (0, 0)),
                  pl.BlockSpec((1, KW), lambda i, j: (0, 0))],
        out_specs=[pl.BlockSpec((tm, tn), lambda i, j: (i, j)),
                   pl.BlockSpec((tm, KW), lambda i, j: (i, 0))],
        out_shape=[jax.ShapeDtypeStruct((T, n_main), F32),
                   jax.ShapeDtypeStruct((T, KW), F32)],
        compiler_params=_cparams(2, vmem=VMEM_LIMIT_MAX),
        name="mix_in_proj",
    )(x, w_in, w_in, w_alpha, b_alpha.reshape(1, KW))


def _sgu_kernel(n_prompt_tiles, u_ref, v_ref, ng_ref, w_ref, bt_ref, o_ref, vs_ref):
    i = pl.program_id(0)
    C = SGU_CHUNK
    gd = u_ref.shape[1] // SGU_GROUPS
    row = lax.broadcasted_iota(jnp.int32, (C, C), 0)
    col = lax.broadcasted_iota(jnp.int32, (C, C), 1)

    def chunk(r0, weights, biases, keep_v):
        gu = _gelu(u_ref[r0:r0 + C, :])
        vn = _rms(_gelu(v_ref[r0:r0 + C, :]), ng_ref[...])
        if keep_v:
            vs_ref[...] = vn
        vb = vn.astype(BF16)
        for g in range(SGU_GROUPS):
            cols = slice(g * gd, (g + 1) * gd)
            mixed = _dot(weights[g], vb[:, cols]) + biases[g]
            o_ref[r0:r0 + C, cols] = (gu[:, cols] * mixed).astype(o_ref.dtype)

    @pl.when(i < n_prompt_tiles)
    def _():
        weights = [jnp.where(col <= row, w_ref[g], 0.0).astype(BF16) for g in range(SGU_GROUPS)]
        biases = [bt_ref[:, g:g + 1] for g in range(SGU_GROUPS)]
        for r0 in range(0, u_ref.shape[0], C):
            chunk(r0, weights, biases, False)

    @pl.when(i >= n_prompt_tiles)
    def _():
        weights = [jnp.where(col == row, w_ref[g, 0:1, 0:1], 0.0).astype(BF16)
                   for g in range(SGU_GROUPS)]
        biases = [bt_ref[0:1, g:g + 1] for g in range(SGU_GROUPS)]
        chunk(0, weights, biases, True)


def _sgu(P, n_prompt, n_sample, width, sgu_norm, sgu_w, sgu_b):
    C = SGU_CHUNK
    assert n_prompt % C == 0 and n_sample == C
    T = n_prompt + n_sample
    tr = C * math.gcd(n_prompt // C, 4)
    npt = n_prompt // tr
    return pl.pallas_call(
        functools.partial(_sgu_kernel, npt),
        grid=(npt + 1,),
        in_specs=[pl.BlockSpec((tr, width), lambda i: (i, 0)),
                  pl.BlockSpec((tr, width), lambda i: (i, 1)),
                  pl.BlockSpec((1, width), lambda i: (0, 0)),
                  pl.BlockSpec((SGU_GROUPS, C, C), lambda i: (0, 0, 0)),
                  pl.BlockSpec((C, SGU_GROUPS), lambda i: (0, 0))],
        out_specs=[pl.BlockSpec((tr, width), lambda i: (i, 0)),
                   pl.BlockSpec((C, width), lambda i: (0, 0))],
        out_shape=[jax.ShapeDtypeStruct((T, width), BF16),
                   jax.ShapeDtypeStruct((n_sample, width), F32)],
        compiler_params=_cparams(1),
        name="sgu_mix",
    )(P, P, sgu_norm.reshape(1, width), sgu_w, sgu_b.T)


def _gla_constants():
    C = GLA_CHUNK
    s = np.arange(C)[None, :]
    i = np.arange(C)[:, None]
    gs = [s <= i, s > i]
    masks = []
    b = C // 2
    while b >= 1:
        upper = (i % (2 * b)) >= b
        m = (i // (2 * b)) * (2 * b) + b - 1
        gs.append(np.where(upper, (s > m) & (s <= i), (s > i) & (s <= m)))
        same = (i // (2 * b)) == (s // (2 * b))
        masks.append(upper & same & ~upper.T)
        b //= 2
    masks.append(np.eye(C, dtype=bool))
    G = np.concatenate(gs, axis=0).astype(np.float32)
    M = np.stack(masks).astype(np.float32)
    return jnp.asarray(G, BF16), jnp.asarray(M, F32)


def _gla_kernel(nseq, *refs):
    ins = [refs[5 * s:5 * s + 5] for s in range(nseq)]
    G_ref, M_ref, gn_ref, o_ref, s_out_ref, st_ref = refs[5 * nseq:]
    c = pl.program_id(1)
    C = GLA_CHUNK
    hk = ins[0][0].shape[1]
    dk = hk // GLA_HEADS
    dv = ins[0][2].shape[1] // GLA_HEADS
    nlev = M_ref.shape[0] - 1

    @pl.when(c == 0)
    def _():
        st_ref[...] = jnp.zeros_like(st_ref)

    parts = []
    for s in range(nseq):
        la = ins[s][4][...]
        hi = la.astype(BF16)
        r1 = la - hi.astype(F32)
        mid = r1.astype(BF16)
        lo = (r1 - mid.astype(F32)).astype(BF16)
        parts.append(jnp.concatenate([hi, mid, lo], axis=0))
    X = jnp.exp(_dot(G_ref[...], jnp.concatenate(parts, axis=1)))

    scale = dk ** -0.5
    for s in range(nseq):
        q_ref, k_ref, v_ref, r_ref, _ = ins[s]
        for h in range(GLA_HEADS):
            Xh = X[:, s * hk + h * dk:s * hk + (h + 1) * dk]
            qh = q_ref[:, h * dk:(h + 1) * dk] * scale
            kh = k_ref[:, h * dk:(h + 1) * dk]
            vh = v_ref[:, h * dv:(h + 1) * dv]
            x_cum = Xh[0:C]
            x_rest = Xh[C:2 * C]
            st_old = st_ref[s, h]
            o = _dot_nt((qh * x_cum).astype(BF16), st_old.astype(BF16))
            att = _dot_nt(qh.astype(BF16), kh.astype(BF16)) * M_ref[nlev]
            for l in range(nlev):
                xl = Xh[(2 + l) * C:(3 + l) * C]
                att = att + _dot_nt((qh * xl).astype(BF16), (kh * xl).astype(BF16)) * M_ref[l]
            o = o + _dot(att.astype(BF16), vh.astype(BF16))
            vt = vh.T.astype(BF16)
            st_ref[s, h] = st_old * x_cum[C - 1:C, :] + _dot(vt, (kh * x_rest).astype(BF16))
            on = _rms(o, gn_ref[...]) * _silu(r_ref[:, h * dv:(h + 1) * dv])
            o_ref[s, :, h * dv:(h + 1) * dv] = on.astype(o_ref.dtype)

    @pl.when(c == pl.num_programs(1) - 1)
    def _():
        for s in range(nseq):
            for h in range(GLA_HEADS):
                s_out_ref[s, h] = st_ref[s, h].T


def _gla_prompt(P, log_a, n_seq, seq, col_q, kw, vw, gla_norm):
    C = GLA_CHUNK
    assert seq % C == 0
    nc = seq // C
    dk = kw // GLA_HEADS
    dv = vw // GLA_HEADS
    qb, kb = col_q // kw, col_q // kw + 1
    vb = (col_q + 2 * kw) // vw
    rb = vb + 1
    assert col_q % kw == 0 and (col_q + 2 * kw) % vw == 0
    nseq = math.gcd(n_seq, 4)
    G, M = _gla_constants()
    G3 = jnp.concatenate([G, G, G], axis=1)

    def seq_specs(s):
        rows = lambda n, c: (n * nseq + s) * nc + c
        return [pl.BlockSpec((C, kw), lambda n, c: (rows(n, c), qb)),
                pl.BlockSpec((C, kw), lambda n, c: (rows(n, c), kb)),
                pl.BlockSpec((C, vw), lambda n, c: (rows(n, c), vb)),
                pl.BlockSpec((C, vw), lambda n, c: (rows(n, c), rb)),
                pl.BlockSpec((C, kw), lambda n, c: (rows(n, c), 0))]

    in_specs = [sp for s in range(nseq) for sp in seq_specs(s)]
    in_specs += [pl.BlockSpec(G3.shape, lambda n, c: (0, 0)),
                 pl.BlockSpec(M.shape, lambda n, c: (0, 0, 0)),
                 pl.BlockSpec((1, dv), lambda n, c: (0, 0))]
    o, states = pl.pallas_call(
        functools.partial(_gla_kernel, nseq),
        grid=(n_seq // nseq, nc),
        in_specs=in_specs,
        out_specs=[pl.BlockSpec((None, nseq, C, vw), lambda n, c: (n, 0, c, 0)),
                   pl.BlockSpec((nseq, GLA_HEADS, dk, dv), lambda n, c: (n, 0, 0, 0))],
        out_shape=[jax.ShapeDtypeStruct((n_seq // nseq, nseq, seq, vw), BF16),
                   jax.ShapeDtypeStruct((n_seq, GLA_HEADS, dk, dv), F32)],
        scratch_shapes=[pltpu.VMEM((nseq, GLA_HEADS, dv, dk), F32)],
        compiler_params=_cparams(2),
        name="gla_prompt",
    )(*([P, P, P, P, log_a] * nseq), G3, M, gla_norm.reshape(1, dv))
    return o.reshape(n_seq * seq, vw), states


def _gla_step_kernel(s_ref, at_ref, kt_ref, qt_ref, v_ref, r_ref, gn_ref, sn_ref, o_ref):
    i = pl.program_id(0)
    bb, H, dk, dv = s_ref.shape
    B = at_ref.shape[1]
    lane = lax.broadcasted_iota(jnp.int32, (dk, B), 1)
    scale = dk ** -0.5
    for j in range(bb):
        pick = lane == i * bb + j

        def column(t_ref, h):
            return jnp.sum(jnp.where(pick, t_ref[h * dk:(h + 1) * dk, :], 0.0), axis=1, keepdims=True)

        for h in range(H):
            s_new = jnp.exp(column(at_ref, h)) * s_ref[j, h] + column(kt_ref, h) * v_ref[j, h]
            sn_ref[j, h] = s_new
            o = jnp.sum((column(qt_ref, h) * scale) * s_new, axis=0, keepdims=True)
            o_ref[j, h] = _rms(o, gn_ref[...]) * _silu(r_ref[j, h])


def _gla_sample(state, la, k, q, v, r, gla_norm):
    B, H, dk, dv = state.shape
    bb = 8
    rowv = lambda t: t.reshape(B, H, 1, dv)
    tspec = pl.BlockSpec((H * dk, B), lambda i: (0, 0))
    rspec = pl.BlockSpec((bb, H, 1, dv), lambda i: (i, 0, 0, 0))
    sspec = pl.BlockSpec((bb, H, dk, dv), lambda i: (i, 0, 0, 0))
    return pl.pallas_call(
        _gla_step_kernel,
        grid=(B // bb,),
        in_specs=[sspec, tspec, tspec, tspec, rspec, rspec,
                  pl.BlockSpec((1, dv), lambda i: (0, 0))],
        out_specs=[sspec, rspec],
        out_shape=[jax.ShapeDtypeStruct((B, H, dk, dv), F32),
                   jax.ShapeDtypeStruct((B, H, 1, dv), F32)],
        compiler_params=_cparams(1),
        name="gla_sample_step",
    )(state, la.T, k.T, q.T, rowv(v), rowv(r), gla_norm.reshape(1, dv))


def _route(logits):
    lane = lax.broadcasted_iota(jnp.int32, logits.shape, 1)
    neg = jnp.float32(-jnp.inf)
    logits = jnp.where(lane < N_EXPERTS, logits, neg)
    v1 = jnp.max(logits, axis=-1, keepdims=True)
    i1 = jnp.min(jnp.where(logits == v1, lane, LANES), axis=-1, keepdims=True)
    rest = jnp.where(lane == i1, neg, logits)
    v2 = jnp.max(rest, axis=-1, keepdims=True)
    i2 = jnp.min(jnp.where(rest == v2, lane, LANES), axis=-1, keepdims=True)
    e2 = jnp.exp(v2 - v1)
    g1 = 1.0 / (1.0 + e2)
    g2 = e2 / (1.0 + e2)
    idx = jnp.where(lane == 0, i1, jnp.where(lane == 1, i2, 0))
    gate = jnp.where(lane == 0, g1, jnp.where(lane == 1, g2, 0.0))
    return idx, gate


def _conv_out_kernel(n_prompt_blocks, blocks_per_seq, n_tail, z_ref, halo_ref, b_ref, cw_ref,
                     h0_ref, h1_ref, res_ref, w_ref, ng_ref, rw_ref, x_ref, xslab_ref, idx_ref,
                     gate_ref):
    i = pl.program_id(0)
    w0, w1, w2 = cw_ref[0:1, :], cw_ref[1:2, :], cw_ref[2:3, :]
    ns = x_ref.shape[1] // LANES

    def finish(rows, yc, res):
        y = res + _dot(yc.astype(BF16), w_ref[...])
        x_ref[0:rows, :] = y
        hn = _rms(y, ng_ref[...])
        for s in range(ns):
            xslab_ref[pl.ds(s, rows, stride=ns), :] = hn[:, s * LANES:(s + 1) * LANES]
        h_hi = hn.astype(BF16)
        h_lo = (hn - h_hi.astype(F32)).astype(BF16)
        rw = rw_ref[...]
        r_hi = rw.astype(BF16)
        r_lo = (rw - r_hi.astype(F32)).astype(BF16)
        logits = _dot(h_hi, r_hi) + (_dot(h_hi, r_lo) + _dot(h_lo, r_hi))
        idx, gate = _route(logits)
        idx_ref[0:rows, :] = idx
        gate_ref[0:rows, :] = gate

    @pl.when(i < n_prompt_blocks)
    def _():
        z = z_ref[...]
        row = lax.broadcasted_iota(jnp.int32, z.shape, 0)
        halo = jnp.where(i % blocks_per_seq == 0, 0.0, halo_ref[...])
        z1 = jnp.where(row == 0, halo[7:8, :], pltpu.roll(z, 1, 0))
        z2 = pltpu.roll(z, 2, 0)
        z2 = jnp.where(row == 0, halo[6:7, :], jnp.where(row == 1, halo[7:8, :], z2))
        finish(z.shape[0], b_ref[...] * (w0 * z2 + w1 * z1 + w2 * z), res_ref[...])

    @pl.when(i >= n_prompt_blocks)
    def _():
        conv = w0 * h0_ref[...] + w1 * h1_ref[...] + w2 * z_ref[0:n_tail, :]
        finish(n_tail, b_ref[0:n_tail, :] * conv, res_ref[0:n_tail, :])


def _conv_out_route(z, bgate, conv_w, hist0, hist1, res, w_out, norm_g, router_w, n_prompt, seq):
    T, Cd = z.shape
    D = w_out.shape[1]
    B = T - n_prompt
    tm = _pick_tile(math.gcd(seq, n_prompt), 256)
    assert B <= tm and hist0.shape == (B, Cd)
    npb = n_prompt // tm
    wpad = jnp.zeros((D, LANES), F32).at[:, :N_EXPERTS].set(router_w)
    row = lambda i: (i, 0)
    fixed = lambda i: (0, 0)
    return pl.pallas_call(
        functools.partial(_conv_out_kernel, npb, seq // tm, B),
        grid=(npb + 1,),
        in_specs=[pl.BlockSpec((tm, Cd), row),
                  pl.BlockSpec((8, Cd), lambda i: (jnp.maximum(i * (tm // 8) - 1, 0), 0)),
                  pl.BlockSpec((tm, Cd), row),
                  pl.BlockSpec(conv_w.shape, fixed),
                  pl.BlockSpec((B, Cd), fixed),
                  pl.BlockSpec((B, Cd), fixed),
                  pl.BlockSpec((tm, D), row),
                  pl.BlockSpec((Cd, D), fixed),
                  pl.BlockSpec((1, D), fixed),
                  pl.BlockSpec((D, LANES), fixed)],
        out_specs=[pl.BlockSpec((tm, D), row),
                   pl.BlockSpec((tm * (D // LANES), LANES), row),
                   pl.BlockSpec((tm, LANES), row),
                   pl.BlockSpec((tm, LANES), row)],
        out_shape=[jax.ShapeDtypeStruct((T, D), F32),
                   jax.ShapeDtypeStruct((T * (D // LANES), LANES), F32),
                   jax.ShapeDtypeStruct((T, LANES), jnp.int32),
                   jax.ShapeDtypeStruct((T, LANES), F32)],
        compiler_params=_cparams(1),
        name="conv_out_route",
    )(z, z, bgate, conv_w, hist0, hist1, res, w_out.astype(BF16), norm_g.reshape(1, D), wpad)


def _dispatch_kernel(tok_ref, nxt_ref, used_ref, x_hbm, o_ref, buf_ref, sem):
    i = pl.program_id(0)
    nsteps = pl.num_programs(0)
    n = o_ref.shape[0]
    ns = x_hbm.shape[1]
    slot = i % 2

    def row_copy(t_ref, r, s):
        return pltpu.make_async_copy(x_hbm.at[t_ref[0, 0, r]],
                                     buf_ref.at[s, pl.ds(pl.multiple_of(r * ns, ns), ns)], sem.at[s])

    def start_rows(t_ref, s):
        def body(r, carry):
            row_copy(t_ref, r, s).start()
            return carry

        lax.fori_loop(0, n, body, 0, unroll=8)

    @pl.when(jnp.logical_and(i == 0, used_ref[0] > 0))
    def _():
        start_rows(tok_ref, 0)

    @pl.when(jnp.logical_and(i + 1 < nsteps, used_ref[jnp.minimum(i + 1, nsteps - 1)] > 0))
    def _():
        start_rows(nxt_ref, 1 - slot)

    @pl.when(used_ref[i] > 0)
    def _():
        pltpu.make_async_copy(buf_ref.at[slot], buf_ref.at[slot], sem.at[slot]).wait()

        for s in range(ns):
            o_ref[:, s * LANES:(s + 1) * LANES] = (
                buf_ref[slot, pl.ds(s, n, stride=ns), :].astype(o_ref.dtype))

    @pl.when(used_ref[i] == 0)
    def _():
        o_ref[...] = jnp.zeros_like(o_ref)


def _dispatch(x_slab, slot_tok, sub_used):
    D = x_slab.shape[1] * x_slab.shape[2]
    nsub = sub_used.shape[0]
    SB = MOE_SUB
    tok3 = slot_tok.reshape(nsub, 1, SB)
    return pl.pallas_call(
        _dispatch_kernel,
        grid_spec=pltpu.PrefetchScalarGridSpec(
            num_scalar_prefetch=0,
            grid=(nsub,),
            in_specs=[pl.BlockSpec((1, 1, SB), lambda i: (i, 0, 0), memory_space=pltpu.SMEM),
                      pl.BlockSpec((1, 1, SB), lambda i: (jnp.minimum(i + 1, nsub - 1), 0, 0),
                                   memory_space=pltpu.SMEM),
                      pl.BlockSpec(memory_space=pltpu.SMEM),
                      pl.BlockSpec(memory_space=pl.ANY)],
            out_specs=pl.BlockSpec((SB, D), lambda i: (i, 0)),
            scratch_shapes=[pltpu.VMEM((2, SB * (D // LANES), LANES), F32),
                            pltpu.SemaphoreType.DMA((2,))],
        ),
        out_shape=jax.ShapeDtypeStruct((nsub * SB, D), BF16),
        compiler_params=_cparams(1),
        name="moe_dispatch",
    )(tok3, tok3, sub_used, x_slab)


def _row_pieces(rows, total, small, compute, skip):
    for s in range(0, total, MOE_BIG):
        size = min(MOE_BIG, total - s)
        assert size % small == 0
        whole = rows > s + size - small

        @pl.when(whole)
        def _():
            compute(s, size)

        @pl.when(jnp.logical_not(whole))
        def _():
            for t in range(s, s + size, small):
                @pl.when(rows > t)
                def _():
                    compute(t, small)

                @pl.when(rows <= t)
                def _():
                    skip(t, small)


def _moe_up_kernel(ce_ref, cs_ref, cr_ref, nu_ref, x_ref, wg_ref, wu_ref, o_ref, wbf_ref):
    c = pl.program_id(0)

    @pl.when(c < nu_ref[0])
    def _():
        def compute(s, n):
            if s == 0:
                wg = wg_ref[0].astype(BF16)
                wu = wu_ref[0].astype(BF16)
                wbf_ref[0] = wg
                wbf_ref[1] = wu
            else:
                wg = wbf_ref[0]
                wu = wbf_ref[1]
            x = x_ref[pl.ds(s, n), :]
            o_ref[pl.ds(s, n), :] = (_silu(_dot(x, wg)) * _dot(x, wu)).astype(o_ref.dtype)

        def skip(s, n):
            o_ref[pl.ds(s, n), :] = jnp.zeros((n, o_ref.shape[1]), o_ref.dtype)

        _row_pieces(cr_ref[c], x_ref.shape[0], MOE_SMALL, compute, skip)

    @pl.when(c >= nu_ref[0])
    def _():
        o_ref[...] = jnp.zeros_like(o_ref)


def _moe_up(xs, w_gu, meta, R, tf, direct):
    ce, cs, cr, nu = meta
    nch = ce.shape[0]
    D = xs.shape[1]
    F = w_gu.shape[2] // 2
    nf = F // tf

    def fidx(c, f, nu):
        return jnp.where(c < nu[0], f, nf - 1)

    if direct:
        in_specs = [pl.BlockSpec((R, D), lambda c, f, *_: (c, 0)),
                    pl.BlockSpec((1, D, tf), lambda c, f, *_: (c, 0, f)),
                    pl.BlockSpec((1, D, tf), lambda c, f, *_: (c, 0, nf + f))]
    else:
        in_specs = [pl.BlockSpec((R, D), lambda c, f, ce, cs, cr, nu: (cs[c], 0),
                                 pipeline_mode=pl.Buffered(1)),
                    pl.BlockSpec((1, D, tf), lambda c, f, ce, cs, cr, nu: (ce[c], 0, fidx(c, f, nu))),
                    pl.BlockSpec((1, D, tf), lambda c, f, ce, cs, cr, nu: (ce[c], 0, nf + fidx(c, f, nu)))]
    return pl.pallas_call(
        _moe_up_kernel,
        grid_spec=pltpu.PrefetchScalarGridSpec(
            num_scalar_prefetch=4,
            grid=(nch, nf),
            in_specs=in_specs,
            out_specs=pl.BlockSpec((R, tf), lambda c, f, ce, cs, cr, nu: (c, f)),
            scratch_shapes=[pltpu.VMEM((2, D, tf), BF16)],
        ),
        out_shape=jax.ShapeDtypeStruct((nch * R, F), BF16),
        compiler_params=_cparams(2, vmem=VMEM_LIMIT_MAX if direct else VMEM_LIMIT),
        name="moe_gate_up",
    )(ce, cs, cr, nu, xs, w_gu, w_gu)


def _moe_down_kernel(ce_ref, cs_ref, cr_ref, nu_ref, h_ref, w_ref, o_ref, wbf_ref):
    c = pl.program_id(0)
    k = pl.program_id(2)

    @pl.when(c < nu_ref[0])
    def _():
        def weight(s):
            if s == 0:
                w = w_ref[0].astype(BF16)
                wbf_ref[...] = w
                return w
            return wbf_ref[...]

        def compute(s, n):
            @pl.when(k == 0)
            def _():
                o_ref[pl.ds(s, n), :] = _dot(h_ref[pl.ds(s, n), :], weight(s))

            @pl.when(k > 0)
            def _():
                o_ref[pl.ds(s, n), :] = (o_ref[pl.ds(s, n), :]
                                         + _dot(h_ref[pl.ds(s, n), :], weight(s)))

        def skip(s, n):
            @pl.when(k == 0)
            def _():
                o_ref[pl.ds(s, n), :] = jnp.zeros((n, o_ref.shape[1]), F32)

        _row_pieces(cr_ref[c], h_ref.shape[0], MOE_SUB, compute, skip)

    @pl.when(jnp.logical_and(c >= nu_ref[0], k == 0))
    def _():
        o_ref[...] = jnp.zeros_like(o_ref)


def _moe_down(hid, w_down, meta, R, tn, tk, direct):
    ce, cs, cr, nu = meta
    nch = ce.shape[0]
    F = hid.shape[1]
    D = w_down.shape[2]
    nn, nk = D // tn, F // tk

    def last_if_unused(c, v, nu, last):
        return jnp.where(c < nu[0], v, last)

    if direct:
        in_specs = [pl.BlockSpec((R, tk), lambda c, n, k, *_: (c, k)),
                    pl.BlockSpec((1, tk, tn), lambda c, n, k, *_: (c, k, n))]
    else:
        in_specs = [pl.BlockSpec((R, tk), lambda c, n, k, ce, cs, cr, nu:
                                 (cs[c], last_if_unused(c, k, nu, nk - 1))),
                    pl.BlockSpec((1, tk, tn), lambda c, n, k, ce, cs, cr, nu:
                                 (ce[c], last_if_unused(c, k, nu, nk - 1),
                                  last_if_unused(c, n, nu, nn - 1)))]
    return pl.pallas_call(
        _moe_down_kernel,
        grid_spec=pltpu.PrefetchScalarGridSpec(
            num_scalar_prefetch=4,
            grid=(nch, nn, nk),
            in_specs=in_specs,
            out_specs=pl.BlockSpec((R, tn), lambda c, n, k, ce, cs, cr, nu: (c, n)),
            scratch_shapes=[pltpu.VMEM((tk, tn), BF16)],
        ),
        out_shape=jax.ShapeDtypeStruct((nch * R, D), F32),
        compiler_params=_cparams(3),
        name="moe_down",
    )(ce, cs, cr, nu, hid, w_down)


def _combine_kernel(n_prompt_blocks, has_overflow, p1_ref, p2_ref, n1_ref, n2_ref, x_ref, g1_ref,
                    g2_ref, fg_ref, ys_hbm, *rest):
    yo_hbm = rest[0] if has_overflow else None
    op_ref, os_ref, buf_ref, sem = rest[-4:]
    i = pl.program_id(0)
    nsteps = pl.num_programs(0)
    n = x_ref.shape[0]
    slot = i % 2
    split = ys_hbm.shape[0]

    def start_row(pos, k, r, s):
        dst = buf_ref.at[s, k, pl.ds(r, 1)]
        if yo_hbm is None:
            pltpu.make_async_copy(ys_hbm.at[pl.ds(pos, 1)], dst, sem.at[s]).start()
        else:
            @pl.when(pos < split)
            def _():
                pltpu.make_async_copy(ys_hbm.at[pl.ds(pos, 1)], dst, sem.at[s]).start()

            @pl.when(pos >= split)
            def _():
                pltpu.make_async_copy(yo_hbm.at[pl.ds(pos - split, 1)], dst, sem.at[s]).start()

    def start_rows(a_ref, b_ref, s):
        def body(r, carry):
            start_row(a_ref[0, 0, r], 0, r, s)
            start_row(b_ref[0, 0, r], 1, r, s)
            return carry

        lax.fori_loop(0, n, body, 0, unroll=8)

    @pl.when(i == 0)
    def _():
        start_rows(p1_ref, p2_ref, 0)

    @pl.when(i + 1 < nsteps)
    def _():
        start_rows(n1_ref, n2_ref, 1 - slot)

    pltpu.make_async_copy(buf_ref.at[slot], buf_ref.at[slot], sem.at[slot]).wait()
    y = x_ref[...] + (g1_ref[...] * buf_ref[slot, 0] + g2_ref[...] * buf_ref[slot, 1])
    yn = _rms(y, fg_ref[...])

    @pl.when(i < n_prompt_blocks)
    def _():
        op_ref[...] = yn

    @pl.when(i >= n_prompt_blocks)
    def _():
        os_ref[...] = yn


def _combine(x, ys, ys_over, pos1, pos2, g1, g2, final_g, n_prompt):
    T, D = x.shape
    srcs = [ys] if ys_over is None else [ys, ys_over]
    tr = T - n_prompt
    assert n_prompt % tr == 0
    nb = T // tr
    npb = n_prompt // tr
    cur = pl.BlockSpec((1, 1, tr), lambda i: (i, 0, 0), memory_space=pltpu.SMEM)
    nxt = pl.BlockSpec((1, 1, tr), lambda i: (jnp.minimum(i + 1, nb - 1), 0, 0),
                       memory_space=pltpu.SMEM)
    p1 = pos1.reshape(nb, 1, tr)
    p2 = pos2.reshape(nb, 1, tr)
    return pl.pallas_call(
        functools.partial(_combine_kernel, npb, ys_over is not None),
        grid_spec=pltpu.PrefetchScalarGridSpec(
            num_scalar_prefetch=0,
            grid=(nb,),
            in_specs=[cur, cur, nxt, nxt,
                      pl.BlockSpec((tr, D), lambda i: (i, 0)),
                      pl.BlockSpec((tr, 1), lambda i: (i, 0)),
                      pl.BlockSpec((tr, 1), lambda i: (i, 0)),
                      pl.BlockSpec((1, D), lambda i: (0, 0))]
                     + [pl.BlockSpec(memory_space=pl.ANY)] * len(srcs),
            out_specs=[pl.BlockSpec((tr, D), lambda i: (jnp.minimum(i, npb - 1), 0)),
                       pl.BlockSpec((tr, D), lambda i: (0, 0))],
            scratch_shapes=[pltpu.VMEM((2, 2, tr, D), F32), pltpu.SemaphoreType.DMA((2,))],
        ),
        out_shape=[jax.ShapeDtypeStruct((n_prompt, D), F32),
                   jax.ShapeDtypeStruct((tr, D), F32)],
        compiler_params=_cparams(1),
        name="moe_combine_final_norm",
    )(p1, p2, p1, p2, x, g1, g2, final_g.reshape(1, D), *srcs)


def _moe_plan(idx, T):
    M = T * TOP_K
    E = N_EXPERTS
    R = -(-int(M / E * MOE_CAPACITY) // MOE_SUB) * MOE_SUB
    n_over = -(-M // R)
    i32 = jnp.int32
    flat_e = idx.reshape(M)
    onehot = (flat_e[:, None] == jnp.arange(E, dtype=i32)[None, :]).astype(i32)
    csum = jnp.cumsum(onehot, axis=0)
    rank = jnp.sum((csum - onehot) * onehot, axis=1)
    counts = csum[-1]
    o_chunks = jnp.maximum((counts + R - 1) // R - 1, 0)
    o_first = jnp.cumsum(o_chunks) - o_chunks
    n_used = jnp.sum(o_chunks).astype(i32)
    over = rank - R
    slot = jnp.where(rank < R, flat_e * R + rank,
                     (E + o_first[flat_e] + over // R) * R + over % R)
    flat_tok = jnp.arange(M, dtype=i32) // TOP_K
    slot_tok = jnp.zeros(((E + n_over) * R,), i32).at[slot].set(
        flat_tok, unique_indices=True, mode="promise_in_bounds")
    spc = R // MOE_SUB

    def sub_used(c_rows):
        sub = jnp.arange(c_rows.shape[0] * spc, dtype=i32)
        return ((sub % spc) * MOE_SUB < c_rows[sub // spc]).astype(i32)

    m_rows = jnp.minimum(counts, R).astype(i32)
    eidx = jnp.arange(E, dtype=i32)
    main = dict(tok=slot_tok[:E * R], used=sub_used(m_rows),
                meta=(eidx, eidx, m_rows, jnp.full((1,), E, i32)))
    cidx = jnp.arange(n_over, dtype=i32)
    c_e = jnp.clip(jnp.searchsorted(jnp.cumsum(o_chunks), cidx, side='right'), 0, E - 1).astype(i32)
    c_rows = jnp.clip(counts[c_e] - R - (cidx - o_first[c_e]) * R, 0, R).astype(i32)
    used = cidx < n_used
    last = jnp.maximum(n_used - 1, 0)
    c_src = jnp.where(used, cidx, last).astype(i32)
    c_e = jnp.where(used, c_e, c_e[last]).astype(i32)
    c_rows = jnp.where(used, c_rows, 0).astype(i32)
    overflow = dict(tok=slot_tok[E * R:], used=sub_used(c_rows),
                    meta=(c_e, c_src, c_rows, n_used.reshape(1)))
    pos = slot.reshape(T, TOP_K).astype(i32)
    return R, main, overflow, n_used, pos[:, 0], pos[:, 1]


def kernel(x_prompt, x_sample, state_gla, state_conv, ev_norm_mix, ev_w_in, ev_sgu_norm, ev_sgu_w, ev_sgu_b, ev_gla_w_alpha, ev_gla_b_alpha, ev_gla_norm, ev_w_out, ev_norm_ffn, ev_ffn_w_gu, ev_ffn_w_down, od_norm_mix, od_conv_w_in, od_conv_w, od_conv_w_out, od_norm_ffn, od_router_w, od_moe_w_gu, od_moe_w_down, final_norm):
    nb, seq, D = x_prompt.shape
    B = x_sample.shape[0]
    assert x_sample.shape[1] == 1
    Tp = nb * seq
    T = Tp + B
    xp0 = x_prompt.reshape(Tp, D)
    xs0 = x_sample.reshape(B, D)

    sw = ev_sgu_norm.shape[1]
    kw = ev_gla_w_alpha.shape[2]
    rank = ev_gla_w_alpha.shape[1]
    vw = GLA_HEADS * ev_gla_norm.shape[1]
    n_main = 2 * sw + 2 * kw + 2 * vw
    assert ev_w_in.shape[2] == n_main + rank
    tm = _pick_tile(T, 2080)

    h = _norm_bf16_stacked(xp0, xs0, ev_norm_mix[0])
    P, log_a = _mix_in(h, ev_w_in, n_main, ev_gla_w_alpha[0], ev_gla_b_alpha[0], tm, 512)
    a_out, v_sample = _sgu(P, Tp, B, sw, ev_sgu_norm[0], ev_sgu_w[0], ev_sgu_b[0])
    col_q = 2 * sw
    o_p, gla_p = _gla_prompt(P, log_a, nb, seq, col_q, kw, vw, ev_gla_norm[0])
    Ps = P[Tp:]
    q_s = Ps[:, col_q:col_q + kw]
    k_s = Ps[:, col_q + kw:col_q + 2 * kw]
    v_s = Ps[:, col_q + 2 * kw:col_q + 2 * kw + vw]
    r_s = Ps[:, col_q + 2 * kw + vw:]
    gla_s, o_s = _gla_sample(state_gla[0], log_a[Tp:], k_s, q_s, v_s, r_s, ev_gla_norm[0])
    x1, h = _mix_out(a_out, o_p, o_s.reshape(B, vw).astype(BF16), xp0, xs0, ev_w_out[0],
                     ev_norm_ffn[0])

    hid = _swiglu_up(h, ev_ffn_w_gu[0], tm, 512)
    x2 = _matmul(hid, ev_ffn_w_down[0], D, _pick_tile(T, 1040), 256, res=x1, name="ffn_down")

    h = _norm_bf16(x2, od_norm_mix[0])
    bgate, z = _conv_in(h, od_conv_w_in[0], tm, 256)
    x3, h3_slab, idx, gates = _conv_out_route(
        z, bgate, od_conv_w[0], state_conv[0][:, 0], state_conv[0][:, 1],
        x2, od_conv_w_out[0], od_norm_ffn[0], od_router_w[0], Tp, seq)
    h3_slab = h3_slab.reshape(T, D // LANES, LANES)
    conv_p = jnp.stack([z[(n + 1) * seq - 2:(n + 1) * seq] for n in range(nb)])
    conv_s = jnp.concatenate([state_conv[0][:, 1:], z[Tp:].reshape(B, 1, D)], axis=1)

    R, main, overflow, n_overflow, pos1, pos2 = _moe_plan(idx[:, :TOP_K], T)
    FE = od_moe_w_down.shape[2]
    tf, tk = _pick_tile(FE, 512, LANES), _pick_tile(FE, 1024, LANES)

    def experts(part, direct):
        xs = _dispatch(h3_slab, part["tok"], part["used"])
        hid = _moe_up(xs, od_moe_w_gu[0], part["meta"], R, tf, direct)
        return _moe_down(hid, od_moe_w_down[0], part["meta"], R, 1024, tk, direct)

    ys = experts(main, True)
    g1, g2 = gates[:, 0:1], gates[:, 1:2]
    y_p, y_s = lax.cond(
        n_overflow > 0,
        lambda: _combine(x3, ys, experts(overflow, False), pos1, pos2, g1, g2, final_norm, Tp),
        lambda: _combine(x3, ys, None, pos1, pos2, g1, g2, final_norm, Tp))

    y_prompt = y_p.reshape(nb, seq, D)
    y_sample = y_s.reshape(B, 1, D)
    return (y_prompt, y_sample, gla_p[None], gla_s[None],
            v_sample.reshape(1, B, 1, sw), conv_p[None], conv_s[None])
```

```python
import functools
import math

import numpy as np
import jax
import jax.numpy as jnp
from jax import lax
from jax.experimental import pallas as pl
from jax.experimental.pallas import tpu as pltpu

F32 = jnp.float32
BF16 = jnp.bfloat16
EPS = 1e-6

SGU_CHUNK = 128
SGU_GROUPS = 4
GLA_HEADS = 4
GLA_TAU = 16.0
GLA_CHUNK = 128
N_EXPERTS = 8
TOP_K = 2
MOE_SUB = 256
MOE_CAPACITY = 1.35
MOE_BIG = 1024
MOE_SMALL = 128
LANES = 128
SLAB_PITCH_PAD = 4
VMEM_LIMIT = 52 * 1024 * 1024
VMEM_LIMIT_MAX = 60 * 1024 * 1024


def _cparams(n_axes, vmem=VMEM_LIMIT):
    return pltpu.CompilerParams(dimension_semantics=("arbitrary",) * n_axes,
                                vmem_limit_bytes=vmem)


def _pick_tile(n, cap, mult=16):
    best = None
    for t in range(mult, min(n, cap) + 1, mult):
        if n % t == 0:
            best = t
    assert best is not None, (n, cap, mult)
    return best


def _rms(x, g):
    ms = jnp.mean(x * x, axis=-1, keepdims=True)
    return (x * lax.rsqrt(ms + EPS)) * g


def _gelu(x):
    c = math.sqrt(2.0 / math.pi)
    return x * (0.5 * (1.0 + jnp.tanh(c * (x + 0.044715 * (x * x * x)))))


def _silu(x):
    return x * (1.0 / (1.0 + jnp.exp(-x)))


def _dot(a, b):
    return jnp.dot(a, b, preferred_element_type=F32)


def _dot_nt(a, b):
    return lax.dot_general(a, b, (((1,), (1,)), ((), ())), preferred_element_type=F32)


def _norm_kernel(x_ref, g_ref, o_ref):
    o_ref[...] = _rms(x_ref[...], g_ref[...]).astype(BF16)


def _norm_bf16(x, g):
    T, D = x.shape
    tr = _pick_tile(T, 512)
    return pl.pallas_call(
        _norm_kernel,
        grid=(T // tr,),
        in_specs=[pl.BlockSpec((tr, D), lambda i: (i, 0)),
                  pl.BlockSpec((1, D), lambda i: (0, 0))],
        out_specs=pl.BlockSpec((tr, D), lambda i: (i, 0)),
        out_shape=jax.ShapeDtypeStruct((T, D), BF16),
        compiler_params=_cparams(1),
        name="rmsnorm_bf16",
    )(x, g.reshape(1, D))


def _norm2_kernel(n_prompt_blocks, xp_ref, xs_ref, g_ref, o_ref):
    i = pl.program_id(0)

    @pl.when(i < n_prompt_blocks)
    def _():
        o_ref[...] = _rms(xp_ref[...], g_ref[...]).astype(BF16)

    @pl.when(i >= n_prompt_blocks)
    def _():
        o_ref[0:xs_ref.shape[0], :] = _rms(xs_ref[...], g_ref[...]).astype(BF16)


def _norm_bf16_stacked(xp, xs, g):
    Tp, D = xp.shape
    B = xs.shape[0]
    tr = _pick_tile(Tp, 512)
    assert B <= tr
    npb = Tp // tr
    return pl.pallas_call(
        functools.partial(_norm2_kernel, npb),
        grid=(npb + 1,),
        in_specs=[pl.BlockSpec((tr, D), lambda i: (jnp.minimum(i, npb - 1), 0)),
                  pl.BlockSpec((B, D), lambda i: (0, 0)),
                  pl.BlockSpec((1, D), lambda i: (0, 0))],
        out_specs=pl.BlockSpec((tr, D), lambda i: (i, 0)),
        out_shape=jax.ShapeDtypeStruct((Tp + B, D), BF16),
        compiler_params=_cparams(1),
        name="rmsnorm_stack_bf16",
    )(xp, xs, g.reshape(1, D))


def _mm_res_kernel(x_ref, w_ref, r_ref, o_ref):
    o_ref[...] = r_ref[...] + _dot(x_ref[...], w_ref[...].astype(BF16))


def _matmul_res(x, w, res, tm, tn, name):
    T, K = x.shape
    N = w.shape[1]
    return pl.pallas_call(
        _mm_res_kernel,
        grid=(T // tm, N // tn),
        in_specs=[pl.BlockSpec((tm, K), lambda i, j: (i, 0)),
                  pl.BlockSpec((K, tn), lambda i, j: (0, j)),
                  pl.BlockSpec((tm, tn), lambda i, j: (i, j))],
        out_specs=pl.BlockSpec((tm, tn), lambda i, j: (i, j)),
        out_shape=jax.ShapeDtypeStruct((T, N), F32),
        compiler_params=_cparams(2),
        name=name,
    )(x, w, res)


def _mix_out_kernel(n_full, n_tail, a_ref, op_ref, os_ref, rp_ref, rs_ref, w_ref, ng_ref,
                    x_ref, h_ref):
    i = pl.program_id(0)
    sw = a_ref.shape[1]

    def finish(rows, a, o, res):
        y = res + _dot(a, w_ref[0:sw, :]) + _dot(o, w_ref[sw:, :])
        x_ref[0:rows, :] = y
        h_ref[0:rows, :] = _rms(y, ng_ref[...]).astype(BF16)

    @pl.when(i < n_full)
    def _():
        finish(a_ref.shape[0], a_ref[...], op_ref[...], rp_ref[...])

    @pl.when(i >= n_full)
    def _():
        finish(n_tail, a_ref[0:n_tail, :], os_ref[...], rs_ref[...])


def _mix_out(a, o_p, o_s, res_p, res_s, w, norm_g):
    T, sw = a.shape
    Tp, vw = o_p.shape
    B = o_s.shape[0]
    D = w.shape[1]
    assert T == Tp + B
    tm = _pick_tile(Tp, 512)
    assert B <= tm
    n_full = Tp // tm
    clamp = lambda i: (jnp.minimum(i, n_full - 1), 0)
    row = lambda i: (i, 0)
    fixed = lambda i: (0, 0)
    return pl.pallas_call(
        functools.partial(_mix_out_kernel, n_full, B),
        grid=(n_full + 1,),
        in_specs=[pl.BlockSpec((tm, sw), row),
                  pl.BlockSpec((tm, vw), clamp),
                  pl.BlockSpec((B, vw), fixed),
                  pl.BlockSpec((tm, D), clamp),
                  pl.BlockSpec((B, D), fixed),
                  pl.BlockSpec((sw + vw, D), fixed),
                  pl.BlockSpec((1, D), fixed)],
        out_specs=[pl.BlockSpec((tm, D), row),
                   pl.BlockSpec((tm, D), row)],
        out_shape=[jax.ShapeDtypeStruct((T, D), F32),
                   jax.ShapeDtypeStruct((T, D), BF16)],
        compiler_params=_cparams(1),
        name="mix_out_proj",
    )(a, o_p, o_s, res_p, res_s, w.astype(BF16), norm_g.reshape(1, D))


def _swiglu_kernel(x_ref, wg_ref, wu_ref, o_ref):
    x = x_ref[...]
    g = _dot(x, wg_ref[...].astype(BF16))
    u = _dot(x, wu_ref[...].astype(BF16))
    o_ref[...] = (_silu(g) * u).astype(o_ref.dtype)


def _swiglu_up(x, w_gu, tm, tn):
    T, K = x.shape
    F = w_gu.shape[1] // 2
    nj = F // tn
    return pl.pallas_call(
        _swiglu_kernel,
        grid=(T // tm, nj),
        in_specs=[pl.BlockSpec((tm, K), lambda i, j: (i, 0)),
                  pl.BlockSpec((K, tn), lambda i, j: (0, j)),
                  pl.BlockSpec((K, tn), lambda i, j: (0, j + nj))],
        out_specs=pl.BlockSpec((tm, tn), lambda i, j: (i, j)),
        out_shape=jax.ShapeDtypeStruct((T, F), BF16),
        compiler_params=_cparams(2, vmem=VMEM_LIMIT_MAX),
        name="ffn_gate_up",
    )(x, w_gu, w_gu)


def _conv_in_kernel(x_ref, wb_ref, wc_ref, wu_ref, b_ref, z_ref):
    x = x_ref[...]
    b_ref[...] = _dot(x, wb_ref[...].astype(BF16))
    c = _dot(x, wc_ref[...].astype(BF16))
    u = _dot(x, wu_ref[...].astype(BF16))
    z_ref[...] = c * u


def _conv_in(x, w, tm, tn):
    T, K = x.shape
    C = w.shape[1] // 3
    nj = C // tn
    return pl.pallas_call(
        _conv_in_kernel,
        grid=(T // tm, nj),
        in_specs=[pl.BlockSpec((tm, K), lambda i, j: (i, 0)),
                  pl.BlockSpec((K, tn), lambda i, j: (0, j)),
                  pl.BlockSpec((K, tn), lambda i, j: (0, j + nj)),
                  pl.BlockSpec((K, tn), lambda i, j: (0, j + 2 * nj))],
        out_specs=[pl.BlockSpec((tm, tn), lambda i, j: (i, j)),
                   pl.BlockSpec((tm, tn), lambda i, j: (i, j))],
        out_shape=[jax.ShapeDtypeStruct((T, C), F32),
                   jax.ShapeDtypeStruct((T, C), F32)],
        compiler_params=_cparams(2),
        name="conv_in_proj",
    )(x, w, w, w)


def _mix_in_kernel(x_ref, w_ref, wl_ref, wa_ref, b_ref, p_ref, la_ref):
    x = x_ref[...]
    p_ref[...] = _dot(x, w_ref[...].astype(BF16))

    @pl.when(pl.program_id(1) == pl.num_programs(1) - 1)
    def _():
        rank = wa_ref.shape[0]
        g = _dot(x, wl_ref[:, 0:rank].astype(BF16))
        z = _dot(g.astype(BF16), wa_ref[...].astype(BF16)) + b_ref[...]
        ls = jnp.minimum(z, 0.0) - jnp.log1p(jnp.exp(-jnp.abs(z)))
        la_ref[...] = ls * (1.0 / GLA_TAU)


def _mix_in(x, w_in, n_main, w_alpha, b_alpha, tm, tn):
    T, K = x.shape
    R, KW = w_alpha.shape
    assert n_main % LANES == 0 and R <= LANES
    return pl.pallas_call(
        _mix_in_kernel,
        grid=(T // tm, n_main // tn),
        in_specs=[pl.BlockSpec((tm, K), lambda i, j: (i, 0)),
                  pl.BlockSpec((None, K, tn), lambda i, j: (0, 0, j)),
                  pl.BlockSpec((None, K, LANES), lambda i, j: (0, 0, n_main // LANES)),
                  pl.BlockSpec((R, KW), lambda i, j: (0, 0)),
                  pl.BlockSpec((1, KW), lambda i, j: (0, 0))],
        out_specs=[pl.BlockSpec((tm, tn), lambda i, j: (i, j)),
                   pl.BlockSpec((tm, KW), lambda i, j: (i, 0))],
        out_shape=[jax.ShapeDtypeStruct((T, n_main), F32),
                   jax.ShapeDtypeStruct((T, KW), F32)],
        compiler_params=_cparams(2, vmem=VMEM_LIMIT_MAX),
        name="mix_in_proj",
    )(x, w_in, w_in, w_alpha, b_alpha.reshape(1, KW))


def _sgu_kernel(n_prompt_tiles, u_ref, v_ref, ng_ref, w_ref, bt_ref, o_ref, vs_ref):
    i = pl.program_id(0)
    C = SGU_CHUNK
    gd = u_ref.shape[1] // SGU_GROUPS
    row = lax.broadcasted_iota(jnp.int32, (C, C), 0)
    col = lax.broadcasted_iota(jnp.int32, (C, C), 1)

    def chunk(r0, weights, biases, keep_v):
        gu = _gelu(u_ref[r0:r0 + C, :])
        vn = _rms(_gelu(v_ref[r0:r0 + C, :]), ng_ref[...])
        if keep_v:
            vs_ref[...] = vn
        vb = vn.astype(BF16)
        for g in range(SGU_GROUPS):
            cols = slice(g * gd, (g + 1) * gd)
            mixed = _dot(weights[g], vb[:, cols]) + biases[g]
            o_ref[r0:r0 + C, cols] = (gu[:, cols] * mixed).astype(o_ref.dtype)

    @pl.when(i < n_prompt_tiles)
    def _():
        weights = [jnp.where(col <= row, w_ref[g], 0.0).astype(BF16) for g in range(SGU_GROUPS)]
        biases = [bt_ref[:, g:g + 1] for g in range(SGU_GROUPS)]
        for r0 in range(0, u_ref.shape[0], C):
            chunk(r0, weights, biases, False)

    @pl.when(i >= n_prompt_tiles)
    def _():
        weights = [jnp.where(col == row, w_ref[g, 0:1, 0:1], 0.0).astype(BF16)
                   for g in range(SGU_GROUPS)]
        biases = [bt_ref[0:1, g:g + 1] for g in range(SGU_GROUPS)]
        chunk(0, weights, biases, True)


def _sgu(P, n_prompt, n_sample, width, sgu_norm, sgu_w, sgu_b):
    C = SGU_CHUNK
    assert n_prompt % C == 0 and n_sample == C
    T = n_prompt + n_sample
    tr = C * math.gcd(n_prompt // C, 4)
    npt = n_prompt // tr
    return pl.pallas_call(
        functools.partial(_sgu_kernel, npt),
        grid=(npt + 1,),
        in_specs=[pl.BlockSpec((tr, width), lambda i: (i, 0)),
                  pl.BlockSpec((tr, width), lambda i: (i, 1)),
                  pl.BlockSpec((1, width), lambda i: (0, 0)),
                  pl.BlockSpec((SGU_GROUPS, C, C), lambda i: (0, 0, 0)),
                  pl.BlockSpec((C, SGU_GROUPS), lambda i: (0, 0))],
        out_specs=[pl.BlockSpec((tr, width), lambda i: (i, 0)),
                   pl.BlockSpec((C, width), lambda i: (0, 0))],
        out_shape=[jax.ShapeDtypeStruct((T, width), BF16),
                   jax.ShapeDtypeStruct((n_sample, width), F32)],
        compiler_params=_cparams(1),
        name="sgu_mix",
    )(P, P, sgu_norm.reshape(1, width), sgu_w, sgu_b.T)


def _gla_constants():
    C = GLA_CHUNK
    s = np.arange(C)[None, :]
    i = np.arange(C)[:, None]
    gs = [s <= i, s > i]
    masks = []
    b = C // 2
    while b >= 1:
        upper = (i % (2 * b)) >= b
        m = (i // (2 * b)) * (2 * b) + b - 1
        gs.append(np.where(upper, (s > m) & (s <= i), (s > i) & (s <= m)))
        same = (i // (2 * b)) == (s // (2 * b))
        masks.append(upper & same & ~upper.T)
        b //= 2
    masks.append(np.eye(C, dtype=bool))
    G = np.concatenate(gs, axis=0).astype(np.float32)
    M = np.stack(masks).astype(np.float32)
    return jnp.asarray(G, BF16), jnp.asarray(M, F32)


def _gla_kernel(nseq, *refs):
    ins = [refs[5 * s:5 * s + 5] for s in range(nseq)]
    G_ref, M_ref, gn_ref, o_ref, s_out_ref, st_ref = refs[5 * nseq:]
    c = pl.program_id(1)
    C = GLA_CHUNK
    hk = ins[0][0].shape[1]
    dk = hk // GLA_HEADS
    dv = ins[0][2].shape[1] // GLA_HEADS
    nlev = M_ref.shape[0] - 1

    @pl.when(c == 0)
    def _():
        st_ref[...] = jnp.zeros_like(st_ref)

    parts = []
    for s in range(nseq):
        la = ins[s][4][...]
        hi = la.astype(BF16)
        r1 = la - hi.astype(F32)
        mid = r1.astype(BF16)
        lo = (r1 - mid.astype(F32)).astype(BF16)
        parts.append(jnp.concatenate([hi, mid, lo], axis=0))
    X = jnp.exp(_dot(G_ref[...], jnp.concatenate(parts, axis=1)))

    scale = dk ** -0.5
    for s in range(nseq):
        q_ref, k_ref, v_ref, r_ref, _ = ins[s]
        for h in range(GLA_HEADS):
            Xh = X[:, s * hk + h * dk:s * hk + (h + 1) * dk]
            qh = q_ref[:, h * dk:(h + 1) * dk] * scale
            kh = k_ref[:, h * dk:(h + 1) * dk]
            vh = v_ref[:, h * dv:(h + 1) * dv]
            x_cum = Xh[0:C]
            x_rest = Xh[C:2 * C]
            st_old = st_ref[s, h]
            o = _dot_nt((qh * x_cum).astype(BF16), st_old.astype(BF16))
            att = _dot_nt(qh.astype(BF16), kh.astype(BF16)) * M_ref[nlev]
            for l in range(nlev):
                xl = Xh[(2 + l) * C:(3 + l) * C]
                att = att + _dot_nt((qh * xl).astype(BF16), (kh * xl).astype(BF16)) * M_ref[l]
            o = o + _dot(att.astype(BF16), vh.astype(BF16))
            vt = vh.T.astype(BF16)
            st_ref[s, h] = st_old * x_cum[C - 1:C, :] + _dot(vt, (kh * x_rest).astype(BF16))
            on = _rms(o, gn_ref[...]) * _silu(r_ref[:, h * dv:(h + 1) * dv])
            o_ref[s, :, h * dv:(h + 1) * dv] = on.astype(o_ref.dtype)

    @pl.when(c == pl.num_programs(1) - 1)
    def _():
        for s in range(nseq):
            for h in range(GLA_HEADS):
                s_out_ref[s, h] = st_ref[s, h].T


def _gla_prompt(P, log_a, n_seq, seq, col_q, kw, vw, gla_norm):
    C = GLA_CHUNK
    assert seq % C == 0
    nc = seq // C
    dk = kw // GLA_HEADS
    dv = vw // GLA_HEADS
    qb, kb = col_q // kw, col_q // kw + 1
    vb = (col_q + 2 * kw) // vw
    rb = vb + 1
    assert col_q % kw == 0 and (col_q + 2 * kw) % vw == 0
    nseq = math.gcd(n_seq, 4)
    G, M = _gla_constants()
    G3 = jnp.concatenate([G, G, G], axis=1)

    def seq_specs(s):
        rows = lambda n, c: (n * nseq + s) * nc + c
        return [pl.BlockSpec((C, kw), lambda n, c: (rows(n, c), qb)),
                pl.BlockSpec((C, kw), lambda n, c: (rows(n, c), kb)),
                pl.BlockSpec((C, vw), lambda n, c: (rows(n, c), vb)),
                pl.BlockSpec((C, vw), lambda n, c: (rows(n, c), rb)),
                pl.BlockSpec((C, kw), lambda n, c: (rows(n, c), 0))]

    in_specs = [sp for s in range(nseq) for sp in seq_specs(s)]
    in_specs += [pl.BlockSpec(G3.shape, lambda n, c: (0, 0)),
                 pl.BlockSpec(M.shape, lambda n, c: (0, 0, 0)),
                 pl.BlockSpec((1, dv), lambda n, c: (0, 0))]
    o, states = pl.pallas_call(
        functools.partial(_gla_kernel, nseq),
        grid=(n_seq // nseq, nc),
        in_specs=in_specs,
        out_specs=[pl.BlockSpec((None, nseq, C, vw), lambda n, c: (n, 0, c, 0)),
                   pl.BlockSpec((nseq, GLA_HEADS, dk, dv), lambda n, c: (n, 0, 0, 0))],
        out_shape=[jax.ShapeDtypeStruct((n_seq // nseq, nseq, seq, vw), BF16),
                   jax.ShapeDtypeStruct((n_seq, GLA_HEADS, dk, dv), F32)],
        scratch_shapes=[pltpu.VMEM((nseq, GLA_HEADS, dv, dk), F32)],
        compiler_params=_cparams(2),
        name="gla_prompt",
    )(*([P, P, P, P, log_a] * nseq), G3, M, gla_norm.reshape(1, dv))
    return o.reshape(n_seq * seq, vw), states


def _gla_step_kernel(s_ref, at_ref, kt_ref, qt_ref, v_ref, r_ref, gn_ref, sn_ref, o_ref):
    i = pl.program_id(0)
    bb, H, dk, dv = s_ref.shape
    B = at_ref.shape[1]
    lane = lax.broadcasted_iota(jnp.int32, (dk, B), 1)
    scale = dk ** -0.5
    for j in range(bb):
        pick = lane == i * bb + j

        def column(t_ref, h):
            return jnp.sum(jnp.where(pick, t_ref[h * dk:(h + 1) * dk, :], 0.0), axis=1, keepdims=True)

        for h in range(H):
            s_new = jnp.exp(column(at_ref, h)) * s_ref[j, h] + column(kt_ref, h) * v_ref[j, h]
            sn_ref[j, h] = s_new
            o = jnp.sum((column(qt_ref, h) * scale) * s_new, axis=0, keepdims=True)
            o_ref[j, h] = _rms(o, gn_ref[...]) * _silu(r_ref[j, h])


def _gla_sample(state, la, k, q, v, r, gla_norm):
    B, H, dk, dv = state.shape
    bb = 8
    rowv = lambda t: t.reshape(B, H, 1, dv)
    tspec = pl.BlockSpec((H * dk, B), lambda i: (0, 0))
    rspec = pl.BlockSpec((bb, H, 1, dv), lambda i: (i, 0, 0, 0))
    sspec = pl.BlockSpec((bb, H, dk, dv), lambda i: (i, 0, 0, 0))
    return pl.pallas_call(
        _gla_step_kernel,
        grid=(B // bb,),
        in_specs=[sspec, tspec, tspec, tspec, rspec, rspec,
                  pl.BlockSpec((1, dv), lambda i: (0, 0))],
        out_specs=[sspec, rspec],
        out_shape=[jax.ShapeDtypeStruct((B, H, dk, dv), F32),
                   jax.ShapeDtypeStruct((B, H, 1, dv), F32)],
        compiler_params=_cparams(1),
        name="gla_sample_step",
    )(state, la.T, k.T, q.T, rowv(v), rowv(r), gla_norm.reshape(1, dv))


def _route(logits):
    lane = lax.broadcasted_iota(jnp.int32, logits.shape, 1)
    neg = jnp.float32(-jnp.inf)
    logits = jnp.where(lane < N_EXPERTS, logits, neg)
    v1 = jnp.max(logits, axis=-1, keepdims=True)
    i1 = jnp.min(jnp.where(logits == v1, lane, LANES), axis=-1, keepdims=True)
    rest = jnp.where(lane == i1, neg, logits)
    v2 = jnp.max(rest, axis=-1, keepdims=True)
    i2 = jnp.min(jnp.where(rest == v2, lane, LANES), axis=-1, keepdims=True)
    e2 = jnp.exp(v2 - v1)
    g1 = 1.0 / (1.0 + e2)
    g2 = e2 / (1.0 + e2)
    idx = jnp.where(lane == 0, i1, jnp.where(lane == 1, i2, 0))
    gate = jnp.where(lane == 0, g1, jnp.where(lane == 1, g2, 0.0))
    return idx, gate


def _conv_out_kernel(n_prompt_blocks, blocks_per_seq, n_tail, z_ref, halo_ref, b_ref, cw_ref,
                     h0_ref, h1_ref, res_ref, w_ref, ng_ref, rw_ref, x_ref, xslab_ref, idx_ref,
                     gate_ref):
    i = pl.program_id(0)
    w0, w1, w2 = cw_ref[0:1, :], cw_ref[1:2, :], cw_ref[2:3, :]
    ns = x_ref.shape[1] // LANES

    def finish(rows, yc, res):
        y = res + _dot(yc.astype(BF16), w_ref[...])
        x_ref[0:rows, :] = y
        hn = _rms(y, ng_ref[...])
        for s in range(ns):
            xslab_ref[pl.ds(s, rows, stride=ns), :] = hn[:, s * LANES:(s + 1) * LANES]
        h_hi = hn.astype(BF16)
        h_lo = (hn - h_hi.astype(F32)).astype(BF16)
        rw = rw_ref[...]
        r_hi = rw.astype(BF16)
        r_lo = (rw - r_hi.astype(F32)).astype(BF16)
        logits = _dot(h_hi, r_hi) + (_dot(h_hi, r_lo) + _dot(h_lo, r_hi))
        idx, gate = _route(logits)
        idx_ref[0:rows, :] = idx
        gate_ref[0:rows, :] = gate

    @pl.when(i < n_prompt_blocks)
    def _():
        z = z_ref[...]
        row = lax.broadcasted_iota(jnp.int32, z.shape, 0)
        halo = jnp.where(i % blocks_per_seq == 0, 0.0, halo_ref[...])
        z1 = jnp.where(row == 0, halo[7:8, :], pltpu.roll(z, 1, 0))
        z2 = pltpu.roll(z, 2, 0)
        z2 = jnp.where(row == 0, halo[6:7, :], jnp.where(row == 1, halo[7:8, :], z2))
        finish(z.shape[0], b_ref[...] * (w0 * z2 + w1 * z1 + w2 * z), res_ref[...])

    @pl.when(i >= n_prompt_blocks)
    def _():
        conv = w0 * h0_ref[...] + w1 * h1_ref[...] + w2 * z_ref[0:n_tail, :]
        finish(n_tail, b_ref[0:n_tail, :] * conv, res_ref[0:n_tail, :])


def _conv_out_route(z, bgate, conv_w, hist0, hist1, res, w_out, norm_g, router_w, n_prompt, seq):
    T, Cd = z.shape
    D = w_out.shape[1]
    B = T - n_prompt
    tm = _pick_tile(math.gcd(seq, n_prompt), 256)
    assert B <= tm and hist0.shape == (B, Cd)
    npb = n_prompt // tm
    wpad = jnp.zeros((D, LANES), F32).at[:, :N_EXPERTS].set(router_w)
    row = lambda i: (i, 0)
    fixed = lambda i: (0, 0)
    return pl.pallas_call(
        functools.partial(_conv_out_kernel, npb, seq // tm, B),
        grid=(npb + 1,),
        in_specs=[pl.BlockSpec((tm, Cd), row),
                  pl.BlockSpec((8, Cd), lambda i: (jnp.maximum(i * (tm // 8) - 1, 0), 0)),
                  pl.BlockSpec((tm, Cd), row),
                  pl.BlockSpec(conv_w.shape, fixed),
                  pl.BlockSpec((B, Cd), fixed),
                  pl.BlockSpec((B, Cd), fixed),
                  pl.BlockSpec((tm, D), row),
                  pl.BlockSpec((Cd, D), fixed),
                  pl.BlockSpec((1, D), fixed),
                  pl.BlockSpec((D, LANES), fixed)],
        out_specs=[pl.BlockSpec((tm, D), row),
                   pl.BlockSpec((tm * (D // LANES), LANES), row),
                   pl.BlockSpec((tm, LANES), row),
                   pl.BlockSpec((tm, LANES), row)],
        out_shape=[jax.ShapeDtypeStruct((T, D), F32),
                   jax.ShapeDtypeStruct((T * (D // LANES), LANES), F32),
                   jax.ShapeDtypeStruct((T, LANES), jnp.int32),
                   jax.ShapeDtypeStruct((T, LANES), F32)],
        compiler_params=_cparams(1),
        name="conv_out_route",
    )(z, z, bgate, conv_w, hist0, hist1, res, w_out.astype(BF16), norm_g.reshape(1, D), wpad)


def _dispatch_kernel(tok_ref, nxt_ref, used_ref, x_hbm, o_ref, buf_ref, sem):
    i = pl.program_id(0)
    nsteps = pl.num_programs(0)
    n = o_ref.shape[0]
    ns = x_hbm.shape[1]
    pitch = buf_ref.shape[1] // n
    slot = i % 2

    def row_copy(t_ref, r, s):
        return pltpu.make_async_copy(x_hbm.at[t_ref[0, 0, r]],
                                     buf_ref.at[s, pl.ds(r * pitch, ns)], sem.at[s])

    def start_rows(t_ref, s):
        def body(r, carry):
            row_copy(t_ref, r, s).start()
            return carry

        lax.fori_loop(0, n, body, 0, unroll=8)

    @pl.when(jnp.logical_and(i == 0, used_ref[0] > 0))
    def _():
        start_rows(tok_ref, 0)

    @pl.when(jnp.logical_and(i + 1 < nsteps, used_ref[jnp.minimum(i + 1, nsteps - 1)] > 0))
    def _():
        start_rows(nxt_ref, 1 - slot)

    @pl.when(used_ref[i] > 0)
    def _():
        landed = buf_ref.at[slot, pl.ds(0, n * ns)]
        pltpu.make_async_copy(landed, landed, sem.at[slot]).wait()

        for s in range(ns):
            o_ref[:, s * LANES:(s + 1) * LANES] = (
                buf_ref[slot, pl.ds(s, n, stride=pitch), :].astype(o_ref.dtype))

    @pl.when(used_ref[i] == 0)
    def _():
        o_ref[...] = jnp.zeros_like(o_ref)


def _dispatch(x_slab, slot_tok, sub_used):
    D = x_slab.shape[1] * x_slab.shape[2]
    nsub = sub_used.shape[0]
    SB = MOE_SUB
    tok3 = slot_tok.reshape(nsub, 1, SB)
    return pl.pallas_call(
        _dispatch_kernel,
        grid_spec=pltpu.PrefetchScalarGridSpec(
            num_scalar_prefetch=0,
            grid=(nsub,),
            in_specs=[pl.BlockSpec((1, 1, SB), lambda i: (i, 0, 0), memory_space=pltpu.SMEM),
                      pl.BlockSpec((1, 1, SB), lambda i: (jnp.minimum(i + 1, nsub - 1), 0, 0),
                                   memory_space=pltpu.SMEM),
                      pl.BlockSpec(memory_space=pltpu.SMEM),
                      pl.BlockSpec(memory_space=pl.ANY)],
            out_specs=pl.BlockSpec((SB, D), lambda i: (i, 0)),
            scratch_shapes=[pltpu.VMEM((2, SB * (D // LANES + SLAB_PITCH_PAD), LANES), F32),
                            pltpu.SemaphoreType.DMA((2,))],
        ),
        out_shape=jax.ShapeDtypeStruct((nsub * SB, D), BF16),
        compiler_params=_cparams(1),
        name="moe_dispatch",
    )(tok3, tok3, sub_used, x_slab)


def _row_pieces(rows, total, small, compute, skip):
    for s in range(0, total, MOE_BIG):
        size = min(MOE_BIG, total - s)
        assert size % small == 0
        whole = rows > s + size - small

        @pl.when(whole)
        def _():
            compute(s, size)

        @pl.when(jnp.logical_not(whole))
        def _():
            for t in range(s, s + size, small):
                @pl.when(rows > t)
                def _():
                    compute(t, small)

                @pl.when(rows <= t)
                def _():
                    skip(t, small)


def _moe_up_kernel(ce_ref, cs_ref, cr_ref, nu_ref, x_ref, wg_ref, wu_ref, o_ref, wbf_ref):
    c = pl.program_id(0)

    @pl.when(c < nu_ref[0])
    def _():
        def compute(s, n):
            if s == 0:
                wg = wg_ref[0].astype(BF16)
                wu = wu_ref[0].astype(BF16)
                wbf_ref[0] = wg
                wbf_ref[1] = wu
            else:
                wg = wbf_ref[0]
                wu = wbf_ref[1]
            x = x_ref[pl.ds(s, n), :]
            o_ref[pl.ds(s, n), :] = (_silu(_dot(x, wg)) * _dot(x, wu)).astype(o_ref.dtype)

        def skip(s, n):
            o_ref[pl.ds(s, n), :] = jnp.zeros((n, o_ref.shape[1]), o_ref.dtype)

        _row_pieces(cr_ref[c], x_ref.shape[0], MOE_SMALL, compute, skip)

    @pl.when(c >= nu_ref[0])
    def _():
        o_ref[...] = jnp.zeros_like(o_ref)


def _moe_up(xs, w_gu, meta, R, tf, direct):
    ce, cs, cr, nu = meta
    nch = ce.shape[0]
    D = xs.shape[1]
    F = w_gu.shape[2] // 2
    nf = F // tf

    def fidx(c, f, nu):
        return jnp.where(c < nu[0], f, nf - 1)

    if direct:
        in_specs = [pl.BlockSpec((R, D), lambda c, f, *_: (c, 0)),
                    pl.BlockSpec((1, D, tf), lambda c, f, *_: (c, 0, f)),
                    pl.BlockSpec((1, D, tf), lambda c, f, *_: (c, 0, nf + f))]
    else:
        in_specs = [pl.BlockSpec((R, D), lambda c, f, ce, cs, cr, nu: (cs[c], 0),
                                 pipeline_mode=pl.Buffered(1)),
                    pl.BlockSpec((1, D, tf), lambda c, f, ce, cs, cr, nu: (ce[c], 0, fidx(c, f, nu))),
                    pl.BlockSpec((1, D, tf), lambda c, f, ce, cs, cr, nu: (ce[c], 0, nf + fidx(c, f, nu)))]
    return pl.pallas_call(
        _moe_up_kernel,
        grid_spec=pltpu.PrefetchScalarGridSpec(
            num_scalar_prefetch=4,
            grid=(nch, nf),
            in_specs=in_specs,
            out_specs=pl.BlockSpec((R, tf), lambda c, f, ce, cs, cr, nu: (c, f)),
            scratch_shapes=[pltpu.VMEM((2, D, tf), BF16)],
        ),
        out_shape=jax.ShapeDtypeStruct((nch * R, F), BF16),
        compiler_params=_cparams(2, vmem=VMEM_LIMIT_MAX if direct else VMEM_LIMIT),
        name="moe_gate_up",
    )(ce, cs, cr, nu, xs, w_gu, w_gu)


def _moe_down_kernel(ce_ref, cs_ref, cr_ref, nu_ref, h_ref, w_ref, o_ref, wbf_ref):
    c = pl.program_id(0)
    k = pl.program_id(2)

    @pl.when(c < nu_ref[0])
    def _():
        def weight(s):
            if s == 0:
                w = w_ref[0].astype(BF16)
                wbf_ref[...] = w
                return w
            return wbf_ref[...]

        def compute(s, n):
            @pl.when(k == 0)
            def _():
                o_ref[pl.ds(s, n), :] = _dot(h_ref[pl.ds(s, n), :], weight(s))

            @pl.when(k > 0)
            def _():
                o_ref[pl.ds(s, n), :] = (o_ref[pl.ds(s, n), :]
                                         + _dot(h_ref[pl.ds(s, n), :], weight(s)))

        def skip(s, n):
            @pl.when(k == 0)
            def _():
                o_ref[pl.ds(s, n), :] = jnp.zeros((n, o_ref.shape[1]), F32)

        _row_pieces(cr_ref[c], h_ref.shape[0], MOE_SUB, compute, skip)

    @pl.when(jnp.logical_and(c >= nu_ref[0], k == 0))
    def _():
        o_ref[...] = jnp.zeros_like(o_ref)


def _moe_down(hid, w_down, meta, R, tn, tk, direct):
    ce, cs, cr, nu = meta
    nch = ce.shape[0]
    F = hid.shape[1]
    D = w_down.shape[2]
    nn, nk = D // tn, F // tk

    def last_if_unused(c, v, nu, last):
        return jnp.where(c < nu[0], v, last)

    if direct:
        in_specs = [pl.BlockSpec((R, tk), lambda c, n, k, *_: (c, k)),
                    pl.BlockSpec((1, tk, tn), lambda c, n, k, *_: (c, k, n))]
    else:
        in_specs = [pl.BlockSpec((R, tk), lambda c, n, k, ce, cs, cr, nu:
                                 (cs[c], last_if_unused(c, k, nu, nk - 1))),
                    pl.BlockSpec((1, tk, tn), lambda c, n, k, ce, cs, cr, nu:
                                 (ce[c], last_if_unused(c, k, nu, nk - 1),
                                  last_if_unused(c, n, nu, nn - 1)))]
    return pl.pallas_call(
        _moe_down_kernel,
        grid_spec=pltpu.PrefetchScalarGridSpec(
            num_scalar_prefetch=4,
            grid=(nch, nn, nk),
            in_specs=in_specs,
            out_specs=pl.BlockSpec((R, tn), lambda c, n, k, ce, cs, cr, nu: (c, n)),
            scratch_shapes=[pltpu.VMEM((tk, tn), BF16)],
        ),
        out_shape=jax.ShapeDtypeStruct((nch * R, D), F32),
        compiler_params=_cparams(3),
        name="moe_down",
    )(ce, cs, cr, nu, hid, w_down)


def _combine_kernel(n_prompt_blocks, has_overflow, p1_ref, p2_ref, n1_ref, n2_ref, x_ref, g1_ref,
                    g2_ref, fg_ref, ys_hbm, *rest):
    yo_hbm = rest[0] if has_overflow else None
    op_ref, os_ref, buf_ref, sem = rest[-4:]
    i = pl.program_id(0)
    nsteps = pl.num_programs(0)
    n = x_ref.shape[0]
    slot = i % 2
    split = ys_hbm.shape[0]

    def start_row(pos, k, r, s):
        dst = buf_ref.at[s, k, pl.ds(r, 1)]
        if yo_hbm is None:
            pltpu.make_async_copy(ys_hbm.at[pl.ds(pos, 1)], dst, sem.at[s]).start()
        else:
            @pl.when(pos < split)
            def _():
                pltpu.make_async_copy(ys_hbm.at[pl.ds(pos, 1)], dst, sem.at[s]).start()

            @pl.when(pos >= split)
            def _():
                pltpu.make_async_copy(yo_hbm.at[pl.ds(pos - split, 1)], dst, sem.at[s]).start()

    def start_rows(a_ref, b_ref, s):
        def body(r, carry):
            start_row(a_ref[0, 0, r], 0, r, s)
            start_row(b_ref[0, 0, r], 1, r, s)
            return carry

        lax.fori_loop(0, n, body, 0, unroll=8)

    @pl.when(i == 0)
    def _():
        start_rows(p1_ref, p2_ref, 0)

    @pl.when(i + 1 < nsteps)
    def _():
        start_rows(n1_ref, n2_ref, 1 - slot)

    pltpu.make_async_copy(buf_ref.at[slot], buf_ref.at[slot], sem.at[slot]).wait()
    y = x_ref[...] + (g1_ref[...] * buf_ref[slot, 0] + g2_ref[...] * buf_ref[slot, 1])
    yn = _rms(y, fg_ref[...])

    @pl.when(i < n_prompt_blocks)
    def _():
        op_ref[...] = yn

    @pl.when(i >= n_prompt_blocks)
    def _():
        os_ref[...] = yn


def _combine(x, ys, ys_over, pos1, pos2, g1, g2, final_g, n_prompt):
    T, D = x.shape
    srcs = [ys] if ys_over is None else [ys, ys_over]
    tr = T - n_prompt
    assert n_prompt % tr == 0
    nb = T // tr
    npb = n_prompt // tr
    cur = pl.BlockSpec((1, 1, tr), lambda i: (i, 0, 0), memory_space=pltpu.SMEM)
    nxt = pl.BlockSpec((1, 1, tr), lambda i: (jnp.minimum(i + 1, nb - 1), 0, 0),
                       memory_space=pltpu.SMEM)
    p1 = pos1.reshape(nb, 1, tr)
    p2 = pos2.reshape(nb, 1, tr)
    return pl.pallas_call(
        functools.partial(_combine_kernel, npb, ys_over is not None),
        grid_spec=pltpu.PrefetchScalarGridSpec(
            num_scalar_prefetch=0,
            grid=(nb,),
            in_specs=[cur, cur, nxt, nxt,
                      pl.BlockSpec((tr, D), lambda i: (i, 0)),
                      pl.BlockSpec((tr, 1), lambda i: (i, 0)),
                      pl.BlockSpec((tr, 1), lambda i: (i, 0)),
                      pl.BlockSpec((1, D), lambda i: (0, 0))]
                     + [pl.BlockSpec(memory_space=pl.ANY)] * len(srcs),
            out_specs=[pl.BlockSpec((tr, D), lambda i: (jnp.minimum(i, npb - 1), 0)),
                       pl.BlockSpec((tr, D), lambda i: (0, 0))],
            scratch_shapes=[pltpu.VMEM((2, 2, tr, D), F32), pltpu.SemaphoreType.DMA((2,))],
        ),
        out_shape=[jax.ShapeDtypeStruct((n_prompt, D), F32),
                   jax.ShapeDtypeStruct((tr, D), F32)],
        compiler_params=_cparams(1),
        name="moe_combine_final_norm",
    )(p1, p2, p1, p2, x, g1, g2, final_g.reshape(1, D), *srcs)


def _moe_plan(idx, T):
    M = T * TOP_K
    E = N_EXPERTS
    R = -(-int(M / E * MOE_CAPACITY) // MOE_SUB) * MOE_SUB
    n_over = -(-M // R)
    i32 = jnp.int32
    flat_e = idx.reshape(M)
    onehot = (flat_e[:, None] == jnp.arange(E, dtype=i32)[None, :]).astype(i32)
    csum = jnp.cumsum(onehot, axis=0)
    rank = jnp.sum((csum - onehot) * onehot, axis=1)
    counts = csum[-1]
    o_chunks = jnp.maximum((counts + R - 1) // R - 1, 0)
    o_first = jnp.cumsum(o_chunks) - o_chunks
    n_used = jnp.sum(o_chunks).astype(i32)
    over = rank - R
    slot = jnp.where(rank < R, flat_e * R + rank,
                     (E + o_first[flat_e] + over // R) * R + over % R)
    flat_tok = jnp.arange(M, dtype=i32) // TOP_K
    slot_tok = jnp.zeros(((E + n_over) * R,), i32).at[slot].set(
        flat_tok, unique_indices=True, mode="promise_in_bounds")
    spc = R // MOE_SUB

    def sub_used(c_rows):
        sub = jnp.arange(c_rows.shape[0] * spc, dtype=i32)
        return ((sub % spc) * MOE_SUB < c_rows[sub // spc]).astype(i32)

    m_rows = jnp.minimum(counts, R).astype(i32)
    eidx = jnp.arange(E, dtype=i32)
    main = dict(tok=slot_tok[:E * R], used=sub_used(m_rows),
                meta=(eidx, eidx, m_rows, jnp.full((1,), E, i32)))
    cidx = jnp.arange(n_over, dtype=i32)
    c_e = jnp.clip(jnp.searchsorted(jnp.cumsum(o_chunks), cidx, side='right'), 0, E - 1).astype(i32)
    c_rows = jnp.clip(counts[c_e] - R - (cidx - o_first[c_e]) * R, 0, R).astype(i32)
    used = cidx < n_used
    last = jnp.maximum(n_used - 1, 0)
    c_src = jnp.where(used, cidx, last).astype(i32)
    c_e = jnp.where(used, c_e, c_e[last]).astype(i32)
    c_rows = jnp.where(used, c_rows, 0).astype(i32)
    overflow = dict(tok=slot_tok[E * R:], used=sub_used(c_rows),
                    meta=(c_e, c_src, c_rows, n_used.reshape(1)))
    pos = slot.reshape(T, TOP_K).astype(i32)
    return R, main, overflow, n_used, pos[:, 0], pos[:, 1]


def kernel(x_prompt, x_sample, state_gla, state_conv, ev_norm_mix, ev_w_in, ev_sgu_norm, ev_sgu_w, ev_sgu_b, ev_gla_w_alpha, ev_gla_b_alpha, ev_gla_norm, ev_w_out, ev_norm_ffn, ev_ffn_w_gu, ev_ffn_w_down, od_norm_mix, od_conv_w_in, od_conv_w, od_conv_w_out, od_norm_ffn, od_router_w, od_moe_w_gu, od_moe_w_down, final_norm):
    nb, seq, D = x_prompt.shape
    B = x_sample.shape[0]
    assert x_sample.shape[1] == 1
    Tp = nb * seq
    T = Tp + B
    xp0 = x_prompt.reshape(Tp, D)
    xs0 = x_sample.reshape(B, D)

    sw = ev_sgu_norm.shape[1]
    kw = ev_gla_w_alpha.shape[2]
    rank = ev_gla_w_alpha.shape[1]
    vw = GLA_HEADS * ev_gla_norm.shape[1]
    n_main = 2 * sw + 2 * kw + 2 * vw
    assert ev_w_in.shape[2] == n_main + rank
    tm = _pick_tile(T, 2080)

    h = _norm_bf16_stacked(xp0, xs0, ev_norm_mix[0])
    P, log_a = _mix_in(h, ev_w_in, n_main, ev_gla_w_alpha[0], ev_gla_b_alpha[0], tm, 512)
    a_out, v_sample = _sgu(P, Tp, B, sw, ev_sgu_norm[0], ev_sgu_w[0], ev_sgu_b[0])
    col_q = 2 * sw
    o_p, gla_p = _gla_prompt(P, log_a, nb, seq, col_q, kw, vw, ev_gla_norm[0])
    Ps = P[Tp:]
    q_s = Ps[:, col_q:col_q + kw]
    k_s = Ps[:, col_q + kw:col_q + 2 * kw]
    v_s = Ps[:, col_q + 2 * kw:col_q + 2 * kw + vw]
    r_s = Ps[:, col_q + 2 * kw + vw:]
    gla_s, o_s = _gla_sample(state_gla[0], log_a[Tp:], k_s, q_s, v_s, r_s, ev_gla_norm[0])
    x1, h = _mix_out(a_out, o_p, o_s.reshape(B, vw).astype(BF16), xp0, xs0, ev_w_out[0],
                     ev_norm_ffn[0])

    hid = _swiglu_up(h, ev_ffn_w_gu[0], tm, 512)
    x2 = _matmul_res(hid, ev_ffn_w_down[0], x1, _pick_tile(T, 1040), 256, "ffn_down")

    h = _norm_bf16(x2, od_norm_mix[0])
    bgate, z = _conv_in(h, od_conv_w_in[0], tm, 256)
    x3, h3_slab, idx, gates = _conv_out_route(
        z, bgate, od_conv_w[0], state_conv[0][:, 0], state_conv[0][:, 1],
        x2, od_conv_w_out[0], od_norm_ffn[0], od_router_w[0], Tp, seq)
    h3_slab = h3_slab.reshape(T, D // LANES, LANES)
    conv_p = jnp.stack([z[(n + 1) * seq - 2:(n + 1) * seq] for n in range(nb)])
    conv_s = jnp.concatenate([state_conv[0][:, 1:], z[Tp:].reshape(B, 1, D)], axis=1)

    R, main, overflow, n_overflow, pos1, pos2 = _moe_plan(idx[:, :TOP_K], T)
    FE = od_moe_w_down.shape[2]
    tf, tk = _pick_tile(FE, 512, LANES), _pick_tile(FE, 1024, LANES)

    def experts(part, direct):
        xs = _dispatch(h3_slab, part["tok"], part["used"])
        hid = _moe_up(xs, od_moe_w_gu[0], part["meta"], R, tf, direct)
        return _moe_down(hid, od_moe_w_down[0], part["meta"], R, 1024, tk, direct)

    ys = experts(main, True)
    g1, g2 = gates[:, 0:1], gates[:, 1:2]
    y_p, y_s = lax.cond(
        n_overflow > 0,
        lambda: _combine(x3, ys, experts(overflow, False), pos1, pos2, g1, g2, final_norm, Tp),
        lambda: _combine(x3, ys, None, pos1, pos2, g1, g2, final_norm, Tp))

    y_prompt = y_p.reshape(nb, seq, D)
    y_sample = y_s.reshape(B, 1, D)
    return (y_prompt, y_sample, gla_p[None], gla_s[None],
            v_sample.reshape(1, B, 1, sw), conv_p[None], conv_s[None])
```

```python
import functools
import math

import numpy as np
import jax
import jax.numpy as jnp
from jax import lax
from jax.experimental import pallas as pl
from jax.experimental.pallas import tpu as pltpu

F32 = jnp.float32
BF16 = jnp.bfloat16
EPS = 1e-6

SGU_CHUNK = 128
SGU_GROUPS = 4
GLA_HEADS = 4
GLA_TAU = 16.0
GLA_CHUNK = 128
N_EXPERTS = 8
TOP_K = 2
MOE_SUB = 256
MOE_CAPACITY = 1.35
MOE_BIG = 1024
MOE_SMALL = 128
LANES = 128
SLAB_PITCH_PAD = 4
VMEM_LIMIT = 52 * 1024 * 1024
VMEM_LIMIT_MAX = 60 * 1024 * 1024


def _cparams(n_axes, vmem=VMEM_LIMIT):
    return pltpu.CompilerParams(dimension_semantics=("arbitrary",) * n_axes,
                                vmem_limit_bytes=vmem)


def _pick_tile(n, cap, mult=16):
    best = None
    for t in range(mult, min(n, cap) + 1, mult):
        if n % t == 0:
            best = t
    assert best is not None, (n, cap, mult)
    return best


def _rms(x, g):
    ms = jnp.mean(x * x, axis=-1, keepdims=True)
    return (x * lax.rsqrt(ms + EPS)) * g


def _gelu(x):
    c = math.sqrt(2.0 / math.pi)
    return x * (0.5 * (1.0 + jnp.tanh(c * (x + 0.044715 * (x * x * x)))))


def _silu(x):
    return x * (1.0 / (1.0 + jnp.exp(-x)))


def _dot(a, b):
    return jnp.dot(a, b, preferred_element_type=F32)


def _dot_nt(a, b):
    return lax.dot_general(a, b, (((1,), (1,)), ((), ())), preferred_element_type=F32)


def _norm_kernel(x_ref, g_ref, o_ref):
    o_ref[...] = _rms(x_ref[...], g_ref[...]).astype(BF16)


def _norm_bf16(x, g):
    T, D = x.shape
    tr = _pick_tile(T, 512)
    return pl.pallas_call(
        _norm_kernel,
        grid=(T // tr,),
        in_specs=[pl.BlockSpec((tr, D), lambda i: (i, 0)),
                  pl.BlockSpec((1, D), lambda i: (0, 0))],
        out_specs=pl.BlockSpec((tr, D), lambda i: (i, 0)),
        out_shape=jax.ShapeDtypeStruct((T, D), BF16),
        compiler_params=_cparams(1),
        name="rmsnorm_bf16",
    )(x, g.reshape(1, D))


def _norm2_kernel(n_prompt_blocks, xp_ref, xs_ref, g_ref, o_ref):
    i = pl.program_id(0)

    @pl.when(i < n_prompt_blocks)
    def _():
        o_ref[...] = _rms(xp_ref[...], g_ref[...]).astype(BF16)

    @pl.when(i >= n_prompt_blocks)
    def _():
        o_ref[0:xs_ref.shape[0], :] = _rms(xs_ref[...], g_ref[...]).astype(BF16)


def _norm_bf16_stacked(xp, xs, g):
    Tp, D = xp.shape
    B = xs.shape[0]
    tr = _pick_tile(Tp, 512)
    assert B <= tr
    npb = Tp // tr
    return pl.pallas_call(
        functools.partial(_norm2_kernel, npb),
        grid=(npb + 1,),
        in_specs=[pl.BlockSpec((tr, D), lambda i: (jnp.minimum(i, npb - 1), 0)),
                  pl.BlockSpec((B, D), lambda i: (0, 0)),
                  pl.BlockSpec((1, D), lambda i: (0, 0))],
        out_specs=pl.BlockSpec((tr, D), lambda i: (i, 0)),
        out_shape=jax.ShapeDtypeStruct((Tp + B, D), BF16),
        compiler_params=_cparams(1),
        name="rmsnorm_stack_bf16",
    )(xp, xs, g.reshape(1, D))


def _mm_res_kernel(x_ref, w_ref, r_ref, o_ref):
    o_ref[...] = r_ref[...] + _dot(x_ref[...], w_ref[...].astype(BF16))


def _matmul_res(x, w, res, tm, tn, name):
    T, K = x.shape
    N = w.shape[1]
    return pl.pallas_call(
        _mm_res_kernel,
        grid=(T // tm, N // tn),
        in_specs=[pl.BlockSpec((tm, K), lambda i, j: (i, 0)),
                  pl.BlockSpec((K, tn), lambda i, j: (0, j)),
                  pl.BlockSpec((tm, tn), lambda i, j: (i, j))],
        out_specs=pl.BlockSpec((tm, tn), lambda i, j: (i, j)),
        out_shape=jax.ShapeDtypeStruct((T, N), F32),
        compiler_params=_cparams(2),
        name=name,
    )(x, w, res)


def _mix_out_kernel(n_full, n_tail, a_ref, op_ref, os_ref, rp_ref, rs_ref, w_ref, ng_ref,
                    x_ref, h_ref):
    i = pl.program_id(0)
    sw = a_ref.shape[1]

    def finish(rows, a, o, res):
        y = res + _dot(a, w_ref[0:sw, :]) + _dot(o, w_ref[sw:, :])
        x_ref[0:rows, :] = y
        h_ref[0:rows, :] = _rms(y, ng_ref[...]).astype(BF16)

    @pl.when(i < n_full)
    def _():
        finish(a_ref.shape[0], a_ref[...], op_ref[...], rp_ref[...])

    @pl.when(i >= n_full)
    def _():
        finish(n_tail, a_ref[0:n_tail, :], os_ref[...], rs_ref[...])


def _mix_out(a, o_p, o_s, res_p, res_s, w, norm_g):
    T, sw = a.shape
    Tp, vw = o_p.shape
    B = o_s.shape[0]
    D = w.shape[1]
    assert T == Tp + B
    tm = _pick_tile(Tp, 512)
    assert B <= tm
    n_full = Tp // tm
    clamp = lambda i: (jnp.minimum(i, n_full - 1), 0)
    row = lambda i: (i, 0)
    fixed = lambda i: (0, 0)
    return pl.pallas_call(
        functools.partial(_mix_out_kernel, n_full, B),
        grid=(n_full + 1,),
        in_specs=[pl.BlockSpec((tm, sw), row),
                  pl.BlockSpec((tm, vw), clamp),
                  pl.BlockSpec((B, vw), fixed),
                  pl.BlockSpec((tm, D), clamp),
                  pl.BlockSpec((B, D), fixed),
                  pl.BlockSpec((sw + vw, D), fixed),
                  pl.BlockSpec((1, D), fixed)],
        out_specs=[pl.BlockSpec((tm, D), row),
                   pl.BlockSpec((tm, D), row)],
        out_shape=[jax.ShapeDtypeStruct((T, D), F32),
                   jax.ShapeDtypeStruct((T, D), BF16)],
        compiler_params=_cparams(1),
        name="mix_out_proj",
    )(a, o_p, o_s, res_p, res_s, w.astype(BF16), norm_g.reshape(1, D))


def _swiglu_kernel(x_ref, wg_ref, wu_ref, o_ref):
    x = x_ref[...]
    g = _dot(x, wg_ref[...].astype(BF16))
    u = _dot(x, wu_ref[...].astype(BF16))
    o_ref[...] = (_silu(g) * u).astype(o_ref.dtype)


def _swiglu_up(x, w_gu, tm, tn):
    T, K = x.shape
    F = w_gu.shape[1] // 2
    nj = F // tn
    return pl.pallas_call(
        _swiglu_kernel,
        grid=(T // tm, nj),
        in_specs=[pl.BlockSpec((tm, K), lambda i, j: (i, 0)),
                  pl.BlockSpec((K, tn), lambda i, j: (0, j)),
                  pl.BlockSpec((K, tn), lambda i, j: (0, j + nj))],
        out_specs=pl.BlockSpec((tm, tn), lambda i, j: (i, j)),
        out_shape=jax.ShapeDtypeStruct((T, F), BF16),
        compiler_params=_cparams(2, vmem=VMEM_LIMIT_MAX),
        name="ffn_gate_up",
    )(x, w_gu, w_gu)


def _conv_in_kernel(x_ref, wb_ref, wc_ref, wu_ref, b_ref, z_ref):
    x = x_ref[...]
    b_ref[...] = _dot(x, wb_ref[...].astype(BF16))
    c = _dot(x, wc_ref[...].astype(BF16))
    u = _dot(x, wu_ref[...].astype(BF16))
    z_ref[...] = c * u


def _conv_in(x, w, tm, tn):
    T, K = x.shape
    C = w.shape[1] // 3
    nj = C // tn
    return pl.pallas_call(
        _conv_in_kernel,
        grid=(T // tm, nj),
        in_specs=[pl.BlockSpec((tm, K), lambda i, j: (i, 0)),
                  pl.BlockSpec((K, tn), lambda i, j: (0, j)),
                  pl.BlockSpec((K, tn), lambda i, j: (0, j + nj)),
                  pl.BlockSpec((K, tn), lambda i, j: (0, j + 2 * nj))],
        out_specs=[pl.BlockSpec((tm, tn), lambda i, j: (i, j)),
                   pl.BlockSpec((tm, tn), lambda i, j: (i, j))],
        out_shape=[jax.ShapeDtypeStruct((T, C), F32),
                   jax.ShapeDtypeStruct((T, C), F32)],
        compiler_params=_cparams(2),
        name="conv_in_proj",
    )(x, w, w, w)


def _mix_in_kernel(x_ref, w_ref, wl_ref, wa_ref, b_ref, p_ref, la_ref):
    x = x_ref[...]
    p_ref[...] = _dot(x, w_ref[...].astype(BF16))

    @pl.when(pl.program_id(1) == pl.num_programs(1) - 1)
    def _():
        rank = wa_ref.shape[0]
        g = _dot(x, wl_ref[:, 0:rank].astype(BF16))
        z = _dot(g.astype(BF16), wa_ref[...].astype(BF16)) + b_ref[...]
        ls = jnp.minimum(z, 0.0) - jnp.log1p(jnp.exp(-jnp.abs(z)))
        la_ref[...] = ls * (1.0 / GLA_TAU)


def _mix_in(x, w_in, n_main, w_alpha, b_alpha, tm, tn):
    T, K = x.shape
    R, KW = w_alpha.shape
    assert n_main % LANES == 0 and R <= LANES
    return pl.pallas_call(
        _mix_in_kernel,
        grid=(T // tm, n_main // tn),
        in_specs=[pl.BlockSpec((tm, K), lambda i, j: (i, 0)),
                  pl.BlockSpec((None, K, tn), lambda i, j: (0, 0, j)),
                  pl.BlockSpec((None, K, LANES), lambda i, j: (0, 0, n_main // LANES)),
                  pl.BlockSpec((R, KW), lambda i, j: (0, 0)),
                  pl.BlockSpec((1, KW), lambda i, j: (0, 0))],
        out_specs=[pl.BlockSpec((tm, tn), lambda i, j: (i, j)),
                   pl.BlockSpec((tm, KW), lambda i, j: (i, 0))],
        out_shape=[jax.ShapeDtypeStruct((T, n_main), F32),
                   jax.ShapeDtypeStruct((T, KW), F32)],
        compiler_params=_cparams(2, vmem=VMEM_LIMIT_MAX),
        name="mix_in_proj",
    )(x, w_in, w_in, w_alpha, b_alpha.reshape(1, KW))


def _sgu_kernel(n_prompt_tiles, u_ref, v_ref, ng_ref, w_ref, bt_ref, o_ref, vs_ref):
    i = pl.program_id(0)
    C = SGU_CHUNK
    gd = u_ref.shape[1] // SGU_GROUPS
    row = lax.broadcasted_iota(jnp.int32, (C, C), 0)
    col = lax.broadcasted_iota(jnp.int32, (C, C), 1)

    def chunk(r0, weights, biases, keep_v):
        gu = _gelu(u_ref[r0:r0 + C, :])
        vn = _rms(_gelu(v_ref[r0:r0 + C, :]), ng_ref[...])
        if keep_v:
            vs_ref[...] = vn
        vb = vn.astype(BF16)
        for g in range(SGU_GROUPS):
            cols = slice(g * gd, (g + 1) * gd)
            mixed = _dot(weights[g], vb[:, cols]) + biases[g]
            o_ref[r0:r0 + C, cols] = (gu[:, cols] * mixed).astype(o_ref.dtype)

    @pl.when(i < n_prompt_tiles)
    def _():
        weights = [jnp.where(col <= row, w_ref[g], 0.0).astype(BF16) for g in range(SGU_GROUPS)]
        biases = [bt_ref[:, g:g + 1] for g in range(SGU_GROUPS)]
        for r0 in range(0, u_ref.shape[0], C):
            chunk(r0, weights, biases, False)

    @pl.when(i >= n_prompt_tiles)
    def _():
        weights = [jnp.where(col == row, w_ref[g, 0:1, 0:1], 0.0).astype(BF16)
                   for g in range(SGU_GROUPS)]
        biases = [bt_ref[0:1, g:g + 1] for g in range(SGU_GROUPS)]
        chunk(0, weights, biases, True)


def _sgu(P, n_prompt, n_sample, width, sgu_norm, sgu_w, sgu_b):
    C = SGU_CHUNK
    assert n_prompt % C == 0 and n_sample == C
    T = n_prompt + n_sample
    tr = C * math.gcd(n_prompt // C, 4)
    npt = n_prompt // tr
    return pl.pallas_call(
        functools.partial(_sgu_kernel, npt),
        grid=(npt + 1,),
        in_specs=[pl.BlockSpec((tr, width), lambda i: (i, 0)),
                  pl.BlockSpec((tr, width), lambda i: (i, 1)),
                  pl.BlockSpec((1, width), lambda i: (0, 0)),
                  pl.BlockSpec((SGU_GROUPS, C, C), lambda i: (0, 0, 0)),
                  pl.BlockSpec((C, SGU_GROUPS), lambda i: (0, 0))],
        out_specs=[pl.BlockSpec((tr, width), lambda i: (i, 0)),
                   pl.BlockSpec((C, width), lambda i: (0, 0))],
        out_shape=[jax.ShapeDtypeStruct((T, width), BF16),
                   jax.ShapeDtypeStruct((n_sample, width), F32)],
        compiler_params=_cparams(1),
        name="sgu_mix",
    )(P, P, sgu_norm.reshape(1, width), sgu_w, sgu_b.T)


def _gla_constants():
    C = GLA_CHUNK
    s = np.arange(C)[None, :]
    i = np.arange(C)[:, None]
    gs = [s <= i, s > i]
    masks = []
    b = C // 2
    while b >= 1:
        upper = (i % (2 * b)) >= b
        m = (i // (2 * b)) * (2 * b) + b - 1
        gs.append(np.where(upper, (s > m) & (s <= i), (s > i) & (s <= m)))
        same = (i // (2 * b)) == (s // (2 * b))
        masks.append(upper & same & ~upper.T)
        b //= 2
    masks.append(np.eye(C, dtype=bool))
    G = np.concatenate(gs, axis=0).astype(np.float32)
    M = np.stack(masks).astype(np.float32)
    return jnp.asarray(G, BF16), jnp.asarray(M, F32)


def _gla_kernel(nseq, *refs):
    ins = [refs[5 * s:5 * s + 5] for s in range(nseq)]
    G_ref, M_ref, gn_ref, o_ref, s_out_ref, st_ref = refs[5 * nseq:]
    c = pl.program_id(1)
    C = GLA_CHUNK
    hk = ins[0][0].shape[1]
    dk = hk // GLA_HEADS
    dv = ins[0][2].shape[1] // GLA_HEADS
    nlev = M_ref.shape[0] - 1

    @pl.when(c == 0)
    def _():
        st_ref[...] = jnp.zeros_like(st_ref)

    parts = []
    for s in range(nseq):
        la = ins[s][4][...]
        hi = la.astype(BF16)
        r1 = la - hi.astype(F32)
        mid = r1.astype(BF16)
        lo = (r1 - mid.astype(F32)).astype(BF16)
        parts.append(jnp.concatenate([hi, mid, lo], axis=0))
    X = jnp.exp(_dot(G_ref[...], jnp.concatenate(parts, axis=1)))

    scale = dk ** -0.5
    for s in range(nseq):
        q_ref, k_ref, v_ref, r_ref, _ = ins[s]
        for h in range(GLA_HEADS):
            Xh = X[:, s * hk + h * dk:s * hk + (h + 1) * dk]
            qh = q_ref[:, h * dk:(h + 1) * dk] * scale
            kh = k_ref[:, h * dk:(h + 1) * dk]
            vh = v_ref[:, h * dv:(h + 1) * dv]
            x_cum = Xh[0:C]
            x_rest = Xh[C:2 * C]
            st_old = st_ref[s, h]
            o = _dot_nt((qh * x_cum).astype(BF16), st_old.astype(BF16))
            att = _dot_nt(qh.astype(BF16), kh.astype(BF16)) * M_ref[nlev]
            for l in range(nlev):
                xl = Xh[(2 + l) * C:(3 + l) * C]
                att = att + _dot_nt((qh * xl).astype(BF16), (kh * xl).astype(BF16)) * M_ref[l]
            o = o + _dot(att.astype(BF16), vh.astype(BF16))
            vt = vh.T.astype(BF16)
            st_ref[s, h] = st_old * x_cum[C - 1:C, :] + _dot(vt, (kh * x_rest).astype(BF16))
            on = _rms(o, gn_ref[...]) * _silu(r_ref[:, h * dv:(h + 1) * dv])
            o_ref[s, :, h * dv:(h + 1) * dv] = on.astype(o_ref.dtype)

    @pl.when(c == pl.num_programs(1) - 1)
    def _():
        for s in range(nseq):
            for h in range(GLA_HEADS):
                s_out_ref[s, h] = st_ref[s, h].T


def _gla_prompt(P, log_a, n_seq, seq, col_q, kw, vw, gla_norm):
    C = GLA_CHUNK
    assert seq % C == 0
    nc = seq // C
    dk = kw // GLA_HEADS
    dv = vw // GLA_HEADS
    qb, kb = col_q // kw, col_q // kw + 1
    vb = (col_q + 2 * kw) // vw
    rb = vb + 1
    assert col_q % kw == 0 and (col_q + 2 * kw) % vw == 0
    nseq = math.gcd(n_seq, 4)
    G, M = _gla_constants()
    G3 = jnp.concatenate([G, G, G], axis=1)

    def seq_specs(s):
        rows = lambda n, c: (n * nseq + s) * nc + c
        return [pl.BlockSpec((C, kw), lambda n, c: (rows(n, c), qb)),
                pl.BlockSpec((C, kw), lambda n, c: (rows(n, c), kb)),
                pl.BlockSpec((C, vw), lambda n, c: (rows(n, c), vb)),
                pl.BlockSpec((C, vw), lambda n, c: (rows(n, c), rb)),
                pl.BlockSpec((C, kw), lambda n, c: (rows(n, c), 0))]

    in_specs = [sp for s in range(nseq) for sp in seq_specs(s)]
    in_specs += [pl.BlockSpec(G3.shape, lambda n, c: (0, 0)),
                 pl.BlockSpec(M.shape, lambda n, c: (0, 0, 0)),
                 pl.BlockSpec((1, dv), lambda n, c: (0, 0))]
    o, states = pl.pallas_call(
        functools.partial(_gla_kernel, nseq),
        grid=(n_seq // nseq, nc),
        in_specs=in_specs,
        out_specs=[pl.BlockSpec((None, nseq, C, vw), lambda n, c: (n, 0, c, 0)),
                   pl.BlockSpec((nseq, GLA_HEADS, dk, dv), lambda n, c: (n, 0, 0, 0))],
        out_shape=[jax.ShapeDtypeStruct((n_seq // nseq, nseq, seq, vw), BF16),
                   jax.ShapeDtypeStruct((n_seq, GLA_HEADS, dk, dv), F32)],
        scratch_shapes=[pltpu.VMEM((nseq, GLA_HEADS, dv, dk), F32)],
        compiler_params=_cparams(2),
        name="gla_prompt",
    )(*([P, P, P, P, log_a] * nseq), G3, M, gla_norm.reshape(1, dv))
    return o.reshape(n_seq * seq, vw), states


def _gla_step_kernel(s_ref, at_ref, kt_ref, qt_ref, v_ref, r_ref, gn_ref, sn_ref, o_ref):
    i = pl.program_id(0)
    bb, H, dk, dv = s_ref.shape
    B = at_ref.shape[1]
    lane = lax.broadcasted_iota(jnp.int32, (dk, B), 1)
    scale = dk ** -0.5
    for j in range(bb):
        pick = lane == i * bb + j

        def column(t_ref, h):
            return jnp.sum(jnp.where(pick, t_ref[h * dk:(h + 1) * dk, :], 0.0), axis=1, keepdims=True)

        for h in range(H):
            s_new = jnp.exp(column(at_ref, h)) * s_ref[j, h] + column(kt_ref, h) * v_ref[j, h]
            sn_ref[j, h] = s_new
            o = jnp.sum((column(qt_ref, h) * scale) * s_new, axis=0, keepdims=True)
            o_ref[j, h] = _rms(o, gn_ref[...]) * _silu(r_ref[j, h])


def _gla_sample(state, la, k, q, v, r, gla_norm):
    B, H, dk, dv = state.shape
    bb = 8
    rowv = lambda t: t.reshape(B, H, 1, dv)
    tspec = pl.BlockSpec((H * dk, B), lambda i: (0, 0))
    rspec = pl.BlockSpec((bb, H, 1, dv), lambda i: (i, 0, 0, 0))
    sspec = pl.BlockSpec((bb, H, dk, dv), lambda i: (i, 0, 0, 0))
    return pl.pallas_call(
        _gla_step_kernel,
        grid=(B // bb,),
        in_specs=[sspec, tspec, tspec, tspec, rspec, rspec,
                  pl.BlockSpec((1, dv), lambda i: (0, 0))],
        out_specs=[sspec, rspec],
        out_shape=[jax.ShapeDtypeStruct((B, H, dk, dv), F32),
                   jax.ShapeDtypeStruct((B, H, 1, dv), F32)],
        compiler_params=_cparams(1),
        name="gla_sample_step",
    )(state, la.T, k.T, q.T, rowv(v), rowv(r), gla_norm.reshape(1, dv))


def _route(logits):
    lane = lax.broadcasted_iota(jnp.int32, logits.shape, 1)
    neg = jnp.float32(-jnp.inf)
    logits = jnp.where(lane < N_EXPERTS, logits, neg)
    v1 = jnp.max(logits, axis=-1, keepdims=True)
    i1 = jnp.min(jnp.where(logits == v1, lane, LANES), axis=-1, keepdims=True)
    rest = jnp.where(lane == i1, neg, logits)
    v2 = jnp.max(rest, axis=-1, keepdims=True)
    i2 = jnp.min(jnp.where(rest == v2, lane, LANES), axis=-1, keepdims=True)
    e2 = jnp.exp(v2 - v1)
    g1 = 1.0 / (1.0 + e2)
    g2 = e2 / (1.0 + e2)
    idx = jnp.where(lane == 0, i1, jnp.where(lane == 1, i2, 0))
    gate = jnp.where(lane == 0, g1, jnp.where(lane == 1, g2, 0.0))
    return idx, gate


def _conv_out_kernel(n_prompt_blocks, blocks_per_seq, n_tail, z_ref, halo_ref, b_ref, cw_ref,
                     h0_ref, h1_ref, res_ref, w_ref, ng_ref, rw_ref, x_ref, xslab_ref, idx_ref,
                     gate_ref):
    i = pl.program_id(0)
    w0, w1, w2 = cw_ref[0:1, :], cw_ref[1:2, :], cw_ref[2:3, :]
    ns = x_ref.shape[1] // LANES

    def finish(rows, yc, res):
        y = res + _dot(yc.astype(BF16), w_ref[...])
        x_ref[0:rows, :] = y
        hn = _rms(y, ng_ref[...])
        for s in range(ns):
            xslab_ref[pl.ds(s, rows, stride=ns), :] = hn[:, s * LANES:(s + 1) * LANES]
        h_hi = hn.astype(BF16)
        h_lo = (hn - h_hi.astype(F32)).astype(BF16)
        rw = rw_ref[...]
        r_hi = rw.astype(BF16)
        r_lo = (rw - r_hi.astype(F32)).astype(BF16)
        logits = _dot(h_hi, r_hi) + (_dot(h_hi, r_lo) + _dot(h_lo, r_hi))
        idx, gate = _route(logits)
        idx_ref[0:rows, :] = idx
        gate_ref[0:rows, :] = gate

    @pl.when(i < n_prompt_blocks)
    def _():
        z = z_ref[...]
        row = lax.broadcasted_iota(jnp.int32, z.shape, 0)
        halo = jnp.where(i % blocks_per_seq == 0, 0.0, halo_ref[...])
        z1 = jnp.where(row == 0, halo[7:8, :], pltpu.roll(z, 1, 0))
        z2 = pltpu.roll(z, 2, 0)
        z2 = jnp.where(row == 0, halo[6:7, :], jnp.where(row == 1, halo[7:8, :], z2))
        finish(z.shape[0], b_ref[...] * (w0 * z2 + w1 * z1 + w2 * z), res_ref[...])

    @pl.when(i >= n_prompt_blocks)
    def _():
        conv = w0 * h0_ref[...] + w1 * h1_ref[...] + w2 * z_ref[0:n_tail, :]
        finish(n_tail, b_ref[0:n_tail, :] * conv, res_ref[0:n_tail, :])


def _conv_out_route(z, bgate, conv_w, hist0, hist1, res, w_out, norm_g, router_w, n_prompt, seq):
    T, Cd = z.shape
    D = w_out.shape[1]
    B = T - n_prompt
    tm = _pick_tile(math.gcd(seq, n_prompt), 256)
    assert B <= tm and hist0.shape == (B, Cd)
    npb = n_prompt // tm
    wpad = jnp.zeros((D, LANES), F32).at[:, :N_EXPERTS].set(router_w)
    row = lambda i: (i, 0)
    fixed = lambda i: (0, 0)
    return pl.pallas_call(
        functools.partial(_conv_out_kernel, npb, seq // tm, B),
        grid=(npb + 1,),
        in_specs=[pl.BlockSpec((tm, Cd), row),
                  pl.BlockSpec((8, Cd), lambda i: (jnp.maximum(i * (tm // 8) - 1, 0), 0)),
                  pl.BlockSpec((tm, Cd), row),
                  pl.BlockSpec(conv_w.shape, fixed),
                  pl.BlockSpec((B, Cd), fixed),
                  pl.BlockSpec((B, Cd), fixed),
                  pl.BlockSpec((tm, D), row),
                  pl.BlockSpec((Cd, D), fixed),
                  pl.BlockSpec((1, D), fixed),
                  pl.BlockSpec((D, LANES), fixed)],
        out_specs=[pl.BlockSpec((tm, D), row),
                   pl.BlockSpec((tm * (D // LANES), LANES), row),
                   pl.BlockSpec((tm, LANES), row),
                   pl.BlockSpec((tm, LANES), row)],
        out_shape=[jax.ShapeDtypeStruct((T, D), F32),
                   jax.ShapeDtypeStruct((T * (D // LANES), LANES), F32),
                   jax.ShapeDtypeStruct((T, LANES), jnp.int32),
                   jax.ShapeDtypeStruct((T, LANES), F32)],
        compiler_params=_cparams(1),
        name="conv_out_route",
    )(z, z, bgate, conv_w, hist0, hist1, res, w_out.astype(BF16), norm_g.reshape(1, D), wpad)


def _dispatch_kernel(tok_ref, nxt_ref, used_ref, x_hbm, o_ref, buf_ref, sem):
    i = pl.program_id(0)
    nsteps = pl.num_programs(0)
    n = o_ref.shape[0]
    ns = x_hbm.shape[1]
    pitch = buf_ref.shape[1] // n
    slot = i % 2

    def row_copy(t_ref, r, s):
        return pltpu.make_async_copy(x_hbm.at[t_ref[0, 0, r]],
                                     buf_ref.at[s, pl.ds(r * pitch, ns)], sem.at[s])

    def start_rows(t_ref, s):
        def body(p, carry):
            row_copy(t_ref, 2 * p, s).start(priority=0)
            row_copy(t_ref, 2 * p + 1, s).start(priority=1)
            return carry

        lax.fori_loop(0, n // 2, body, 0, unroll=4)

    @pl.when(jnp.logical_and(i == 0, used_ref[0] > 0))
    def _():
        start_rows(tok_ref, 0)

    @pl.when(jnp.logical_and(i + 1 < nsteps, used_ref[jnp.minimum(i + 1, nsteps - 1)] > 0))
    def _():
        start_rows(nxt_ref, 1 - slot)

    @pl.when(used_ref[i] > 0)
    def _():
        landed = buf_ref.at[slot, pl.ds(0, n * ns)]
        pltpu.make_async_copy(landed, landed, sem.at[slot]).wait()

        for s in range(ns):
            o_ref[:, s * LANES:(s + 1) * LANES] = (
                buf_ref[slot, pl.ds(s, n, stride=pitch), :].astype(o_ref.dtype))

    @pl.when(used_ref[i] == 0)
    def _():
        o_ref[...] = jnp.zeros_like(o_ref)


def _dispatch(x_slab, slot_tok, sub_used):
    D = x_slab.shape[1] * x_slab.shape[2]
    nsub = sub_used.shape[0]
    SB = MOE_SUB
    tok3 = slot_tok.reshape(nsub, 1, SB)
    return pl.pallas_call(
        _dispatch_kernel,
        grid_spec=pltpu.PrefetchScalarGridSpec(
            num_scalar_prefetch=0,
            grid=(nsub,),
            in_specs=[pl.BlockSpec((1, 1, SB), lambda i: (i, 0, 0), memory_space=pltpu.SMEM),
                      pl.BlockSpec((1, 1, SB), lambda i: (jnp.minimum(i + 1, nsub - 1), 0, 0),
                                   memory_space=pltpu.SMEM),
                      pl.BlockSpec(memory_space=pltpu.SMEM),
                      pl.BlockSpec(memory_space=pl.ANY)],
            out_specs=pl.BlockSpec((SB, D), lambda i: (i, 0)),
            scratch_shapes=[pltpu.VMEM((2, SB * (D // LANES + SLAB_PITCH_PAD), LANES), F32),
                            pltpu.SemaphoreType.DMA((2,))],
        ),
        out_shape=jax.ShapeDtypeStruct((nsub * SB, D), BF16),
        compiler_params=_cparams(1),
        name="moe_dispatch",
    )(tok3, tok3, sub_used, x_slab)


def _row_pieces(rows, total, small, compute, skip):
    for s in range(0, total, MOE_BIG):
        size = min(MOE_BIG, total - s)
        assert size % small == 0
        whole = rows > s + size - small

        @pl.when(whole)
        def _():
            compute(s, size)

        @pl.when(jnp.logical_not(whole))
        def _():
            for t in range(s, s + size, small):
                @pl.when(rows > t)
                def _():
                    compute(t, small)

                @pl.when(rows <= t)
                def _():
                    skip(t, small)


def _moe_up_kernel(ce_ref, cs_ref, cr_ref, nu_ref, x_ref, wg_ref, wu_ref, o_ref, wbf_ref):
    c = pl.program_id(0)

    @pl.when(c < nu_ref[0])
    def _():
        def compute(s, n):
            if s == 0:
                wg = wg_ref[0].astype(BF16)
                wu = wu_ref[0].astype(BF16)
                wbf_ref[0] = wg
                wbf_ref[1] = wu
            else:
                wg = wbf_ref[0]
                wu = wbf_ref[1]
            x = x_ref[pl.ds(s, n), :]
            o_ref[pl.ds(s, n), :] = (_silu(_dot(x, wg)) * _dot(x, wu)).astype(o_ref.dtype)

        def skip(s, n):
            o_ref[pl.ds(s, n), :] = jnp.zeros((n, o_ref.shape[1]), o_ref.dtype)

        _row_pieces(cr_ref[c], x_ref.shape[0], MOE_SMALL, compute, skip)

    @pl.when(c >= nu_ref[0])
    def _():
        o_ref[...] = jnp.zeros_like(o_ref)


def _moe_up(xs, w_gu, meta, R, tf, direct):
    ce, cs, cr, nu = meta
    nch = ce.shape[0]
    D = xs.shape[1]
    F = w_gu.shape[2] // 2
    nf = F // tf

    def fidx(c, f, nu):
        return jnp.where(c < nu[0], f, nf - 1)

    if direct:
        in_specs = [pl.BlockSpec((R, D), lambda c, f, *_: (c, 0)),
                    pl.BlockSpec((1, D, tf), lambda c, f, *_: (c, 0, f)),
                    pl.BlockSpec((1, D, tf), lambda c, f, *_: (c, 0, nf + f))]
    else:
        in_specs = [pl.BlockSpec((R, D), lambda c, f, ce, cs, cr, nu: (cs[c], 0),
                                 pipeline_mode=pl.Buffered(1)),
                    pl.BlockSpec((1, D, tf), lambda c, f, ce, cs, cr, nu: (ce[c], 0, fidx(c, f, nu))),
                    pl.BlockSpec((1, D, tf), lambda c, f, ce, cs, cr, nu: (ce[c], 0, nf + fidx(c, f, nu)))]
    return pl.pallas_call(
        _moe_up_kernel,
        grid_spec=pltpu.PrefetchScalarGridSpec(
            num_scalar_prefetch=4,
            grid=(nch, nf),
            in_specs=in_specs,
            out_specs=pl.BlockSpec((R, tf), lambda c, f, ce, cs, cr, nu: (c, f)),
            scratch_shapes=[pltpu.VMEM((2, D, tf), BF16)],
        ),
        out_shape=jax.ShapeDtypeStruct((nch * R, F), BF16),
        compiler_params=_cparams(2, vmem=VMEM_LIMIT_MAX if direct else VMEM_LIMIT),
        name="moe_gate_up",
    )(ce, cs, cr, nu, xs, w_gu, w_gu)


def _moe_down_kernel(ce_ref, cs_ref, cr_ref, nu_ref, h_ref, w_ref, o_ref, wbf_ref):
    c = pl.program_id(0)
    k = pl.program_id(2)

    @pl.when(c < nu_ref[0])
    def _():
        def weight(s):
            if s == 0:
                w = w_ref[0].astype(BF16)
                wbf_ref[...] = w
                return w
            return wbf_ref[...]

        def compute(s, n):
            @pl.when(k == 0)
            def _():
                o_ref[pl.ds(s, n), :] = _dot(h_ref[pl.ds(s, n), :], weight(s))

            @pl.when(k > 0)
            def _():
                o_ref[pl.ds(s, n), :] = (o_ref[pl.ds(s, n), :]
                                         + _dot(h_ref[pl.ds(s, n), :], weight(s)))

        def skip(s, n):
            @pl.when(k == 0)
            def _():
                o_ref[pl.ds(s, n), :] = jnp.zeros((n, o_ref.shape[1]), F32)

        _row_pieces(cr_ref[c], h_ref.shape[0], MOE_SUB, compute, skip)

    @pl.when(jnp.logical_and(c >= nu_ref[0], k == 0))
    def _():
        o_ref[...] = jnp.zeros_like(o_ref)


def _moe_down(hid, w_down, meta, R, tn, tk, direct):
    ce, cs, cr, nu = meta
    nch = ce.shape[0]
    F = hid.shape[1]
    D = w_down.shape[2]
    nn, nk = D // tn, F // tk

    def last_if_unused(c, v, nu, last):
        return jnp.where(c < nu[0], v, last)

    if direct:
        in_specs = [pl.BlockSpec((R, tk), lambda c, n, k, *_: (c, k)),
                    pl.BlockSpec((1, tk, tn), lambda c, n, k, *_: (c, k, n))]
    else:
        in_specs = [pl.BlockSpec((R, tk), lambda c, n, k, ce, cs, cr, nu:
                                 (cs[c], last_if_unused(c, k, nu, nk - 1))),
                    pl.BlockSpec((1, tk, tn), lambda c, n, k, ce, cs, cr, nu:
                                 (ce[c], last_if_unused(c, k, nu, nk - 1),
                                  last_if_unused(c, n, nu, nn - 1)))]
    return pl.pallas_call(
        _moe_down_kernel,
        grid_spec=pltpu.PrefetchScalarGridSpec(
            num_scalar_prefetch=4,
            grid=(nch, nn, nk),
            in_specs=in_specs,
            out_specs=pl.BlockSpec((R, tn), lambda c, n, k, ce, cs, cr, nu: (c, n)),
            scratch_shapes=[pltpu.VMEM((tk, tn), BF16)],
        ),
        out_shape=jax.ShapeDtypeStruct((nch * R, D), F32),
        compiler_params=_cparams(3),
        name="moe_down",
    )(ce, cs, cr, nu, hid, w_down)


def _combine_kernel(n_prompt_blocks, has_overflow, p1_ref, p2_ref, n1_ref, n2_ref, x_ref, g1_ref,
                    g2_ref, fg_ref, ys_hbm, *rest):
    yo_hbm = rest[0] if has_overflow else None
    op_ref, os_ref, buf_ref, sem = rest[-4:]
    i = pl.program_id(0)
    nsteps = pl.num_programs(0)
    n = x_ref.shape[0]
    slot = i % 2
    split = ys_hbm.shape[0]

    def start_row(pos, k, r, s):
        dst = buf_ref.at[s, k, pl.ds(r, 1)]
        if yo_hbm is None:
            pltpu.make_async_copy(ys_hbm.at[pl.ds(pos, 1)], dst, sem.at[s]).start(priority=k)
        else:
            @pl.when(pos < split)
            def _():
                pltpu.make_async_copy(ys_hbm.at[pl.ds(pos, 1)], dst, sem.at[s]).start(priority=k)

            @pl.when(pos >= split)
            def _():
                pltpu.make_async_copy(yo_hbm.at[pl.ds(pos - split, 1)], dst,
                                      sem.at[s]).start(priority=k)

    def start_rows(a_ref, b_ref, s):
        def body(r, carry):
            start_row(a_ref[0, 0, r], 0, r, s)
            start_row(b_ref[0, 0, r], 1, r, s)
            return carry

        lax.fori_loop(0, n, body, 0, unroll=8)

    @pl.when(i == 0)
    def _():
        start_rows(p1_ref, p2_ref, 0)

    @pl.when(i + 1 < nsteps)
    def _():
        start_rows(n1_ref, n2_ref, 1 - slot)

    pltpu.make_async_copy(buf_ref.at[slot], buf_ref.at[slot], sem.at[slot]).wait()
    y = x_ref[...] + (g1_ref[...] * buf_ref[slot, 0] + g2_ref[...] * buf_ref[slot, 1])
    yn = _rms(y, fg_ref[...])

    @pl.when(i < n_prompt_blocks)
    def _():
        op_ref[...] = yn

    @pl.when(i >= n_prompt_blocks)
    def _():
        os_ref[...] = yn


def _combine(x, ys, ys_over, pos1, pos2, g1, g2, final_g, n_prompt):
    T, D = x.shape
    srcs = [ys] if ys_over is None else [ys, ys_over]
    tr = T - n_prompt
    assert n_prompt % tr == 0
    nb = T // tr
    npb = n_prompt // tr
    cur = pl.BlockSpec((1, 1, tr), lambda i: (i, 0, 0), memory_space=pltpu.SMEM)
    nxt = pl.BlockSpec((1, 1, tr), lambda i: (jnp.minimum(i + 1, nb - 1), 0, 0),
                       memory_space=pltpu.SMEM)
    p1 = pos1.reshape(nb, 1, tr)
    p2 = pos2.reshape(nb, 1, tr)
    return pl.pallas_call(
        functools.partial(_combine_kernel, npb, ys_over is not None),
        grid_spec=pltpu.PrefetchScalarGridSpec(
            num_scalar_prefetch=0,
            grid=(nb,),
            in_specs=[cur, cur, nxt, nxt,
                      pl.BlockSpec((tr, D), lambda i: (i, 0)),
                      pl.BlockSpec((tr, 1), lambda i: (i, 0)),
                      pl.BlockSpec((tr, 1), lambda i: (i, 0)),
                      pl.BlockSpec((1, D), lambda i: (0, 0))]
                     + [pl.BlockSpec(memory_space=pl.ANY)] * len(srcs),
            out_specs=[pl.BlockSpec((tr, D), lambda i: (jnp.minimum(i, npb - 1), 0)),
                       pl.BlockSpec((tr, D), lambda i: (0, 0))],
            scratch_shapes=[pltpu.VMEM((2, 2, tr, D), F32), pltpu.SemaphoreType.DMA((2,))],
        ),
        out_shape=[jax.ShapeDtypeStruct((n_prompt, D), F32),
                   jax.ShapeDtypeStruct((tr, D), F32)],
        compiler_params=_cparams(1),
        name="moe_combine_final_norm",
    )(p1, p2, p1, p2, x, g1, g2, final_g.reshape(1, D), *srcs)


def _moe_plan(idx, T):
    M = T * TOP_K
    E = N_EXPERTS
    R = -(-int(M / E * MOE_CAPACITY) // MOE_SUB) * MOE_SUB
    n_over = -(-M // R)
    i32 = jnp.int32
    flat_e = idx.reshape(M)
    onehot = (flat_e[:, None] == jnp.arange(E, dtype=i32)[None, :]).astype(i32)
    csum = jnp.cumsum(onehot, axis=0)
    rank = jnp.sum((csum - onehot) * onehot, axis=1)
    counts = csum[-1]
    o_chunks = jnp.maximum((counts + R - 1) // R - 1, 0)
    o_first = jnp.cumsum(o_chunks) - o_chunks
    n_used = jnp.sum(o_chunks).astype(i32)
    over = rank - R
    slot = jnp.where(rank < R, flat_e * R + rank,
                     (E + o_first[flat_e] + over // R) * R + over % R)
    flat_tok = jnp.arange(M, dtype=i32) // TOP_K
    slot_tok = jnp.zeros(((E + n_over) * R,), i32).at[slot].set(
        flat_tok, unique_indices=True, mode="promise_in_bounds")
    spc = R // MOE_SUB

    def sub_used(c_rows):
        sub = jnp.arange(c_rows.shape[0] * spc, dtype=i32)
        return ((sub % spc) * MOE_SUB < c_rows[sub // spc]).astype(i32)

    m_rows = jnp.minimum(counts, R).astype(i32)
    eidx = jnp.arange(E, dtype=i32)
    main = dict(tok=slot_tok[:E * R], used=sub_used(m_rows),
                meta=(eidx, eidx, m_rows, jnp.full((1,), E, i32)))
    cidx = jnp.arange(n_over, dtype=i32)
    c_e = jnp.clip(jnp.searchsorted(jnp.cumsum(o_chunks), cidx, side='right'), 0, E - 1).astype(i32)
    c_rows = jnp.clip(counts[c_e] - R - (cidx - o_first[c_e]) * R, 0, R).astype(i32)
    used = cidx < n_used
    last = jnp.maximum(n_used - 1, 0)
    c_src = jnp.where(used, cidx, last).astype(i32)
    c_e = jnp.where(used, c_e, c_e[last]).astype(i32)
    c_rows = jnp.where(used, c_rows, 0).astype(i32)
    overflow = dict(tok=slot_tok[E * R:], used=sub_used(c_rows),
                    meta=(c_e, c_src, c_rows, n_used.reshape(1)))
    pos = slot.reshape(T, TOP_K).astype(i32)
    return R, main, overflow, n_used, pos[:, 0], pos[:, 1]


def kernel(x_prompt, x_sample, state_gla, state_conv, ev_norm_mix, ev_w_in, ev_sgu_norm, ev_sgu_w, ev_sgu_b, ev_gla_w_alpha, ev_gla_b_alpha, ev_gla_norm, ev_w_out, ev_norm_ffn, ev_ffn_w_gu, ev_ffn_w_down, od_norm_mix, od_conv_w_in, od_conv_w, od_conv_w_out, od_norm_ffn, od_router_w, od_moe_w_gu, od_moe_w_down, final_norm):
    nb, seq, D = x_prompt.shape
    B = x_sample.shape[0]
    assert x_sample.shape[1] == 1
    Tp = nb * seq
    T = Tp + B
    xp0 = x_prompt.reshape(Tp, D)
    xs0 = x_sample.reshape(B, D)

    sw = ev_sgu_norm.shape[1]
    kw = ev_gla_w_alpha.shape[2]
    rank = ev_gla_w_alpha.shape[1]
    vw = GLA_HEADS * ev_gla_norm.shape[1]
    n_main = 2 * sw + 2 * kw + 2 * vw
    assert ev_w_in.shape[2] == n_main + rank
    tm = _pick_tile(T, 2080)

    h = _norm_bf16_stacked(xp0, xs0, ev_norm_mix[0])
    P, log_a = _mix_in(h, ev_w_in, n_main, ev_gla_w_alpha[0], ev_gla_b_alpha[0], tm, 512)
    a_out, v_sample = _sgu(P, Tp, B, sw, ev_sgu_norm[0], ev_sgu_w[0], ev_sgu_b[0])
    col_q = 2 * sw
    o_p, gla_p = _gla_prompt(P, log_a, nb, seq, col_q, kw, vw, ev_gla_norm[0])
    Ps = P[Tp:]
    q_s = Ps[:, col_q:col_q + kw]
    k_s = Ps[:, col_q + kw:col_q + 2 * kw]
    v_s = Ps[:, col_q + 2 * kw:col_q + 2 * kw + vw]
    r_s = Ps[:, col_q + 2 * kw + vw:]
    gla_s, o_s = _gla_sample(state_gla[0], log_a[Tp:], k_s, q_s, v_s, r_s, ev_gla_norm[0])
    x1, h = _mix_out(a_out, o_p, o_s.reshape(B, vw).astype(BF16), xp0, xs0, ev_w_out[0],
                     ev_norm_ffn[0])

    hid = _swiglu_up(h, ev_ffn_w_gu[0], tm, 512)
    x2 = _matmul_res(hid, ev_ffn_w_down[0], x1, _pick_tile(T, 1040), 256, "ffn_down")

    h = _norm_bf16(x2, od_norm_mix[0])
    bgate, z = _conv_in(h, od_conv_w_in[0], tm, 256)
    x3, h3_slab, idx, gates = _conv_out_route(
        z, bgate, od_conv_w[0], state_conv[0][:, 0], state_conv[0][:, 1],
        x2, od_conv_w_out[0], od_norm_ffn[0], od_router_w[0], Tp, seq)
    h3_slab = h3_slab.reshape(T, D // LANES, LANES)
    conv_p = jnp.stack([z[(n + 1) * seq - 2:(n + 1) * seq] for n in range(nb)])
    conv_s = jnp.concatenate([state_conv[0][:, 1:], z[Tp:].reshape(B, 1, D)], axis=1)

    R, main, overflow, n_overflow, pos1, pos2 = _moe_plan(idx[:, :TOP_K], T)
    FE = od_moe_w_down.shape[2]
    tf, tk = _pick_tile(FE, 512, LANES), _pick_tile(FE, 1024, LANES)

    def experts(part, direct):
        xs = _dispatch(h3_slab, part["tok"], part["used"])
        hid = _moe_up(xs, od_moe_w_gu[0], part["meta"], R, tf, direct)
        return _moe_down(hid, od_moe_w_down[0], part["meta"], R, 1024, tk, direct)

    ys = experts(main, True)
    g1, g2 = gates[:, 0:1], gates[:, 1:2]
    y_p, y_s = lax.cond(
        n_overflow > 0,
        lambda: _combine(x3, ys, experts(overflow, False), pos1, pos2, g1, g2, final_norm, Tp),
        lambda: _combine(x3, ys, None, pos1, pos2, g1, g2, final_norm, Tp))

    y_prompt = y_p.reshape(nb, seq, D)
    y_sample = y_s.reshape(B, 1, D)
    return (y_prompt, y_sample, gla_p[None], gla_s[None],
            v_sample.reshape(1, B, 1, sw), conv_p[None], conv_s[None])
```

```python
import functools
import math

import numpy as np
import jax
import jax.numpy as jnp
from jax import lax
from jax.experimental import pallas as pl
from jax.experimental.pallas import tpu as pltpu

F32 = jnp.float32
BF16 = jnp.bfloat16
EPS = 1e-6

SGU_CHUNK = 128
SGU_GROUPS = 4
GLA_HEADS = 4
GLA_TAU = 16.0
GLA_CHUNK = 128
N_EXPERTS = 8
TOP_K = 2
MOE_SUB = 256
MOE_CAPACITY = 1.35
MOE_BIG = 1024
MOE_SMALL = 128
LANES = 128
SLAB_PITCH_PAD = 4
VMEM_LIMIT = 52 * 1024 * 1024
VMEM_LIMIT_MAX = 60 * 1024 * 1024


def _cparams(n_axes, vmem=VMEM_LIMIT):
    return pltpu.CompilerParams(dimension_semantics=("arbitrary",) * n_axes,
                                vmem_limit_bytes=vmem)


def _pick_tile(n, cap, mult=16):
    best = None
    for t in range(mult, min(n, cap) + 1, mult):
        if n % t == 0:
            best = t
    assert best is not None, (n, cap, mult)
    return best


def _rms(x, g):
    ms = jnp.mean(x * x, axis=-1, keepdims=True)
    return (x * lax.rsqrt(ms + EPS)) * g


def _gelu(x):
    c = math.sqrt(2.0 / math.pi)
    return x * (0.5 * (1.0 + jnp.tanh(c * (x + 0.044715 * (x * x * x)))))


def _silu(x):
    return x * (1.0 / (1.0 + jnp.exp(-x)))


def _dot(a, b):
    return jnp.dot(a, b, preferred_element_type=F32)


def _dot_nt(a, b):
    return lax.dot_general(a, b, (((1,), (1,)), ((), ())), preferred_element_type=F32)


def _norm_kernel(x_ref, g_ref, o_ref):
    o_ref[...] = _rms(x_ref[...], g_ref[...]).astype(BF16)


def _norm_bf16(x, g):
    T, D = x.shape
    tr = _pick_tile(T, 512)
    return pl.pallas_call(
        _norm_kernel,
        grid=(T // tr,),
        in_specs=[pl.BlockSpec((tr, D), lambda i: (i, 0)),
                  pl.BlockSpec((1, D), lambda i: (0, 0))],
        out_specs=pl.BlockSpec((tr, D), lambda i: (i, 0)),
        out_shape=jax.ShapeDtypeStruct((T, D), BF16),
        compiler_params=_cparams(1),
        name="rmsnorm_bf16",
    )(x, g.reshape(1, D))


def _norm2_kernel(n_prompt_blocks, xp_ref, xs_ref, g_ref, o_ref):
    i = pl.program_id(0)

    @pl.when(i < n_prompt_blocks)
    def _():
        o_ref[...] = _rms(xp_ref[...], g_ref[...]).astype(BF16)

    @pl.when(i >= n_prompt_blocks)
    def _():
        o_ref[0:xs_ref.shape[0], :] = _rms(xs_ref[...], g_ref[...]).astype(BF16)


def _norm_bf16_stacked(xp, xs, g):
    Tp, D = xp.shape
    B = xs.shape[0]
    tr = _pick_tile(Tp, 512)
    assert B <= tr
    npb = Tp // tr
    return pl.pallas_call(
        functools.partial(_norm2_kernel, npb),
        grid=(npb + 1,),
        in_specs=[pl.BlockSpec((tr, D), lambda i: (jnp.minimum(i, npb - 1), 0)),
                  pl.BlockSpec((B, D), lambda i: (0, 0)),
                  pl.BlockSpec((1, D), lambda i: (0, 0))],
        out_specs=pl.BlockSpec((tr, D), lambda i: (i, 0)),
        out_shape=jax.ShapeDtypeStruct((Tp + B, D), BF16),
        compiler_params=_cparams(1),
        name="rmsnorm_stack_bf16",
    )(xp, xs, g.reshape(1, D))


def _mm_res_kernel(x_ref, w_ref, r_ref, o_ref):
    o_ref[...] = r_ref[...] + _dot(x_ref[...], w_ref[...].astype(BF16))


def _matmul_res(x, w, res, tm, tn, name):
    T, K = x.shape
    N = w.shape[1]
    return pl.pallas_call(
        _mm_res_kernel,
        grid=(T // tm, N // tn),
        in_specs=[pl.BlockSpec((tm, K), lambda i, j: (i, 0)),
                  pl.BlockSpec((K, tn), lambda i, j: (0, j)),
                  pl.BlockSpec((tm, tn), lambda i, j: (i, j))],
        out_specs=pl.BlockSpec((tm, tn), lambda i, j: (i, j)),
        out_shape=jax.ShapeDtypeStruct((T, N), F32),
        compiler_params=_cparams(2),
        name=name,
    )(x, w, res)


def _mix_out_kernel(n_full, n_tail, a_ref, op_ref, os_ref, rp_ref, rs_ref, w_ref, ng_ref,
                    x_ref, h_ref):
    i = pl.program_id(0)
    sw = a_ref.shape[1]

    def finish(rows, a, o, res):
        y = res + _dot(a, w_ref[0:sw, :]) + _dot(o, w_ref[sw:, :])
        x_ref[0:rows, :] = y
        h_ref[0:rows, :] = _rms(y, ng_ref[...]).astype(BF16)

    @pl.when(i < n_full)
    def _():
        finish(a_ref.shape[0], a_ref[...], op_ref[...], rp_ref[...])

    @pl.when(i >= n_full)
    def _():
        finish(n_tail, a_ref[0:n_tail, :], os_ref[...], rs_ref[...])


def _mix_out(a, o_p, o_s, res_p, res_s, w, norm_g):
    T, sw = a.shape
    Tp, vw = o_p.shape
    B = o_s.shape[0]
    D = w.shape[1]
    assert T == Tp + B
    tm = _pick_tile(Tp, 512)
    assert B <= tm
    n_full = Tp // tm
    clamp = lambda i: (jnp.minimum(i, n_full - 1), 0)
    row = lambda i: (i, 0)
    fixed = lambda i: (0, 0)
    return pl.pallas_call(
        functools.partial(_mix_out_kernel, n_full, B),
        grid=(n_full + 1,),
        in_specs=[pl.BlockSpec((tm, sw), row),
                  pl.BlockSpec((tm, vw), clamp),
                  pl.BlockSpec((B, vw), fixed),
                  pl.BlockSpec((tm, D), clamp),
                  pl.BlockSpec((B, D), fixed),
                  pl.BlockSpec((sw + vw, D), fixed),
                  pl.BlockSpec((1, D), fixed)],
        out_specs=[pl.BlockSpec((tm, D), row),
                   pl.BlockSpec((tm, D), row)],
        out_shape=[jax.ShapeDtypeStruct((T, D), F32),
                   jax.ShapeDtypeStruct((T, D), BF16)],
        compiler_params=_cparams(1),
        name="mix_out_proj",
    )(a, o_p, o_s, res_p, res_s, w.astype(BF16), norm_g.reshape(1, D))


def _swiglu_kernel(x_ref, wg_ref, wu_ref, o_ref):
    x = x_ref[...]
    half = o_ref.shape[1] // 2
    for c in range(2):
        cols = slice(c * half, (c + 1) * half)
        g = _dot(x, wg_ref[:, cols].astype(BF16))
        u = _dot(x, wu_ref[:, cols].astype(BF16))
        o_ref[:, cols] = (_silu(g) * u).astype(o_ref.dtype)


def _swiglu_up(x, w_gu, tm, tn):
    T, K = x.shape
    F = w_gu.shape[1] // 2
    nj = F // tn
    return pl.pallas_call(
        _swiglu_kernel,
        grid=(T // tm, nj),
        in_specs=[pl.BlockSpec((tm, K), lambda i, j: (i, 0)),
                  pl.BlockSpec((K, tn), lambda i, j: (0, j)),
                  pl.BlockSpec((K, tn), lambda i, j: (0, j + nj))],
        out_specs=pl.BlockSpec((tm, tn), lambda i, j: (i, j)),
        out_shape=jax.ShapeDtypeStruct((T, F), BF16),
        compiler_params=_cparams(2, vmem=VMEM_LIMIT_MAX),
        name="ffn_gate_up",
    )(x, w_gu, w_gu)


def _conv_in_kernel(x_ref, wb_ref, wc_ref, wu_ref, b_ref, z_ref):
    x = x_ref[...]
    b_ref[...] = _dot(x, wb_ref[...].astype(BF16))
    c = _dot(x, wc_ref[...].astype(BF16))
    u = _dot(x, wu_ref[...].astype(BF16))
    z_ref[...] = c * u


def _conv_in(x, w, tm, tn):
    T, K = x.shape
    C = w.shape[1] // 3
    nj = C // tn
    return pl.pallas_call(
        _conv_in_kernel,
        grid=(T // tm, nj),
        in_specs=[pl.BlockSpec((tm, K), lambda i, j: (i, 0)),
                  pl.BlockSpec((K, tn), lambda i, j: (0, j)),
                  pl.BlockSpec((K, tn), lambda i, j: (0, j + nj)),
                  pl.BlockSpec((K, tn), lambda i, j: (0, j + 2 * nj))],
        out_specs=[pl.BlockSpec((tm, tn), lambda i, j: (i, j)),
                   pl.BlockSpec((tm, tn), lambda i, j: (i, j))],
        out_shape=[jax.ShapeDtypeStruct((T, C), F32),
                   jax.ShapeDtypeStruct((T, C), F32)],
        compiler_params=_cparams(2),
        name="conv_in_proj",
    )(x, w, w, w)


def _mix_in_kernel(x_ref, w_ref, wl_ref, wa_ref, b_ref, p_ref, la_ref):
    x = x_ref[...]
    p_ref[...] = _dot(x, w_ref[...].astype(BF16))

    @pl.when(pl.program_id(1) == pl.num_programs(1) - 1)
    def _():
        rank = wa_ref.shape[0]
        g = _dot(x, wl_ref[:, 0:rank].astype(BF16))
        z = _dot(g.astype(BF16), wa_ref[...].astype(BF16)) + b_ref[...]
        ls = jnp.minimum(z, 0.0) - jnp.log1p(jnp.exp(-jnp.abs(z)))
        la_ref[...] = ls * (1.0 / GLA_TAU)


def _mix_in(x, w_in, n_main, w_alpha, b_alpha, tm, tn):
    T, K = x.shape
    R, KW = w_alpha.shape
    assert n_main % LANES == 0 and R <= LANES
    return pl.pallas_call(
        _mix_in_kernel,
        grid=(T // tm, n_main // tn),
        in_specs=[pl.BlockSpec((tm, K), lambda i, j: (i, 0)),
                  pl.BlockSpec((None, K, tn), lambda i, j: (0, 0, j)),
                  pl.BlockSpec((None, K, LANES), lambda i, j: (0, 0, n_main // LANES)),
                  pl.BlockSpec((R, KW), lambda i, j: (0, 0)),
                  pl.BlockSpec((1, KW), lambda i, j: (0, 0))],
        out_specs=[pl.BlockSpec((tm, tn), lambda i, j: (i, j)),
                   pl.BlockSpec((tm, KW), lambda i, j: (i, 0))],
        out_shape=[jax.ShapeDtypeStruct((T, n_main), F32),
                   jax.ShapeDtypeStruct((T, KW), F32)],
        compiler_params=_cparams(2, vmem=VMEM_LIMIT_MAX),
        name="mix_in_proj",
    )(x, w_in, w_in, w_alpha, b_alpha.reshape(1, KW))


def _sgu_kernel(n_prompt_tiles, u_ref, v_ref, ng_ref, w_ref, bt_ref, o_ref, vs_ref):
    i = pl.program_id(0)
    C = SGU_CHUNK
    gd = u_ref.shape[1] // SGU_GROUPS
    row = lax.broadcasted_iota(jnp.int32, (C, C), 0)
    col = lax.broadcasted_iota(jnp.int32, (C, C), 1)

    def chunk(r0, weights, biases, keep_v):
        gu = _gelu(u_ref[r0:r0 + C, :])
        vn = _rms(_gelu(v_ref[r0:r0 + C, :]), ng_ref[...])
        if keep_v:
            vs_ref[...] = vn
        vb = vn.astype(BF16)
        for g in range(SGU_GROUPS):
            cols = slice(g * gd, (g + 1) * gd)
            mixed = _dot(weights[g], vb[:, cols]) + biases[g]
            o_ref[r0:r0 + C, cols] = (gu[:, cols] * mixed).astype(o_ref.dtype)

    @pl.when(i < n_prompt_tiles)
    def _():
        weights = [jnp.where(col <= row, w_ref[g], 0.0).astype(BF16) for g in range(SGU_GROUPS)]
        biases = [bt_ref[:, g:g + 1] for g in range(SGU_GROUPS)]
        for r0 in range(0, u_ref.shape[0], C):
            chunk(r0, weights, biases, False)

    @pl.when(i >= n_prompt_tiles)
    def _():
        weights = [jnp.where(col == row, w_ref[g, 0:1, 0:1], 0.0).astype(BF16)
                   for g in range(SGU_GROUPS)]
        biases = [bt_ref[0:1, g:g + 1] for g in range(SGU_GROUPS)]
        chunk(0, weights, biases, True)


def _sgu(P, n_prompt, n_sample, width, sgu_norm, sgu_w, sgu_b):
    C = SGU_CHUNK
    assert n_prompt % C == 0 and n_sample == C
    T = n_prompt + n_sample
    tr = C * math.gcd(n_prompt // C, 4)
    npt = n_prompt // tr
    return pl.pallas_call(
        functools.partial(_sgu_kernel, npt),
        grid=(npt + 1,),
        in_specs=[pl.BlockSpec((tr, width), lambda i: (i, 0)),
                  pl.BlockSpec((tr, width), lambda i: (i, 1)),
                  pl.BlockSpec((1, width), lambda i: (0, 0)),
                  pl.BlockSpec((SGU_GROUPS, C, C), lambda i: (0, 0, 0)),
                  pl.BlockSpec((C, SGU_GROUPS), lambda i: (0, 0))],
        out_specs=[pl.BlockSpec((tr, width), lambda i: (i, 0)),
                   pl.BlockSpec((C, width), lambda i: (0, 0))],
        out_shape=[jax.ShapeDtypeStruct((T, width), BF16),
                   jax.ShapeDtypeStruct((n_sample, width), F32)],
        compiler_params=_cparams(1),
        name="sgu_mix",
    )(P, P, sgu_norm.reshape(1, width), sgu_w, sgu_b.T)


def _gla_constants():
    C = GLA_CHUNK
    s = np.arange(C)[None, :]
    i = np.arange(C)[:, None]
    gs = [s <= i, s > i]
    masks = []
    b = C // 2
    while b >= 1:
        upper = (i % (2 * b)) >= b
        m = (i // (2 * b)) * (2 * b) + b - 1
        gs.append(np.where(upper, (s > m) & (s <= i), (s > i) & (s <= m)))
        same = (i // (2 * b)) == (s // (2 * b))
        masks.append(upper & same & ~upper.T)
        b //= 2
    masks.append(np.eye(C, dtype=bool))
    G = np.concatenate(gs, axis=0).astype(np.float32)
    M = np.stack(masks).astype(np.float32)
    return jnp.asarray(G, BF16), jnp.asarray(M, F32)


def _gla_kernel(nseq, *refs):
    ins = [refs[5 * s:5 * s + 5] for s in range(nseq)]
    G_ref, M_ref, gn_ref, o_ref, s_out_ref, st_ref = refs[5 * nseq:]
    c = pl.program_id(1)
    C = GLA_CHUNK
    hk = ins[0][0].shape[1]
    dk = hk // GLA_HEADS
    dv = ins[0][2].shape[1] // GLA_HEADS
    nlev = M_ref.shape[0] - 1

    @pl.when(c == 0)
    def _():
        st_ref[...] = jnp.zeros_like(st_ref)

    parts = []
    for s in range(nseq):
        la = ins[s][4][...]
        hi = la.astype(BF16)
        r1 = la - hi.astype(F32)
        mid = r1.astype(BF16)
        lo = (r1 - mid.astype(F32)).astype(BF16)
        parts.append(jnp.concatenate([hi, mid, lo], axis=0))
    X = jnp.exp(_dot(G_ref[...], jnp.concatenate(parts, axis=1)))

    scale = dk ** -0.5
    for s in range(nseq):
        q_ref, k_ref, v_ref, r_ref, _ = ins[s]
        for h in range(GLA_HEADS):
            Xh = X[:, s * hk + h * dk:s * hk + (h + 1) * dk]
            qh = q_ref[:, h * dk:(h + 1) * dk] * scale
            kh = k_ref[:, h * dk:(h + 1) * dk]
            vh = v_ref[:, h * dv:(h + 1) * dv]
            x_cum = Xh[0:C]
            x_rest = Xh[C:2 * C]
            st_old = st_ref[s, h]
            o = _dot_nt((qh * x_cum).astype(BF16), st_old.astype(BF16))
            att = _dot_nt(qh.astype(BF16), kh.astype(BF16)) * M_ref[nlev]
            for l in range(nlev):
                xl = Xh[(2 + l) * C:(3 + l) * C]
                att = att + _dot_nt((qh * xl).astype(BF16), (kh * xl).astype(BF16)) * M_ref[l]
            o = o + _dot(att.astype(BF16), vh.astype(BF16))
            vt = vh.T.astype(BF16)
            st_ref[s, h] = st_old * x_cum[C - 1:C, :] + _dot(vt, (kh * x_rest).astype(BF16))
            on = _rms(o, gn_ref[...]) * _silu(r_ref[:, h * dv:(h + 1) * dv])
            o_ref[s, :, h * dv:(h + 1) * dv] = on.astype(o_ref.dtype)

    @pl.when(c == pl.num_programs(1) - 1)
    def _():
        for s in range(nseq):
            for h in range(GLA_HEADS):
                s_out_ref[s, h] = st_ref[s, h].T


def _gla_prompt(P, log_a, n_seq, seq, col_q, kw, vw, gla_norm):
    C = GLA_CHUNK
    assert seq % C == 0
    nc = seq // C
    dk = kw // GLA_HEADS
    dv = vw // GLA_HEADS
    qb, kb = col_q // kw, col_q // kw + 1
    vb = (col_q + 2 * kw) // vw
    rb = vb + 1
    assert col_q % kw == 0 and (col_q + 2 * kw) % vw == 0
    nseq = math.gcd(n_seq, 4)
    G, M = _gla_constants()
    G3 = jnp.concatenate([G, G, G], axis=1)

    def seq_specs(s):
        rows = lambda n, c: (n * nseq + s) * nc + c
        return [pl.BlockSpec((C, kw), lambda n, c: (rows(n, c), qb)),
                pl.BlockSpec((C, kw), lambda n, c: (rows(n, c), kb)),
                pl.BlockSpec((C, vw), lambda n, c: (rows(n, c), vb)),
                pl.BlockSpec((C, vw), lambda n, c: (rows(n, c), rb)),
                pl.BlockSpec((C, kw), lambda n, c: (rows(n, c), 0))]

    in_specs = [sp for s in range(nseq) for sp in seq_specs(s)]
    in_specs += [pl.BlockSpec(G3.shape, lambda n, c: (0, 0)),
                 pl.BlockSpec(M.shape, lambda n, c: (0, 0, 0)),
                 pl.BlockSpec((1, dv), lambda n, c: (0, 0))]
    o, states = pl.pallas_call(
        functools.partial(_gla_kernel, nseq),
        grid=(n_seq // nseq, nc),
        in_specs=in_specs,
        out_specs=[pl.BlockSpec((None, nseq, C, vw), lambda n, c: (n, 0, c, 0)),
                   pl.BlockSpec((nseq, GLA_HEADS, dk, dv), lambda n, c: (n, 0, 0, 0))],
        out_shape=[jax.ShapeDtypeStruct((n_seq // nseq, nseq, seq, vw), BF16),
                   jax.ShapeDtypeStruct((n_seq, GLA_HEADS, dk, dv), F32)],
        scratch_shapes=[pltpu.VMEM((nseq, GLA_HEADS, dv, dk), F32)],
        compiler_params=_cparams(2),
        name="gla_prompt",
    )(*([P, P, P, P, log_a] * nseq), G3, M, gla_norm.reshape(1, dv))
    return o.reshape(n_seq * seq, vw), states


def _gla_step_kernel(s_ref, at_ref, kt_ref, qt_ref, v_ref, r_ref, gn_ref, sn_ref, o_ref):
    i = pl.program_id(0)
    bb, H, dk, dv = s_ref.shape
    B = at_ref.shape[1]
    lane = lax.broadcasted_iota(jnp.int32, (dk, B), 1)
    scale = dk ** -0.5
    for j in range(bb):
        pick = lane == i * bb + j

        def column(t_ref, h):
            return jnp.sum(jnp.where(pick, t_ref[h * dk:(h + 1) * dk, :], 0.0), axis=1, keepdims=True)

        for h in range(H):
            s_new = jnp.exp(column(at_ref, h)) * s_ref[j, h] + column(kt_ref, h) * v_ref[j, h]
            sn_ref[j, h] = s_new
            o = jnp.sum((column(qt_ref, h) * scale) * s_new, axis=0, keepdims=True)
            o_ref[j, h] = _rms(o, gn_ref[...]) * _silu(r_ref[j, h])


def _gla_sample(state, la, k, q, v, r, gla_norm):
    B, H, dk, dv = state.shape
    bb = 8
    rowv = lambda t: t.reshape(B, H, 1, dv)
    tspec = pl.BlockSpec((H * dk, B), lambda i: (0, 0))
    rspec = pl.BlockSpec((bb, H, 1, dv), lambda i: (i, 0, 0, 0))
    sspec = pl.BlockSpec((bb, H, dk, dv), lambda i: (i, 0, 0, 0))
    return pl.pallas_call(
        _gla_step_kernel,
        grid=(B // bb,),
        in_specs=[sspec, tspec, tspec, tspec, rspec, rspec,
                  pl.BlockSpec((1, dv), lambda i: (0, 0))],
        out_specs=[sspec, rspec],
        out_shape=[jax.ShapeDtypeStruct((B, H, dk, dv), F32),
                   jax.ShapeDtypeStruct((B, H, 1, dv), F32)],
        compiler_params=_cparams(1),
        name="gla_sample_step",
    )(state, la.T, k.T, q.T, rowv(v), rowv(r), gla_norm.reshape(1, dv))


def _route(logits):
    lane = lax.broadcasted_iota(jnp.int32, logits.shape, 1)
    neg = jnp.float32(-jnp.inf)
    logits = jnp.where(lane < N_EXPERTS, logits, neg)
    v1 = jnp.max(logits, axis=-1, keepdims=True)
    i1 = jnp.min(jnp.where(logits == v1, lane, LANES), axis=-1, keepdims=True)
    rest = jnp.where(lane == i1, neg, logits)
    v2 = jnp.max(rest, axis=-1, keepdims=True)
    i2 = jnp.min(jnp.where(rest == v2, lane, LANES), axis=-1, keepdims=True)
    e2 = jnp.exp(v2 - v1)
    g1 = 1.0 / (1.0 + e2)
    g2 = e2 / (1.0 + e2)
    idx = jnp.where(lane == 0, i1, jnp.where(lane == 1, i2, 0))
    gate = jnp.where(lane == 0, g1, jnp.where(lane == 1, g2, 0.0))
    return idx, gate


def _conv_out_kernel(n_prompt_blocks, blocks_per_seq, n_tail, z_ref, halo_ref, b_ref, cw_ref,
                     h0_ref, h1_ref, res_ref, w_ref, ng_ref, rw_ref, x_ref, xslab_ref, idx_ref,
                     gate_ref):
    i = pl.program_id(0)
    w0, w1, w2 = cw_ref[0:1, :], cw_ref[1:2, :], cw_ref[2:3, :]
    ns = x_ref.shape[1] // LANES

    def finish(rows, yc, res):
        y = res + _dot(yc.astype(BF16), w_ref[...])
        x_ref[0:rows, :] = y
        hn = _rms(y, ng_ref[...])
        for s in range(ns):
            xslab_ref[pl.ds(s, rows, stride=ns), :] = hn[:, s * LANES:(s + 1) * LANES]
        h_hi = hn.astype(BF16)
        h_lo = (hn - h_hi.astype(F32)).astype(BF16)
        rw = rw_ref[...]
        r_hi = rw.astype(BF16)
        r_lo = (rw - r_hi.astype(F32)).astype(BF16)
        logits = _dot(h_hi, r_hi) + (_dot(h_hi, r_lo) + _dot(h_lo, r_hi))
        idx, gate = _route(logits)
        idx_ref[0:rows, :] = idx
        gate_ref[0:rows, :] = gate

    @pl.when(i < n_prompt_blocks)
    def _():
        z = z_ref[...]
        row = lax.broadcasted_iota(jnp.int32, z.shape, 0)
        halo = jnp.where(i % blocks_per_seq == 0, 0.0, halo_ref[...])
        z1 = jnp.where(row == 0, halo[7:8, :], pltpu.roll(z, 1, 0))
        z2 = pltpu.roll(z, 2, 0)
        z2 = jnp.where(row == 0, halo[6:7, :], jnp.where(row == 1, halo[7:8, :], z2))
        finish(z.shape[0], b_ref[...] * (w0 * z2 + w1 * z1 + w2 * z), res_ref[...])

    @pl.when(i >= n_prompt_blocks)
    def _():
        conv = w0 * h0_ref[...] + w1 * h1_ref[...] + w2 * z_ref[0:n_tail, :]
        finish(n_tail, b_ref[0:n_tail, :] * conv, res_ref[0:n_tail, :])


def _conv_out_route(z, bgate, conv_w, hist0, hist1, res, w_out, norm_g, router_w, n_prompt, seq):
    T, Cd = z.shape
    D = w_out.shape[1]
    B = T - n_prompt
    tm = _pick_tile(math.gcd(seq, n_prompt), 256)
    assert B <= tm and hist0.shape == (B, Cd)
    npb = n_prompt // tm
    wpad = jnp.zeros((D, LANES), F32).at[:, :N_EXPERTS].set(router_w)
    row = lambda i: (i, 0)
    fixed = lambda i: (0, 0)
    return pl.pallas_call(
        functools.partial(_conv_out_kernel, npb, seq // tm, B),
        grid=(npb + 1,),
        in_specs=[pl.BlockSpec((tm, Cd), row),
                  pl.BlockSpec((8, Cd), lambda i: (jnp.maximum(i * (tm // 8) - 1, 0), 0)),
                  pl.BlockSpec((tm, Cd), row),
                  pl.BlockSpec(conv_w.shape, fixed),
                  pl.BlockSpec((B, Cd), fixed),
                  pl.BlockSpec((B, Cd), fixed),
                  pl.BlockSpec((tm, D), row),
                  pl.BlockSpec((Cd, D), fixed),
                  pl.BlockSpec((1, D), fixed),
                  pl.BlockSpec((D, LANES), fixed)],
        out_specs=[pl.BlockSpec((tm, D), row),
                   pl.BlockSpec((tm * (D // LANES), LANES), row),
                   pl.BlockSpec((tm, LANES), row),
                   pl.BlockSpec((tm, LANES), row)],
        out_shape=[jax.ShapeDtypeStruct((T, D), F32),
                   jax.ShapeDtypeStruct((T * (D // LANES), LANES), F32),
                   jax.ShapeDtypeStruct((T, LANES), jnp.int32),
                   jax.ShapeDtypeStruct((T, LANES), F32)],
        compiler_params=_cparams(1),
        name="conv_out_route",
    )(z, z, bgate, conv_w, hist0, hist1, res, w_out.astype(BF16), norm_g.reshape(1, D), wpad)


def _dispatch_kernel(tok_ref, nxt_ref, used_ref, x_hbm, o_ref, buf_ref, sem):
    i = pl.program_id(0)
    nsteps = pl.num_programs(0)
    n = o_ref.shape[0]
    ns = x_hbm.shape[1]
    pitch = buf_ref.shape[1] // n
    slot = i % 2

    def row_copy(t_ref, r, s):
        return pltpu.make_async_copy(x_hbm.at[t_ref[0, 0, r]],
                                     buf_ref.at[s, pl.ds(r * pitch, ns)], sem.at[s])

    def start_rows(t_ref, s):
        def body(p, carry):
            row_copy(t_ref, 2 * p, s).start(priority=0)
            row_copy(t_ref, 2 * p + 1, s).start(priority=1)
            return carry

        lax.fori_loop(0, n // 2, body, 0, unroll=4)

    @pl.when(jnp.logical_and(i == 0, used_ref[0] > 0))
    def _():
        start_rows(tok_ref, 0)

    @pl.when(jnp.logical_and(i + 1 < nsteps, used_ref[jnp.minimum(i + 1, nsteps - 1)] > 0))
    def _():
        start_rows(nxt_ref, 1 - slot)

    @pl.when(used_ref[i] > 0)
    def _():
        landed = buf_ref.at[slot, pl.ds(0, n * ns)]
        pltpu.make_async_copy(landed, landed, sem.at[slot]).wait()

        for s in range(ns):
            o_ref[:, s * LANES:(s + 1) * LANES] = (
                buf_ref[slot, pl.ds(s, n, stride=pitch), :].astype(o_ref.dtype))

    @pl.when(used_ref[i] == 0)
    def _():
        o_ref[...] = jnp.zeros_like(o_ref)


def _dispatch(x_slab, slot_tok, sub_used):
    D = x_slab.shape[1] * x_slab.shape[2]
    nsub = sub_used.shape[0]
    SB = MOE_SUB
    tok3 = slot_tok.reshape(nsub, 1, SB)
    return pl.pallas_call(
        _dispatch_kernel,
        grid_spec=pltpu.PrefetchScalarGridSpec(
            num_scalar_prefetch=0,
            grid=(nsub,),
            in_specs=[pl.BlockSpec((1, 1, SB), lambda i: (i, 0, 0), memory_space=pltpu.SMEM),
                      pl.BlockSpec((1, 1, SB), lambda i: (jnp.minimum(i + 1, nsub - 1), 0, 0),
                                   memory_space=pltpu.SMEM),
                      pl.BlockSpec(memory_space=pltpu.SMEM),
                      pl.BlockSpec(memory_space=pl.ANY)],
            out_specs=pl.BlockSpec((SB, D), lambda i: (i, 0)),
            scratch_shapes=[pltpu.VMEM((2, SB * (D // LANES + SLAB_PITCH_PAD), LANES), F32),
                            pltpu.SemaphoreType.DMA((2,))],
        ),
        out_shape=jax.ShapeDtypeStruct((nsub * SB, D), BF16),
        compiler_params=_cparams(1),
        name="moe_dispatch",
    )(tok3, tok3, sub_used, x_slab)


def _row_pieces(rows, total, small, compute, skip):
    for s in range(0, total, MOE_BIG):
        size = min(MOE_BIG, total - s)
        assert size % small == 0
        whole = rows > s + size - small

        @pl.when(whole)
        def _():
            compute(s, size)

        @pl.when(jnp.logical_not(whole))
        def _():
            for t in range(s, s + size, small):
                @pl.when(rows > t)
                def _():
                    compute(t, small)

                @pl.when(rows <= t)
                def _():
                    skip(t, small)


def _moe_up_kernel(ce_ref, cs_ref, cr_ref, nu_ref, x_ref, wg_ref, wu_ref, o_ref, wbf_ref):
    c = pl.program_id(0)

    @pl.when(c < nu_ref[0])
    def _():
        def compute(s, n):
            if s == 0:
                wg = wg_ref[0].astype(BF16)
                wu = wu_ref[0].astype(BF16)
                wbf_ref[0] = wg
                wbf_ref[1] = wu
            else:
                wg = wbf_ref[0]
                wu = wbf_ref[1]
            x = x_ref[pl.ds(s, n), :]
            o_ref[pl.ds(s, n), :] = (_silu(_dot(x, wg)) * _dot(x, wu)).astype(o_ref.dtype)

        def skip(s, n):
            o_ref[pl.ds(s, n), :] = jnp.zeros((n, o_ref.shape[1]), o_ref.dtype)

        _row_pieces(cr_ref[c], x_ref.shape[0], MOE_SMALL, compute, skip)

    @pl.when(c >= nu_ref[0])
    def _():
        o_ref[...] = jnp.zeros_like(o_ref)


def _moe_up(xs, w_gu, meta, R, tf, direct):
    ce, cs, cr, nu = meta
    nch = ce.shape[0]
    D = xs.shape[1]
    F = w_gu.shape[2] // 2
    nf = F // tf

    def fidx(c, f, nu):
        return jnp.where(c < nu[0], f, nf - 1)

    if direct:
        in_specs = [pl.BlockSpec((R, D), lambda c, f, *_: (c, 0)),
                    pl.BlockSpec((1, D, tf), lambda c, f, *_: (c, 0, f)),
                    pl.BlockSpec((1, D, tf), lambda c, f, *_: (c, 0, nf + f))]
    else:
        in_specs = [pl.BlockSpec((R, D), lambda c, f, ce, cs, cr, nu: (cs[c], 0),
                                 pipeline_mode=pl.Buffered(1)),
                    pl.BlockSpec((1, D, tf), lambda c, f, ce, cs, cr, nu: (ce[c], 0, fidx(c, f, nu))),
                    pl.BlockSpec((1, D, tf), lambda c, f, ce, cs, cr, nu: (ce[c], 0, nf + fidx(c, f, nu)))]
    return pl.pallas_call(
        _moe_up_kernel,
        grid_spec=pltpu.PrefetchScalarGridSpec(
            num_scalar_prefetch=4,
            grid=(nch, nf),
            in_specs=in_specs,
            out_specs=pl.BlockSpec((R, tf), lambda c, f, ce, cs, cr, nu: (c, f)),
            scratch_shapes=[pltpu.VMEM((2, D, tf), BF16)],
        ),
        out_shape=jax.ShapeDtypeStruct((nch * R, F), BF16),
        compiler_params=_cparams(2, vmem=VMEM_LIMIT_MAX if direct else VMEM_LIMIT),
        name="moe_gate_up",
    )(ce, cs, cr, nu, xs, w_gu, w_gu)


def _moe_down_kernel(ce_ref, cs_ref, cr_ref, nu_ref, h_ref, w_ref, o_ref, wbf_ref):
    c = pl.program_id(0)
    k = pl.program_id(2)

    @pl.when(c < nu_ref[0])
    def _():
        def weight(s):
            if s == 0:
                w = w_ref[0].astype(BF16)
                wbf_ref[...] = w
                return w
            return wbf_ref[...]

        def compute(s, n):
            @pl.when(k == 0)
            def _():
                o_ref[pl.ds(s, n), :] = _dot(h_ref[pl.ds(s, n), :], weight(s))

            @pl.when(k > 0)
            def _():
                o_ref[pl.ds(s, n), :] = (o_ref[pl.ds(s, n), :]
                                         + _dot(h_ref[pl.ds(s, n), :], weight(s)))

        def skip(s, n):
            @pl.when(k == 0)
            def _():
                o_ref[pl.ds(s, n), :] = jnp.zeros((n, o_ref.shape[1]), F32)

        _row_pieces(cr_ref[c], h_ref.shape[0], MOE_SUB, compute, skip)

    @pl.when(jnp.logical_and(c >= nu_ref[0], k == 0))
    def _():
        o_ref[...] = jnp.zeros_like(o_ref)


def _moe_down(hid, w_down, meta, R, tn, tk, direct):
    ce, cs, cr, nu = meta
    nch = ce.shape[0]
    F = hid.shape[1]
    D = w_down.shape[2]
    nn, nk = D // tn, F // tk

    def last_if_unused(c, v, nu, last):
        return jnp.where(c < nu[0], v, last)

    if direct:
        in_specs = [pl.BlockSpec((R, tk), lambda c, n, k, *_: (c, k)),
                    pl.BlockSpec((1, tk, tn), lambda c, n, k, *_: (c, k, n))]
    else:
        in_specs = [pl.BlockSpec((R, tk), lambda c, n, k, ce, cs, cr, nu:
                                 (cs[c], last_if_unused(c, k, nu, nk - 1))),
                    pl.BlockSpec((1, tk, tn), lambda c, n, k, ce, cs, cr, nu:
                                 (ce[c], last_if_unused(c, k, nu, nk - 1),
                                  last_if_unused(c, n, nu, nn - 1)))]
    return pl.pallas_call(
        _moe_down_kernel,
        grid_spec=pltpu.PrefetchScalarGridSpec(
            num_scalar_prefetch=4,
            grid=(nch, nn, nk),
            in_specs=in_specs,
            out_specs=pl.BlockSpec((R, tn), lambda c, n, k, ce, cs, cr, nu: (c, n)),
            scratch_shapes=[pltpu.VMEM((tk, tn), BF16)],
        ),
        out_shape=jax.ShapeDtypeStruct((nch * R, D), F32),
        compiler_params=_cparams(3),
        name="moe_down",
    )(ce, cs, cr, nu, hid, w_down)


def _combine_kernel(n_prompt_blocks, has_overflow, p1_ref, p2_ref, n1_ref, n2_ref, x_ref, g1_ref,
                    g2_ref, fg_ref, ys_hbm, *rest):
    yo_hbm = rest[0] if has_overflow else None
    op_ref, os_ref, buf_ref, sem = rest[-4:]
    i = pl.program_id(0)
    nsteps = pl.num_programs(0)
    n = x_ref.shape[0]
    slot = i % 2
    split = ys_hbm.shape[0]

    def start_row(pos, k, r, s):
        dst = buf_ref.at[s, k, pl.ds(r, 1)]
        if yo_hbm is None:
            pltpu.make_async_copy(ys_hbm.at[pl.ds(pos, 1)], dst, sem.at[s]).start(priority=k)
        else:
            @pl.when(pos < split)
            def _():
                pltpu.make_async_copy(ys_hbm.at[pl.ds(pos, 1)], dst, sem.at[s]).start(priority=k)

            @pl.when(pos >= split)
            def _():
                pltpu.make_async_copy(yo_hbm.at[pl.ds(pos - split, 1)], dst,
                                      sem.at[s]).start(priority=k)

    def start_rows(a_ref, b_ref, s):
        def body(r, carry):
            start_row(a_ref[0, 0, r], 0, r, s)
            start_row(b_ref[0, 0, r], 1, r, s)
            return carry

        lax.fori_loop(0, n, body, 0, unroll=8)

    @pl.when(i == 0)
    def _():
        start_rows(p1_ref, p2_ref, 0)

    @pl.when(i + 1 < nsteps)
    def _():
        start_rows(n1_ref, n2_ref, 1 - slot)

    pltpu.make_async_copy(buf_ref.at[slot], buf_ref.at[slot], sem.at[slot]).wait()
    y = x_ref[...] + (g1_ref[...] * buf_ref[slot, 0] + g2_ref[...] * buf_ref[slot, 1])
    yn = _rms(y, fg_ref[...])

    @pl.when(i < n_prompt_blocks)
    def _():
        op_ref[...] = yn

    @pl.when(i >= n_prompt_blocks)
    def _():
        os_ref[...] = yn


def _combine(x, ys, ys_over, pos1, pos2, g1, g2, final_g, n_prompt):
    T, D = x.shape
    srcs = [ys] if ys_over is None else [ys, ys_over]
    tr = T - n_prompt
    assert n_prompt % tr == 0
    nb = T // tr
    npb = n_prompt // tr
    cur = pl.BlockSpec((1, 1, tr), lambda i: (i, 0, 0), memory_space=pltpu.SMEM)
    nxt = pl.BlockSpec((1, 1, tr), lambda i: (jnp.minimum(i + 1, nb - 1), 0, 0),
                       memory_space=pltpu.SMEM)
    p1 = pos1.reshape(nb, 1, tr)
    p2 = pos2.reshape(nb, 1, tr)
    return pl.pallas_call(
        functools.partial(_combine_kernel, npb, ys_over is not None),
        grid_spec=pltpu.PrefetchScalarGridSpec(
            num_scalar_prefetch=0,
            grid=(nb,),
            in_specs=[cur, cur, nxt, nxt,
                      pl.BlockSpec((tr, D), lambda i: (i, 0)),
                      pl.BlockSpec((tr, 1), lambda i: (i, 0)),
                      pl.BlockSpec((tr, 1), lambda i: (i, 0)),
                      pl.BlockSpec((1, D), lambda i: (0, 0))]
                     + [pl.BlockSpec(memory_space=pl.ANY)] * len(srcs),
            out_specs=[pl.BlockSpec((tr, D), lambda i: (jnp.minimum(i, npb - 1), 0)),
                       pl.BlockSpec((tr, D), lambda i: (0, 0))],
            scratch_shapes=[pltpu.VMEM((2, 2, tr, D), F32), pltpu.SemaphoreType.DMA((2,))],
        ),
        out_shape=[jax.ShapeDtypeStruct((n_prompt, D), F32),
                   jax.ShapeDtypeStruct((tr, D), F32)],
        compiler_params=_cparams(1),
        name="moe_combine_final_norm",
    )(p1, p2, p1, p2, x, g1, g2, final_g.reshape(1, D), *srcs)


def _moe_plan(idx, T):
    M = T * TOP_K
    E = N_EXPERTS
    R = -(-int(M / E * MOE_CAPACITY) // MOE_SUB) * MOE_SUB
    n_over = -(-M // R)
    i32 = jnp.int32
    flat_e = idx.reshape(M)
    onehot = (flat_e[:, None] == jnp.arange(E, dtype=i32)[None, :]).astype(i32)
    csum = jnp.cumsum(onehot, axis=0)
    rank = jnp.sum((csum - onehot) * onehot, axis=1)
    counts = csum[-1]
    o_chunks = jnp.maximum((counts + R - 1) // R - 1, 0)
    o_first = jnp.cumsum(o_chunks) - o_chunks
    n_used = jnp.sum(o_chunks).astype(i32)
    over = rank - R
    slot = jnp.where(rank < R, flat_e * R + rank,
                     (E + o_first[flat_e] + over // R) * R + over % R)
    flat_tok = jnp.arange(M, dtype=i32) // TOP_K
    slot_tok = jnp.zeros(((E + n_over) * R,), i32).at[slot].set(
        flat_tok, unique_indices=True, mode="promise_in_bounds")
    spc = R // MOE_SUB

    def sub_used(c_rows):
        sub = jnp.arange(c_rows.shape[0] * spc, dtype=i32)
        return ((sub % spc) * MOE_SUB < c_rows[sub // spc]).astype(i32)

    m_rows = jnp.minimum(counts, R).astype(i32)
    eidx = jnp.arange(E, dtype=i32)
    main = dict(tok=slot_tok[:E * R], used=sub_used(m_rows),
                meta=(eidx, eidx, m_rows, jnp.full((1,), E, i32)))
    cidx = jnp.arange(n_over, dtype=i32)
    c_e = jnp.clip(jnp.searchsorted(jnp.cumsum(o_chunks), cidx, side='right'), 0, E - 1).astype(i32)
    c_rows = jnp.clip(counts[c_e] - R - (cidx - o_first[c_e]) * R, 0, R).astype(i32)
    used = cidx < n_used
    last = jnp.maximum(n_used - 1, 0)
    c_src = jnp.where(used, cidx, last).astype(i32)
    c_e = jnp.where(used, c_e, c_e[last]).astype(i32)
    c_rows = jnp.where(used, c_rows, 0).astype(i32)
    overflow = dict(tok=slot_tok[E * R:], used=sub_used(c_rows),
                    meta=(c_e, c_src, c_rows, n_used.reshape(1)))
    pos = slot.reshape(T, TOP_K).astype(i32)
    return R, main, overflow, n_used, pos[:, 0], pos[:, 1]


def kernel(x_prompt, x_sample, state_gla, state_conv, ev_norm_mix, ev_w_in, ev_sgu_norm, ev_sgu_w, ev_sgu_b, ev_gla_w_alpha, ev_gla_b_alpha, ev_gla_norm, ev_w_out, ev_norm_ffn, ev_ffn_w_gu, ev_ffn_w_down, od_norm_mix, od_conv_w_in, od_conv_w, od_conv_w_out, od_norm_ffn, od_router_w, od_moe_w_gu, od_moe_w_down, final_norm):
    nb, seq, D = x_prompt.shape
    B = x_sample.shape[0]
    assert x_sample.shape[1] == 1
    Tp = nb * seq
    T = Tp + B
    xp0 = x_prompt.reshape(Tp, D)
    xs0 = x_sample.reshape(B, D)

    sw = ev_sgu_norm.shape[1]
    kw = ev_gla_w_alpha.shape[2]
    rank = ev_gla_w_alpha.shape[1]
    vw = GLA_HEADS * ev_gla_norm.shape[1]
    n_main = 2 * sw + 2 * kw + 2 * vw
    assert ev_w_in.shape[2] == n_main + rank
    tm = _pick_tile(T, 2080)

    h = _norm_bf16_stacked(xp0, xs0, ev_norm_mix[0])
    P, log_a = _mix_in(h, ev_w_in, n_main, ev_gla_w_alpha[0], ev_gla_b_alpha[0], tm, 512)
    a_out, v_sample = _sgu(P, Tp, B, sw, ev_sgu_norm[0], ev_sgu_w[0], ev_sgu_b[0])
    col_q = 2 * sw
    o_p, gla_p = _gla_prompt(P, log_a, nb, seq, col_q, kw, vw, ev_gla_norm[0])
    Ps = P[Tp:]
    q_s = Ps[:, col_q:col_q + kw]
    k_s = Ps[:, col_q + kw:col_q + 2 * kw]
    v_s = Ps[:, col_q + 2 * kw:col_q + 2 * kw + vw]
    r_s = Ps[:, col_q + 2 * kw + vw:]
    gla_s, o_s = _gla_sample(state_gla[0], log_a[Tp:], k_s, q_s, v_s, r_s, ev_gla_norm[0])
    x1, h = _mix_out(a_out, o_p, o_s.reshape(B, vw).astype(BF16), xp0, xs0, ev_w_out[0],
                     ev_norm_ffn[0])

    hid = _swiglu_up(h, ev_ffn_w_gu[0], tm, 512)
    x2 = _matmul_res(hid, ev_ffn_w_down[0], x1, _pick_tile(T, 1040), 256, "ffn_down")

    h = _norm_bf16(x2, od_norm_mix[0])
    bgate, z = _conv_in(h, od_conv_w_in[0], tm, 256)
    x3, h3_slab, idx, gates = _conv_out_route(
        z, bgate, od_conv_w[0], state_conv[0][:, 0], state_conv[0][:, 1],
        x2, od_conv_w_out[0], od_norm_ffn[0], od_router_w[0], Tp, seq)
    h3_slab = h3_slab.reshape(T, D // LANES, LANES)
    conv_p = jnp.stack([z[(n + 1) * seq - 2:(n + 1) * seq] for n in range(nb)])
    conv_s = jnp.concatenate([state_conv[0][:, 1:], z[Tp:].reshape(B, 1, D)], axis=1)

    R, main, overflow, n_overflow, pos1, pos2 = _moe_plan(idx[:, :TOP_K], T)
    FE = od_moe_w_down.shape[2]
    tf, tk = _pick_tile(FE, 512, LANES), _pick_tile(FE, 1024, LANES)

    def experts(part, direct):
        xs = _dispatch(h3_slab, part["tok"], part["used"])
        hid = _moe_up(xs, od_moe_w_gu[0], part["meta"], R, tf, direct)
        return _moe_down(hid, od_moe_w_down[0], part["meta"], R, 1024, tk, direct)

    ys = experts(main, True)
    g1, g2 = gates[:, 0:1], gates[:, 1:2]
    y_p, y_s = lax.cond(
        n_overflow > 0,
        lambda: _combine(x3, ys, experts(overflow, False), pos1, pos2, g1, g2, final_norm, Tp),
        lambda: _combine(x3, ys, None, pos1, pos2, g1, g2, final_norm, Tp))

    y_prompt = y_p.reshape(nb, seq, D)
    y_sample = y_s.reshape(B, 1, D)
    return (y_prompt, y_sample, gla_p[None], gla_s[None],
            v_sample.reshape(1, B, 1, sw), conv_p[None], conv_s[None])
```
